```python
import math
import jax, jax.numpy as jnp
from jax import lax
import numpy as np

D_MODEL = 2048
BATCH = 8
SEQ = 8192
DEPTH = 4

N_MIXERS = 2
N_MEM = 256
MIX_WIDTH = 2 * D_MODEL
MEM_WIDTH = MIX_WIDTH // 4
TOK_WIDTH = MIX_WIDTH - MEM_WIDTH
MEM_HEADS = 4
MEM_HEAD_DIM = MEM_WIDTH // MEM_HEADS

SSD_HEAD_DIM = 64
SSD_HEADS = TOK_WIDTH // SSD_HEAD_DIM
SSD_GROUPS = 8
SSD_HEADS_PER_GROUP = SSD_HEADS // SSD_GROUPS
SSD_STATE = 128
SSD_CONV = 4
SSD_CHUNK = 128
SSD_CONV_DIM = TOK_WIDTH + 2 * SSD_GROUPS * SSD_STATE
SSD_IN_COLS = SSD_CONV_DIM + SSD_HEADS + MEM_WIDTH + MIX_WIDTH

ATTN_HEAD_DIM = 128
ATTN_HEADS_PER_GROUP = TOK_WIDTH // ATTN_HEAD_DIM
DILATED_GROUPS = ((128, 1), (512, 4), (2048, 16))
N_DIL = len(DILATED_GROUPS)
N_ALIBI_HEADS = N_DIL * ATTN_HEADS_PER_GROUP
ALIBI_MAX_EXP = 8.0
ATTN_GROUP_COLS = 3 * TOK_WIDTH
ATTN_IN_COLS = N_DIL * ATTN_GROUP_COLS + MEM_WIDTH + MIX_WIDTH
ATTN_BLOCK = 128
EPS = 1e-6

kernel_name = 'hybrid_ssd_dilated_memory_trunk'


def _rmsnorm(x, g):
    xf = x.astype(jnp.float32)
    xf = xf * lax.rsqrt(jnp.mean(xf * xf, axis=-1, keepdims=True) + EPS)
    return xf.astype(x.dtype) * g


def _grouped_rmsnorm(x, g, groups):
    shp = x.shape
    xg = x.reshape(shp[:-1] + (groups, shp[-1] // groups)).astype(jnp.float32)
    xg = xg * lax.rsqrt(jnp.mean(xg * xg, axis=-1, keepdims=True) + EPS)
    return xg.reshape(shp).astype(x.dtype) * g


def _causal_depthwise_conv(u, w, b):
    y = lax.conv_general_dilated(u, w[:, None, :], window_strides=(1,), padding=[(SSD_CONV - 1, 0)],
                                 dimension_numbers=('NWC', 'WIO', 'NWC'), feature_group_count=u.shape[-1])
    return y + b


def _memory_cross_attention(q_mem, mem_n, w_mem_kv):
    b_, t_, _ = q_mem.shape
    mk, mv = jnp.split(mem_n @ w_mem_kv, 2, axis=-1)
    q = q_mem.reshape(b_, t_, MEM_HEADS, MEM_HEAD_DIM)
    mk = mk.reshape(b_, -1, MEM_HEADS, MEM_HEAD_DIM)
    mv = mv.reshape(b_, -1, MEM_HEADS, MEM_HEAD_DIM)
    s = jnp.einsum('bthe,bmhe->bhtm', q, mk).astype(jnp.float32) * (MEM_HEAD_DIM ** -0.5)
    p = jax.nn.softmax(s, axis=-1).astype(mv.dtype)
    return jnp.einsum('bhtm,bmhe->bthe', p, mv).reshape(b_, t_, MEM_WIDTH)


def _ssd_chunked(xh, dt, a, bm, cm):
    b_, t_ = xh.shape[:2]
    nc = t_ // SSD_CHUNK
    def chunk(v):
        return v.reshape((b_, nc, SSD_CHUNK) + v.shape[2:])
    xc, dtc, bc, cc = chunk(xh), chunk(dt), chunk(bm), chunk(cm)
    a_cs = jnp.cumsum(dtc * a, axis=2)
    xdt = xc * dtc[..., None]
    pos = jnp.arange(SSD_CHUNK)
    causal = (pos[:, None] >= pos[None, :])[None, None, :, :, None, None]
    seg = a_cs[:, :, :, None] - a_cs[:, :, None, :]
    decay_ls = jnp.exp(jnp.where(causal, seg, -jnp.inf))
    cb = jnp.einsum('bclgn,bcsgn->bclsg', cc, bc)
    y_diag = jnp.einsum('bclsg,bclsgh,bcsghp->bclghp', cb, decay_ls, xdt)
    decay_to_end = jnp.exp(a_cs[:, :, -1:] - a_cs)
    chunk_states = jnp.einsum('bcsgn,bcsgh,bcsghp->bcghpn', bc, decay_to_end, xdt)
    chunk_decay = jnp.exp(a_cs[:, :, -1])

    def step(h, inp):
        st, dec = inp
        return dec[..., None, None] * h + st, h

    h0 = jnp.zeros((b_,) + chunk_states.shape[2:], chunk_states.dtype)
    _, h_prev = lax.scan(step, h0, (jnp.moveaxis(chunk_states, 1, 0), jnp.moveaxis(chunk_decay, 1, 0)))
    h_prev = jnp.moveaxis(h_prev, 0, 1)
    y_off = jnp.einsum('bclgn,bcghpn,bclgh->bclghp', cc, h_prev, jnp.exp(a_cs))
    return (y_diag + y_off).reshape(xh.shape)


def _dilated_group_attention(q, k, v, window, dilation, slopes):
    b_, t_, nh, e_ = q.shape
    n_sub = t_ // dilation
    w = window // dilation
    c = min(ATTN_BLOCK, n_sub)
    nb = -(-n_sub // c)
    lp = nb * c
    tail = lp - n_sub
    def strided(arr):
        return arr.reshape(b_, n_sub, dilation, nh, e_)
    qs = jnp.pad(strided(q), ((0, 0), (0, tail), (0, 0), (0, 0), (0, 0)))
    ks = jnp.pad(strided(k), ((0, 0), (w, tail), (0, 0), (0, 0), (0, 0)))
    vs = jnp.pad(strided(v), ((0, 0), (w, tail), (0, 0), (0, 0), (0, 0)))
    rel = jnp.arange(c)[:, None] + w - jnp.arange(c + w)[None, :]
    band = (rel >= 0) & (rel <= w)
    bias = -slopes[:, None, None] * (dilation * rel).astype(jnp.float32)[None]
    scale = e_ ** -0.5

    def block(n):
        start = n * c
        qb = lax.dynamic_slice_in_dim(qs, start, c, axis=1)
        kb = lax.dynamic_slice_in_dim(ks, start, c + w, axis=1)
        vb = lax.dynamic_slice_in_dim(vs, start, c + w, axis=1)
        key_pos = start - w + jnp.arange(c + w)
        valid = band & (key_pos >= 0)[None, :]
        s = jnp.einsum('bqrhe,bkrhe->brhqk', qb, kb).astype(jnp.float32) * scale + bias
        s = jnp.where(valid, s, -jnp.inf)
        m = jnp.max(s, axis=-1, keepdims=True)
        p = jnp.exp(s - m)
        den = jnp.transpose(jnp.sum(p, axis=-1), (0, 3, 1, 2))
        o = jnp.einsum('brhqk,bkrhe->bqrhe', p.astype(vb.dtype), vb).astype(jnp.float32)
        lse = jnp.transpose(m[..., 0], (0, 3, 1, 2)) + jnp.log(den)
        return o / den[..., None], lse

    o, lse = lax.map(block, jnp.arange(nb))
    o = jnp.moveaxis(o, 0, 1).reshape(b_, lp, dilation, nh, e_)[:, :n_sub].reshape(b_, t_, nh, e_)
    lse = jnp.moveaxis(lse, 0, 1).reshape(b_, lp, dilation, nh)[:, :n_sub].reshape(b_, t_, nh)
    return o, lse


def _ssd_layer(x, mem_n, norm_g, w_in, conv_w, conv_b, dt_bias, a_log, d_skip, ssd_norm_g, w_mem_kv, w_out):
    b_, t_, _ = x.shape
    h = _rmsnorm(x, norm_g)
    proj = h @ w_in
    xbc, dt_raw, q_mem, z = jnp.split(
        proj, [SSD_CONV_DIM, SSD_CONV_DIM + SSD_HEADS, SSD_CONV_DIM + SSD_HEADS + MEM_WIDTH], axis=-1)
    xbc = jax.nn.silu(_causal_depthwise_conv(xbc, conv_w, conv_b))
    xs, bm, cm = jnp.split(xbc, [TOK_WIDTH, TOK_WIDTH + SSD_GROUPS * SSD_STATE], axis=-1)
    xs = xs.reshape(b_, t_, SSD_GROUPS, SSD_HEADS_PER_GROUP, SSD_HEAD_DIM)
    bm = bm.reshape(b_, t_, SSD_GROUPS, SSD_STATE)
    cm = cm.reshape(b_, t_, SSD_GROUPS, SSD_STATE)
    dt = jax.nn.softplus(dt_raw.astype(jnp.float32) + dt_bias.astype(jnp.float32))
    dt = dt.reshape(b_, t_, SSD_GROUPS, SSD_HEADS_PER_GROUP)
    a = -jnp.exp(a_log.astype(jnp.float32)).reshape(SSD_GROUPS, SSD_HEADS_PER_GROUP)
    y = _ssd_chunked(xs, dt, a, bm, cm) + d_skip.reshape(SSD_GROUPS, SSD_HEADS_PER_GROUP, 1) * xs
    y_tok = y.reshape(b_, t_, TOK_WIDTH).astype(x.dtype)
    y_mem = _memory_cross_attention(q_mem, mem_n, w_mem_kv)
    gated = jnp.concatenate([y_tok, y_mem], axis=-1) * jax.nn.silu(z)
    gated = jnp.concatenate([_grouped_rmsnorm(gated[..., :TOK_WIDTH], ssd_norm_g, SSD_GROUPS),
                             gated[..., TOK_WIDTH:]], axis=-1)
    return x + gated @ w_out


def _dilated_attention_layer(x, mem_n, norm_g, w_in, w_mem_kv, w_out):
    b_, t_, _ = x.shape
    h = _rmsnorm(x, norm_g)
    slopes = jnp.exp2(-ALIBI_MAX_EXP * jnp.arange(1, N_ALIBI_HEADS + 1, dtype=jnp.float32) / N_ALIBI_HEADS)
    slopes = slopes.reshape(N_DIL, ATTN_HEADS_PER_GROUP)
    outs, lses = [], []
    for g, (window, dilation) in enumerate(DILATED_GROUPS):
        qkv = (h @ w_in[:, g * ATTN_GROUP_COLS:(g + 1) * ATTN_GROUP_COLS])
        qkv = qkv.reshape(b_, t_, 3, ATTN_HEADS_PER_GROUP, ATTN_HEAD_DIM)
        o, lse = _dilated_group_attention(qkv[:, :, 0], qkv[:, :, 1], qkv[:, :, 2], window, dilation, slopes[g])
        outs.append(o)
        lses.append(lse)
    wts = jax.nn.softmax(jnp.stack(lses), axis=0)
    y_tok = jnp.einsum('gbth,gbthe->bthe', wts, jnp.stack(outs)).reshape(b_, t_, TOK_WIDTH).astype(x.dtype)
    q_mem, z = jnp.split(h @ w_in[:, N_DIL * ATTN_GROUP_COLS:], [MEM_WIDTH], axis=-1)
    y_mem = _memory_cross_attention(q_mem, mem_n, w_mem_kv)
    gated = jnp.concatenate([y_tok, y_mem], axis=-1) * jax.nn.silu(z)
    return x + gated @ w_out


def _fwd_setup_inputs(seed: int = 0) -> dict:
    key = jax.random.key(seed)
    keys = iter(jax.random.split(key, 64))
    f32 = jnp.float32

    def nrm(shape, scale):
        return scale * jax.random.normal(next(keys), shape, f32)

    def gain(n):
        return 1.0 + nrm((n,), 0.02)

    inp = {
        'x': nrm((BATCH, SEQ, D_MODEL), 1.0),
        'mem': nrm((BATCH, N_MEM, D_MODEL), 1.0),
        'mem_norm_g': gain(D_MODEL),
        'final_norm_g': gain(D_MODEL),
    }
    for i in range(DEPTH):
        inp[f'norm_g_{i}'] = gain(D_MODEL)
        if i % N_MIXERS == 0:
            inp[f'w_in_{i}'] = nrm((D_MODEL, SSD_IN_COLS), D_MODEL ** -0.5)
            inp[f'conv_w_{i}'] = nrm((SSD_CONV, SSD_CONV_DIM), SSD_CONV ** -0.5)
            inp[f'conv_b_{i}'] = nrm((SSD_CONV_DIM,), 0.01)
            dt0 = jnp.exp(jax.random.uniform(next(keys), (SSD_HEADS,), f32, math.log(1e-3), math.log(1e-1)))
            inp[f'dt_bias_{i}'] = dt0 + jnp.log(-jnp.expm1(-dt0))
            inp[f'a_log_{i}'] = jnp.log(jax.random.uniform(next(keys), (SSD_HEADS,), f32, 1.0, 16.0))
            inp[f'd_skip_{i}'] = gain(SSD_HEADS)
            inp[f'ssd_norm_g_{i}'] = gain(TOK_WIDTH)
        else:
            inp[f'w_in_{i}'] = nrm((D_MODEL, ATTN_IN_COLS), D_MODEL ** -0.5)
        inp[f'w_mem_kv_{i}'] = nrm((D_MODEL, 2 * MEM_WIDTH), D_MODEL ** -0.5)
        inp[f'w_out_{i}'] = nrm((MIX_WIDTH, D_MODEL), MIX_WIDTH ** -0.5)
    return inp


def _fwd_reference(x, mem, mem_norm_g, final_norm_g,
              norm_g_0, w_in_0, conv_w_0, conv_b_0, dt_bias_0, a_log_0, d_skip_0, ssd_norm_g_0, w_mem_kv_0, w_out_0,
              norm_g_1, w_in_1, w_mem_kv_1, w_out_1,
              norm_g_2, w_in_2, conv_w_2, conv_b_2, dt_bias_2, a_log_2, d_skip_2, ssd_norm_g_2, w_mem_kv_2, w_out_2,
              norm_g_3, w_in_3, w_mem_kv_3, w_out_3):
    mem_n = _rmsnorm(mem, mem_norm_g)
    params = [
        (norm_g_0, w_in_0, conv_w_0, conv_b_0, dt_bias_0, a_log_0, d_skip_0, ssd_norm_g_0, w_mem_kv_0, w_out_0),
        (norm_g_1, w_in_1, w_mem_kv_1, w_out_1),
        (norm_g_2, w_in_2, conv_w_2, conv_b_2, dt_bias_2, a_log_2, d_skip_2, ssd_norm_g_2, w_mem_kv_2, w_out_2),
        (norm_g_3, w_in_3, w_mem_kv_3, w_out_3),
    ]
    for i in range(DEPTH):
        layer_fn = _ssd_layer if i % N_MIXERS == 0 else _dilated_attention_layer
        x = layer_fn(x, mem_n, *params[i])
    return _rmsnorm(x, final_norm_g)


import jax as _jax
import jax.numpy as _jnp

TWIN_FORMAT = 'train_step'
FWD_PARAMS = ['x', 'mem', 'mem_norm_g', 'final_norm_g', 'norm_g_0', 'w_in_0', 'conv_w_0', 'conv_b_0', 'dt_bias_0', 'a_log_0', 'd_skip_0', 'ssd_norm_g_0', 'w_mem_kv_0', 'w_out_0', 'norm_g_1', 'w_in_1', 'w_mem_kv_1', 'w_out_1', 'norm_g_2', 'w_in_2', 'conv_w_2', 'conv_b_2', 'dt_bias_2', 'a_log_2', 'd_skip_2', 'ssd_norm_g_2', 'w_mem_kv_2', 'w_out_2', 'norm_g_3', 'w_in_3', 'w_mem_kv_3', 'w_out_3']
TWIN_WEIGHTS = ['mem_norm_g', 'final_norm_g', 'norm_g_0', 'w_in_0', 'conv_w_0', 'conv_b_0', 'dt_bias_0', 'a_log_0', 'd_skip_0', 'ssd_norm_g_0', 'w_mem_kv_0', 'w_out_0', 'norm_g_1', 'w_in_1', 'w_mem_kv_1', 'w_out_1', 'norm_g_2', 'w_in_2', 'conv_w_2', 'conv_b_2', 'dt_bias_2', 'a_log_2', 'd_skip_2', 'ssd_norm_g_2', 'w_mem_kv_2', 'w_out_2', 'norm_g_3', 'w_in_3', 'w_mem_kv_3', 'w_out_3']
TWIN_DIFF_INPUT = 'x'
TWIN_INPUTS = ['x', 'mem', 'mem_norm_g', 'final_norm_g', 'norm_g_0', 'w_in_0', 'conv_w_0', 'conv_b_0', 'dt_bias_0', 'a_log_0', 'd_skip_0', 'ssd_norm_g_0', 'w_mem_kv_0', 'w_out_0', 'norm_g_1', 'w_in_1', 'w_mem_kv_1', 'w_out_1', 'norm_g_2', 'w_in_2', 'conv_w_2', 'conv_b_2', 'dt_bias_2', 'a_log_2', 'd_skip_2', 'ssd_norm_g_2', 'w_mem_kv_2', 'w_out_2', 'norm_g_3', 'w_in_3', 'w_mem_kv_3', 'w_out_3', 'loss_target', 'm_mem_norm_g', 'm_final_norm_g', 'm_norm_g_0', 'm_w_in_0', 'm_conv_w_0', 'm_conv_b_0', 'm_dt_bias_0', 'm_a_log_0', 'm_d_skip_0', 'm_ssd_norm_g_0', 'm_w_mem_kv_0', 'm_w_out_0', 'm_norm_g_1', 'm_w_in_1', 'm_w_mem_kv_1', 'm_w_out_1', 'm_norm_g_2', 'm_w_in_2', 'm_conv_w_2', 'm_conv_b_2', 'm_dt_bias_2', 'm_a_log_2', 'm_d_skip_2', 'm_ssd_norm_g_2', 'm_w_mem_kv_2', 'm_w_out_2', 'm_norm_g_3', 'm_w_in_3', 'm_w_mem_kv_3', 'm_w_out_3', 'v_mem_norm_g', 'v_final_norm_g', 'v_norm_g_0', 'v_w_in_0', 'v_conv_w_0', 'v_conv_b_0', 'v_dt_bias_0', 'v_a_log_0', 'v_d_skip_0', 'v_ssd_norm_g_0', 'v_w_mem_kv_0', 'v_w_out_0', 'v_norm_g_1', 'v_w_in_1', 'v_w_mem_kv_1', 'v_w_out_1', 'v_norm_g_2', 'v_w_in_2', 'v_conv_w_2', 'v_conv_b_2', 'v_dt_bias_2', 'v_a_log_2', 'v_d_skip_2', 'v_ssd_norm_g_2', 'v_w_mem_kv_2', 'v_w_out_2', 'v_norm_g_3', 'v_w_in_3', 'v_w_mem_kv_3', 'v_w_out_3']
TWIN_OUTPUTS = ['loss', 'grad_x', 'grad_mem_norm_g', 'grad_final_norm_g', 'grad_norm_g_0', 'grad_w_in_0', 'grad_conv_w_0', 'grad_conv_b_0', 'grad_dt_bias_0', 'grad_a_log_0', 'grad_d_skip_0', 'grad_ssd_norm_g_0', 'grad_w_mem_kv_0', 'grad_w_out_0', 'grad_norm_g_1', 'grad_w_in_1', 'grad_w_mem_kv_1', 'grad_w_out_1', 'grad_norm_g_2', 'grad_w_in_2', 'grad_conv_w_2', 'grad_conv_b_2', 'grad_dt_bias_2', 'grad_a_log_2', 'grad_d_skip_2', 'grad_ssd_norm_g_2', 'grad_w_mem_kv_2', 'grad_w_out_2', 'grad_norm_g_3', 'grad_w_in_3', 'grad_w_mem_kv_3', 'grad_w_out_3', 'delta_mem_norm_g', 'delta_final_norm_g', 'delta_norm_g_0', 'delta_w_in_0', 'delta_conv_w_0', 'delta_conv_b_0', 'delta_dt_bias_0', 'delta_a_log_0', 'delta_d_skip_0', 'delta_ssd_norm_g_0', 'delta_w_mem_kv_0', 'delta_w_out_0', 'delta_norm_g_1', 'delta_w_in_1', 'delta_w_mem_kv_1', 'delta_w_out_1', 'delta_norm_g_2', 'delta_w_in_2', 'delta_conv_w_2', 'delta_conv_b_2', 'delta_dt_bias_2', 'delta_a_log_2', 'delta_d_skip_2', 'delta_ssd_norm_g_2', 'delta_w_mem_kv_2', 'delta_w_out_2', 'delta_norm_g_3', 'delta_w_in_3', 'delta_w_mem_kv_3', 'delta_w_out_3', 'new_m_mem_norm_g', 'new_m_final_norm_g', 'new_m_norm_g_0', 'new_m_w_in_0', 'new_m_conv_w_0', 'new_m_conv_b_0', 'new_m_dt_bias_0', 'new_m_a_log_0', 'new_m_d_skip_0', 'new_m_ssd_norm_g_0', 'new_m_w_mem_kv_0', 'new_m_w_out_0', 'new_m_norm_g_1', 'new_m_w_in_1', 'new_m_w_mem_kv_1', 'new_m_w_out_1', 'new_m_norm_g_2', 'new_m_w_in_2', 'new_m_conv_w_2', 'new_m_conv_b_2', 'new_m_dt_bias_2', 'new_m_a_log_2', 'new_m_d_skip_2', 'new_m_ssd_norm_g_2', 'new_m_w_mem_kv_2', 'new_m_w_out_2', 'new_m_norm_g_3', 'new_m_w_in_3', 'new_m_w_mem_kv_3', 'new_m_w_out_3', 'new_v_mem_norm_g', 'new_v_final_norm_g', 'new_v_norm_g_0', 'new_v_w_in_0', 'new_v_conv_w_0', 'new_v_conv_b_0', 'new_v_dt_bias_0', 'new_v_a_log_0', 'new_v_d_skip_0', 'new_v_ssd_norm_g_0', 'new_v_w_mem_kv_0', 'new_v_w_out_0', 'new_v_norm_g_1', 'new_v_w_in_1', 'new_v_w_mem_kv_1', 'new_v_w_out_1', 'new_v_norm_g_2', 'new_v_w_in_2', 'new_v_conv_w_2', 'new_v_conv_b_2', 'new_v_dt_bias_2', 'new_v_a_log_2', 'new_v_d_skip_2', 'new_v_ssd_norm_g_2', 'new_v_w_mem_kv_2', 'new_v_w_out_2', 'new_v_norm_g_3', 'new_v_w_in_3', 'new_v_w_mem_kv_3', 'new_v_w_out_3']
TWIN_LEAF_KINDS = {'loss': 'loss', 'grad_x': 'grad_x', 'grad_mem_norm_g': 'grad_w', 'grad_final_norm_g': 'grad_w', 'grad_norm_g_0': 'grad_w', 'grad_w_in_0': 'grad_w', 'grad_conv_w_0': 'grad_w', 'grad_conv_b_0': 'grad_w', 'grad_dt_bias_0': 'grad_w', 'grad_a_log_0': 'grad_w', 'grad_d_skip_0': 'grad_w', 'grad_ssd_norm_g_0': 'grad_w', 'grad_w_mem_kv_0': 'grad_w', 'grad_w_out_0': 'grad_w', 'grad_norm_g_1': 'grad_w', 'grad_w_in_1': 'grad_w', 'grad_w_mem_kv_1': 'grad_w', 'grad_w_out_1': 'grad_w', 'grad_norm_g_2': 'grad_w', 'grad_w_in_2': 'grad_w', 'grad_conv_w_2': 'grad_w', 'grad_conv_b_2': 'grad_w', 'grad_dt_bias_2': 'grad_w', 'grad_a_log_2': 'grad_w', 'grad_d_skip_2': 'grad_w', 'grad_ssd_norm_g_2': 'grad_w', 'grad_w_mem_kv_2': 'grad_w', 'grad_w_out_2': 'grad_w', 'grad_norm_g_3': 'grad_w', 'grad_w_in_3': 'grad_w', 'grad_w_mem_kv_3': 'grad_w', 'grad_w_out_3': 'grad_w', 'delta_mem_norm_g': 'delta_w', 'delta_final_norm_g': 'delta_w', 'delta_norm_g_0': 'delta_w', 'delta_w_in_0': 'delta_w', 'delta_conv_w_0': 'delta_w', 'delta_conv_b_0': 'delta_w', 'delta_dt_bias_0': 'delta_w', 'delta_a_log_0': 'delta_w', 'delta_d_skip_0': 'delta_w', 'delta_ssd_norm_g_0': 'delta_w', 'delta_w_mem_kv_0': 'delta_w', 'delta_w_out_0': 'delta_w', 'delta_norm_g_1': 'delta_w', 'delta_w_in_1': 'delta_w', 'delta_w_mem_kv_1': 'delta_w', 'delta_w_out_1': 'delta_w', 'delta_norm_g_2': 'delta_w', 'delta_w_in_2': 'delta_w', 'delta_conv_w_2': 'delta_w', 'delta_conv_b_2': 'delta_w', 'delta_dt_bias_2': 'delta_w', 'delta_a_log_2': 'delta_w', 'delta_d_skip_2': 'delta_w', 'delta_ssd_norm_g_2': 'delta_w', 'delta_w_mem_kv_2': 'delta_w', 'delta_w_out_2': 'delta_w', 'delta_norm_g_3': 'delta_w', 'delta_w_in_3': 'delta_w', 'delta_w_mem_kv_3': 'delta_w', 'delta_w_out_3': 'delta_w', 'new_m_mem_norm_g': 'new_m', 'new_m_final_norm_g': 'new_m', 'new_m_norm_g_0': 'new_m', 'new_m_w_in_0': 'new_m', 'new_m_conv_w_0': 'new_m', 'new_m_conv_b_0': 'new_m', 'new_m_dt_bias_0': 'new_m', 'new_m_a_log_0': 'new_m', 'new_m_d_skip_0': 'new_m', 'new_m_ssd_norm_g_0': 'new_m', 'new_m_w_mem_kv_0': 'new_m', 'new_m_w_out_0': 'new_m', 'new_m_norm_g_1': 'new_m', 'new_m_w_in_1': 'new_m', 'new_m_w_mem_kv_1': 'new_m', 'new_m_w_out_1': 'new_m', 'new_m_norm_g_2': 'new_m', 'new_m_w_in_2': 'new_m', 'new_m_conv_w_2': 'new_m', 'new_m_conv_b_2': 'new_m', 'new_m_dt_bias_2': 'new_m', 'new_m_a_log_2': 'new_m', 'new_m_d_skip_2': 'new_m', 'new_m_ssd_norm_g_2': 'new_m', 'new_m_w_mem_kv_2': 'new_m', 'new_m_w_out_2': 'new_m', 'new_m_norm_g_3': 'new_m', 'new_m_w_in_3': 'new_m', 'new_m_w_mem_kv_3': 'new_m', 'new_m_w_out_3': 'new_m', 'new_v_mem_norm_g': 'new_v', 'new_v_final_norm_g': 'new_v', 'new_v_norm_g_0': 'new_v', 'new_v_w_in_0': 'new_v', 'new_v_conv_w_0': 'new_v', 'new_v_conv_b_0': 'new_v', 'new_v_dt_bias_0': 'new_v', 'new_v_a_log_0': 'new_v', 'new_v_d_skip_0': 'new_v', 'new_v_ssd_norm_g_0': 'new_v', 'new_v_w_mem_kv_0': 'new_v', 'new_v_w_out_0': 'new_v', 'new_v_norm_g_1': 'new_v', 'new_v_w_in_1': 'new_v', 'new_v_w_mem_kv_1': 'new_v', 'new_v_w_out_1': 'new_v', 'new_v_norm_g_2': 'new_v', 'new_v_w_in_2': 'new_v', 'new_v_conv_w_2': 'new_v', 'new_v_conv_b_2': 'new_v', 'new_v_dt_bias_2': 'new_v', 'new_v_a_log_2': 'new_v', 'new_v_d_skip_2': 'new_v', 'new_v_ssd_norm_g_2': 'new_v', 'new_v_w_mem_kv_2': 'new_v', 'new_v_w_out_2': 'new_v', 'new_v_norm_g_3': 'new_v', 'new_v_w_in_3': 'new_v', 'new_v_w_mem_kv_3': 'new_v', 'new_v_w_out_3': 'new_v'}


def _forward(args):
    return _fwd_reference(*[args[k] for k in FWD_PARAMS])


def _output_shape():
    def fwd():
        inp = _fwd_setup_inputs(0)
        return _fwd_reference(*[inp[k] for k in FWD_PARAMS])
    out = _jax.eval_shape(fwd)
    return out.shape, out.dtype

N_MICROBATCH = 1
ADAM_LR = 0.001
ADAM_B1 = 0.9
ADAM_B2 = 0.999
ADAM_EPS = 1e-08
ADAM_WD = 0.01
ADAM_STEP = 10
PER_EXAMPLE_BATCH_AXIS = {'x': 0, 'mem': 0, 'loss_target': 0}
SHARED_INPUTS = []
_WEIGHT_DTYPES = {'mem_norm_g': _jnp.float32, 'final_norm_g': _jnp.float32, 'norm_g_0': _jnp.float32, 'w_in_0': _jnp.float32, 'conv_w_0': _jnp.float32, 'conv_b_0': _jnp.float32, 'dt_bias_0': _jnp.float32, 'a_log_0': _jnp.float32, 'd_skip_0': _jnp.float32, 'ssd_norm_g_0': _jnp.float32, 'w_mem_kv_0': _jnp.float32, 'w_out_0': _jnp.float32, 'norm_g_1': _jnp.float32, 'w_in_1': _jnp.float32, 'w_mem_kv_1': _jnp.float32, 'w_out_1': _jnp.float32, 'norm_g_2': _jnp.float32, 'w_in_2': _jnp.float32, 'conv_w_2': _jnp.float32, 'conv_b_2': _jnp.float32, 'dt_bias_2': _jnp.float32, 'a_log_2': _jnp.float32, 'd_skip_2': _jnp.float32, 'ssd_norm_g_2': _jnp.float32, 'w_mem_kv_2': _jnp.float32, 'w_out_2': _jnp.float32, 'norm_g_3': _jnp.float32, 'w_in_3': _jnp.float32, 'w_mem_kv_3': _jnp.float32, 'w_out_3': _jnp.float32}
MOMENT_SCALE = {'mem_norm_g': 9.042017e-03, 'final_norm_g': 3.200243e+01, 'norm_g_0': 1.482465e-01, 'w_in_0': 6.577502e-02, 'conv_w_0': 6.601237e-02, 'conv_b_0': 1.043172e-01, 'dt_bias_0': 1.600823e-01, 'a_log_0': 2.837829e-01, 'd_skip_0': 4.250704e-01, 'ssd_norm_g_0': 8.204180e-02, 'w_mem_kv_0': 4.997708e-03, 'w_out_0': 9.933146e-02, 'norm_g_1': 3.880119e-02, 'w_in_1': 9.273853e-03, 'w_mem_kv_1': 4.859464e-03, 'w_out_1': 2.045946e-02, 'norm_g_2': 1.043516e-01, 'w_in_2': 4.596879e-02, 'conv_w_2': 4.626510e-02, 'conv_b_2': 6.644489e-02, 'dt_bias_2': 8.976959e-02, 'a_log_2': 3.464738e-01, 'd_skip_2': 3.364002e-01, 'ssd_norm_g_2': 5.861533e-02, 'w_mem_kv_2': 3.495709e-03, 'w_out_2': 7.030890e-02, 'norm_g_3': 2.628382e-02, 'w_in_3': 6.557055e-03, 'w_mem_kv_3': 3.419531e-03, 'w_out_3': 1.450196e-02}


def _to_microbatches(a, axis):
    t = _jnp.moveaxis(a, axis, 0)
    t = t.reshape((N_MICROBATCH, t.shape[0] // N_MICROBATCH) + t.shape[1:])
    return _jnp.moveaxis(t, 1, axis + 1)


def setup_inputs(seed: int = 0) -> dict:
    inp = _fwd_setup_inputs(seed)
    key = _jax.random.fold_in(_jax.random.key(seed), 7919)
    shape, _ = _output_shape()
    out = dict(inp)
    out["loss_target"] = _jax.random.normal(_jax.random.fold_in(key, 0), shape, _jnp.float32)
    for i, name in enumerate(TWIN_WEIGHTS):
        w = inp[name].astype(_jnp.float32)
        if MOMENT_SCALE is None:
            s = _jnp.sqrt(_jnp.mean(_jnp.square(w)) + 1e-30)
        else:
            s = MOMENT_SCALE[name]
        km, kv = _jax.random.split(_jax.random.fold_in(key, i + 1))
        out[name] = w
        out["m_" + name] = s * _jax.random.normal(km, w.shape, _jnp.float32)
        out["v_" + name] = (s * s) * _jax.random.uniform(kv, w.shape, _jnp.float32, 0.5, 1.5)
    if N_MICROBATCH > 1:
        for name, axis in PER_EXAMPLE_BATCH_AXIS.items():
            out[name] = _to_microbatches(out[name], axis)
    return {'x': out['x'], 'mem': out['mem'], 'mem_norm_g': out['mem_norm_g'], 'final_norm_g': out['final_norm_g'], 'norm_g_0': out['norm_g_0'], 'w_in_0': out['w_in_0'], 'conv_w_0': out['conv_w_0'], 'conv_b_0': out['conv_b_0'], 'dt_bias_0': out['dt_bias_0'], 'a_log_0': out['a_log_0'], 'd_skip_0': out['d_skip_0'], 'ssd_norm_g_0': out['ssd_norm_g_0'], 'w_mem_kv_0': out['w_mem_kv_0'], 'w_out_0': out['w_out_0'], 'norm_g_1': out['norm_g_1'], 'w_in_1': out['w_in_1'], 'w_mem_kv_1': out['w_mem_kv_1'], 'w_out_1': out['w_out_1'], 'norm_g_2': out['norm_g_2'], 'w_in_2': out['w_in_2'], 'conv_w_2': out['conv_w_2'], 'conv_b_2': out['conv_b_2'], 'dt_bias_2': out['dt_bias_2'], 'a_log_2': out['a_log_2'], 'd_skip_2': out['d_skip_2'], 'ssd_norm_g_2': out['ssd_norm_g_2'], 'w_mem_kv_2': out['w_mem_kv_2'], 'w_out_2': out['w_out_2'], 'norm_g_3': out['norm_g_3'], 'w_in_3': out['w_in_3'], 'w_mem_kv_3': out['w_mem_kv_3'], 'w_out_3': out['w_out_3'], 'loss_target': out['loss_target'], 'm_mem_norm_g': out['m_mem_norm_g'], 'm_final_norm_g': out['m_final_norm_g'], 'm_norm_g_0': out['m_norm_g_0'], 'm_w_in_0': out['m_w_in_0'], 'm_conv_w_0': out['m_conv_w_0'], 'm_conv_b_0': out['m_conv_b_0'], 'm_dt_bias_0': out['m_dt_bias_0'], 'm_a_log_0': out['m_a_log_0'], 'm_d_skip_0': out['m_d_skip_0'], 'm_ssd_norm_g_0': out['m_ssd_norm_g_0'], 'm_w_mem_kv_0': out['m_w_mem_kv_0'], 'm_w_out_0': out['m_w_out_0'], 'm_norm_g_1': out['m_norm_g_1'], 'm_w_in_1': out['m_w_in_1'], 'm_w_mem_kv_1': out['m_w_mem_kv_1'], 'm_w_out_1': out['m_w_out_1'], 'm_norm_g_2': out['m_norm_g_2'], 'm_w_in_2': out['m_w_in_2'], 'm_conv_w_2': out['m_conv_w_2'], 'm_conv_b_2': out['m_conv_b_2'], 'm_dt_bias_2': out['m_dt_bias_2'], 'm_a_log_2': out['m_a_log_2'], 'm_d_skip_2': out['m_d_skip_2'], 'm_ssd_norm_g_2': out['m_ssd_norm_g_2'], 'm_w_mem_kv_2': out['m_w_mem_kv_2'], 'm_w_out_2': out['m_w_out_2'], 'm_norm_g_3': out['m_norm_g_3'], 'm_w_in_3': out['m_w_in_3'], 'm_w_mem_kv_3': out['m_w_mem_kv_3'], 'm_w_out_3': out['m_w_out_3'], 'v_mem_norm_g': out['v_mem_norm_g'], 'v_final_norm_g': out['v_final_norm_g'], 'v_norm_g_0': out['v_norm_g_0'], 'v_w_in_0': out['v_w_in_0'], 'v_conv_w_0': out['v_conv_w_0'], 'v_conv_b_0': out['v_conv_b_0'], 'v_dt_bias_0': out['v_dt_bias_0'], 'v_a_log_0': out['v_a_log_0'], 'v_d_skip_0': out['v_d_skip_0'], 'v_ssd_norm_g_0': out['v_ssd_norm_g_0'], 'v_w_mem_kv_0': out['v_w_mem_kv_0'], 'v_w_out_0': out['v_w_out_0'], 'v_norm_g_1': out['v_norm_g_1'], 'v_w_in_1': out['v_w_in_1'], 'v_w_mem_kv_1': out['v_w_mem_kv_1'], 'v_w_out_1': out['v_w_out_1'], 'v_norm_g_2': out['v_norm_g_2'], 'v_w_in_2': out['v_w_in_2'], 'v_conv_w_2': out['v_conv_w_2'], 'v_conv_b_2': out['v_conv_b_2'], 'v_dt_bias_2': out['v_dt_bias_2'], 'v_a_log_2': out['v_a_log_2'], 'v_d_skip_2': out['v_d_skip_2'], 'v_ssd_norm_g_2': out['v_ssd_norm_g_2'], 'v_w_mem_kv_2': out['v_w_mem_kv_2'], 'v_w_out_2': out['v_w_out_2'], 'v_norm_g_3': out['v_norm_g_3'], 'v_w_in_3': out['v_w_in_3'], 'v_w_mem_kv_3': out['v_w_mem_kv_3'], 'v_w_out_3': out['v_w_out_3']}


def _loss(weights, diff, rest, loss_target):
    with _jax.named_scope("forward"):
        args = {**rest, TWIN_DIFF_INPUT: diff, **{k: w.astype(_WEIGHT_DTYPES[k]) for k, w in weights.items()}}
        y = _forward(args)
    with _jax.named_scope("loss_head"):
        err = _jnp.square(y.astype(_jnp.float32) - loss_target)
        return 0.5 * _jnp.sum(_jnp.mean(err, axis=-1)) if err.ndim else 0.5 * err


def _adamw(w, g, m, v):
    m = ADAM_B1 * m + (1.0 - ADAM_B1) * g
    v = ADAM_B2 * v + (1.0 - ADAM_B2) * _jnp.square(g)
    m_hat = m / (1.0 - ADAM_B1 ** ADAM_STEP)
    v_hat = v / (1.0 - ADAM_B2 ** ADAM_STEP)
    delta = -ADAM_LR * (m_hat / (_jnp.sqrt(v_hat) + ADAM_EPS) + ADAM_WD * w)
    return delta, m, v


def reference(x, mem, mem_norm_g, final_norm_g, norm_g_0, w_in_0, conv_w_0, conv_b_0, dt_bias_0, a_log_0, d_skip_0, ssd_norm_g_0, w_mem_kv_0, w_out_0, norm_g_1, w_in_1, w_mem_kv_1, w_out_1, norm_g_2, w_in_2, conv_w_2, conv_b_2, dt_bias_2, a_log_2, d_skip_2, ssd_norm_g_2, w_mem_kv_2, w_out_2, norm_g_3, w_in_3, w_mem_kv_3, w_out_3, loss_target, m_mem_norm_g, m_final_norm_g, m_norm_g_0, m_w_in_0, m_conv_w_0, m_conv_b_0, m_dt_bias_0, m_a_log_0, m_d_skip_0, m_ssd_norm_g_0, m_w_mem_kv_0, m_w_out_0, m_norm_g_1, m_w_in_1, m_w_mem_kv_1, m_w_out_1, m_norm_g_2, m_w_in_2, m_conv_w_2, m_conv_b_2, m_dt_bias_2, m_a_log_2, m_d_skip_2, m_ssd_norm_g_2, m_w_mem_kv_2, m_w_out_2, m_norm_g_3, m_w_in_3, m_w_mem_kv_3, m_w_out_3, v_mem_norm_g, v_final_norm_g, v_norm_g_0, v_w_in_0, v_conv_w_0, v_conv_b_0, v_dt_bias_0, v_a_log_0, v_d_skip_0, v_ssd_norm_g_0, v_w_mem_kv_0, v_w_out_0, v_norm_g_1, v_w_in_1, v_w_mem_kv_1, v_w_out_1, v_norm_g_2, v_w_in_2, v_conv_w_2, v_conv_b_2, v_dt_bias_2, v_a_log_2, v_d_skip_2, v_ssd_norm_g_2, v_w_mem_kv_2, v_w_out_2, v_norm_g_3, v_w_in_3, v_w_mem_kv_3, v_w_out_3):
    given = dict(x=x, mem=mem, mem_norm_g=mem_norm_g, final_norm_g=final_norm_g, norm_g_0=norm_g_0, w_in_0=w_in_0, conv_w_0=conv_w_0, conv_b_0=conv_b_0, dt_bias_0=dt_bias_0, a_log_0=a_log_0, d_skip_0=d_skip_0, ssd_norm_g_0=ssd_norm_g_0, w_mem_kv_0=w_mem_kv_0, w_out_0=w_out_0, norm_g_1=norm_g_1, w_in_1=w_in_1, w_mem_kv_1=w_mem_kv_1, w_out_1=w_out_1, norm_g_2=norm_g_2, w_in_2=w_in_2, conv_w_2=conv_w_2, conv_b_2=conv_b_2, dt_bias_2=dt_bias_2, a_log_2=a_log_2, d_skip_2=d_skip_2, ssd_norm_g_2=ssd_norm_g_2, w_mem_kv_2=w_mem_kv_2, w_out_2=w_out_2, norm_g_3=norm_g_3, w_in_3=w_in_3, w_mem_kv_3=w_mem_kv_3, w_out_3=w_out_3, loss_target=loss_target, m_mem_norm_g=m_mem_norm_g, m_final_norm_g=m_final_norm_g, m_norm_g_0=m_norm_g_0, m_w_in_0=m_w_in_0, m_conv_w_0=m_conv_w_0, m_conv_b_0=m_conv_b_0, m_dt_bias_0=m_dt_bias_0, m_a_log_0=m_a_log_0, m_d_skip_0=m_d_skip_0, m_ssd_norm_g_0=m_ssd_norm_g_0, m_w_mem_kv_0=m_w_mem_kv_0, m_w_out_0=m_w_out_0, m_norm_g_1=m_norm_g_1, m_w_in_1=m_w_in_1, m_w_mem_kv_1=m_w_mem_kv_1, m_w_out_1=m_w_out_1, m_norm_g_2=m_norm_g_2, m_w_in_2=m_w_in_2, m_conv_w_2=m_conv_w_2, m_conv_b_2=m_conv_b_2, m_dt_bias_2=m_dt_bias_2, m_a_log_2=m_a_log_2, m_d_skip_2=m_d_skip_2, m_ssd_norm_g_2=m_ssd_norm_g_2, m_w_mem_kv_2=m_w_mem_kv_2, m_w_out_2=m_w_out_2, m_norm_g_3=m_norm_g_3, m_w_in_3=m_w_in_3, m_w_mem_kv_3=m_w_mem_kv_3, m_w_out_3=m_w_out_3, v_mem_norm_g=v_mem_norm_g, v_final_norm_g=v_final_norm_g, v_norm_g_0=v_norm_g_0, v_w_in_0=v_w_in_0, v_conv_w_0=v_conv_w_0, v_conv_b_0=v_conv_b_0, v_dt_bias_0=v_dt_bias_0, v_a_log_0=v_a_log_0, v_d_skip_0=v_d_skip_0, v_ssd_norm_g_0=v_ssd_norm_g_0, v_w_mem_kv_0=v_w_mem_kv_0, v_w_out_0=v_w_out_0, v_norm_g_1=v_norm_g_1, v_w_in_1=v_w_in_1, v_w_mem_kv_1=v_w_mem_kv_1, v_w_out_1=v_w_out_1, v_norm_g_2=v_norm_g_2, v_w_in_2=v_w_in_2, v_conv_w_2=v_conv_w_2, v_conv_b_2=v_conv_b_2, v_dt_bias_2=v_dt_bias_2, v_a_log_2=v_a_log_2, v_d_skip_2=v_d_skip_2, v_ssd_norm_g_2=v_ssd_norm_g_2, v_w_mem_kv_2=v_w_mem_kv_2, v_w_out_2=v_w_out_2, v_norm_g_3=v_norm_g_3, v_w_in_3=v_w_in_3, v_w_mem_kv_3=v_w_mem_kv_3, v_w_out_3=v_w_out_3)
    weights = {n: given[n] for n in TWIN_WEIGHTS}
    shared = {n: given[n] for n in SHARED_INPUTS}
    per_example = {n: given[n] for n in ['x', 'mem']}
    grad_fn = _jax.value_and_grad(_loss, argnums=(0, 1))

    def one_microbatch(ex, loss_target):
        ex = dict(ex)
        diff = ex.pop(TWIN_DIFF_INPUT)
        return grad_fn(weights, diff, {**shared, **ex}, loss_target)

    if N_MICROBATCH == 1:
        loss, (grad_w, grad_x) = one_microbatch(per_example, given["loss_target"])
    else:
        def body(carry, xs):
            loss_sum, grad_sum = carry
            l_k, (gw_k, gx_k) = one_microbatch(xs[0], xs[1])
            with _jax.named_scope("update"):
                return (loss_sum + l_k, _jax.tree.map(_jnp.add, grad_sum, gw_k)), gx_k

        init = (_jnp.zeros((), _jnp.float32), _jax.tree.map(_jnp.zeros_like, weights))
        (loss, grad_w), grad_x = _jax.lax.scan(body, init, (per_example, given["loss_target"]))
    with _jax.named_scope("update"):
        delta_w, new_m, new_v = {}, {}, {}
        for n in TWIN_WEIGHTS:
            delta_w[n], new_m[n], new_v[n] = _adamw(weights[n], grad_w[n], given["m_" + n], given["v_" + n])
    return (loss, grad_x, *[grad_w[n] for n in TWIN_WEIGHTS], *[delta_w[n] for n in TWIN_WEIGHTS],
            *[new_m[n] for n in TWIN_WEIGHTS], *[new_v[n] for n in TWIN_WEIGHTS])
```

```python
import functools
import math

import jax
import jax.numpy as jnp
from jax import lax
from jax.experimental import pallas as pl
from jax.experimental.pallas import tpu as pltpu

F32 = jnp.float32
BF16 = jnp.bfloat16
EPS = 1e-6
N_DEV = 8
VMEM_LIMIT_BYTES = 56 * 1024 * 1024


def _pick(n, prefs):
    for p in prefs:
        if n % p == 0:
            return p
    return n


def _params(sem):
    return pltpu.CompilerParams(dimension_semantics=sem, vmem_limit_bytes=VMEM_LIMIT_BYTES)


def _matmul(a, b, *, ta=False, tb=False, out_dtype=F32, add=None, name="mm"):
    if ta:
        k_dim, m_dim = a.shape
    else:
        m_dim, k_dim = a.shape
    n_dim = b.shape[0] if tb else b.shape[1]
    tm = _pick(m_dim, (512, 256, 128))
    tn = _pick(n_dim, (1024, 512, 256, 128))
    tk = _pick(k_dim, (2048, 1024, 512, 256, 128))
    nk = k_dim // tk
    dims = (((0,) if ta else (1,), (1,) if tb else (0,)), ((), ()))

    def body(*refs):
        if add is None:
            a_ref, b_ref, o_ref, acc_ref = refs
            add_ref = None
        else:
            a_ref, b_ref, add_ref, o_ref, acc_ref = refs
        k = pl.program_id(2)
        part = lax.dot_general(a_ref[...].astype(BF16), b_ref[...].astype(BF16), dims,
                               preferred_element_type=F32)

        @pl.when(k == 0)
        def _():
            acc_ref[...] = part if add_ref is None else part + add_ref[...].astype(F32)

        @pl.when(k > 0)
        def _():
            acc_ref[...] += part

        @pl.when(k == nk - 1)
        def _():
            o_ref[...] = acc_ref[...].astype(o_ref.dtype)

    a_spec = pl.BlockSpec((tk, tm), lambda i, j, k: (k, i)) if ta else pl.BlockSpec((tm, tk), lambda i, j, k: (i, k))
    b_spec = pl.BlockSpec((tn, tk), lambda i, j, k: (j, k)) if tb else pl.BlockSpec((tk, tn), lambda i, j, k: (k, j))
    o_spec = pl.BlockSpec((tm, tn), lambda i, j, k: (i, j))
    in_specs = [a_spec, b_spec] + ([o_spec] if add is not None else [])
    args = (a, b) + ((add,) if add is not None else ())
    return pl.pallas_call(
        body, name=name, grid=(m_dim // tm, n_dim // tn, nk),
        in_specs=in_specs, out_specs=o_spec,
        out_shape=jax.ShapeDtypeStruct((m_dim, n_dim), out_dtype),
        scratch_shapes=[pltpu.VMEM((tm, tn), F32)],
        compiler_params=_params(("parallel", "parallel", "arbitrary")),
    )(*args)


def _iota(shape, dim):
    return lax.broadcasted_iota(jnp.int32, shape, dim)


def _col(x, j):
    return jnp.sum(jnp.where(_iota(x.shape, 1) == j, x, 0.0), axis=1, keepdims=True)


def _silu(x):
    return x * jax.nn.sigmoid(x)


def _dsilu(x):
    s = jax.nn.sigmoid(x)
    return s * (1.0 + x * (1.0 - s))


def _rmsnorm_fwd(x, g, name):
    t, d = x.shape
    tm = _pick(t, (512, 256, 128))

    def body(x_ref, g_ref, h_ref):
        xv = x_ref[...]
        r = lax.rsqrt(jnp.mean(xv * xv, axis=-1, keepdims=True) + EPS)
        h_ref[...] = (xv * r * g_ref[...]).astype(h_ref.dtype)

    return pl.pallas_call(
        body, name=name, grid=(t // tm,),
        in_specs=[pl.BlockSpec((tm, d), lambda i: (i, 0)), pl.BlockSpec((1, d), lambda i: (0, 0))],
        out_specs=pl.BlockSpec((tm, d), lambda i: (i, 0)),
        out_shape=jax.ShapeDtypeStruct((t, d), BF16),
        compiler_params=_params(("parallel",)),
    )(x, g.reshape(1, d))


def _rmsnorm_bwd(x, g, dh, dres, name):
    t, d = x.shape
    tm = _pick(t, (512, 256, 128))

    def body(*refs):
        if dres is None:
            x_ref, g_ref, dh_ref, dx_ref, dg_ref = refs
        else:
            x_ref, g_ref, dh_ref, dres_ref, dx_ref, dg_ref = refs
        xv = x_ref[...]
        r = lax.rsqrt(jnp.mean(xv * xv, axis=-1, keepdims=True) + EPS)
        xhat = xv * r
        dhv = dh_ref[...].astype(F32)
        dxh = dhv * g_ref[...]
        dx = r * (dxh - xhat * jnp.mean(dxh * xhat, axis=-1, keepdims=True))
        if dres is not None:
            dx = dx + dres_ref[...]
        dx_ref[...] = dx
        part = jnp.sum(dhv * xhat, axis=0, keepdims=True)

        @pl.when(pl.program_id(0) == 0)
        def _():
            dg_ref[...] = part

        @pl.when(pl.program_id(0) > 0)
        def _():
            dg_ref[...] += part

    row = pl.BlockSpec((tm, d), lambda i: (i, 0))
    vec = pl.BlockSpec((1, d), lambda i: (0, 0))
    in_specs = [row, vec, row] + ([row] if dres is not None else [])
    args = (x, g.reshape(1, d), dh) + ((dres,) if dres is not None else ())
    return pl.pallas_call(
        body, name=name, grid=(t // tm,), in_specs=in_specs, out_specs=[row, vec],
        out_shape=[jax.ShapeDtypeStruct((t, d), F32), jax.ShapeDtypeStruct((1, d), F32)],
        compiler_params=_params(("arbitrary",)),
    )(*args)


def _final_loss(x, g, target, name="final_loss"):
    t, d = x.shape
    tm = _pick(t, (512, 256, 128))

    def body(x_ref, g_ref, t_ref, loss_ref, dx_ref, dg_ref):
        xv = x_ref[...]
        gv = g_ref[...]
        r = lax.rsqrt(jnp.mean(xv * xv, axis=-1, keepdims=True) + EPS)
        xhat = xv * r
        e = xhat * gv - t_ref[...]
        lpart = jnp.zeros((1, 128), F32) + (0.5 / d) * jnp.sum(e * e)
        dy = e * (1.0 / d)
        dxh = dy * gv
        dx_ref[...] = r * (dxh - xhat * jnp.mean(dxh * xhat, axis=-1, keepdims=True))
        gpart = jnp.sum(dy * xhat, axis=0, keepdims=True)

        @pl.when(pl.program_id(0) == 0)
        def _():
            dg_ref[...] = gpart
            loss_ref[...] = lpart

        @pl.when(pl.program_id(0) > 0)
        def _():
            dg_ref[...] += gpart
            loss_ref[...] += lpart

    row = pl.BlockSpec((tm, d), lambda i: (i, 0))
    vec = pl.BlockSpec((1, d), lambda i: (0, 0))
    return pl.pallas_call(
        body, name=name, grid=(t // tm,), in_specs=[row, vec, row],
        out_specs=[pl.BlockSpec((1, 128), lambda i: (0, 0)), row, vec],
        out_shape=[jax.ShapeDtypeStruct((1, 128), F32), jax.ShapeDtypeStruct((t, d), F32),
                   jax.ShapeDtypeStruct((1, d), F32)],
        compiler_params=_params(("arbitrary",)),
    )(x, g.reshape(1, d), target)


CONV_K = 4
HALO = 8


def _shift_down(cur, prev8, s):
    rolled = pltpu.roll(cur, s, 0)
    fix = pltpu.roll(prev8, s, 0)
    head = jnp.where(_iota((HALO, cur.shape[1]), 0) < s, fix, rolled[:HALO])
    return jnp.concatenate([head, rolled[HALO:]], axis=0)


def _shift_up(cur, next8, s):
    n = cur.shape[0]
    rolled = pltpu.roll(cur, n - s, 0)
    fix = pltpu.roll(next8, HALO - s, 0)
    tail = jnp.where(_iota((HALO, cur.shape[1]), 0) >= HALO - s, fix, rolled[n - HALO:])
    return jnp.concatenate([rolled[:n - HALO], tail], axis=0)


def _conv_pre(u_ref, up_ref, w_ref, b_ref, first):
    cur = u_ref[...]
    prev8 = jnp.where(first, 0.0, up_ref[...])
    w = w_ref[...]
    shifted = [cur] + [_shift_down(cur, prev8, s) for s in (1, 2, 3)]
    pre = b_ref[...] + sum(w[CONV_K - 1 - s:CONV_K - s, :] * shifted[s] for s in range(CONV_K))
    return pre, shifted


def _conv_specs(tm, tc):
    nb = tm // HALO
    cur = pl.BlockSpec((tm, tc), lambda j, i: (i, j))
    prev = pl.BlockSpec((HALO, tc), lambda j, i: (jnp.maximum(i * nb - 1, 0), j))
    wspec = pl.BlockSpec((CONV_K, tc), lambda j, i: (0, j))
    bspec = pl.BlockSpec((1, tc), lambda j, i: (0, j))
    return cur, prev, wspec, bspec


def _conv_fwd(u, w, b, name):
    t, c = u.shape
    tm, tc = _pick(t, (512, 256, 128)), _pick(c, (1024, 512, 256, 128))
    cur, prev, wspec, bspec = _conv_specs(tm, tc)

    def body(u_ref, up_ref, w_ref, b_ref, o_ref):
        pre, _ = _conv_pre(u_ref, up_ref, w_ref, b_ref, pl.program_id(1) == 0)
        o_ref[...] = _silu(pre)

    return pl.pallas_call(
        body, name=name, grid=(c // tc, t // tm), in_specs=[cur, prev, wspec, bspec], out_specs=cur,
        out_shape=jax.ShapeDtypeStruct((t, c), F32),
        compiler_params=_params(("parallel", "parallel")),
    )(u, u, w, b.reshape(1, c))


def _conv_bwd_pre(u, w, b, dy, name):
    t, c = u.shape
    tm, tc = _pick(t, (512, 256, 128)), _pick(c, (1024, 512, 256, 128))
    cur, prev, wspec, bspec = _conv_specs(tm, tc)

    def body(u_ref, up_ref, w_ref, b_ref, dy_ref, dpre_ref, dw_ref, db_ref):
        i = pl.program_id(1)
        pre, shifted = _conv_pre(u_ref, up_ref, w_ref, b_ref, i == 0)
        dpre = dy_ref[...] * _dsilu(pre)
        dpre_ref[...] = dpre
        dw = jnp.concatenate([jnp.sum(dpre * shifted[CONV_K - 1 - k], axis=0, keepdims=True) for k in range(CONV_K)], axis=0)
        db = jnp.sum(dpre, axis=0, keepdims=True)

        @pl.when(i == 0)
        def _():
            dw_ref[...] = dw
            db_ref[...] = db

        @pl.when(i > 0)
        def _():
            dw_ref[...] += dw
            db_ref[...] += db

    return pl.pallas_call(
        body, name=name, grid=(c // tc, t // tm), in_specs=[cur, prev, wspec, bspec, cur],
        out_specs=[cur, wspec, bspec],
        out_shape=[jax.ShapeDtypeStruct((t, c), F32), jax.ShapeDtypeStruct((CONV_K, c), F32),
                   jax.ShapeDtypeStruct((1, c), F32)],
        compiler_params=_params(("parallel", "arbitrary")),
    )(u, u, w, b.reshape(1, c), dy)


def _conv_bwd_in(dpre, w, name):
    t, c = dpre.shape
    tm, tc = _pick(t, (512, 256, 128)), _pick(c, (1024, 512, 256, 128))
    nb = tm // HALO
    last = t // tm - 1
    cur = pl.BlockSpec((tm, tc), lambda j, i: (i, j))
    nxt = pl.BlockSpec((HALO, tc), lambda j, i: (jnp.minimum((i + 1) * nb, t // HALO - 1), j))
    wspec = pl.BlockSpec((CONV_K, tc), lambda j, i: (0, j))

    def body(d_ref, dn_ref, w_ref, o_ref):
        cur_v = d_ref[...]
        next8 = jnp.where(pl.program_id(1) == last, 0.0, dn_ref[...])
        wv = w_ref[...]
        acc = wv[CONV_K - 1:CONV_K, :] * cur_v
        for s in (1, 2, 3):
            acc = acc + wv[CONV_K - 1 - s:CONV_K - s, :] * _shift_up(cur_v, next8, s)
        o_ref[...] = acc.astype(o_ref.dtype)

    return pl.pallas_call(
        body, name=name, grid=(c // tc, t // tm), in_specs=[cur, nxt, wspec], out_specs=cur,
        out_shape=jax.ShapeDtypeStruct((t, c), BF16),
        compiler_params=_params(("parallel", "parallel")),
    )(dpre, dpre, w)


MEM_HEADS = 4
NT_DIMS = (((1,), (1,)), ((), ()))
TN_DIMS = (((0,), (0,)), ((), ()))


def _dot(a, b, dims=None):
    if dims is None:
        return jnp.dot(a, b, preferred_element_type=F32)
    return lax.dot_general(a, b, dims, preferred_element_type=F32)


def _memattn_probs(q, mk, scale):
    s = _dot(q, mk, NT_DIMS) * scale
    s = s - jnp.max(s, axis=-1, keepdims=True)
    p = jnp.exp(s)
    return p / jnp.sum(p, axis=-1, keepdims=True)


def _memattn_fwd(q, mkv, name):
    t, wd = q.shape
    m = mkv.shape[0]
    hd = wd // MEM_HEADS
    scale = hd ** -0.5
    tm = _pick(t, (512, 256, 128))

    def body(q_ref, mkv_ref, o_ref):
        for h in range(MEM_HEADS):
            cols = slice(h * hd, (h + 1) * hd)
            p = _memattn_probs(q_ref[:, cols], mkv_ref[:, cols], scale)
            o_ref[:, cols] = _dot(p.astype(BF16), mkv_ref[:, wd + h * hd:wd + (h + 1) * hd])

    return pl.pallas_call(
        body, name=name, grid=(t // tm,),
        in_specs=[pl.BlockSpec((tm, wd), lambda i: (i, 0)), pl.BlockSpec((m, 2 * wd), lambda i: (0, 0))],
        out_specs=pl.BlockSpec((tm, wd), lambda i: (i, 0)),
        out_shape=jax.ShapeDtypeStruct((t, wd), F32),
        compiler_params=_params(("parallel",)),
    )(q, mkv)


def _memattn_bwd(q, mkv, dy, name):
    t, wd = q.shape
    m = mkv.shape[0]
    hd = wd // MEM_HEADS
    scale = hd ** -0.5
    tm = _pick(t, (512, 256, 128))

    def body(q_ref, mkv_ref, dy_ref, dq_ref, dmkv_ref):
        i = pl.program_id(0)

        @pl.when(i == 0)
        def _():
            dmkv_ref[...] = jnp.zeros_like(dmkv_ref)

        for h in range(MEM_HEADS):
            cols = slice(h * hd, (h + 1) * hd)
            vcols = slice(wd + h * hd, wd + (h + 1) * hd)
            qh = q_ref[:, cols]
            p = _memattn_probs(qh, mkv_ref[:, cols], scale)
            dyh = dy_ref[:, cols].astype(BF16)
            dp = _dot(dyh, mkv_ref[:, vcols], NT_DIMS)
            ds = (p * (dp - jnp.sum(dp * p, axis=-1, keepdims=True)) * scale).astype(BF16)
            dq_ref[:, cols] = _dot(ds, mkv_ref[:, cols]).astype(dq_ref.dtype)
            dmkv_ref[:, cols] += _dot(ds, qh, TN_DIMS)
            dmkv_ref[:, vcols] += _dot(p.astype(BF16), dyh, TN_DIMS)

    return pl.pallas_call(
        body, name=name, grid=(t // tm,),
        in_specs=[pl.BlockSpec((tm, wd), lambda i: (i, 0)), pl.BlockSpec((m, 2 * wd), lambda i: (0, 0)),
                  pl.BlockSpec((tm, wd), lambda i: (i, 0))],
        out_specs=[pl.BlockSpec((tm, wd), lambda i: (i, 0)), pl.BlockSpec((m, 2 * wd), lambda i: (0, 0))],
        out_shape=[jax.ShapeDtypeStruct((t, wd), BF16), jax.ShapeDtypeStruct((m, 2 * wd), F32)],
        compiler_params=_params(("arbitrary",)),
    )(q, mkv, dy)


NORM_GROUPS = 8


def _gate_fwd(y_tok, y_mem, z, norm_g, name):
    t, tok = y_tok.shape
    mem = y_mem.shape[1]
    mix = tok + mem
    gw = tok // NORM_GROUPS
    tm = _pick(t, (256, 128))

    def body(*refs):
        if norm_g is None:
            yt_ref, ym_ref, z_ref, o_ref = refs
        else:
            yt_ref, ym_ref, z_ref, g_ref, o_ref = refs
        u = yt_ref[...] * _silu(z_ref[:, :tok])
        if norm_g is None:
            o_ref[:, :tok] = u.astype(o_ref.dtype)
        else:
            for k in range(NORM_GROUPS):
                uk = u[:, k * gw:(k + 1) * gw]
                r = lax.rsqrt(jnp.mean(uk * uk, axis=-1, keepdims=True) + EPS)
                o_ref[:, k * gw:(k + 1) * gw] = (uk * r * g_ref[:, k * gw:(k + 1) * gw]).astype(o_ref.dtype)
        o_ref[:, tok:] = (ym_ref[...] * _silu(z_ref[:, tok:])).astype(o_ref.dtype)

    in_specs = [pl.BlockSpec((tm, tok), lambda i: (i, 0)), pl.BlockSpec((tm, mem), lambda i: (i, 0)),
                pl.BlockSpec((tm, mix), lambda i: (i, 0))]
    args = [y_tok, y_mem, z]
    if norm_g is not None:
        in_specs.append(pl.BlockSpec((1, tok), lambda i: (0, 0)))
        args.append(norm_g.reshape(1, tok))
    return pl.pallas_call(
        body, name=name, grid=(t // tm,), in_specs=in_specs,
        out_specs=pl.BlockSpec((tm, mix), lambda i: (i, 0)),
        out_shape=jax.ShapeDtypeStruct((t, mix), BF16),
        compiler_params=_params(("parallel",)),
    )(*args)


def _gate_bwd(y_tok, y_mem, z, norm_g, dgated, name):
    t, tok = y_tok.shape
    mem = y_mem.shape[1]
    mix = tok + mem
    gw = tok // NORM_GROUPS
    tm = _pick(t, (256, 128))

    def body(*refs):
        if norm_g is None:
            yt_ref, ym_ref, z_ref, dg_ref, dyt_ref, dym_ref, dz_ref, dn_ref = refs
        else:
            yt_ref, ym_ref, z_ref, dg_ref, g_ref, dyt_ref, dym_ref, dz_ref, dn_ref = refs
        i = pl.program_id(0)
        zt = z_ref[:, :tok]
        yt = yt_ref[...]
        sz = _silu(zt)
        dout = dg_ref[:, :tok].astype(F32)
        if norm_g is None:
            du = dout
            dn = jnp.zeros((1, tok), F32)
        else:
            u = yt * sz
            dus, dns = [], []
            for k in range(NORM_GROUPS):
                uk = u[:, k * gw:(k + 1) * gw]
                r = lax.rsqrt(jnp.mean(uk * uk, axis=-1, keepdims=True) + EPS)
                nk = uk * r
                dk = dout[:, k * gw:(k + 1) * gw]
                dns.append(jnp.sum(dk * nk, axis=0, keepdims=True))
                dnk = dk * g_ref[:, k * gw:(k + 1) * gw]
                dus.append(r * (dnk - nk * jnp.mean(dnk * nk, axis=-1, keepdims=True)))
            du = jnp.concatenate(dus, axis=1)
            dn = jnp.concatenate(dns, axis=1)
        dyt_ref[...] = du * sz
        dz_ref[:, :tok] = (du * yt * _dsilu(zt)).astype(dz_ref.dtype)
        zm = z_ref[:, tok:]
        dm = dg_ref[:, tok:].astype(F32)
        dym_ref[...] = dm * _silu(zm)
        dz_ref[:, tok:] = (dm * ym_ref[...] * _dsilu(zm)).astype(dz_ref.dtype)

        @pl.when(i == 0)
        def _():
            dn_ref[...] = dn

        @pl.when(i > 0)
        def _():
            dn_ref[...] += dn

    tok_spec = pl.BlockSpec((tm, tok), lambda i: (i, 0))
    mem_spec = pl.BlockSpec((tm, mem), lambda i: (i, 0))
    mix_spec = pl.BlockSpec((tm, mix), lambda i: (i, 0))
    vec = pl.BlockSpec((1, tok), lambda i: (0, 0))
    in_specs = [tok_spec, mem_spec, mix_spec, mix_spec]
    args = [y_tok, y_mem, z, dgated]
    if norm_g is not None:
        in_specs.append(vec)
        args.append(norm_g.reshape(1, tok))
    return pl.pallas_call(
        body, name=name, grid=(t // tm,), in_specs=in_specs,
        out_specs=[tok_spec, mem_spec, mix_spec, vec],
        out_shape=[jax.ShapeDtypeStruct((t, tok), F32), jax.ShapeDtypeStruct((t, mem), F32),
                   jax.ShapeDtypeStruct((t, mix), BF16), jax.ShapeDtypeStruct((1, tok), F32)],
        compiler_params=_params(("arbitrary",)),
    )(*args)


SSD_Q = 128
SSD_N = 128
SSD_P = 64
SSD_G = 8
SSD_HPG = 6
SSD_H = SSD_G * SSD_HPG
SSD_TOK = SSD_H * SSD_P
SSD_XBC = SSD_TOK + 2 * SSD_G * SSD_N
LANES = 128
HIGHEST = lax.Precision.HIGHEST


def _softplus(x):
    return jnp.maximum(x, 0.0) + jnp.log(1.0 + jnp.exp(-jnp.abs(x)))


def _ssd_common(dtr_ref, bias_ref, alog_ref):
    sq = (SSD_Q, LANES)
    pre = dtr_ref[...] + bias_ref[...]
    dt = _softplus(pre)
    a = -jnp.exp(alog_ref[...])
    tril = (_iota(sq, 0) >= _iota(sq, 1)).astype(F32)
    acs = jnp.dot(tril, dt * a, precision=HIGHEST, preferred_element_type=F32)
    return pre, dt, a, tril, acs, acs.T


def _pair_terms(dt, acs, acs_t, h0):
    hi = _iota((SSD_Q, LANES), 1) >= SSD_P
    heads = []
    for j in range(2):
        h = h0 + j
        a_col = _col(acs, h)
        a_row = acs_t[h:h + 1, :]
        a_last = _col(acs[SSD_Q - 1:SSD_Q, :], h)
        heads.append((h, a_col, a_row, a_last, hi if j else jnp.logical_not(hi)))
    dtl = jnp.where(hi, _col(dt, h0 + 1), _col(dt, h0))
    scale = jnp.where(hi, jnp.exp(heads[1][1]), jnp.exp(heads[0][1]))
    dec_last = jnp.where(hi[:1], jnp.exp(heads[1][3]), jnp.exp(heads[0][3]))
    return heads, dtl, scale, dec_last


def _decay(a_col, a_row):
    causal = _iota((SSD_Q, SSD_Q), 0) >= _iota((SSD_Q, SSD_Q), 1)
    return jnp.where(causal, jnp.exp(jnp.minimum(a_col - a_row, 0.0)), 0.0)


def _ssd_fwd(xbc, dt_raw, dt_bias, a_log, dskip_lane, name):
    t = xbc.shape[0]
    nc = t // SSD_Q

    def body(xbc_ref, dtr_ref, bias_ref, alog_ref, dsk_ref, y_ref, hs_ref, h_ref):
        @pl.when(pl.program_id(0) == 0)
        def _():
            h_ref[...] = jnp.zeros_like(h_ref)

        _, dt, _, _, acs, acs_t = _ssd_common(dtr_ref, bias_ref, alog_ref)
        for g in range(SSD_G):
            bg_f = xbc_ref[:, SSD_TOK + g * SSD_N:SSD_TOK + (g + 1) * SSD_N]
            bg = bg_f.astype(BF16)
            cg = xbc_ref[:, SSD_TOK + SSD_G * SSD_N + g * SSD_N:SSD_TOK + SSD_G * SSD_N + (g + 1) * SSD_N].astype(BF16)
            cb = _dot(cg, bg, NT_DIMS)
            for pr in range(SSD_HPG // 2):
                h0 = g * SSD_HPG + 2 * pr
                lanes = slice(h0 * SSD_P, (h0 + 2) * SSD_P)
                heads, dtl, scale, dec_last = _pair_terms(dt, acs, acs_t, h0)
                xs = xbc_ref[:, lanes]
                xdt = xs * dtl
                hp = h_ref[:, lanes]
                hs_ref[:, lanes] = hp
                y = _dot(cg, hp.astype(BF16)) * scale + dsk_ref[:, lanes] * xs
                snew = hp * dec_last
                for _, a_col, a_row, a_last, mask in heads:
                    xm = jnp.where(mask, xdt, 0.0).astype(BF16)
                    y = y + _dot((cb * _decay(a_col, a_row)).astype(BF16), xm)
                    bw = (bg_f * jnp.exp(a_last - a_col)).astype(BF16)
                    snew = snew + _dot(bw, xm, TN_DIMS)
                y_ref[:, lanes] = y
                h_ref[:, lanes] = snew

    row = lambda w: pl.BlockSpec((SSD_Q, w), lambda c: (c, 0))
    vec = lambda w: pl.BlockSpec((1, w), lambda c: (0, 0))
    return pl.pallas_call(
        body, name=name, grid=(nc,),
        in_specs=[row(SSD_XBC), row(LANES), vec(LANES), vec(LANES), vec(SSD_TOK)],
        out_specs=[row(SSD_TOK), row(SSD_TOK)],
        out_shape=[jax.ShapeDtypeStruct((t, SSD_TOK), F32), jax.ShapeDtypeStruct((nc * SSD_N, SSD_TOK), F32)],
        scratch_shapes=[pltpu.VMEM((SSD_N, SSD_TOK), F32)],
        compiler_params=_params(("arbitrary",)),
    )(xbc, dt_raw, dt_bias, a_log, dskip_lane)


def _ssd_bwd(xbc, dt_raw, dt_bias, a_log, dskip_lane, hs, dy, name):
    t = xbc.shape[0]
    nc = t // SSD_Q
    sq = (SSD_Q, LANES)

    def body(xbc_ref, dtr_ref, bias_ref, alog_ref, dsk_ref, hs_ref, dy_ref,
             dxbc_ref, ddtr_ref, dbias_ref, dalog_ref, ddsk_ref, dh_ref):
        first = pl.program_id(0) == 0

        @pl.when(first)
        def _():
            dh_ref[...] = jnp.zeros_like(dh_ref)
            dbias_ref[...] = jnp.zeros_like(dbias_ref)
            dalog_ref[...] = jnp.zeros_like(dalog_ref)
            ddsk_ref[...] = jnp.zeros_like(ddsk_ref)

        pre, dt, a, tril, acs, acs_t = _ssd_common(dtr_ref, bias_ref, alog_ref)
        lane = _iota(sq, 1)
        sub = _iota(sq, 0)
        causal = sub >= lane
        d_acs = jnp.zeros(sq, F32)
        d_acs_row = jnp.zeros(sq, F32)
        d_last = jnp.zeros((1, LANES), F32)
        ddt = jnp.zeros(sq, F32)
        for g in range(SSD_G):
            bcols = slice(SSD_TOK + g * SSD_N, SSD_TOK + (g + 1) * SSD_N)
            ccols = slice(SSD_TOK + SSD_G * SSD_N + g * SSD_N, SSD_TOK + SSD_G * SSD_N + (g + 1) * SSD_N)
            bg_f = xbc_ref[:, bcols]
            bg = bg_f.astype(BF16)
            cg = xbc_ref[:, ccols].astype(BF16)
            cb = _dot(cg, bg, NT_DIMS)
            dcb = jnp.zeros(sq, F32)
            dbg = jnp.zeros(sq, F32)
            dcg = jnp.zeros(sq, F32)
            for pr in range(SSD_HPG // 2):
                h0 = g * SSD_HPG + 2 * pr
                lanes = slice(h0 * SSD_P, (h0 + 2) * SSD_P)
                heads, dtl, scale, dec_last = _pair_terms(dt, acs, acs_t, h0)
                xs = xbc_ref[:, lanes]
                xdt = xs * dtl
                dyv = dy_ref[:, lanes]
                hp = hs_ref[:, lanes]
                dhn = dh_ref[:, lanes]
                hp_b = hp.astype(BF16)
                dys = (dyv * scale).astype(BF16)
                yoff_dy = dyv * _dot(cg, hp_b) * scale
                dcg = dcg + _dot(dys, hp_b, NT_DIMS)
                dhc = _dot(cg, dys, TN_DIMS)
                hh = dhn * hp
                dxdt = jnp.zeros(sq, F32)
                for h, a_col, a_row, a_last, mask in heads:
                    dec = _decay(a_col, a_row)
                    m = cb * dec
                    dym = jnp.where(mask, dyv, 0.0).astype(BF16)
                    xm = jnp.where(mask, xdt, 0.0).astype(BF16)
                    dhm = jnp.where(mask, dhn, 0.0).astype(BF16)
                    w = jnp.exp(a_last - a_col)
                    dxdt = dxdt + _dot(m.astype(BF16), dym, TN_DIMS) + _dot((bg_f * w).astype(BF16), dhm)
                    dm = jnp.where(causal, _dot(dym, xm, NT_DIMS), 0.0)
                    dcb = dcb + dm * dec
                    e = dm * m
                    gj = _dot(xm, dhm, NT_DIMS)
                    dbg = dbg + w * gj
                    wdw = w * jnp.sum(bg_f * gj, axis=1, keepdims=True)
                    col = (jnp.sum(e, axis=1, keepdims=True)
                           + jnp.sum(jnp.where(mask, yoff_dy, 0.0), axis=1, keepdims=True) - wdw)
                    d_acs = d_acs + jnp.where(lane == h, col, 0.0)
                    d_acs_row = d_acs_row + jnp.where(sub == h, jnp.sum(e, axis=0, keepdims=True), 0.0)
                    last = jnp.sum(wdw) + jnp.exp(a_last) * jnp.sum(jnp.where(mask, hh, 0.0))
                    d_last = d_last + jnp.where(lane[:1] == h, last, 0.0)
                dxbc_ref[:, lanes] = dxdt * dtl + dsk_ref[:, lanes] * dyv
                tt = dxdt * xs
                for h, _, _, _, mask in heads:
                    ddt = ddt + jnp.where(lane == h, jnp.sum(jnp.where(mask, tt, 0.0), axis=1, keepdims=True), 0.0)
                ddsk_ref[:, lanes] += jnp.sum(dyv * xs, axis=0, keepdims=True)
                dh_ref[:, lanes] = dhn * dec_last + dhc
            dcb_b = dcb.astype(BF16)
            dxbc_ref[:, bcols] = dbg + _dot(dcb_b, cg, TN_DIMS)
            dxbc_ref[:, ccols] = dcg + _dot(dcb_b, bg)
        d_tot = d_acs - d_acs_row.T + jnp.where(sub == SSD_Q - 1, d_last, 0.0)
        ddta = lax.dot_general(tril, d_tot, TN_DIMS, precision=HIGHEST, preferred_element_type=F32)
        ddt = ddt + ddta * a
        dalog_ref[...] += jnp.sum(ddta * dt, axis=0, keepdims=True) * a
        ddtr = ddt * jax.nn.sigmoid(pre)
        ddtr_ref[...] = ddtr
        dbias_ref[...] += jnp.sum(ddtr, axis=0, keepdims=True)

    rev = lambda w: pl.BlockSpec((SSD_Q, w), lambda i: (nc - 1 - i, 0))
    vec = lambda w: pl.BlockSpec((1, w), lambda i: (0, 0))
    return pl.pallas_call(
        body, name=name, grid=(nc,),
        in_specs=[rev(SSD_XBC), rev(LANES), vec(LANES), vec(LANES), vec(SSD_TOK), rev(SSD_TOK), rev(SSD_TOK)],
        out_specs=[rev(SSD_XBC), rev(LANES), vec(LANES), vec(LANES), vec(SSD_TOK)],
        out_shape=[jax.ShapeDtypeStruct((t, SSD_XBC), F32), jax.ShapeDtypeStruct((t, LANES), F32),
                   jax.ShapeDtypeStruct((1, LANES), F32), jax.ShapeDtypeStruct((1, LANES), F32),
                   jax.ShapeDtypeStruct((1, SSD_TOK), F32)],
        scratch_shapes=[pltpu.VMEM((SSD_N, SSD_TOK), F32)],
        compiler_params=_params(("arbitrary",)),
    )(xbc, dt_raw, dt_bias, a_log, dskip_lane, hs, dy)


ATT_E = 128
ATT_H = 24
ATT_W = 128
ATT_TOK = ATT_H * ATT_E
DILATED_GROUPS = ((128, 1), (512, 4), (2048, 16))
N_DIL = len(DILATED_GROUPS)
ALIBI_MAX_EXP = 8.0
MASKED = -1e30


def _alibi_slopes(group):
    n = N_DIL * ATT_H
    return [2.0 ** (-ALIBI_MAX_EXP * (group * ATT_H + h + 1) / n) for h in range(ATT_H)]


def _att_scores(qh, kk, rel, valid, slope_d):
    s = _dot(qh, kk, NT_DIMS) * (ATT_E ** -0.5) - slope_d * rel
    return jnp.where(valid, s, MASKED)


def _att_rel(j):
    shp = (ATT_W, 2 * ATT_W)
    kpos = _iota(shp, 1)
    rel = _iota(shp, 0) + ATT_W - kpos
    valid = (rel >= 0) & (rel <= ATT_W) & ((kpos >= ATT_W) | (j > 0))
    return rel.astype(F32), valid


def _dil_fwd(q, k, v, group, name):
    t = q.shape[0]
    dil = DILATED_GROUPS[group][1]
    slopes = _alibi_slopes(group)
    ns = t // dil
    nb = ns // ATT_W
    view = lambda arr: arr.reshape(ns, dil * arr.shape[1])

    def body(q_ref, kp_ref, kc_ref, vp_ref, vc_ref, o_ref, lse_ref):
        rel, valid = _att_rel(pl.program_id(1))
        lane = _iota((ATT_W, LANES), 1)
        lse_all = jnp.zeros((ATT_W, LANES), F32)
        for h in range(ATT_H):
            cols = slice(h * ATT_E, (h + 1) * ATT_E)
            kk = jnp.concatenate([kp_ref[:, cols], kc_ref[:, cols]], axis=0)
            vv = jnp.concatenate([vp_ref[:, cols], vc_ref[:, cols]], axis=0)
            s = _att_scores(q_ref[:, cols], kk, rel, valid, slopes[h] * dil)
            m = jnp.max(s, axis=-1, keepdims=True)
            p = jnp.exp(s - m)
            den = jnp.sum(p, axis=-1, keepdims=True)
            o_ref[:, cols] = _dot(p.astype(BF16), vv) / den
            lse_all = jnp.where(lane == h, m + jnp.log(den), lse_all)
        lse_ref[...] = lse_all

    cur = pl.BlockSpec((ATT_W, ATT_TOK), lambda r, j: (j, r))
    prev = pl.BlockSpec((ATT_W, ATT_TOK), lambda r, j: (jnp.maximum(j - 1, 0), r))
    small = pl.BlockSpec((ATT_W, LANES), lambda r, j: (j, r))
    o, lse = pl.pallas_call(
        body, name=name, grid=(dil, nb),
        in_specs=[cur, prev, cur, prev, cur], out_specs=[cur, small],
        out_shape=[jax.ShapeDtypeStruct((ns, dil * ATT_TOK), F32), jax.ShapeDtypeStruct((ns, dil * LANES), F32)],
        compiler_params=_params(("parallel", "parallel")),
    )(view(q), view(k), view(k), view(v), view(v))
    return o.reshape(t, ATT_TOK), lse.reshape(t, LANES)


def _dil_bwd(q, k, v, dy, wgt, cterm, lse, group, name):
    t = q.shape[0]
    dil = DILATED_GROUPS[group][1]
    slopes = _alibi_slopes(group)
    ns = t // dil
    nb = ns // ATT_W
    view = lambda arr: arr.reshape(ns, dil * arr.shape[1])

    def body(q_ref, kp_ref, kc_ref, vp_ref, vc_ref, dy_ref, w_ref, c_ref, lse_ref,
             dq_ref, dk_ref, dv_ref, ck_ref, cv_ref):
        j = pl.program_id(1)

        @pl.when(j == 0)
        def _():
            ck_ref[...] = jnp.zeros_like(ck_ref)
            cv_ref[...] = jnp.zeros_like(cv_ref)

        @pl.when(j < nb)
        def _():
            rel, valid = _att_rel(j)
            wv, cv_, lv = w_ref[...], c_ref[...], lse_ref[...]
            for h in range(ATT_H):
                cols = slice(h * ATT_E, (h + 1) * ATT_E)
                qh = q_ref[:, cols]
                kk = jnp.concatenate([kp_ref[:, cols], kc_ref[:, cols]], axis=0)
                vv = jnp.concatenate([vp_ref[:, cols], vc_ref[:, cols]], axis=0)
                s = _att_scores(qh, kk, rel, valid, slopes[h] * dil)
                p = jnp.where(valid, jnp.exp(s - _col(lv, h)), 0.0)
                do = (dy_ref[:, cols] * _col(wv, h)).astype(BF16)
                dp = _dot(do, vv, NT_DIMS)
                ds = (p * (dp + _col(cv_, h)) * (ATT_E ** -0.5)).astype(BF16)
                dq_ref[:, cols] = _dot(ds, kk).astype(dq_ref.dtype)
                dkk = _dot(ds, qh, TN_DIMS)
                dvv = _dot(p.astype(BF16), do, TN_DIMS)
                dk_ref[:, cols] = (ck_ref[:, cols] + dkk[:ATT_W]).astype(dk_ref.dtype)
                dv_ref[:, cols] = (cv_ref[:, cols] + dvv[:ATT_W]).astype(dv_ref.dtype)
                ck_ref[:, cols] = dkk[ATT_W:]
                cv_ref[:, cols] = dvv[ATT_W:]

        @pl.when(j == nb)
        def _():
            dk_ref[...] = ck_ref[...].astype(dk_ref.dtype)
            dv_ref[...] = cv_ref[...].astype(dv_ref.dtype)

    jq = lambda j: jnp.minimum(j, nb - 1)
    cur = pl.BlockSpec((ATT_W, ATT_TOK), lambda r, j: (jq(j), r))
    prev = pl.BlockSpec((ATT_W, ATT_TOK), lambda r, j: (jnp.maximum(jq(j) - 1, 0), r))
    small = pl.BlockSpec((ATT_W, LANES), lambda r, j: (jq(j), r))
    late = pl.BlockSpec((ATT_W, ATT_TOK), lambda r, j: (jnp.maximum(j - 1, 0), r))
    big = jax.ShapeDtypeStruct((ns, dil * ATT_TOK), BF16)
    dq, dk, dv = pl.pallas_call(
        body, name=name, grid=(dil, nb + 1),
        in_specs=[cur, prev, cur, prev, cur, cur, small, small, small], out_specs=[cur, late, late],
        out_shape=[big, big, big],
        scratch_shapes=[pltpu.VMEM((ATT_W, ATT_TOK), F32), pltpu.VMEM((ATT_W, ATT_TOK), F32)],
        compiler_params=_params(("parallel", "arbitrary")),
    )(view(q), view(k), view(k), view(v), view(v), view(dy), view(wgt), view(cterm), view(lse))
    return dq.reshape(t, ATT_TOK), dk.reshape(t, ATT_TOK), dv.reshape(t, ATT_TOK)


def _combine_weights(lses):
    m = functools.reduce(jnp.maximum, lses)
    es = [jnp.exp(l - m) for l in lses]
    tot = functools.reduce(lambda a, b: a + b, es)
    return [e / tot for e in es]


def _combine_fwd(outs, lses, name):
    t = outs[0].shape[0]
    tm = _pick(t, (256, 128))

    def body(*refs):
        o_refs, l_refs, y_ref = refs[:N_DIL], refs[N_DIL:2 * N_DIL], refs[2 * N_DIL]
        ws = _combine_weights([r[...] for r in l_refs])
        for h in range(ATT_H):
            cols = slice(h * ATT_E, (h + 1) * ATT_E)
            y_ref[:, cols] = sum(_col(ws[g], h) * o_refs[g][:, cols] for g in range(N_DIL))

    big = pl.BlockSpec((tm, ATT_TOK), lambda i: (i, 0))
    small = pl.BlockSpec((tm, LANES), lambda i: (i, 0))
    return pl.pallas_call(
        body, name=name, grid=(t // tm,), in_specs=[big] * N_DIL + [small] * N_DIL, out_specs=big,
        out_shape=jax.ShapeDtypeStruct((t, ATT_TOK), F32),
        compiler_params=_params(("parallel",)),
    )(*outs, *lses)


def _combine_bwd(outs, lses, dy, name):
    t = outs[0].shape[0]
    tm = _pick(t, (256, 128))

    def body(*refs):
        o_refs, l_refs, dy_ref = refs[:N_DIL], refs[N_DIL:2 * N_DIL], refs[2 * N_DIL]
        w_refs, c_refs = refs[2 * N_DIL + 1:3 * N_DIL + 1], refs[3 * N_DIL + 1:]
        ws = _combine_weights([r[...] for r in l_refs])
        lane = _iota((tm, LANES), 1)
        sdw = jnp.zeros((tm, LANES), F32)
        for h in range(ATT_H):
            cols = slice(h * ATT_E, (h + 1) * ATT_E)
            dyh = dy_ref[:, cols]
            tot = sum(_col(ws[g], h) * jnp.sum(dyh * o_refs[g][:, cols], axis=1, keepdims=True) for g in range(N_DIL))
            sdw = jnp.where(lane == h, tot, sdw)
        for g in range(N_DIL):
            w_refs[g][...] = ws[g]
            c_refs[g][...] = -ws[g] * sdw

    big = pl.BlockSpec((tm, ATT_TOK), lambda i: (i, 0))
    small = pl.BlockSpec((tm, LANES), lambda i: (i, 0))
    res = pl.pallas_call(
        body, name=name, grid=(t // tm,), in_specs=[big] * N_DIL + [small] * N_DIL + [big],
        out_specs=[small] * (2 * N_DIL),
        out_shape=[jax.ShapeDtypeStruct((t, LANES), F32)] * (2 * N_DIL),
        compiler_params=_params(("parallel",)),
    )(*outs, *lses, dy)
    return res[:N_DIL], res[N_DIL:]


ADAM_LR, ADAM_B1, ADAM_B2, ADAM_EPS, ADAM_WD, ADAM_STEP = 0.001, 0.9, 0.999, 1e-08, 0.01, 10


def _adamw(parts, w, m, v, name):
    r, c = w.shape
    tc = _pick(c, (1024, 512, 256, 128)) if c % 128 == 0 else c
    tm = _pick(r, (128, 64, 32, 16, 8))

    def body(p_ref, w_ref, m_ref, v_ref, g_ref, d_ref, nm_ref, nv_ref):
        g = p_ref[0]
        for k in range(1, N_DEV):
            g = g + p_ref[k]
        nm = ADAM_B1 * m_ref[...] + (1.0 - ADAM_B1) * g
        nv = ADAM_B2 * v_ref[...] + (1.0 - ADAM_B2) * (g * g)
        m_hat = nm / (1.0 - ADAM_B1 ** ADAM_STEP)
        v_hat = nv / (1.0 - ADAM_B2 ** ADAM_STEP)
        g_ref[...] = g
        d_ref[...] = -ADAM_LR * (m_hat / (jnp.sqrt(v_hat) + ADAM_EPS) + ADAM_WD * w_ref[...])
        nm_ref[...] = nm
        nv_ref[...] = nv

    blk = pl.BlockSpec((tm, tc), lambda i, j: (i, j))
    pblk = pl.BlockSpec((N_DEV, tm, tc), lambda i, j: (0, i, j))
    return pl.pallas_call(
        body, name=name, grid=(r // tm, c // tc), in_specs=[pblk, blk, blk, blk], out_specs=[blk] * 4,
        out_shape=[jax.ShapeDtypeStruct((r, c), F32)] * 4,
        compiler_params=_params(("parallel", "parallel")),
    )(parts, w, m, v)


def _exchange(arrays, scatter, name):
    n = len(arrays)
    out_shape = [jax.ShapeDtypeStruct(a.shape if scatter else (N_DEV,) + a.shape, a.dtype) for a in arrays]

    def body(*refs):
        in_refs, out_refs = refs[:n], refs[n:2 * n]
        send_sems, recv_sems, local_sems = refs[2 * n:]
        x, y, c = lax.axis_index("x"), lax.axis_index("y"), lax.axis_index("c")
        me = 4 * x + 2 * y + c
        started = []
        for a in range(n):
            src_mine = in_refs[a].at[me] if scatter else in_refs[a]
            local = pltpu.make_async_copy(src_mine, out_refs[a].at[me], local_sems.at[a])
            local.start()
            started.append(local)
        sends = []
        for k in range(1, N_DEV):
            px, py, pc = x ^ ((k >> 2) & 1), y ^ ((k >> 1) & 1), c ^ (k & 1)
            peer = 4 * px + 2 * py + pc
            for a in range(n):
                cp = pltpu.make_async_remote_copy(
                    src_ref=in_refs[a].at[peer] if scatter else in_refs[a],
                    dst_ref=out_refs[a].at[me],
                    send_sem=send_sems.at[a * (N_DEV - 1) + k - 1],
                    recv_sem=recv_sems.at[a * (N_DEV - 1) + k - 1],
                    device_id=(px, py, pc), device_id_type=pl.DeviceIdType.MESH)
                cp.start()
                sends.append((cp, a, k, peer))
        for cp, a, k, peer in sends:
            landed = pltpu.make_async_remote_copy(
                src_ref=out_refs[a].at[peer], dst_ref=out_refs[a].at[peer],
                send_sem=send_sems.at[a * (N_DEV - 1) + k - 1],
                recv_sem=recv_sems.at[a * (N_DEV - 1) + k - 1],
                device_id=(x, y, c), device_id_type=pl.DeviceIdType.MESH)
            landed.wait_recv()
        for cp, _, _, _ in sends:
            cp.wait_send()
        for local in started:
            local.wait()

    any_spec = pl.BlockSpec(memory_space=pl.ANY)
    return pl.pallas_call(
        body, name=name, in_specs=[any_spec] * n, out_specs=[any_spec] * n, out_shape=out_shape,
        scratch_shapes=[pltpu.SemaphoreType.DMA((n * (N_DEV - 1),)), pltpu.SemaphoreType.DMA((n * (N_DEV - 1),)),
                        pltpu.SemaphoreType.DMA((n,))],
        compiler_params=pltpu.CompilerParams(has_side_effects=True),
    )(*arrays)


DEPTH = 4
MEM_W = 1024
MIX_W = SSD_TOK + MEM_W
DT_PAD = LANES - SSD_H


def _is_ssd(i):
    return i % 2 == 0


def _weight_names():
    names = ["mem_norm_g", "final_norm_g"]
    for i in range(DEPTH):
        names += [f"norm_g_{i}", f"w_in_{i}"]
        if _is_ssd(i):
            names += [f"conv_w_{i}", f"conv_b_{i}", f"dt_bias_{i}", f"a_log_{i}", f"d_skip_{i}", f"ssd_norm_g_{i}"]
        names += [f"w_mem_kv_{i}", f"w_out_{i}"]
    return names


WEIGHTS = _weight_names()
INPUTS = ["x", "mem"] + WEIGHTS + ["loss_target"] + ["m_" + n for n in WEIGHTS] + ["v_" + n for n in WEIGHTS]


def _in_segments(i):
    if _is_ssd(i):
        return [("xbc", 0, SSD_XBC), ("dt", SSD_XBC, SSD_H), ("qm", SSD_XBC + SSD_H, MEM_W),
                ("z", SSD_XBC + SSD_H + MEM_W, MIX_W)]
    segs = []
    for g in range(N_DIL):
        for j, nm in enumerate("qkv"):
            segs.append((f"{nm}{g}", (3 * g + j) * ATT_TOK, ATT_TOK))
    segs += [("qm", 3 * N_DIL * ATT_TOK, MEM_W), ("z", 3 * N_DIL * ATT_TOK + MEM_W, MIX_W)]
    return segs


def _split_w_in(i, w_in):
    out = {}
    for nm, start, width in _in_segments(i):
        seg = w_in[:, start:start + width]
        out[nm] = jnp.pad(seg, ((0, 0), (0, DT_PAD))) if nm == "dt" else seg
    return out


def _join_dw_in(i, dws):
    return jnp.concatenate([dws[nm][:, :width] for nm, _, width in _in_segments(i)], axis=1)


SEG_DTYPE = {"xbc": F32, "dt": F32, "z": F32}


def _layer_fwd(i, x, mem_b, p):
    tag = f"l{i}"
    h = _rmsnorm_fwd(x, p["norm_g"], tag + "_norm")
    proj = {nm: _matmul(h, w, out_dtype=SEG_DTYPE.get(nm, BF16), name=f"{tag}_in_{nm}") for nm, w in p["win"].items()}
    sv = {"x": x, "h": h, "proj": proj}
    if _is_ssd(i):
        xbc = _conv_fwd(proj["xbc"], p["conv_w"], p["conv_b"], tag + "_conv")
        y_tok, hs = _ssd_fwd(xbc, proj["dt"], p["dt_bias_p"], p["a_log_p"], p["dskip_lane"], tag + "_ssd")
        sv.update(xbc=xbc, hs=hs)
    else:
        outs, lses = [], []
        for g in range(N_DIL):
            o, lse = _dil_fwd(proj[f"q{g}"], proj[f"k{g}"], proj[f"v{g}"], g, f"{tag}_att{g}")
            outs.append(o)
            lses.append(lse)
        y_tok = _combine_fwd(outs, lses, tag + "_comb")
        sv.update(outs=outs, lses=lses)
    mkv = _matmul(mem_b, p["wmkv"], out_dtype=BF16, name=tag + "_mkv")
    y_mem = _memattn_fwd(proj["qm"], mkv, tag + "_mem")
    gated = _gate_fwd(y_tok, y_mem, proj["z"], p.get("ssd_norm_g"), tag + "_gate")
    x_out = _matmul(gated, p["wout"], out_dtype=F32, add=x, name=tag + "_out")
    sv.update(y_tok=y_tok, y_mem=y_mem, mkv=mkv, gated=gated)
    return x_out, sv


def _layer_bwd(i, sv, dx_out, dmem_n, mem_b, p):
    tag = f"l{i}b"
    proj = sv["proj"]
    gr = {}
    dgated = _matmul(dx_out, p["wout"], tb=True, out_dtype=F32, name=tag + "_dgated")
    gr["w_out"] = _matmul(sv["gated"], dx_out, ta=True, out_dtype=F32, name=tag + "_dwout")
    dy_tok, dy_mem, dz, dssd_g = _gate_bwd(sv["y_tok"], sv["y_mem"], proj["z"], p.get("ssd_norm_g"), dgated, tag + "_gate")
    dq_mem, dmkv = _memattn_bwd(proj["qm"], sv["mkv"], dy_mem, tag + "_mem")
    gr["w_mem_kv"] = _matmul(mem_b, dmkv, ta=True, out_dtype=F32, name=tag + "_dwmkv")
    dmem_n = _matmul(dmkv, p["wmkv"], tb=True, out_dtype=F32, add=dmem_n, name=tag + "_dmem")
    dproj = {"qm": dq_mem, "z": dz}
    if _is_ssd(i):
        dxbc, ddt_raw, dbias, dalog, ddsk = _ssd_bwd(sv["xbc"], proj["dt"], p["dt_bias_p"], p["a_log_p"], p["dskip_lane"],
                                                     sv["hs"], dy_tok, tag + "_ssd")
        dpre, dconv_w, dconv_b = _conv_bwd_pre(proj["xbc"], p["conv_w"], p["conv_b"], dxbc, tag + "_convpre")
        dproj["xbc"] = _conv_bwd_in(dpre, p["conv_w"], tag + "_convin")
        dproj["dt"] = ddt_raw
        gr.update(conv_w=dconv_w, conv_b=dconv_b[0], dt_bias=dbias[0, :SSD_H], a_log=dalog[0, :SSD_H],
                  d_skip=jnp.sum(ddsk.reshape(SSD_H, SSD_P), axis=1), ssd_norm_g=dssd_g[0])
    else:
        ws, cs = _combine_bwd(sv["outs"], sv["lses"], dy_tok, tag + "_comb")
        for g in range(N_DIL):
            dq, dk, dv = _dil_bwd(proj[f"q{g}"], proj[f"k{g}"], proj[f"v{g}"], dy_tok, ws[g], cs[g], sv["lses"][g], g,
                                  f"{tag}_att{g}")
            dproj.update({f"q{g}": dq, f"k{g}": dk, f"v{g}": dv})
    dh = None
    dws = {}
    for nm, w in p["win"].items():
        dh = _matmul(dproj[nm], w, tb=True, out_dtype=F32, add=dh, name=f"{tag}_dh_{nm}")
        dws[nm] = _matmul(sv["h"], dproj[nm], ta=True, out_dtype=F32, name=f"{tag}_dw_{nm}")
    gr["w_in"] = _join_dw_in(i, dws)
    dx, dnorm_g = _rmsnorm_bwd(sv["x"], p["norm_g"], dh, dx_out, tag + "_norm")
    gr["norm_g"] = dnorm_g[0]
    return dx, dmem_n, gr


def _pad_heads(v):
    return jnp.pad(v.reshape(1, SSD_H), ((0, 0), (0, DT_PAD)))


def _layer_params(i, small, w_in, w_mem_kv, w_out):
    p = {"norm_g": small[f"norm_g_{i}"], "win": _split_w_in(i, w_in), "wmkv": w_mem_kv, "wout": w_out}
    if _is_ssd(i):
        p.update(conv_w=small[f"conv_w_{i}"], conv_b=small[f"conv_b_{i}"], ssd_norm_g=small[f"ssd_norm_g_{i}"],
                 dt_bias_p=_pad_heads(small[f"dt_bias_{i}"]), a_log_p=_pad_heads(small[f"a_log_{i}"]),
                 dskip_lane=jnp.repeat(small[f"d_skip_{i}"], SSD_P).reshape(1, SSD_TOK))
    return p


def _local_step(x, mem, target, small, big):
    params = [_layer_params(i, small, *big[i]) for i in range(DEPTH)]
    mem_b = _rmsnorm_fwd(mem, small["mem_norm_g"], "mem_norm")
    saved = []
    for i in range(DEPTH):
        x, sv = _layer_fwd(i, x, mem_b, params[i])
        saved.append(sv)
    loss, dx, dfinal = _final_loss(x, small["final_norm_g"], target)
    grads = {"final_norm_g": dfinal[0]}
    dmem_n = None
    for i in reversed(range(DEPTH)):
        dx, dmem_n, gr = _layer_bwd(i, saved[i], dx, dmem_n, mem_b, params[i])
        grads.update({f"{nm}_{i}": g for nm, g in gr.items()})
    _, dmem_g = _rmsnorm_bwd(mem, small["mem_norm_g"], dmem_n, None, "mem_norm_b")
    grads["mem_norm_g"] = dmem_g[0]
    return loss[0, 0], dx, grads


BIG = ("w_in", "w_mem_kv", "w_out")
SMALL = [n for n in WEIGHTS if not n.startswith(BIG)]
PACK_ROWS = 8 * LANES


def _pack(vals):
    flat = jnp.concatenate([v.reshape(-1).astype(F32) for v in vals])
    padded = -(-flat.shape[0] // PACK_ROWS) * PACK_ROWS
    return jnp.pad(flat, (0, padded - flat.shape[0])).reshape(padded // LANES, LANES)


def _train_step(a, local_step):
    x, y, c = lax.axis_index("x"), lax.axis_index("y"), lax.axis_index("c")
    me = 4 * x + 2 * y + c
    big = []
    for i in range(DEPTH):
        shards = [a[f"{nm}_{i}"].astype(BF16) for nm in BIG]
        g_in, g_kv, g_out = _exchange(shards, False, f"gather_w{i}")
        d, cs = shards[0].shape
        big.append((jnp.transpose(g_in, (1, 0, 2)).reshape(d, N_DEV * cs),
                    g_kv.reshape(N_DEV * g_kv.shape[1], g_kv.shape[2]),
                    g_out.reshape(N_DEV * g_out.shape[1], g_out.shape[2])))
    conv_names = [n for n in SMALL if n.startswith("conv_w")]
    conv_full = _exchange([a[n] for n in conv_names], False, "gather_conv")
    small = {n: a[n] for n in SMALL}
    for n, gathered in zip(conv_names, conv_full):
        small[n] = jnp.transpose(gathered, (1, 0, 2)).reshape(gathered.shape[1], N_DEV * gathered.shape[2])

    loss_local, grad_x, grads = local_step(a["x"][0], a["mem"][0], a["loss_target"][0], small, big)
    loss = lax.psum(loss_local, ("x", "y", "c"))

    res = {}
    for i in range(DEPTH):
        g_in = grads[f"w_in_{i}"]
        d, cols = g_in.shape
        parts = [jnp.transpose(g_in.reshape(d, N_DEV, cols // N_DEV), (1, 0, 2))]
        for nm in BIG[1:]:
            g = grads[f"{nm}_{i}"]
            parts.append(g.reshape(N_DEV, g.shape[0] // N_DEV, g.shape[1]))
        landed = _exchange(parts, True, f"scatter_w{i}")
        for nm, p in zip(BIG, landed):
            n = f"{nm}_{i}"
            res[n] = _adamw(p, a[n], a["m_" + n], a["v_" + n], "adamw_" + n)

    gathered = _exchange([_pack([grads[n] for n in SMALL])], False, "gather_small")[0]
    zero_conv = lambda pre: [jnp.zeros(small[n].shape, F32) if n in conv_names else a[pre + n] for n in SMALL]
    packed = _adamw(gathered, _pack(zero_conv("")), _pack(zero_conv("m_")), _pack(zero_conv("v_")), "adamw_small")
    off = 0
    for n in SMALL:
        size = math.prod(small[n].shape)
        if n in conv_names:
            rows, cols = a[n].shape
            whole = gathered.reshape(N_DEV, -1)[:, off:off + size].reshape(N_DEV, rows, N_DEV * cols)
            mine = lax.dynamic_slice_in_dim(whole, me * cols, cols, axis=2)
            res[n] = _adamw(mine, a[n], a["m_" + n], a["v_" + n], "adamw_" + n)
        else:
            res[n] = [o.reshape(-1)[off:off + size].reshape(a[n].shape) for o in packed]
        off += size
    outs = [loss, grad_x[None]]
    for k in range(4):
        outs += [res[n][k] for n in WEIGHTS]
    return tuple(outs)


def kernel(x, mem, mem_norm_g, final_norm_g, norm_g_0, w_in_0, conv_w_0, conv_b_0, dt_bias_0, a_log_0, d_skip_0, ssd_norm_g_0, w_mem_kv_0, w_out_0, norm_g_1, w_in_1, w_mem_kv_1, w_out_1, norm_g_2, w_in_2, conv_w_2, conv_b_2, dt_bias_2, a_log_2, d_skip_2, ssd_norm_g_2, w_mem_kv_2, w_out_2, norm_g_3, w_in_3, w_mem_kv_3, w_out_3, loss_target, m_mem_norm_g, m_final_norm_g, m_norm_g_0, m_w_in_0, m_conv_w_0, m_conv_b_0, m_dt_bias_0, m_a_log_0, m_d_skip_0, m_ssd_norm_g_0, m_w_mem_kv_0, m_w_out_0, m_norm_g_1, m_w_in_1, m_w_mem_kv_1, m_w_out_1, m_norm_g_2, m_w_in_2, m_conv_w_2, m_conv_b_2, m_dt_bias_2, m_a_log_2, m_d_skip_2, m_ssd_norm_g_2, m_w_mem_kv_2, m_w_out_2, m_norm_g_3, m_w_in_3, m_w_mem_kv_3, m_w_out_3, v_mem_norm_g, v_final_norm_g, v_norm_g_0, v_w_in_0, v_conv_w_0, v_conv_b_0, v_dt_bias_0, v_a_log_0, v_d_skip_0, v_ssd_norm_g_0, v_w_mem_kv_0, v_w_out_0, v_norm_g_1, v_w_in_1, v_w_mem_kv_1, v_w_out_1, v_norm_g_2, v_w_in_2, v_conv_w_2, v_conv_b_2, v_dt_bias_2, v_a_log_2, v_d_skip_2, v_ssd_norm_g_2, v_w_mem_kv_2, v_w_out_2, v_norm_g_3, v_w_in_3, v_w_mem_kv_3, v_w_out_3):
    vals = locals()
    return _train_step({n: vals[n] for n in INPUTS}, _local_step)
```

```python
import functools
import math

import jax
import jax.numpy as jnp
from jax import lax
from jax.experimental import pallas as pl
from jax.experimental.pallas import tpu as pltpu

F32 = jnp.float32
BF16 = jnp.bfloat16
EPS = 1e-6
N_DEV = 8
VMEM_LIMIT_BYTES = 56 * 1024 * 1024


def _pick(n, prefs):
    for p in prefs:
        if n % p == 0:
            return p
    return n


def _params(sem):
    return pltpu.CompilerParams(dimension_semantics=sem, vmem_limit_bytes=VMEM_LIMIT_BYTES)


def _matmul(a, b, *, ta=False, tb=False, out_dtype=F32, add=None, name="mm"):
    if ta:
        k_dim, m_dim = a.shape
    else:
        m_dim, k_dim = a.shape
    n_dim = b.shape[0] if tb else b.shape[1]
    out_bytes = jnp.dtype(out_dtype).itemsize + (0 if add is None else add.dtype.itemsize)
    tm, tn, tk = _matmul_tiles(m_dim, n_dim, k_dim, a.dtype.itemsize, b.dtype.itemsize, out_bytes)
    nk = k_dim // tk
    dims = (((0,) if ta else (1,), (1,) if tb else (0,)), ((), ()))

    def body(*refs):
        if add is None:
            a_ref, b_ref, o_ref = refs[:3]
            add_ref = None
        else:
            a_ref, b_ref, add_ref, o_ref = refs[:4]
        part = lax.dot_general(a_ref[...].astype(BF16), b_ref[...].astype(BF16), dims,
                               preferred_element_type=F32)
        if nk == 1:
            o_ref[...] = (part if add_ref is None else part + add_ref[...].astype(F32)).astype(o_ref.dtype)
            return
        acc_ref = refs[-1]
        k = pl.program_id(2)

        @pl.when(k == 0)
        def _():
            acc_ref[...] = part if add_ref is None else part + add_ref[...].astype(F32)

        @pl.when(k > 0)
        def _():
            acc_ref[...] += part

        @pl.when(k == nk - 1)
        def _():
            o_ref[...] = acc_ref[...].astype(o_ref.dtype)

    a_spec = pl.BlockSpec((tk, tm), lambda i, j, k: (k, i)) if ta else pl.BlockSpec((tm, tk), lambda i, j, k: (i, k))
    b_spec = pl.BlockSpec((tn, tk), lambda i, j, k: (j, k)) if tb else pl.BlockSpec((tk, tn), lambda i, j, k: (k, j))
    o_spec = pl.BlockSpec((tm, tn), lambda i, j, k: (i, j))
    in_specs = [a_spec, b_spec] + ([o_spec] if add is not None else [])
    args = (a, b) + ((add,) if add is not None else ())
    return pl.pallas_call(
        body, name=name, grid=(m_dim // tm, n_dim // tn, nk),
        in_specs=in_specs, out_specs=o_spec,
        out_shape=jax.ShapeDtypeStruct((m_dim, n_dim), out_dtype),
        scratch_shapes=[pltpu.VMEM((tm, tn), F32)] if nk > 1 else [],
        compiler_params=_params(("parallel", "parallel", "arbitrary")),
    )(*args)


MATMUL_VMEM_BUDGET = 40 * 1024 * 1024


def _matmul_tiles(m_dim, n_dim, k_dim, a_bytes, b_bytes, out_bytes):
    best = None
    for tk in (k_dim, 4096, 2048, 1024, 512, 256, 128):
        if tk > k_dim or k_dim % tk:
            continue
        for tm in (1024, 512, 256, 128):
            if m_dim % tm:
                continue
            for tn in (2048, 1024, 512, 256, 128):
                if n_dim % tn:
                    continue
                vmem = 2 * (tm * tk * a_bytes + tk * tn * b_bytes + tm * tn * out_bytes) + 2 * tm * tn * 4
                if a_bytes == 4:
                    vmem += tm * tk * 2
                if vmem > MATMUL_VMEM_BUDGET:
                    continue
                score = (tm * tn * tk, tk, min(tm, tn))
                if best is None or score > best[0]:
                    best = (score, (tm, tn, tk))
    return best[1]


def _iota(shape, dim):
    return lax.broadcasted_iota(jnp.int32, shape, dim)


def _col(x, j):
    return jnp.sum(jnp.where(_iota(x.shape, 1) == j, x, 0.0), axis=1, keepdims=True)


def _silu(x):
    return x * jax.nn.sigmoid(x)


def _dsilu(x):
    s = jax.nn.sigmoid(x)
    return s * (1.0 + x * (1.0 - s))


def _rmsnorm_fwd(x, g, name):
    t, d = x.shape
    tm = _pick(t, (512, 256, 128))

    def body(x_ref, g_ref, h_ref):
        xv = x_ref[...]
        r = lax.rsqrt(jnp.mean(xv * xv, axis=-1, keepdims=True) + EPS)
        h_ref[...] = (xv * r * g_ref[...]).astype(h_ref.dtype)

    return pl.pallas_call(
        body, name=name, grid=(t // tm,),
        in_specs=[pl.BlockSpec((tm, d), lambda i: (i, 0)), pl.BlockSpec((1, d), lambda i: (0, 0))],
        out_specs=pl.BlockSpec((tm, d), lambda i: (i, 0)),
        out_shape=jax.ShapeDtypeStruct((t, d), BF16),
        compiler_params=_params(("parallel",)),
    )(x, g.reshape(1, d))


def _rmsnorm_bwd(x, g, dh, dres, name):
    t, d = x.shape
    tm = _pick(t, (512, 256, 128))

    def body(*refs):
        if dres is None:
            x_ref, g_ref, dh_ref, dx_ref, dg_ref = refs
        else:
            x_ref, g_ref, dh_ref, dres_ref, dx_ref, dg_ref = refs
        xv = x_ref[...]
        r = lax.rsqrt(jnp.mean(xv * xv, axis=-1, keepdims=True) + EPS)
        xhat = xv * r
        dhv = dh_ref[...].astype(F32)
        dxh = dhv * g_ref[...]
        dx = r * (dxh - xhat * jnp.mean(dxh * xhat, axis=-1, keepdims=True))
        if dres is not None:
            dx = dx + dres_ref[...]
        dx_ref[...] = dx
        part = jnp.sum(dhv * xhat, axis=0, keepdims=True)

        @pl.when(pl.program_id(0) == 0)
        def _():
            dg_ref[...] = part

        @pl.when(pl.program_id(0) > 0)
        def _():
            dg_ref[...] += part

    row = pl.BlockSpec((tm, d), lambda i: (i, 0))
    vec = pl.BlockSpec((1, d), lambda i: (0, 0))
    in_specs = [row, vec, row] + ([row] if dres is not None else [])
    args = (x, g.reshape(1, d), dh) + ((dres,) if dres is not None else ())
    return pl.pallas_call(
        body, name=name, grid=(t // tm,), in_specs=in_specs, out_specs=[row, vec],
        out_shape=[jax.ShapeDtypeStruct((t, d), F32), jax.ShapeDtypeStruct((1, d), F32)],
        compiler_params=_params(("arbitrary",)),
    )(*args)


def _final_loss(x, g, target, name="final_loss"):
    t, d = x.shape
    tm = _pick(t, (512, 256, 128))

    def body(x_ref, g_ref, t_ref, loss_ref, dx_ref, dg_ref):
        xv = x_ref[...]
        gv = g_ref[...]
        r = lax.rsqrt(jnp.mean(xv * xv, axis=-1, keepdims=True) + EPS)
        xhat = xv * r
        e = xhat * gv - t_ref[...]
        lpart = jnp.zeros((1, 128), F32) + (0.5 / d) * jnp.sum(e * e)
        dy = e * (1.0 / d)
        dxh = dy * gv
        dx_ref[...] = r * (dxh - xhat * jnp.mean(dxh * xhat, axis=-1, keepdims=True))
        gpart = jnp.sum(dy * xhat, axis=0, keepdims=True)

        @pl.when(pl.program_id(0) == 0)
        def _():
            dg_ref[...] = gpart
            loss_ref[...] = lpart

        @pl.when(pl.program_id(0) > 0)
        def _():
            dg_ref[...] += gpart
            loss_ref[...] += lpart

    row = pl.BlockSpec((tm, d), lambda i: (i, 0))
    vec = pl.BlockSpec((1, d), lambda i: (0, 0))
    return pl.pallas_call(
        body, name=name, grid=(t // tm,), in_specs=[row, vec, row],
        out_specs=[pl.BlockSpec((1, 128), lambda i: (0, 0)), row, vec],
        out_shape=[jax.ShapeDtypeStruct((1, 128), F32), jax.ShapeDtypeStruct((t, d), F32),
                   jax.ShapeDtypeStruct((1, d), F32)],
        compiler_params=_params(("arbitrary",)),
    )(x, g.reshape(1, d), target)


CONV_K = 4
HALO = 8


def _shift_down(cur, prev8, s):
    rolled = pltpu.roll(cur, s, 0)
    fix = pltpu.roll(prev8, s, 0)
    head = jnp.where(_iota((HALO, cur.shape[1]), 0) < s, fix, rolled[:HALO])
    return jnp.concatenate([head, rolled[HALO:]], axis=0)


def _shift_up(cur, next8, s):
    n = cur.shape[0]
    rolled = pltpu.roll(cur, n - s, 0)
    fix = pltpu.roll(next8, HALO - s, 0)
    tail = jnp.where(_iota((HALO, cur.shape[1]), 0) >= HALO - s, fix, rolled[n - HALO:])
    return jnp.concatenate([rolled[:n - HALO], tail], axis=0)


def _conv_pre(u_ref, up_ref, w_ref, b_ref, first):
    cur = u_ref[...]
    prev8 = jnp.where(first, 0.0, up_ref[...])
    w = w_ref[...]
    shifted = [cur] + [_shift_down(cur, prev8, s) for s in (1, 2, 3)]
    pre = b_ref[...] + sum(w[CONV_K - 1 - s:CONV_K - s, :] * shifted[s] for s in range(CONV_K))
    return pre, shifted


def _conv_specs(tm, tc):
    nb = tm // HALO
    cur = pl.BlockSpec((tm, tc), lambda j, i: (i, j))
    prev = pl.BlockSpec((HALO, tc), lambda j, i: (jnp.maximum(i * nb - 1, 0), j))
    wspec = pl.BlockSpec((CONV_K, tc), lambda j, i: (0, j))
    bspec = pl.BlockSpec((1, tc), lambda j, i: (0, j))
    return cur, prev, wspec, bspec


def _conv_fwd(u, w, b, name):
    t, c = u.shape
    tm, tc = _pick(t, (512, 256, 128)), _pick(c, (1024, 512, 256, 128))
    cur, prev, wspec, bspec = _conv_specs(tm, tc)

    def body(u_ref, up_ref, w_ref, b_ref, o_ref):
        pre, _ = _conv_pre(u_ref, up_ref, w_ref, b_ref, pl.program_id(1) == 0)
        o_ref[...] = _silu(pre)

    return pl.pallas_call(
        body, name=name, grid=(c // tc, t // tm), in_specs=[cur, prev, wspec, bspec], out_specs=cur,
        out_shape=jax.ShapeDtypeStruct((t, c), F32),
        compiler_params=_params(("parallel", "parallel")),
    )(u, u, w, b.reshape(1, c))


def _conv_bwd_pre(u, w, b, dy, name):
    t, c = u.shape
    tm, tc = _pick(t, (512, 256, 128)), _pick(c, (1024, 512, 256, 128))
    cur, prev, wspec, bspec = _conv_specs(tm, tc)

    def body(u_ref, up_ref, w_ref, b_ref, dy_ref, dpre_ref, dw_ref, db_ref):
        i = pl.program_id(1)
        pre, shifted = _conv_pre(u_ref, up_ref, w_ref, b_ref, i == 0)
        dpre = dy_ref[...] * _dsilu(pre)
        dpre_ref[...] = dpre
        dw = jnp.concatenate([jnp.sum(dpre * shifted[CONV_K - 1 - k], axis=0, keepdims=True) for k in range(CONV_K)], axis=0)
        db = jnp.sum(dpre, axis=0, keepdims=True)

        @pl.when(i == 0)
        def _():
            dw_ref[...] = dw
            db_ref[...] = db

        @pl.when(i > 0)
        def _():
            dw_ref[...] += dw
            db_ref[...] += db

    return pl.pallas_call(
        body, name=name, grid=(c // tc, t // tm), in_specs=[cur, prev, wspec, bspec, cur],
        out_specs=[cur, wspec, bspec],
        out_shape=[jax.ShapeDtypeStruct((t, c), F32), jax.ShapeDtypeStruct((CONV_K, c), F32),
                   jax.ShapeDtypeStruct((1, c), F32)],
        compiler_params=_params(("parallel", "arbitrary")),
    )(u, u, w, b.reshape(1, c), dy)


def _conv_bwd_in(dpre, w, name):
    t, c = dpre.shape
    tm, tc = _pick(t, (512, 256, 128)), _pick(c, (1024, 512, 256, 128))
    nb = tm // HALO
    last = t // tm - 1
    cur = pl.BlockSpec((tm, tc), lambda j, i: (i, j))
    nxt = pl.BlockSpec((HALO, tc), lambda j, i: (jnp.minimum((i + 1) * nb, t // HALO - 1), j))
    wspec = pl.BlockSpec((CONV_K, tc), lambda j, i: (0, j))

    def body(d_ref, dn_ref, w_ref, o_ref):
        cur_v = d_ref[...]
        next8 = jnp.where(pl.program_id(1) == last, 0.0, dn_ref[...])
        wv = w_ref[...]
        acc = wv[CONV_K - 1:CONV_K, :] * cur_v
        for s in (1, 2, 3):
            acc = acc + wv[CONV_K - 1 - s:CONV_K - s, :] * _shift_up(cur_v, next8, s)
        o_ref[...] = acc.astype(o_ref.dtype)

    return pl.pallas_call(
        body, name=name, grid=(c // tc, t // tm), in_specs=[cur, nxt, wspec], out_specs=cur,
        out_shape=jax.ShapeDtypeStruct((t, c), BF16),
        compiler_params=_params(("parallel", "parallel")),
    )(dpre, dpre, w)


MEM_HEADS = 4
NT_DIMS = (((1,), (1,)), ((), ()))
TN_DIMS = (((0,), (0,)), ((), ()))


def _dot(a, b, dims=None):
    if dims is None:
        return jnp.dot(a, b, preferred_element_type=F32)
    return lax.dot_general(a, b, dims, preferred_element_type=F32)


def _memattn_probs(q, mk, scale):
    s = _dot(q, mk, NT_DIMS) * scale
    s = s - jnp.max(s, axis=-1, keepdims=True)
    p = jnp.exp(s)
    return p / jnp.sum(p, axis=-1, keepdims=True)


def _memattn_fwd(q, mkv, name):
    t, wd = q.shape
    m = mkv.shape[0]
    hd = wd // MEM_HEADS
    scale = hd ** -0.5
    tm = _pick(t, (512, 256, 128))

    def body(q_ref, mkv_ref, o_ref):
        for h in range(MEM_HEADS):
            cols = slice(h * hd, (h + 1) * hd)
            p = _memattn_probs(q_ref[:, cols], mkv_ref[:, cols], scale)
            o_ref[:, cols] = _dot(p.astype(BF16), mkv_ref[:, wd + h * hd:wd + (h + 1) * hd])

    return pl.pallas_call(
        body, name=name, grid=(t // tm,),
        in_specs=[pl.BlockSpec((tm, wd), lambda i: (i, 0)), pl.BlockSpec((m, 2 * wd), lambda i: (0, 0))],
        out_specs=pl.BlockSpec((tm, wd), lambda i: (i, 0)),
        out_shape=jax.ShapeDtypeStruct((t, wd), F32),
        compiler_params=_params(("parallel",)),
    )(q, mkv)


def _memattn_bwd(q, mkv, dy, name):
    t, wd = q.shape
    m = mkv.shape[0]
    hd = wd // MEM_HEADS
    scale = hd ** -0.5
    tm = _pick(t, (512, 256, 128))

    def body(q_ref, mkv_ref, dy_ref, dq_ref, dmkv_ref):
        i = pl.program_id(0)

        @pl.when(i == 0)
        def _():
            dmkv_ref[...] = jnp.zeros_like(dmkv_ref)

        for h in range(MEM_HEADS):
            cols = slice(h * hd, (h + 1) * hd)
            vcols = slice(wd + h * hd, wd + (h + 1) * hd)
            qh = q_ref[:, cols]
            p = _memattn_probs(qh, mkv_ref[:, cols], scale)
            dyh = dy_ref[:, cols].astype(BF16)
            dp = _dot(dyh, mkv_ref[:, vcols], NT_DIMS)
            ds = (p * (dp - jnp.sum(dp * p, axis=-1, keepdims=True)) * scale).astype(BF16)
            dq_ref[:, cols] = _dot(ds, mkv_ref[:, cols]).astype(dq_ref.dtype)
            dmkv_ref[:, cols] += _dot(ds, qh, TN_DIMS)
            dmkv_ref[:, vcols] += _dot(p.astype(BF16), dyh, TN_DIMS)

    return pl.pallas_call(
        body, name=name, grid=(t // tm,),
        in_specs=[pl.BlockSpec((tm, wd), lambda i: (i, 0)), pl.BlockSpec((m, 2 * wd), lambda i: (0, 0)),
                  pl.BlockSpec((tm, wd), lambda i: (i, 0))],
        out_specs=[pl.BlockSpec((tm, wd), lambda i: (i, 0)), pl.BlockSpec((m, 2 * wd), lambda i: (0, 0))],
        out_shape=[jax.ShapeDtypeStruct((t, wd), BF16), jax.ShapeDtypeStruct((m, 2 * wd), F32)],
        compiler_params=_params(("arbitrary",)),
    )(q, mkv, dy)


NORM_GROUPS = 8


def _gate_fwd(y_tok, y_mem, z, norm_g, name):
    t, tok = y_tok.shape
    mem = y_mem.shape[1]
    mix = tok + mem
    gw = tok // NORM_GROUPS
    tm = _pick(t, (256, 128))

    def body(*refs):
        if norm_g is None:
            yt_ref, ym_ref, z_ref, o_ref = refs
        else:
            yt_ref, ym_ref, z_ref, g_ref, o_ref = refs
        u = yt_ref[...] * _silu(z_ref[:, :tok])
        if norm_g is None:
            o_ref[:, :tok] = u.astype(o_ref.dtype)
        else:
            for k in range(NORM_GROUPS):
                uk = u[:, k * gw:(k + 1) * gw]
                r = lax.rsqrt(jnp.mean(uk * uk, axis=-1, keepdims=True) + EPS)
                o_ref[:, k * gw:(k + 1) * gw] = (uk * r * g_ref[:, k * gw:(k + 1) * gw]).astype(o_ref.dtype)
        o_ref[:, tok:] = (ym_ref[...] * _silu(z_ref[:, tok:])).astype(o_ref.dtype)

    in_specs = [pl.BlockSpec((tm, tok), lambda i: (i, 0)), pl.BlockSpec((tm, mem), lambda i: (i, 0)),
                pl.BlockSpec((tm, mix), lambda i: (i, 0))]
    args = [y_tok, y_mem, z]
    if norm_g is not None:
        in_specs.append(pl.BlockSpec((1, tok), lambda i: (0, 0)))
        args.append(norm_g.reshape(1, tok))
    return pl.pallas_call(
        body, name=name, grid=(t // tm,), in_specs=in_specs,
        out_specs=pl.BlockSpec((tm, mix), lambda i: (i, 0)),
        out_shape=jax.ShapeDtypeStruct((t, mix), BF16),
        compiler_params=_params(("parallel",)),
    )(*args)


def _gate_bwd(y_tok, y_mem, z, norm_g, dgated, name):
    t, tok = y_tok.shape
    mem = y_mem.shape[1]
    mix = tok + mem
    gw = tok // NORM_GROUPS
    tm = _pick(t, (256, 128))

    def body(*refs):
        if norm_g is None:
            yt_ref, ym_ref, z_ref, dg_ref, dyt_ref, dym_ref, dz_ref, dn_ref = refs
        else:
            yt_ref, ym_ref, z_ref, dg_ref, g_ref, dyt_ref, dym_ref, dz_ref, dn_ref = refs
        i = pl.program_id(0)
        zt = z_ref[:, :tok]
        yt = yt_ref[...]
        sz = _silu(zt)
        dout = dg_ref[:, :tok].astype(F32)
        if norm_g is None:
            du = dout
            dn = jnp.zeros((1, tok), F32)
        else:
            u = yt * sz
            dus, dns = [], []
            for k in range(NORM_GROUPS):
                uk = u[:, k * gw:(k + 1) * gw]
                r = lax.rsqrt(jnp.mean(uk * uk, axis=-1, keepdims=True) + EPS)
                nk = uk * r
                dk = dout[:, k * gw:(k + 1) * gw]
                dns.append(jnp.sum(dk * nk, axis=0, keepdims=True))
                dnk = dk * g_ref[:, k * gw:(k + 1) * gw]
                dus.append(r * (dnk - nk * jnp.mean(dnk * nk, axis=-1, keepdims=True)))
            du = jnp.concatenate(dus, axis=1)
            dn = jnp.concatenate(dns, axis=1)
        dyt_ref[...] = du * sz
        dz_ref[:, :tok] = (du * yt * _dsilu(zt)).astype(dz_ref.dtype)
        zm = z_ref[:, tok:]
        dm = dg_ref[:, tok:].astype(F32)
        dym_ref[...] = dm * _silu(zm)
        dz_ref[:, tok:] = (dm * ym_ref[...] * _dsilu(zm)).astype(dz_ref.dtype)

        @pl.when(i == 0)
        def _():
            dn_ref[...] = dn

        @pl.when(i > 0)
        def _():
            dn_ref[...] += dn

    tok_spec = pl.BlockSpec((tm, tok), lambda i: (i, 0))
    mem_spec = pl.BlockSpec((tm, mem), lambda i: (i, 0))
    mix_spec = pl.BlockSpec((tm, mix), lambda i: (i, 0))
    vec = pl.BlockSpec((1, tok), lambda i: (0, 0))
    in_specs = [tok_spec, mem_spec, mix_spec, mix_spec]
    args = [y_tok, y_mem, z, dgated]
    if norm_g is not None:
        in_specs.append(vec)
        args.append(norm_g.reshape(1, tok))
    return pl.pallas_call(
        body, name=name, grid=(t // tm,), in_specs=in_specs,
        out_specs=[tok_spec, mem_spec, mix_spec, vec],
        out_shape=[jax.ShapeDtypeStruct((t, tok), F32), jax.ShapeDtypeStruct((t, mem), F32),
                   jax.ShapeDtypeStruct((t, mix), BF16), jax.ShapeDtypeStruct((1, tok), F32)],
        compiler_params=_params(("arbitrary",)),
    )(*args)


SSD_Q = 128
SSD_N = 128
SSD_P = 64
SSD_G = 8
SSD_HPG = 6
SSD_H = SSD_G * SSD_HPG
SSD_TOK = SSD_H * SSD_P
SSD_XBC = SSD_TOK + 2 * SSD_G * SSD_N
LANES = 128
HIGHEST = lax.Precision.HIGHEST


def _softplus(x):
    return jnp.maximum(x, 0.0) + jnp.log(1.0 + jnp.exp(-jnp.abs(x)))


def _ssd_common(dtr_ref, bias_ref, alog_ref):
    sq = (SSD_Q, LANES)
    pre = dtr_ref[...] + bias_ref[...]
    dt = _softplus(pre)
    a = -jnp.exp(alog_ref[...])
    tril = (_iota(sq, 0) >= _iota(sq, 1)).astype(F32)
    acs = jnp.dot(tril, dt * a, precision=HIGHEST, preferred_element_type=F32)
    return pre, dt, a, tril, acs, acs.T


def _pair_terms(dt, acs, acs_t, h0):
    hi = _iota((SSD_Q, LANES), 1) >= SSD_P
    heads = []
    for j in range(2):
        h = h0 + j
        a_col = _col(acs, h)
        a_row = acs_t[h:h + 1, :]
        a_last = _col(acs[SSD_Q - 1:SSD_Q, :], h)
        heads.append((h, a_col, a_row, a_last, hi if j else jnp.logical_not(hi)))
    dtl = jnp.where(hi, _col(dt, h0 + 1), _col(dt, h0))
    scale = jnp.where(hi, jnp.exp(heads[1][1]), jnp.exp(heads[0][1]))
    dec_last = jnp.where(hi[:1], jnp.exp(heads[1][3]), jnp.exp(heads[0][3]))
    return heads, dtl, scale, dec_last


def _decay(a_col, a_row):
    causal = _iota((SSD_Q, SSD_Q), 0) >= _iota((SSD_Q, SSD_Q), 1)
    return jnp.where(causal, jnp.exp(jnp.minimum(a_col - a_row, 0.0)), 0.0)


def _ssd_fwd(xbc, dt_raw, dt_bias, a_log, dskip_lane, name):
    t = xbc.shape[0]
    nc = t // SSD_Q

    def body(xbc_ref, dtr_ref, bias_ref, alog_ref, dsk_ref, y_ref, hs_ref, h_ref):
        @pl.when(pl.program_id(0) == 0)
        def _():
            h_ref[...] = jnp.zeros_like(h_ref)

        _, dt, _, _, acs, acs_t = _ssd_common(dtr_ref, bias_ref, alog_ref)
        for g in range(SSD_G):
            bg_f = xbc_ref[:, SSD_TOK + g * SSD_N:SSD_TOK + (g + 1) * SSD_N]
            bg = bg_f.astype(BF16)
            cg = xbc_ref[:, SSD_TOK + SSD_G * SSD_N + g * SSD_N:SSD_TOK + SSD_G * SSD_N + (g + 1) * SSD_N].astype(BF16)
            cb = _dot(cg, bg, NT_DIMS)
            for pr in range(SSD_HPG // 2):
                h0 = g * SSD_HPG + 2 * pr
                lanes = slice(h0 * SSD_P, (h0 + 2) * SSD_P)
                heads, dtl, scale, dec_last = _pair_terms(dt, acs, acs_t, h0)
                xs = xbc_ref[:, lanes]
                xdt = xs * dtl
                hp = h_ref[:, lanes]
                hs_ref[:, lanes] = hp
                y = _dot(cg, hp.astype(BF16)) * scale + dsk_ref[:, lanes] * xs
                snew = hp * dec_last
                for _, a_col, a_row, a_last, mask in heads:
                    xm = jnp.where(mask, xdt, 0.0).astype(BF16)
                    y = y + _dot((cb * _decay(a_col, a_row)).astype(BF16), xm)
                    bw = (bg_f * jnp.exp(a_last - a_col)).astype(BF16)
                    snew = snew + _dot(bw, xm, TN_DIMS)
                y_ref[:, lanes] = y
                h_ref[:, lanes] = snew

    row = lambda w: pl.BlockSpec((SSD_Q, w), lambda c: (c, 0))
    vec = lambda w: pl.BlockSpec((1, w), lambda c: (0, 0))
    return pl.pallas_call(
        body, name=name, grid=(nc,),
        in_specs=[row(SSD_XBC), row(LANES), vec(LANES), vec(LANES), vec(SSD_TOK)],
        out_specs=[row(SSD_TOK), row(SSD_TOK)],
        out_shape=[jax.ShapeDtypeStruct((t, SSD_TOK), F32), jax.ShapeDtypeStruct((nc * SSD_N, SSD_TOK), F32)],
        scratch_shapes=[pltpu.VMEM((SSD_N, SSD_TOK), F32)],
        compiler_params=_params(("arbitrary",)),
    )(xbc, dt_raw, dt_bias, a_log, dskip_lane)


def _ssd_bwd(xbc, dt_raw, dt_bias, a_log, dskip_lane, hs, dy, name):
    t = xbc.shape[0]
    nc = t // SSD_Q
    sq = (SSD_Q, LANES)

    def body(xbc_ref, dtr_ref, bias_ref, alog_ref, dsk_ref, hs_ref, dy_ref,
             dxbc_ref, ddtr_ref, dbias_ref, dalog_ref, ddsk_ref, dh_ref):
        first = pl.program_id(0) == 0

        @pl.when(first)
        def _():
            dh_ref[...] = jnp.zeros_like(dh_ref)
            dbias_ref[...] = jnp.zeros_like(dbias_ref)
            dalog_ref[...] = jnp.zeros_like(dalog_ref)
            ddsk_ref[...] = jnp.zeros_like(ddsk_ref)

        pre, dt, a, tril, acs, acs_t = _ssd_common(dtr_ref, bias_ref, alog_ref)
        lane = _iota(sq, 1)
        sub = _iota(sq, 0)
        causal = sub >= lane
        d_acs = jnp.zeros(sq, F32)
        d_acs_row = jnp.zeros(sq, F32)
        d_last = jnp.zeros((1, LANES), F32)
        ddt = jnp.zeros(sq, F32)
        for g in range(SSD_G):
            bcols = slice(SSD_TOK + g * SSD_N, SSD_TOK + (g + 1) * SSD_N)
            ccols = slice(SSD_TOK + SSD_G * SSD_N + g * SSD_N, SSD_TOK + SSD_G * SSD_N + (g + 1) * SSD_N)
            bg_f = xbc_ref[:, bcols]
            bg = bg_f.astype(BF16)
            cg = xbc_ref[:, ccols].astype(BF16)
            cb = _dot(cg, bg, NT_DIMS)
            dcb = jnp.zeros(sq, F32)
            dbg = jnp.zeros(sq, F32)
            dcg = jnp.zeros(sq, F32)
            for pr in range(SSD_HPG // 2):
                h0 = g * SSD_HPG + 2 * pr
                lanes = slice(h0 * SSD_P, (h0 + 2) * SSD_P)
                heads, dtl, scale, dec_last = _pair_terms(dt, acs, acs_t, h0)
                xs = xbc_ref[:, lanes]
                xdt = xs * dtl
                dyv = dy_ref[:, lanes]
                hp = hs_ref[:, lanes]
                dhn = dh_ref[:, lanes]
                hp_b = hp.astype(BF16)
                dys = (dyv * scale).astype(BF16)
                yoff_dy = dyv * _dot(cg, hp_b) * scale
                dcg = dcg + _dot(dys, hp_b, NT_DIMS)
                dhc = _dot(cg, dys, TN_DIMS)
                hh = dhn * hp
                dxdt = jnp.zeros(sq, F32)
                for h, a_col, a_row, a_last, mask in heads:
                    dec = _decay(a_col, a_row)
                    m = cb * dec
                    dym = jnp.where(mask, dyv, 0.0).astype(BF16)
                    xm = jnp.where(mask, xdt, 0.0).astype(BF16)
                    dhm = jnp.where(mask, dhn, 0.0).astype(BF16)
                    w = jnp.exp(a_last - a_col)
                    dxdt = dxdt + _dot(m.astype(BF16), dym, TN_DIMS) + _dot((bg_f * w).astype(BF16), dhm)
                    dm = jnp.where(causal, _dot(dym, xm, NT_DIMS), 0.0)
                    dcb = dcb + dm * dec
                    e = dm * m
                    gj = _dot(xm, dhm, NT_DIMS)
                    dbg = dbg + w * gj
                    wdw = w * jnp.sum(bg_f * gj, axis=1, keepdims=True)
                    col = (jnp.sum(e, axis=1, keepdims=True)
                           + jnp.sum(jnp.where(mask, yoff_dy, 0.0), axis=1, keepdims=True) - wdw)
                    d_acs = d_acs + jnp.where(lane == h, col, 0.0)
                    d_acs_row = d_acs_row + jnp.where(sub == h, jnp.sum(e, axis=0, keepdims=True), 0.0)
                    last = jnp.sum(wdw) + jnp.exp(a_last) * jnp.sum(jnp.where(mask, hh, 0.0))
                    d_last = d_last + jnp.where(lane[:1] == h, last, 0.0)
                dxbc_ref[:, lanes] = dxdt * dtl + dsk_ref[:, lanes] * dyv
                tt = dxdt * xs
                for h, _, _, _, mask in heads:
                    ddt = ddt + jnp.where(lane == h, jnp.sum(jnp.where(mask, tt, 0.0), axis=1, keepdims=True), 0.0)
                ddsk_ref[:, lanes] += jnp.sum(dyv * xs, axis=0, keepdims=True)
                dh_ref[:, lanes] = dhn * dec_last + dhc
            dcb_b = dcb.astype(BF16)
            dxbc_ref[:, bcols] = dbg + _dot(dcb_b, cg, TN_DIMS)
            dxbc_ref[:, ccols] = dcg + _dot(dcb_b, bg)
        d_tot = d_acs - d_acs_row.T + jnp.where(sub == SSD_Q - 1, d_last, 0.0)
        ddta = lax.dot_general(tril, d_tot, TN_DIMS, precision=HIGHEST, preferred_element_type=F32)
        ddt = ddt + ddta * a
        dalog_ref[...] += jnp.sum(ddta * dt, axis=0, keepdims=True) * a
        ddtr = ddt * jax.nn.sigmoid(pre)
        ddtr_ref[...] = ddtr
        dbias_ref[...] += jnp.sum(ddtr, axis=0, keepdims=True)

    rev = lambda w: pl.BlockSpec((SSD_Q, w), lambda i: (nc - 1 - i, 0))
    vec = lambda w: pl.BlockSpec((1, w), lambda i: (0, 0))
    return pl.pallas_call(
        body, name=name, grid=(nc,),
        in_specs=[rev(SSD_XBC), rev(LANES), vec(LANES), vec(LANES), vec(SSD_TOK), rev(SSD_TOK), rev(SSD_TOK)],
        out_specs=[rev(SSD_XBC), rev(LANES), vec(LANES), vec(LANES), vec(SSD_TOK)],
        out_shape=[jax.ShapeDtypeStruct((t, SSD_XBC), F32), jax.ShapeDtypeStruct((t, LANES), F32),
                   jax.ShapeDtypeStruct((1, LANES), F32), jax.ShapeDtypeStruct((1, LANES), F32),
                   jax.ShapeDtypeStruct((1, SSD_TOK), F32)],
        scratch_shapes=[pltpu.VMEM((SSD_N, SSD_TOK), F32)],
        compiler_params=_params(("arbitrary",)),
    )(xbc, dt_raw, dt_bias, a_log, dskip_lane, hs, dy)


ATT_E = 128
ATT_H = 24
ATT_W = 128
ATT_TOK = ATT_H * ATT_E
DILATED_GROUPS = ((128, 1), (512, 4), (2048, 16))
N_DIL = len(DILATED_GROUPS)
ALIBI_MAX_EXP = 8.0
MASKED = -1e30


def _alibi_slopes(group):
    n = N_DIL * ATT_H
    return [2.0 ** (-ALIBI_MAX_EXP * (group * ATT_H + h + 1) / n) for h in range(ATT_H)]


def _att_scores(qh, kk, rel, valid, slope_d):
    s = _dot(qh, kk, NT_DIMS) * (ATT_E ** -0.5) - slope_d * rel
    return jnp.where(valid, s, MASKED)


def _att_rel(j):
    shp = (ATT_W, 2 * ATT_W)
    kpos = _iota(shp, 1)
    rel = _iota(shp, 0) + ATT_W - kpos
    valid = (rel >= 0) & (rel <= ATT_W) & ((kpos >= ATT_W) | (j > 0))
    return rel.astype(F32), valid


def _dil_fwd(q, k, v, group, name):
    t = q.shape[0]
    dil = DILATED_GROUPS[group][1]
    slopes = _alibi_slopes(group)
    ns = t // dil
    nb = ns // ATT_W
    view = lambda arr: arr.reshape(ns, dil * arr.shape[1])

    def body(q_ref, kp_ref, kc_ref, vp_ref, vc_ref, o_ref, lse_ref):
        rel, valid = _att_rel(pl.program_id(1))
        lane = _iota((ATT_W, LANES), 1)
        lse_all = jnp.zeros((ATT_W, LANES), F32)
        for h in range(ATT_H):
            cols = slice(h * ATT_E, (h + 1) * ATT_E)
            kk = jnp.concatenate([kp_ref[:, cols], kc_ref[:, cols]], axis=0)
            vv = jnp.concatenate([vp_ref[:, cols], vc_ref[:, cols]], axis=0)
            s = _att_scores(q_ref[:, cols], kk, rel, valid, slopes[h] * dil)
            m = jnp.max(s, axis=-1, keepdims=True)
            p = jnp.exp(s - m)
            den = jnp.sum(p, axis=-1, keepdims=True)
            o_ref[:, cols] = _dot(p.astype(BF16), vv) / den
            lse_all = jnp.where(lane == h, m + jnp.log(den), lse_all)
        lse_ref[...] = lse_all

    cur = pl.BlockSpec((ATT_W, ATT_TOK), lambda r, j: (j, r))
    prev = pl.BlockSpec((ATT_W, ATT_TOK), lambda r, j: (jnp.maximum(j - 1, 0), r))
    small = pl.BlockSpec((ATT_W, LANES), lambda r, j: (j, r))
    o, lse = pl.pallas_call(
        body, name=name, grid=(dil, nb),
        in_specs=[cur, prev, cur, prev, cur], out_specs=[cur, small],
        out_shape=[jax.ShapeDtypeStruct((ns, dil * ATT_TOK), F32), jax.ShapeDtypeStruct((ns, dil * LANES), F32)],
        compiler_params=_params(("parallel", "parallel")),
    )(view(q), view(k), view(k), view(v), view(v))
    return o.reshape(t, ATT_TOK), lse.reshape(t, LANES)


def _dil_bwd(q, k, v, dy, wgt, cterm, lse, group, name):
    t = q.shape[0]
    dil = DILATED_GROUPS[group][1]
    slopes = _alibi_slopes(group)
    ns = t // dil
    nb = ns // ATT_W
    view = lambda arr: arr.reshape(ns, dil * arr.shape[1])

    def body(q_ref, kp_ref, kc_ref, vp_ref, vc_ref, dy_ref, w_ref, c_ref, lse_ref,
             dq_ref, dk_ref, dv_ref, ck_ref, cv_ref):
        j = pl.program_id(1)

        @pl.when(j == 0)
        def _():
            ck_ref[...] = jnp.zeros_like(ck_ref)
            cv_ref[...] = jnp.zeros_like(cv_ref)

        @pl.when(j < nb)
        def _():
            rel, valid = _att_rel(j)
            wv, cv_, lv = w_ref[...], c_ref[...], lse_ref[...]
            for h in range(ATT_H):
                cols = slice(h * ATT_E, (h + 1) * ATT_E)
                qh = q_ref[:, cols]
                kk = jnp.concatenate([kp_ref[:, cols], kc_ref[:, cols]], axis=0)
                vv = jnp.concatenate([vp_ref[:, cols], vc_ref[:, cols]], axis=0)
                s = _att_scores(qh, kk, rel, valid, slopes[h] * dil)
                p = jnp.where(valid, jnp.exp(s - _col(lv, h)), 0.0)
                do = (dy_ref[:, cols] * _col(wv, h)).astype(BF16)
                dp = _dot(do, vv, NT_DIMS)
                ds = (p * (dp + _col(cv_, h)) * (ATT_E ** -0.5)).astype(BF16)
                dq_ref[:, cols] = _dot(ds, kk).astype(dq_ref.dtype)
                dkk = _dot(ds, qh, TN_DIMS)
                dvv = _dot(p.astype(BF16), do, TN_DIMS)
                dk_ref[:, cols] = (ck_ref[:, cols] + dkk[:ATT_W]).astype(dk_ref.dtype)
                dv_ref[:, cols] = (cv_ref[:, cols] + dvv[:ATT_W]).astype(dv_ref.dtype)
                ck_ref[:, cols] = dkk[ATT_W:]
                cv_ref[:, cols] = dvv[ATT_W:]

        @pl.when(j == nb)
        def _():
            dk_ref[...] = ck_ref[...].astype(dk_ref.dtype)
            dv_ref[...] = cv_ref[...].astype(dv_ref.dtype)

    jq = lambda j: jnp.minimum(j, nb - 1)
    cur = pl.BlockSpec((ATT_W, ATT_TOK), lambda r, j: (jq(j), r))
    prev = pl.BlockSpec((ATT_W, ATT_TOK), lambda r, j: (jnp.maximum(jq(j) - 1, 0), r))
    small = pl.BlockSpec((ATT_W, LANES), lambda r, j: (jq(j), r))
    late = pl.BlockSpec((ATT_W, ATT_TOK), lambda r, j: (jnp.maximum(j - 1, 0), r))
    big = jax.ShapeDtypeStruct((ns, dil * ATT_TOK), BF16)
    dq, dk, dv = pl.pallas_call(
        body, name=name, grid=(dil, nb + 1),
        in_specs=[cur, prev, cur, prev, cur, cur, small, small, small], out_specs=[cur, late, late],
        out_shape=[big, big, big],
        scratch_shapes=[pltpu.VMEM((ATT_W, ATT_TOK), F32), pltpu.VMEM((ATT_W, ATT_TOK), F32)],
        compiler_params=_params(("parallel", "arbitrary")),
    )(view(q), view(k), view(k), view(v), view(v), view(dy), view(wgt), view(cterm), view(lse))
    return dq.reshape(t, ATT_TOK), dk.reshape(t, ATT_TOK), dv.reshape(t, ATT_TOK)


def _combine_weights(lses):
    m = functools.reduce(jnp.maximum, lses)
    es = [jnp.exp(l - m) for l in lses]
    tot = functools.reduce(lambda a, b: a + b, es)
    return [e / tot for e in es]


def _combine_fwd(outs, lses, name):
    t = outs[0].shape[0]
    tm = _pick(t, (256, 128))

    def body(*refs):
        o_refs, l_refs, y_ref = refs[:N_DIL], refs[N_DIL:2 * N_DIL], refs[2 * N_DIL]
        ws = _combine_weights([r[...] for r in l_refs])
        for h in range(ATT_H):
            cols = slice(h * ATT_E, (h + 1) * ATT_E)
            y_ref[:, cols] = sum(_col(ws[g], h) * o_refs[g][:, cols] for g in range(N_DIL))

    big = pl.BlockSpec((tm, ATT_TOK), lambda i: (i, 0))
    small = pl.BlockSpec((tm, LANES), lambda i: (i, 0))
    return pl.pallas_call(
        body, name=name, grid=(t // tm,), in_specs=[big] * N_DIL + [small] * N_DIL, out_specs=big,
        out_shape=jax.ShapeDtypeStruct((t, ATT_TOK), F32),
        compiler_params=_params(("parallel",)),
    )(*outs, *lses)


def _combine_bwd(outs, lses, dy, name):
    t = outs[0].shape[0]
    tm = _pick(t, (256, 128))

    def body(*refs):
        o_refs, l_refs, dy_ref = refs[:N_DIL], refs[N_DIL:2 * N_DIL], refs[2 * N_DIL]
        w_refs, c_refs = refs[2 * N_DIL + 1:3 * N_DIL + 1], refs[3 * N_DIL + 1:]
        ws = _combine_weights([r[...] for r in l_refs])
        lane = _iota((tm, LANES), 1)
        sdw = jnp.zeros((tm, LANES), F32)
        for h in range(ATT_H):
            cols = slice(h * ATT_E, (h + 1) * ATT_E)
            dyh = dy_ref[:, cols]
            tot = sum(_col(ws[g], h) * jnp.sum(dyh * o_refs[g][:, cols], axis=1, keepdims=True) for g in range(N_DIL))
            sdw = jnp.where(lane == h, tot, sdw)
        for g in range(N_DIL):
            w_refs[g][...] = ws[g]
            c_refs[g][...] = -ws[g] * sdw

    big = pl.BlockSpec((tm, ATT_TOK), lambda i: (i, 0))
    small = pl.BlockSpec((tm, LANES), lambda i: (i, 0))
    res = pl.pallas_call(
        body, name=name, grid=(t // tm,), in_specs=[big] * N_DIL + [small] * N_DIL + [big],
        out_specs=[small] * (2 * N_DIL),
        out_shape=[jax.ShapeDtypeStruct((t, LANES), F32)] * (2 * N_DIL),
        compiler_params=_params(("parallel",)),
    )(*outs, *lses, dy)
    return res[:N_DIL], res[N_DIL:]


ADAM_LR, ADAM_B1, ADAM_B2, ADAM_EPS, ADAM_WD, ADAM_STEP = 0.001, 0.9, 0.999, 1e-08, 0.01, 10


def _adamw(parts, w, m, v, name):
    r, c = w.shape
    n_parts = parts.shape[0]
    tc = _pick(c, (1024, 512, 256, 128)) if c % 128 == 0 else c
    tm = _pick(r, (128, 64, 32, 16, 8))

    def body(p_ref, w_ref, m_ref, v_ref, g_ref, d_ref, nm_ref, nv_ref):
        g = p_ref[0].astype(F32)
        for k in range(1, n_parts):
            g = g + p_ref[k].astype(F32)
        nm = ADAM_B1 * m_ref[...] + (1.0 - ADAM_B1) * g
        nv = ADAM_B2 * v_ref[...] + (1.0 - ADAM_B2) * (g * g)
        m_hat = nm / (1.0 - ADAM_B1 ** ADAM_STEP)
        v_hat = nv / (1.0 - ADAM_B2 ** ADAM_STEP)
        g_ref[...] = g
        d_ref[...] = -ADAM_LR * (m_hat / (jnp.sqrt(v_hat) + ADAM_EPS) + ADAM_WD * w_ref[...])
        nm_ref[...] = nm
        nv_ref[...] = nv

    blk = pl.BlockSpec((tm, tc), lambda i, j: (i, j))
    pblk = pl.BlockSpec((n_parts, tm, tc), lambda i, j: (0, i, j))
    return pl.pallas_call(
        body, name=name, grid=(r // tm, c // tc), in_specs=[pblk, blk, blk, blk], out_specs=[blk] * 4,
        out_shape=[jax.ShapeDtypeStruct((r, c), F32)] * 4,
        compiler_params=_params(("parallel", "parallel")),
    )(parts, w, m, v)


N_CHIP = 4
MESH_ID = pl.DeviceIdType.MESH


def _other_chips(x, y):
    return [(1 - x, y), (x, 1 - y), (1 - x, 1 - y)]


def _gather_two_level(arrays, name):
    n = len(arrays)
    per = N_DEV - 1

    def body(*refs):
        in_refs, out_refs = refs[:n], refs[n:2 * n]
        send_sems, recv_sems, local_sems = refs[2 * n:]
        x, y, c = lax.axis_index("x"), lax.axis_index("y"), lax.axis_index("c")
        sibling = (x, y, 1 - c)
        chips = _other_chips(x, y)

        def copy(a, k, block, to, src=None):
            rows = out_refs[a].at[4 * block[0] + 2 * block[1] + block[2]]
            return pltpu.make_async_remote_copy(
                src_ref=rows if src is None else src, dst_ref=rows,
                send_sem=send_sems.at[a * per + k], recv_sem=recv_sems.at[a * per + k],
                device_id=to, device_id_type=MESH_ID)

        started = []
        for a in range(n):
            local = pltpu.make_async_copy(in_refs[a], out_refs[a].at[4 * x + 2 * y + c], local_sems.at[a])
            local.start()
            started.append(local)
        sends = []
        for j, chip in enumerate(chips):
            for a in range(n):
                sends.append(copy(a, 1 + j, (x, y, c), (*chip, c), src=in_refs[a]))
                sends[-1].start()
        for a in range(n):
            sends.append(copy(a, 0, (x, y, c), sibling, src=in_refs[a]))
            sends[-1].start()
        for j, chip in enumerate(chips):
            for a in range(n):
                copy(a, 1 + j, (*chip, c), (x, y, c)).wait_recv()
                sends.append(copy(a, 4 + j, (*chip, c), sibling))
                sends[-1].start()
        for a in range(n):
            copy(a, 0, sibling, (x, y, c)).wait_recv()
            for j, chip in enumerate(chips):
                copy(a, 4 + j, (*chip, 1 - c), (x, y, c)).wait_recv()
        for cp in sends:
            cp.wait_send()
        for local in started:
            local.wait()

    any_spec = pl.BlockSpec(memory_space=pl.ANY)
    return pl.pallas_call(
        body, name=name, in_specs=[any_spec] * n, out_specs=[any_spec] * n,
        out_shape=[jax.ShapeDtypeStruct((N_DEV,) + a.shape, a.dtype) for a in arrays],
        scratch_shapes=[pltpu.SemaphoreType.DMA((n * per,)), pltpu.SemaphoreType.DMA((n * per,)),
                        pltpu.SemaphoreType.DMA((n,))],
        compiler_params=pltpu.CompilerParams(has_side_effects=True),
    )(*arrays)


def _sibling_swap(parts, name):
    n = len(parts)

    def body(*refs):
        in_refs, out_refs = refs[:n], refs[n:2 * n]
        send_sems, recv_sems = refs[2 * n:]
        x, y, c = lax.axis_index("x"), lax.axis_index("y"), lax.axis_index("c")
        sends = []
        for a in range(n):
            for q in range(N_CHIP):
                cp = pltpu.make_async_remote_copy(
                    src_ref=in_refs[a].at[2 * q + 1 - c], dst_ref=out_refs[a].at[q],
                    send_sem=send_sems.at[a * N_CHIP + q], recv_sem=recv_sems.at[a * N_CHIP + q],
                    device_id=(x, y, 1 - c), device_id_type=MESH_ID)
                cp.start()
                sends.append(cp)
        for cp in sends:
            cp.wait_recv()
        for cp in sends:
            cp.wait_send()

    any_spec = pl.BlockSpec(memory_space=pl.ANY)
    return pl.pallas_call(
        body, name=name, in_specs=[any_spec] * n, out_specs=[any_spec] * n,
        out_shape=[jax.ShapeDtypeStruct((N_CHIP,) + p.shape[1:], p.dtype) for p in parts],
        scratch_shapes=[pltpu.SemaphoreType.DMA((n * N_CHIP,)), pltpu.SemaphoreType.DMA((n * N_CHIP,))],
        compiler_params=pltpu.CompilerParams(has_side_effects=True),
    )(*parts)


def _chip_sum(part, landed, core, name):
    _, r, c = part.shape
    tc = _pick(c, (1024, 512, 256, 128)) if c % 128 == 0 else c
    tm = _pick(r, (256, 128, 64, 32, 16, 8))

    def body(core_ref, p_ref, l_ref, o_ref):
        o_ref[...] = (p_ref[...].astype(F32) + l_ref[...].astype(F32)).astype(o_ref.dtype)

    grid_spec = pltpu.PrefetchScalarGridSpec(
        num_scalar_prefetch=1, grid=(N_CHIP, r // tm, c // tc),
        in_specs=[pl.BlockSpec((None, tm, tc), lambda q, i, j, core_ref: (2 * q + core_ref[0], i, j)),
                  pl.BlockSpec((None, tm, tc), lambda q, i, j, core_ref: (q, i, j))],
        out_specs=pl.BlockSpec((None, tm, tc), lambda q, i, j, core_ref: (q, i, j)))
    return pl.pallas_call(
        body, name=name, grid_spec=grid_spec, out_shape=jax.ShapeDtypeStruct(landed.shape, landed.dtype),
        compiler_params=_params(("parallel", "parallel", "parallel")),
    )(core, part, landed)


def _chip_exchange(sums, name):
    n = len(sums)
    per = N_CHIP - 1

    def body(*refs):
        in_refs, out_refs = refs[:n], refs[n:2 * n]
        send_sems, recv_sems, local_sems = refs[2 * n:]
        x, y, c = lax.axis_index("x"), lax.axis_index("y"), lax.axis_index("c")
        mine = 2 * x + y
        started = []
        for a in range(n):
            local = pltpu.make_async_copy(in_refs[a].at[mine], out_refs[a].at[mine], local_sems.at[a])
            local.start()
            started.append(local)
        sends = []
        for j, (px, py) in enumerate(_other_chips(x, y)):
            for a in range(n):
                cp = pltpu.make_async_remote_copy(
                    src_ref=in_refs[a].at[2 * px + py], dst_ref=out_refs[a].at[mine],
                    send_sem=send_sems.at[a * per + j], recv_sem=recv_sems.at[a * per + j],
                    device_id=(px, py, c), device_id_type=MESH_ID)
                cp.start()
                sends.append((cp, a, j, 2 * px + py))
        for cp, a, j, peer in sends:
            pltpu.make_async_remote_copy(
                src_ref=out_refs[a].at[peer], dst_ref=out_refs[a].at[peer],
                send_sem=send_sems.at[a * per + j], recv_sem=recv_sems.at[a * per + j],
                device_id=(x, y, c), device_id_type=MESH_ID).wait_recv()
        for cp, _, _, _ in sends:
            cp.wait_send()
        for local in started:
            local.wait()

    any_spec = pl.BlockSpec(memory_space=pl.ANY)
    return pl.pallas_call(
        body, name=name, in_specs=[any_spec] * n, out_specs=[any_spec] * n,
        out_shape=[jax.ShapeDtypeStruct(s.shape, s.dtype) for s in sums],
        scratch_shapes=[pltpu.SemaphoreType.DMA((n * per,)), pltpu.SemaphoreType.DMA((n * per,)),
                        pltpu.SemaphoreType.DMA((n,))],
        compiler_params=pltpu.CompilerParams(has_side_effects=True),
    )(*sums)


DEPTH = 4
MEM_W = 1024
MIX_W = SSD_TOK + MEM_W
DT_PAD = LANES - SSD_H


def _is_ssd(i):
    return i % 2 == 0


def _weight_names():
    names = ["mem_norm_g", "final_norm_g"]
    for i in range(DEPTH):
        names += [f"norm_g_{i}", f"w_in_{i}"]
        if _is_ssd(i):
            names += [f"conv_w_{i}", f"conv_b_{i}", f"dt_bias_{i}", f"a_log_{i}", f"d_skip_{i}", f"ssd_norm_g_{i}"]
        names += [f"w_mem_kv_{i}", f"w_out_{i}"]
    return names


WEIGHTS = _weight_names()
INPUTS = ["x", "mem"] + WEIGHTS + ["loss_target"] + ["m_" + n for n in WEIGHTS] + ["v_" + n for n in WEIGHTS]


def _in_segments(i):
    if _is_ssd(i):
        return [("xbc", 0, SSD_XBC), ("dt", SSD_XBC, SSD_H), ("qm", SSD_XBC + SSD_H, MEM_W),
                ("z", SSD_XBC + SSD_H + MEM_W, MIX_W)]
    segs = []
    for g in range(N_DIL):
        for j, nm in enumerate("qkv"):
            segs.append((f"{nm}{g}", (3 * g + j) * ATT_TOK, ATT_TOK))
    segs += [("qm", 3 * N_DIL * ATT_TOK, MEM_W), ("z", 3 * N_DIL * ATT_TOK + MEM_W, MIX_W)]
    return segs


def _split_w_in(i, w_in):
    out = {}
    for nm, start, width in _in_segments(i):
        seg = w_in[:, start:start + width]
        out[nm] = jnp.pad(seg, ((0, 0), (0, DT_PAD))) if nm == "dt" else seg
    return out


def _join_dw_in(i, dws):
    return jnp.concatenate([dws[nm][:, :width] for nm, _, width in _in_segments(i)], axis=1)


SEG_DTYPE = {"xbc": F32, "dt": F32, "z": F32}


def _layer_fwd(i, x, mem_b, p):
    tag = f"l{i}"
    h = _rmsnorm_fwd(x, p["norm_g"], tag + "_norm")
    proj = {nm: _matmul(h, w, out_dtype=SEG_DTYPE.get(nm, BF16), name=f"{tag}_in_{nm}") for nm, w in p["win"].items()}
    sv = {"x": x, "h": h, "proj": proj}
    if _is_ssd(i):
        xbc = _conv_fwd(proj["xbc"], p["conv_w"], p["conv_b"], tag + "_conv")
        y_tok, hs = _ssd_fwd(xbc, proj["dt"], p["dt_bias_p"], p["a_log_p"], p["dskip_lane"], tag + "_ssd")
        sv.update(xbc=xbc, hs=hs)
    else:
        outs, lses = [], []
        for g in range(N_DIL):
            o, lse = _dil_fwd(proj[f"q{g}"], proj[f"k{g}"], proj[f"v{g}"], g, f"{tag}_att{g}")
            outs.append(o)
            lses.append(lse)
        y_tok = _combine_fwd(outs, lses, tag + "_comb")
        sv.update(outs=outs, lses=lses)
    mkv = _matmul(mem_b, p["wmkv"], out_dtype=BF16, name=tag + "_mkv")
    y_mem = _memattn_fwd(proj["qm"], mkv, tag + "_mem")
    gated = _gate_fwd(y_tok, y_mem, proj["z"], p.get("ssd_norm_g"), tag + "_gate")
    x_out = _matmul(gated, p["wout"], out_dtype=F32, add=x, name=tag + "_out")
    sv.update(y_tok=y_tok, y_mem=y_mem, mkv=mkv, gated=gated)
    return x_out, sv


def _layer_bwd(i, sv, dx_out, dmem_n, mem_b, p):
    tag = f"l{i}b"
    proj = sv["proj"]
    gr = {}
    dgated = _matmul(dx_out, p["wout"], tb=True, out_dtype=F32, name=tag + "_dgated")
    gr["w_out"] = _matmul(sv["gated"], dx_out, ta=True, out_dtype=BF16, name=tag + "_dwout")
    dy_tok, dy_mem, dz, dssd_g = _gate_bwd(sv["y_tok"], sv["y_mem"], proj["z"], p.get("ssd_norm_g"), dgated, tag + "_gate")
    dq_mem, dmkv = _memattn_bwd(proj["qm"], sv["mkv"], dy_mem, tag + "_mem")
    gr["w_mem_kv"] = _matmul(mem_b, dmkv, ta=True, out_dtype=BF16, name=tag + "_dwmkv")
    dmem_n = _matmul(dmkv, p["wmkv"], tb=True, out_dtype=F32, add=dmem_n, name=tag + "_dmem")
    dproj = {"qm": dq_mem, "z": dz}
    if _is_ssd(i):
        dxbc, ddt_raw, dbias, dalog, ddsk = _ssd_bwd(sv["xbc"], proj["dt"], p["dt_bias_p"], p["a_log_p"], p["dskip_lane"],
                                                     sv["hs"], dy_tok, tag + "_ssd")
        dpre, dconv_w, dconv_b = _conv_bwd_pre(proj["xbc"], p["conv_w"], p["conv_b"], dxbc, tag + "_convpre")
        dproj["xbc"] = _conv_bwd_in(dpre, p["conv_w"], tag + "_convin")
        dproj["dt"] = ddt_raw
        gr.update(conv_w=dconv_w, conv_b=dconv_b[0], dt_bias=dbias[0, :SSD_H], a_log=dalog[0, :SSD_H],
                  d_skip=jnp.sum(ddsk.reshape(SSD_H, SSD_P), axis=1), ssd_norm_g=dssd_g[0])
    else:
        ws, cs = _combine_bwd(sv["outs"], sv["lses"], dy_tok, tag + "_comb")
        for g in range(N_DIL):
            dq, dk, dv = _dil_bwd(proj[f"q{g}"], proj[f"k{g}"], proj[f"v{g}"], dy_tok, ws[g], cs[g], sv["lses"][g], g,
                                  f"{tag}_att{g}")
            dproj.update({f"q{g}": dq, f"k{g}": dk, f"v{g}": dv})
    dh = None
    dws = {}
    for nm, w in p["win"].items():
        dh = _matmul(dproj[nm], w, tb=True, out_dtype=F32, add=dh, name=f"{tag}_dh_{nm}")
        dws[nm] = _matmul(sv["h"], dproj[nm], ta=True, out_dtype=BF16, name=f"{tag}_dw_{nm}")
    gr["w_in"] = _join_dw_in(i, dws)
    dx, dnorm_g = _rmsnorm_bwd(sv["x"], p["norm_g"], dh, dx_out, tag + "_norm")
    gr["norm_g"] = dnorm_g[0]
    return dx, dmem_n, gr


def _pad_heads(v):
    return jnp.pad(v.reshape(1, SSD_H), ((0, 0), (0, DT_PAD)))


def _layer_params(i, small, w_in, w_mem_kv, w_out):
    p = {"norm_g": small[f"norm_g_{i}"], "win": _split_w_in(i, w_in), "wmkv": w_mem_kv, "wout": w_out}
    if _is_ssd(i):
        p.update(conv_w=small[f"conv_w_{i}"], conv_b=small[f"conv_b_{i}"], ssd_norm_g=small[f"ssd_norm_g_{i}"],
                 dt_bias_p=_pad_heads(small[f"dt_bias_{i}"]), a_log_p=_pad_heads(small[f"a_log_{i}"]),
                 dskip_lane=jnp.repeat(small[f"d_skip_{i}"], SSD_P).reshape(1, SSD_TOK))
    return p


def _local_step(x, mem, target, small, big):
    params = [_layer_params(i, small, *big[i]) for i in range(DEPTH)]
    mem_b = _rmsnorm_fwd(mem, small["mem_norm_g"], "mem_norm")
    saved = []
    for i in range(DEPTH):
        x, sv = _layer_fwd(i, x, mem_b, params[i])
        saved.append(sv)
    loss, dx, dfinal = _final_loss(x, small["final_norm_g"], target)
    grads = {"final_norm_g": dfinal[0]}
    dmem_n = None
    for i in reversed(range(DEPTH)):
        dx, dmem_n, gr = _layer_bwd(i, saved[i], dx, dmem_n, mem_b, params[i])
        grads.update({f"{nm}_{i}": g for nm, g in gr.items()})
    _, dmem_g = _rmsnorm_bwd(mem, small["mem_norm_g"], dmem_n, None, "mem_norm_b")
    grads["mem_norm_g"] = dmem_g[0]
    return loss[0, 0], dx, grads


BIG = ("w_in", "w_mem_kv", "w_out")
SMALL = [n for n in WEIGHTS if not n.startswith(BIG)]
PACK_ROWS = 8 * LANES


def _pack(vals):
    flat = jnp.concatenate([v.reshape(-1).astype(F32) for v in vals])
    padded = -(-flat.shape[0] // PACK_ROWS) * PACK_ROWS
    return jnp.pad(flat, (0, padded - flat.shape[0])).reshape(padded // LANES, LANES)


def _train_step(a, local_step):
    x, y, c = lax.axis_index("x"), lax.axis_index("y"), lax.axis_index("c")
    me = 4 * x + 2 * y + c
    big = []
    for i in range(DEPTH):
        shards = [a[f"{nm}_{i}"].astype(BF16) for nm in BIG]
        g_in, g_kv, g_out = _gather_two_level(shards, f"gather_w{i}")
        d, cs = shards[0].shape
        big.append((jnp.transpose(g_in, (1, 0, 2)).reshape(d, N_DEV * cs),
                    g_kv.reshape(N_DEV * g_kv.shape[1], g_kv.shape[2]),
                    g_out.reshape(N_DEV * g_out.shape[1], g_out.shape[2])))
    conv_names = [n for n in SMALL if n.startswith("conv_w")]
    conv_full = _gather_two_level([a[n] for n in conv_names], "gather_conv")
    small = {n: a[n] for n in SMALL}
    for n, gathered in zip(conv_names, conv_full):
        small[n] = jnp.transpose(gathered, (1, 0, 2)).reshape(gathered.shape[1], N_DEV * gathered.shape[2])

    loss_local, grad_x, grads = local_step(a["x"][0], a["mem"][0], a["loss_target"][0], small, big)
    loss = lax.psum(loss_local, ("x", "y", "c"))

    res = {}
    core = c.astype(jnp.int32).reshape(1)
    for i in range(DEPTH):
        g_in = grads[f"w_in_{i}"]
        d, cols = g_in.shape
        parts = [jnp.transpose(g_in.reshape(d, N_DEV, cols // N_DEV), (1, 0, 2))]
        for nm in BIG[1:]:
            g = grads[f"{nm}_{i}"]
            parts.append(g.reshape(N_DEV, g.shape[0] // N_DEV, g.shape[1]))
        swapped = _sibling_swap(parts, f"swap_w{i}")
        sums = [_chip_sum(p, s, core, f"chipsum_{nm}_{i}") for nm, p, s in zip(BIG, parts, swapped)]
        landed = _chip_exchange(sums, f"scatter_w{i}")
        for nm, p in zip(BIG, landed):
            n = f"{nm}_{i}"
            res[n] = _adamw(p, a[n], a["m_" + n], a["v_" + n], "adamw_" + n)

    gathered = _gather_two_level([_pack([grads[n] for n in SMALL])], "gather_small")[0]
    zero_conv = lambda pre: [jnp.zeros(small[n].shape, F32) if n in conv_names else a[pre + n] for n in SMALL]
    packed = _adamw(gathered, _pack(zero_conv("")), _pack(zero_conv("m_")), _pack(zero_conv("v_")), "adamw_small")
    off = 0
    for n in SMALL:
        size = math.prod(small[n].shape)
        if n in conv_names:
            rows, cols = a[n].shape
            whole = gathered.reshape(N_DEV, -1)[:, off:off + size].reshape(N_DEV, rows, N_DEV * cols)
            mine = lax.dynamic_slice_in_dim(whole, me * cols, cols, axis=2)
            res[n] = _adamw(mine, a[n], a["m_" + n], a["v_" + n], "adamw_" + n)
        else:
            res[n] = [o.reshape(-1)[off:off + size].reshape(a[n].shape) for o in packed]
        off += size
    outs = [loss, grad_x[None]]
    for k in range(4):
        outs += [res[n][k] for n in WEIGHTS]
    return tuple(outs)


def kernel(x, mem, mem_norm_g, final_norm_g, norm_g_0, w_in_0, conv_w_0, conv_b_0, dt_bias_0, a_log_0, d_skip_0, ssd_norm_g_0, w_mem_kv_0, w_out_0, norm_g_1, w_in_1, w_mem_kv_1, w_out_1, norm_g_2, w_in_2, conv_w_2, conv_b_2, dt_bias_2, a_log_2, d_skip_2, ssd_norm_g_2, w_mem_kv_2, w_out_2, norm_g_3, w_in_3, w_mem_kv_3, w_out_3, loss_target, m_mem_norm_g, m_final_norm_g, m_norm_g_0, m_w_in_0, m_conv_w_0, m_conv_b_0, m_dt_bias_0, m_a_log_0, m_d_skip_0, m_ssd_norm_g_0, m_w_mem_kv_0, m_w_out_0, m_norm_g_1, m_w_in_1, m_w_mem_kv_1, m_w_out_1, m_norm_g_2, m_w_in_2, m_conv_w_2, m_conv_b_2, m_dt_bias_2, m_a_log_2, m_d_skip_2, m_ssd_norm_g_2, m_w_mem_kv_2, m_w_out_2, m_norm_g_3, m_w_in_3, m_w_mem_kv_3, m_w_out_3, v_mem_norm_g, v_final_norm_g, v_norm_g_0, v_w_in_0, v_conv_w_0, v_conv_b_0, v_dt_bias_0, v_a_log_0, v_d_skip_0, v_ssd_norm_g_0, v_w_mem_kv_0, v_w_out_0, v_norm_g_1, v_w_in_1, v_w_mem_kv_1, v_w_out_1, v_norm_g_2, v_w_in_2, v_conv_w_2, v_conv_b_2, v_dt_bias_2, v_a_log_2, v_d_skip_2, v_ssd_norm_g_2, v_w_mem_kv_2, v_w_out_2, v_norm_g_3, v_w_in_3, v_w_mem_kv_3, v_w_out_3):
    vals = locals()
    return _train_step({n: vals[n] for n in INPUTS}, _local_step)
```

```python
import functools
import math

import jax
import jax.numpy as jnp
from jax import lax
from jax.experimental import pallas as pl
from jax.experimental.pallas import tpu as pltpu

F32 = jnp.float32
BF16 = jnp.bfloat16
EPS = 1e-6
N_DEV = 8
VMEM_LIMIT_BYTES = 56 * 1024 * 1024


def _pick(n, prefs):
    for p in prefs:
        if n % p == 0:
            return p
    return n


def _params(sem):
    return pltpu.CompilerParams(dimension_semantics=sem, vmem_limit_bytes=VMEM_LIMIT_BYTES)


def _matmul(a, b, *, ta=False, tb=False, out_dtype=F32, add=None, name="mm"):
    if ta:
        k_dim, m_dim = a.shape
    else:
        m_dim, k_dim = a.shape
    n_dim = b.shape[0] if tb else b.shape[1]
    out_bytes = jnp.dtype(out_dtype).itemsize + (0 if add is None else add.dtype.itemsize)
    tm, tn, tk = _matmul_tiles(m_dim, n_dim, k_dim, a.dtype.itemsize, b.dtype.itemsize, out_bytes)
    nk = k_dim // tk
    dims = (((0,) if ta else (1,), (1,) if tb else (0,)), ((), ()))

    def body(*refs):
        if add is None:
            a_ref, b_ref, o_ref = refs[:3]
            add_ref = None
        else:
            a_ref, b_ref, add_ref, o_ref = refs[:4]
        part = lax.dot_general(a_ref[...].astype(BF16), b_ref[...].astype(BF16), dims,
                               preferred_element_type=F32)
        if nk == 1:
            o_ref[...] = (part if add_ref is None else part + add_ref[...].astype(F32)).astype(o_ref.dtype)
            return
        acc_ref = refs[-1]
        k = pl.program_id(2)

        @pl.when(k == 0)
        def _():
            acc_ref[...] = part if add_ref is None else part + add_ref[...].astype(F32)

        @pl.when(k > 0)
        def _():
            acc_ref[...] += part

        @pl.when(k == nk - 1)
        def _():
            o_ref[...] = acc_ref[...].astype(o_ref.dtype)

    a_spec = pl.BlockSpec((tk, tm), lambda i, j, k: (k, i)) if ta else pl.BlockSpec((tm, tk), lambda i, j, k: (i, k))
    b_spec = pl.BlockSpec((tn, tk), lambda i, j, k: (j, k)) if tb else pl.BlockSpec((tk, tn), lambda i, j, k: (k, j))
    o_spec = pl.BlockSpec((tm, tn), lambda i, j, k: (i, j))
    in_specs = [a_spec, b_spec] + ([o_spec] if add is not None else [])
    args = (a, b) + ((add,) if add is not None else ())
    return pl.pallas_call(
        body, name=name, grid=(m_dim // tm, n_dim // tn, nk),
        in_specs=in_specs, out_specs=o_spec,
        out_shape=jax.ShapeDtypeStruct((m_dim, n_dim), out_dtype),
        scratch_shapes=[pltpu.VMEM((tm, tn), F32)] if nk > 1 else [],
        compiler_params=_params(("parallel", "parallel", "arbitrary")),
    )(*args)


MATMUL_VMEM_BUDGET = 40 * 1024 * 1024


def _matmul_tiles(m_dim, n_dim, k_dim, a_bytes, b_bytes, out_bytes):
    best = None
    for tk in (k_dim, 4096, 2048, 1024, 512, 256, 128):
        if tk > k_dim or k_dim % tk:
            continue
        for tm in (1024, 512, 256, 128):
            if m_dim % tm:
                continue
            for tn in (2048, 1024, 512, 256, 128):
                if n_dim % tn:
                    continue
                vmem = 2 * (tm * tk * a_bytes + tk * tn * b_bytes + tm * tn * out_bytes) + 2 * tm * tn * 4
                if a_bytes == 4:
                    vmem += tm * tk * 2
                if vmem > MATMUL_VMEM_BUDGET:
                    continue
                score = (tm * tn * tk, tk, min(tm, tn))
                if best is None or score > best[0]:
                    best = (score, (tm, tn, tk))
    return best[1]


def _iota(shape, dim):
    return lax.broadcasted_iota(jnp.int32, shape, dim)


def _col(x, j):
    return jnp.sum(jnp.where(_iota(x.shape, 1) == j, x, 0.0), axis=1, keepdims=True)


def _silu(x):
    return x * jax.nn.sigmoid(x)


def _dsilu(x):
    s = jax.nn.sigmoid(x)
    return s * (1.0 + x * (1.0 - s))


def _chunks(width):
    return width // 128


def _scatter_rows(src_ref, nat_ref, dil, tm):
    n = tm // dil
    for cb in range(nat_ref.shape[0]):
        for r in range(dil):
            nat_ref[cb, pl.ds(r, n, stride=dil), :] = src_ref[r, :, cb * 128:(cb + 1) * 128].astype(F32)


def _gather_rows(nat_ref, dst_ref, dil, tm):
    n = tm // dil
    for cb in range(nat_ref.shape[0]):
        for r in range(dil):
            dst_ref[r, :, cb * 128:(cb + 1) * 128] = nat_ref[cb, pl.ds(r, n, stride=dil), :].astype(dst_ref.dtype)


def _load_chunks(nat_ref):
    return jnp.concatenate([nat_ref[cb] for cb in range(nat_ref.shape[0])], axis=1)


def _store_chunks(nat_ref, val):
    for cb in range(nat_ref.shape[0]):
        nat_ref[cb] = val[:, cb * 128:(cb + 1) * 128]


def _perm_spec(tm, dil, width):
    if dil == 1:
        return pl.BlockSpec((tm, width), lambda i: (i, 0))
    return pl.BlockSpec((dil, tm // dil, width), lambda i: (0, i, 0))


def _perm_shape(t, dil, width, dtype):
    return jax.ShapeDtypeStruct((t, width) if dil == 1 else (dil, t // dil, width), dtype)


def _rmsnorm_fwd(x, g, name, dils=(1,)):
    t, d = x.shape
    tm = _pick(t, (512, 256, 128))

    permuted = any(dil > 1 for dil in dils)

    def body(x_ref, g_ref, *refs):
        h_refs = refs[:len(dils)]
        xv = x_ref[...]
        rs = lax.rsqrt(jnp.mean(xv * xv, axis=-1, keepdims=True) + EPS)
        hv = xv * rs * g_ref[...]
        if permuted:
            _store_chunks(refs[-1], hv)
        for dil, h_ref in zip(dils, h_refs):
            if dil == 1:
                h_ref[...] = hv.astype(BF16)
            else:
                _gather_rows(refs[-1], h_ref, dil, tm)

    outs = pl.pallas_call(
        body, name=name, grid=(t // tm,),
        in_specs=[pl.BlockSpec((tm, d), lambda i: (i, 0)), pl.BlockSpec((1, d), lambda i: (0, 0))],
        out_specs=[_perm_spec(tm, dil, d) for dil in dils],
        out_shape=[_perm_shape(t, dil, d, BF16) for dil in dils],
        scratch_shapes=[pltpu.VMEM((_chunks(d), tm, 128), F32)] if permuted else [],
        compiler_params=_params(("parallel",)),
    )(x, g.reshape(1, d))
    return [o.reshape(t, d) for o in outs]


def _rmsnorm_bwd(x, g, dhs, dres, name, dils=(1,)):
    t, d = x.shape
    tm = _pick(t, (512, 256, 128) if len(dils) == 1 else (256, 128))
    n_in = len(dils)

    def body(*refs):
        x_ref, g_ref = refs[:2]
        dh_refs = refs[2:2 + n_in]
        dres_ref = refs[2 + n_in] if dres is not None else None
        dx_ref, dxb_ref, dg_ref = refs[-4:-1]
        nat_ref = refs[-1]
        dhv = None
        for dil, dh_ref in zip(dils, dh_refs):
            if dil == 1:
                term = dh_ref[...].astype(F32)
            else:
                _scatter_rows(dh_ref, nat_ref, dil, tm)
                term = _load_chunks(nat_ref)
            dhv = term if dhv is None else dhv + term
        xv = x_ref[...]
        r = lax.rsqrt(jnp.mean(xv * xv, axis=-1, keepdims=True) + EPS)
        xhat = xv * r
        dxh = dhv * g_ref[...]
        dx = r * (dxh - xhat * jnp.mean(dxh * xhat, axis=-1, keepdims=True))
        if dres_ref is not None:
            dx = dx + dres_ref[...]
        dx_ref[...] = dx
        dxb_ref[...] = dx.astype(BF16)
        part = jnp.sum(dhv * xhat, axis=0, keepdims=True)

        @pl.when(pl.program_id(0) == 0)
        def _():
            dg_ref[...] = part

        @pl.when(pl.program_id(0) > 0)
        def _():
            dg_ref[...] += part

    row = pl.BlockSpec((tm, d), lambda i: (i, 0))
    vec = pl.BlockSpec((1, d), lambda i: (0, 0))
    in_specs = [row, vec] + [_perm_spec(tm, dil, d) for dil in dils] + ([row] if dres is not None else [])
    dh_args = [dh if dil == 1 else dh.reshape(dil, t // dil, d) for dil, dh in zip(dils, dhs)]
    args = (x, g.reshape(1, d), *dh_args) + ((dres,) if dres is not None else ())
    return pl.pallas_call(
        body, name=name, grid=(t // tm,), in_specs=in_specs, out_specs=[row, row, vec],
        out_shape=[jax.ShapeDtypeStruct((t, d), F32), jax.ShapeDtypeStruct((t, d), BF16),
                   jax.ShapeDtypeStruct((1, d), F32)],
        scratch_shapes=[pltpu.VMEM((_chunks(d), tm, 128), F32)],
        compiler_params=_params(("arbitrary",)),
    )(*args)


def _final_loss(x, g, target, name="final_loss"):
    t, d = x.shape
    tm = _pick(t, (512, 256, 128))

    def body(x_ref, g_ref, t_ref, loss_ref, dx_ref, dxb_ref, dg_ref):
        xv = x_ref[...]
        gv = g_ref[...]
        r = lax.rsqrt(jnp.mean(xv * xv, axis=-1, keepdims=True) + EPS)
        xhat = xv * r
        e = xhat * gv - t_ref[...]
        lpart = jnp.zeros((1, 128), F32) + (0.5 / d) * jnp.sum(e * e)
        dy = e * (1.0 / d)
        dxh = dy * gv
        dx = r * (dxh - xhat * jnp.mean(dxh * xhat, axis=-1, keepdims=True))
        dx_ref[...] = dx
        dxb_ref[...] = dx.astype(BF16)
        gpart = jnp.sum(dy * xhat, axis=0, keepdims=True)

        @pl.when(pl.program_id(0) == 0)
        def _():
            dg_ref[...] = gpart
            loss_ref[...] = lpart

        @pl.when(pl.program_id(0) > 0)
        def _():
            dg_ref[...] += gpart
            loss_ref[...] += lpart

    row = pl.BlockSpec((tm, d), lambda i: (i, 0))
    vec = pl.BlockSpec((1, d), lambda i: (0, 0))
    return pl.pallas_call(
        body, name=name, grid=(t // tm,), in_specs=[row, vec, row],
        out_specs=[pl.BlockSpec((1, 128), lambda i: (0, 0)), row, row, vec],
        out_shape=[jax.ShapeDtypeStruct((1, 128), F32), jax.ShapeDtypeStruct((t, d), F32),
                   jax.ShapeDtypeStruct((t, d), BF16), jax.ShapeDtypeStruct((1, d), F32)],
        compiler_params=_params(("arbitrary",)),
    )(x, g.reshape(1, d), target)


CONV_K = 4
HALO = 8


def _shift_down(cur, prev8, s):
    rolled = pltpu.roll(cur, s, 0)
    fix = pltpu.roll(prev8, s, 0)
    head = jnp.where(_iota((HALO, cur.shape[1]), 0) < s, fix, rolled[:HALO])
    return jnp.concatenate([head, rolled[HALO:]], axis=0)


def _shift_up(cur, next8, s):
    n = cur.shape[0]
    rolled = pltpu.roll(cur, n - s, 0)
    fix = pltpu.roll(next8, HALO - s, 0)
    tail = jnp.where(_iota((HALO, cur.shape[1]), 0) >= HALO - s, fix, rolled[n - HALO:])
    return jnp.concatenate([rolled[:n - HALO], tail], axis=0)


def _conv_pre(u_ref, up_ref, w_ref, b_ref, first):
    cur = u_ref[...]
    prev8 = jnp.where(first, 0.0, up_ref[...])
    w = w_ref[...]
    shifted = [cur] + [_shift_down(cur, prev8, s) for s in (1, 2, 3)]
    pre = b_ref[...] + sum(w[CONV_K - 1 - s:CONV_K - s, :] * shifted[s] for s in range(CONV_K))
    return pre, shifted


def _conv_specs(tm, tc):
    nb = tm // HALO
    cur = pl.BlockSpec((tm, tc), lambda j, i: (i, j))
    prev = pl.BlockSpec((HALO, tc), lambda j, i: (jnp.maximum(i * nb - 1, 0), j))
    wspec = pl.BlockSpec((CONV_K, tc), lambda j, i: (0, j))
    bspec = pl.BlockSpec((1, tc), lambda j, i: (0, j))
    return cur, prev, wspec, bspec


def _conv_fwd(u, w, b, name):
    t, c = u.shape
    tm, tc = _pick(t, (512, 256, 128)), _pick(c, (1024, 512, 256, 128))
    cur, prev, wspec, bspec = _conv_specs(tm, tc)

    def body(u_ref, up_ref, w_ref, b_ref, o_ref):
        pre, _ = _conv_pre(u_ref, up_ref, w_ref, b_ref, pl.program_id(1) == 0)
        o_ref[...] = _silu(pre)

    return pl.pallas_call(
        body, name=name, grid=(c // tc, t // tm), in_specs=[cur, prev, wspec, bspec], out_specs=cur,
        out_shape=jax.ShapeDtypeStruct((t, c), F32),
        compiler_params=_params(("parallel", "parallel")),
    )(u, u, w, b.reshape(1, c))


def _conv_bwd_pre(u, w, b, dy, name):
    t, c = u.shape
    tm, tc = _pick(t, (512, 256, 128)), _pick(c, (1024, 512, 256, 128))
    cur, prev, wspec, bspec = _conv_specs(tm, tc)

    def body(u_ref, up_ref, w_ref, b_ref, dy_ref, dpre_ref, dw_ref, db_ref):
        i = pl.program_id(1)
        pre, shifted = _conv_pre(u_ref, up_ref, w_ref, b_ref, i == 0)
        dpre = dy_ref[...] * _dsilu(pre)
        dpre_ref[...] = dpre
        dw = jnp.concatenate([jnp.sum(dpre * shifted[CONV_K - 1 - k], axis=0, keepdims=True) for k in range(CONV_K)], axis=0)
        db = jnp.sum(dpre, axis=0, keepdims=True)

        @pl.when(i == 0)
        def _():
            dw_ref[...] = dw
            db_ref[...] = db

        @pl.when(i > 0)
        def _():
            dw_ref[...] += dw
            db_ref[...] += db

    return pl.pallas_call(
        body, name=name, grid=(c // tc, t // tm), in_specs=[cur, prev, wspec, bspec, cur],
        out_specs=[cur, wspec, bspec],
        out_shape=[jax.ShapeDtypeStruct((t, c), F32), jax.ShapeDtypeStruct((CONV_K, c), F32),
                   jax.ShapeDtypeStruct((1, c), F32)],
        compiler_params=_params(("parallel", "arbitrary")),
    )(u, u, w, b.reshape(1, c), dy)


def _conv_bwd_in(dpre, w, name):
    t, c = dpre.shape
    tm, tc = _pick(t, (512, 256, 128)), _pick(c, (1024, 512, 256, 128))
    nb = tm // HALO
    last = t // tm - 1
    cur = pl.BlockSpec((tm, tc), lambda j, i: (i, j))
    nxt = pl.BlockSpec((HALO, tc), lambda j, i: (jnp.minimum((i + 1) * nb, t // HALO - 1), j))
    wspec = pl.BlockSpec((CONV_K, tc), lambda j, i: (0, j))

    def body(d_ref, dn_ref, w_ref, o_ref):
        cur_v = d_ref[...]
        next8 = jnp.where(pl.program_id(1) == last, 0.0, dn_ref[...])
        wv = w_ref[...]
        acc = wv[CONV_K - 1:CONV_K, :] * cur_v
        for s in (1, 2, 3):
            acc = acc + wv[CONV_K - 1 - s:CONV_K - s, :] * _shift_up(cur_v, next8, s)
        o_ref[...] = acc.astype(o_ref.dtype)

    return pl.pallas_call(
        body, name=name, grid=(c // tc, t // tm), in_specs=[cur, nxt, wspec], out_specs=cur,
        out_shape=jax.ShapeDtypeStruct((t, c), BF16),
        compiler_params=_params(("parallel", "parallel")),
    )(dpre, dpre, w)


MEM_HEADS = 4
NT_DIMS = (((1,), (1,)), ((), ()))
TN_DIMS = (((0,), (0,)), ((), ()))


def _dot(a, b, dims=None):
    if dims is None:
        return jnp.dot(a, b, preferred_element_type=F32)
    return lax.dot_general(a, b, dims, preferred_element_type=F32)


def _memattn_probs(q, mk, scale):
    s = _dot(q, mk, NT_DIMS) * scale
    s = s - jnp.max(s, axis=-1, keepdims=True)
    p = jnp.exp(s)
    return p / jnp.sum(p, axis=-1, keepdims=True)


def _memattn_fwd(q, mkv, name):
    t, wd = q.shape
    m = mkv.shape[0]
    hd = wd // MEM_HEADS
    scale = hd ** -0.5
    tm = _pick(t, (512, 256, 128))

    def body(q_ref, mkv_ref, o_ref):
        for h in range(MEM_HEADS):
            cols = slice(h * hd, (h + 1) * hd)
            p = _memattn_probs(q_ref[:, cols], mkv_ref[:, cols], scale)
            o_ref[:, cols] = _dot(p.astype(BF16), mkv_ref[:, wd + h * hd:wd + (h + 1) * hd])

    return pl.pallas_call(
        body, name=name, grid=(t // tm,),
        in_specs=[pl.BlockSpec((tm, wd), lambda i: (i, 0)), pl.BlockSpec((m, 2 * wd), lambda i: (0, 0))],
        out_specs=pl.BlockSpec((tm, wd), lambda i: (i, 0)),
        out_shape=jax.ShapeDtypeStruct((t, wd), F32),
        compiler_params=_params(("parallel",)),
    )(q, mkv)


def _memattn_bwd(q, mkv, dy, name):
    t, wd = q.shape
    m = mkv.shape[0]
    hd = wd // MEM_HEADS
    scale = hd ** -0.5
    tm = _pick(t, (512, 256, 128))

    def body(q_ref, mkv_ref, dy_ref, dq_ref, dmkv_ref):
        i = pl.program_id(0)

        @pl.when(i == 0)
        def _():
            dmkv_ref[...] = jnp.zeros_like(dmkv_ref)

        for h in range(MEM_HEADS):
            cols = slice(h * hd, (h + 1) * hd)
            vcols = slice(wd + h * hd, wd + (h + 1) * hd)
            qh = q_ref[:, cols]
            p = _memattn_probs(qh, mkv_ref[:, cols], scale)
            dyh = dy_ref[:, cols].astype(BF16)
            dp = _dot(dyh, mkv_ref[:, vcols], NT_DIMS)
            ds = (p * (dp - jnp.sum(dp * p, axis=-1, keepdims=True)) * scale).astype(BF16)
            dq_ref[:, cols] = _dot(ds, mkv_ref[:, cols]).astype(dq_ref.dtype)
            dmkv_ref[:, cols] += _dot(ds, qh, TN_DIMS)
            dmkv_ref[:, vcols] += _dot(p.astype(BF16), dyh, TN_DIMS)

    return pl.pallas_call(
        body, name=name, grid=(t // tm,),
        in_specs=[pl.BlockSpec((tm, wd), lambda i: (i, 0)), pl.BlockSpec((m, 2 * wd), lambda i: (0, 0)),
                  pl.BlockSpec((tm, wd), lambda i: (i, 0))],
        out_specs=[pl.BlockSpec((tm, wd), lambda i: (i, 0)), pl.BlockSpec((m, 2 * wd), lambda i: (0, 0))],
        out_shape=[jax.ShapeDtypeStruct((t, wd), BF16), jax.ShapeDtypeStruct((m, 2 * wd), F32)],
        compiler_params=_params(("arbitrary",)),
    )(q, mkv, dy)


NORM_GROUPS = 8


def _gate_fwd(y_tok, y_mem, z, norm_g, name):
    t, tok = y_tok.shape
    mem = y_mem.shape[1]
    mix = tok + mem
    gw = tok // NORM_GROUPS
    tm = _pick(t, (256, 128))

    def body(*refs):
        if norm_g is None:
            yt_ref, ym_ref, z_ref, o_ref = refs
        else:
            yt_ref, ym_ref, z_ref, g_ref, o_ref = refs
        u = yt_ref[...] * _silu(z_ref[:, :tok])
        if norm_g is None:
            o_ref[:, :tok] = u.astype(o_ref.dtype)
        else:
            for k in range(NORM_GROUPS):
                uk = u[:, k * gw:(k + 1) * gw]
                r = lax.rsqrt(jnp.mean(uk * uk, axis=-1, keepdims=True) + EPS)
                o_ref[:, k * gw:(k + 1) * gw] = (uk * r * g_ref[:, k * gw:(k + 1) * gw]).astype(o_ref.dtype)
        o_ref[:, tok:] = (ym_ref[...] * _silu(z_ref[:, tok:])).astype(o_ref.dtype)

    in_specs = [pl.BlockSpec((tm, tok), lambda i: (i, 0)), pl.BlockSpec((tm, mem), lambda i: (i, 0)),
                pl.BlockSpec((tm, mix), lambda i: (i, 0))]
    args = [y_tok, y_mem, z]
    if norm_g is not None:
        in_specs.append(pl.BlockSpec((1, tok), lambda i: (0, 0)))
        args.append(norm_g.reshape(1, tok))
    return pl.pallas_call(
        body, name=name, grid=(t // tm,), in_specs=in_specs,
        out_specs=pl.BlockSpec((tm, mix), lambda i: (i, 0)),
        out_shape=jax.ShapeDtypeStruct((t, mix), BF16),
        compiler_params=_params(("parallel",)),
    )(*args)


def _gate_bwd(y_tok, y_mem, z, norm_g, dgated, name):
    t, tok = y_tok.shape
    mem = y_mem.shape[1]
    mix = tok + mem
    gw = tok // NORM_GROUPS
    tm = _pick(t, (256, 128))

    def body(*refs):
        if norm_g is None:
            yt_ref, ym_ref, z_ref, dg_ref, dyt_ref, dym_ref, dz_ref, dn_ref = refs
        else:
            yt_ref, ym_ref, z_ref, dg_ref, g_ref, dyt_ref, dym_ref, dz_ref, dn_ref = refs
        i = pl.program_id(0)
        zt = z_ref[:, :tok]
        yt = yt_ref[...]
        sz = _silu(zt)
        dout = dg_ref[:, :tok].astype(F32)
        if norm_g is None:
            du = dout
            dn = jnp.zeros((1, tok), F32)
        else:
            u = yt * sz
            dus, dns = [], []
            for k in range(NORM_GROUPS):
                uk = u[:, k * gw:(k + 1) * gw]
                r = lax.rsqrt(jnp.mean(uk * uk, axis=-1, keepdims=True) + EPS)
                nk = uk * r
                dk = dout[:, k * gw:(k + 1) * gw]
                dns.append(jnp.sum(dk * nk, axis=0, keepdims=True))
                dnk = dk * g_ref[:, k * gw:(k + 1) * gw]
                dus.append(r * (dnk - nk * jnp.mean(dnk * nk, axis=-1, keepdims=True)))
            du = jnp.concatenate(dus, axis=1)
            dn = jnp.concatenate(dns, axis=1)
        dyt_ref[...] = du * sz
        dz_ref[:, :tok] = (du * yt * _dsilu(zt)).astype(dz_ref.dtype)
        zm = z_ref[:, tok:]
        dm = dg_ref[:, tok:].astype(F32)
        dym_ref[...] = dm * _silu(zm)
        dz_ref[:, tok:] = (dm * ym_ref[...] * _dsilu(zm)).astype(dz_ref.dtype)

        @pl.when(i == 0)
        def _():
            dn_ref[...] = dn

        @pl.when(i > 0)
        def _():
            dn_ref[...] += dn

    tok_spec = pl.BlockSpec((tm, tok), lambda i: (i, 0))
    mem_spec = pl.BlockSpec((tm, mem), lambda i: (i, 0))
    mix_spec = pl.BlockSpec((tm, mix), lambda i: (i, 0))
    vec = pl.BlockSpec((1, tok), lambda i: (0, 0))
    in_specs = [tok_spec, mem_spec, mix_spec, mix_spec]
    args = [y_tok, y_mem, z, dgated]
    if norm_g is not None:
        in_specs.append(vec)
        args.append(norm_g.reshape(1, tok))
    return pl.pallas_call(
        body, name=name, grid=(t // tm,), in_specs=in_specs,
        out_specs=[tok_spec, mem_spec, mix_spec, vec],
        out_shape=[jax.ShapeDtypeStruct((t, tok), F32), jax.ShapeDtypeStruct((t, mem), F32),
                   jax.ShapeDtypeStruct((t, mix), BF16), jax.ShapeDtypeStruct((1, tok), F32)],
        compiler_params=_params(("arbitrary",)),
    )(*args)


SSD_Q = 128
SSD_N = 128
SSD_P = 64
SSD_G = 8
SSD_HPG = 6
SSD_H = SSD_G * SSD_HPG
SSD_TOK = SSD_H * SSD_P
SSD_XBC = SSD_TOK + 2 * SSD_G * SSD_N
LANES = 128
HIGHEST = lax.Precision.HIGHEST


def _softplus(x):
    return jnp.maximum(x, 0.0) + jnp.log(1.0 + jnp.exp(-jnp.abs(x)))


def _ssd_common(dtr_ref, bias_ref, alog_ref):
    sq = (SSD_Q, LANES)
    pre = dtr_ref[...] + bias_ref[...]
    dt = _softplus(pre)
    a = -jnp.exp(alog_ref[...])
    tril = (_iota(sq, 0) >= _iota(sq, 1)).astype(F32)
    acs = jnp.dot(tril, dt * a, precision=HIGHEST, preferred_element_type=F32)
    return pre, dt, a, tril, acs, acs.T


def _pair_terms(dt, acs, acs_t, h0):
    hi = _iota((SSD_Q, LANES), 1) >= SSD_P
    heads = []
    for j in range(2):
        h = h0 + j
        a_col = _col(acs, h)
        a_row = acs_t[h:h + 1, :]
        a_last = _col(acs[SSD_Q - 1:SSD_Q, :], h)
        heads.append((h, a_col, a_row, a_last, hi if j else jnp.logical_not(hi)))
    dtl = jnp.where(hi, _col(dt, h0 + 1), _col(dt, h0))
    scale = jnp.where(hi, jnp.exp(heads[1][1]), jnp.exp(heads[0][1]))
    dec_last = jnp.where(hi[:1], jnp.exp(heads[1][3]), jnp.exp(heads[0][3]))
    return heads, dtl, scale, dec_last


def _decay(a_col, a_row):
    causal = _iota((SSD_Q, SSD_Q), 0) >= _iota((SSD_Q, SSD_Q), 1)
    return jnp.where(causal, jnp.exp(jnp.minimum(a_col - a_row, 0.0)), 0.0)


def _ssd_fwd(xbc, dt_raw, dt_bias, a_log, dskip_lane, name):
    t = xbc.shape[0]
    nc = t // SSD_Q

    def body(xbc_ref, dtr_ref, bias_ref, alog_ref, dsk_ref, y_ref, hs_ref, h_ref):
        @pl.when(pl.program_id(0) == 0)
        def _():
            h_ref[...] = jnp.zeros_like(h_ref)

        _, dt, _, _, acs, acs_t = _ssd_common(dtr_ref, bias_ref, alog_ref)
        for g in range(SSD_G):
            bg_f = xbc_ref[:, SSD_TOK + g * SSD_N:SSD_TOK + (g + 1) * SSD_N]
            bg = bg_f.astype(BF16)
            cg = xbc_ref[:, SSD_TOK + SSD_G * SSD_N + g * SSD_N:SSD_TOK + SSD_G * SSD_N + (g + 1) * SSD_N].astype(BF16)
            cb = _dot(cg, bg, NT_DIMS)
            for pr in range(SSD_HPG // 2):
                h0 = g * SSD_HPG + 2 * pr
                lanes = slice(h0 * SSD_P, (h0 + 2) * SSD_P)
                heads, dtl, scale, dec_last = _pair_terms(dt, acs, acs_t, h0)
                xs = xbc_ref[:, lanes]
                xdt = xs * dtl
                hp = h_ref[:, lanes]
                hs_ref[:, lanes] = hp
                y = _dot(cg, hp.astype(BF16)) * scale + dsk_ref[:, lanes] * xs
                snew = hp * dec_last
                for _, a_col, a_row, a_last, mask in heads:
                    xm = jnp.where(mask, xdt, 0.0).astype(BF16)
                    y = y + _dot((cb * _decay(a_col, a_row)).astype(BF16), xm)
                    bw = (bg_f * jnp.exp(a_last - a_col)).astype(BF16)
                    snew = snew + _dot(bw, xm, TN_DIMS)
                y_ref[:, lanes] = y
                h_ref[:, lanes] = snew

    row = lambda w: pl.BlockSpec((SSD_Q, w), lambda c: (c, 0))
    vec = lambda w: pl.BlockSpec((1, w), lambda c: (0, 0))
    return pl.pallas_call(
        body, name=name, grid=(nc,),
        in_specs=[row(SSD_XBC), row(LANES), vec(LANES), vec(LANES), vec(SSD_TOK)],
        out_specs=[row(SSD_TOK), row(SSD_TOK)],
        out_shape=[jax.ShapeDtypeStruct((t, SSD_TOK), F32), jax.ShapeDtypeStruct((nc * SSD_N, SSD_TOK), F32)],
        scratch_shapes=[pltpu.VMEM((SSD_N, SSD_TOK), F32)],
        compiler_params=_params(("arbitrary",)),
    )(xbc, dt_raw, dt_bias, a_log, dskip_lane)


def _ssd_bwd(xbc, dt_raw, dt_bias, a_log, dskip_lane, hs, dy, name):
    t = xbc.shape[0]
    nc = t // SSD_Q
    sq = (SSD_Q, LANES)

    def body(xbc_ref, dtr_ref, bias_ref, alog_ref, dsk_ref, hs_ref, dy_ref,
             dxbc_ref, ddtr_ref, dbias_ref, dalog_ref, ddsk_ref, dh_ref):
        first = pl.program_id(0) == 0

        @pl.when(first)
        def _():
            dh_ref[...] = jnp.zeros_like(dh_ref)
            dbias_ref[...] = jnp.zeros_like(dbias_ref)
            dalog_ref[...] = jnp.zeros_like(dalog_ref)
            ddsk_ref[...] = jnp.zeros_like(ddsk_ref)

        pre, dt, a, tril, acs, acs_t = _ssd_common(dtr_ref, bias_ref, alog_ref)
        lane = _iota(sq, 1)
        sub = _iota(sq, 0)
        causal = sub >= lane
        d_acs = jnp.zeros(sq, F32)
        d_acs_row = jnp.zeros(sq, F32)
        d_last = jnp.zeros((1, LANES), F32)
        ddt = jnp.zeros(sq, F32)
        for g in range(SSD_G):
            bcols = slice(SSD_TOK + g * SSD_N, SSD_TOK + (g + 1) * SSD_N)
            ccols = slice(SSD_TOK + SSD_G * SSD_N + g * SSD_N, SSD_TOK + SSD_G * SSD_N + (g + 1) * SSD_N)
            bg_f = xbc_ref[:, bcols]
            bg = bg_f.astype(BF16)
            cg = xbc_ref[:, ccols].astype(BF16)
            cb = _dot(cg, bg, NT_DIMS)
            dcb = jnp.zeros(sq, F32)
            dbg = jnp.zeros(sq, F32)
            dcg = jnp.zeros(sq, F32)
            for pr in range(SSD_HPG // 2):
                h0 = g * SSD_HPG + 2 * pr
                lanes = slice(h0 * SSD_P, (h0 + 2) * SSD_P)
                heads, dtl, scale, dec_last = _pair_terms(dt, acs, acs_t, h0)
                xs = xbc_ref[:, lanes]
                xdt = xs * dtl
                dyv = dy_ref[:, lanes]
                hp = hs_ref[:, lanes]
                dhn = dh_ref[:, lanes]
                hp_b = hp.astype(BF16)
                dys = (dyv * scale).astype(BF16)
                yoff_dy = dyv * _dot(cg, hp_b) * scale
                dcg = dcg + _dot(dys, hp_b, NT_DIMS)
                dhc = _dot(cg, dys, TN_DIMS)
                hh = dhn * hp
                dxdt = jnp.zeros(sq, F32)
                for h, a_col, a_row, a_last, mask in heads:
                    dec = _decay(a_col, a_row)
                    m = cb * dec
                    dym = jnp.where(mask, dyv, 0.0).astype(BF16)
                    xm = jnp.where(mask, xdt, 0.0).astype(BF16)
                    dhm = jnp.where(mask, dhn, 0.0).astype(BF16)
                    w = jnp.exp(a_last - a_col)
                    dxdt = dxdt + _dot(m.astype(BF16), dym, TN_DIMS) + _dot((bg_f * w).astype(BF16), dhm)
                    dm = jnp.where(causal, _dot(dym, xm, NT_DIMS), 0.0)
                    dcb = dcb + dm * dec
                    e = dm * m
                    gj = _dot(xm, dhm, NT_DIMS)
                    dbg = dbg + w * gj
                    wdw = w * jnp.sum(bg_f * gj, axis=1, keepdims=True)
                    col = (jnp.sum(e, axis=1, keepdims=True)
                           + jnp.sum(jnp.where(mask, yoff_dy, 0.0), axis=1, keepdims=True) - wdw)
                    d_acs = d_acs + jnp.where(lane == h, col, 0.0)
                    d_acs_row = d_acs_row + jnp.where(sub == h, jnp.sum(e, axis=0, keepdims=True), 0.0)
                    last = jnp.sum(wdw) + jnp.exp(a_last) * jnp.sum(jnp.where(mask, hh, 0.0))
                    d_last = d_last + jnp.where(lane[:1] == h, last, 0.0)
                dxbc_ref[:, lanes] = dxdt * dtl + dsk_ref[:, lanes] * dyv
                tt = dxdt * xs
                for h, _, _, _, mask in heads:
                    ddt = ddt + jnp.where(lane == h, jnp.sum(jnp.where(mask, tt, 0.0), axis=1, keepdims=True), 0.0)
                ddsk_ref[:, lanes] += jnp.sum(dyv * xs, axis=0, keepdims=True)
                dh_ref[:, lanes] = dhn * dec_last + dhc
            dcb_b = dcb.astype(BF16)
            dxbc_ref[:, bcols] = dbg + _dot(dcb_b, cg, TN_DIMS)
            dxbc_ref[:, ccols] = dcg + _dot(dcb_b, bg)
        d_tot = d_acs - d_acs_row.T + jnp.where(sub == SSD_Q - 1, d_last, 0.0)
        ddta = lax.dot_general(tril, d_tot, TN_DIMS, precision=HIGHEST, preferred_element_type=F32)
        ddt = ddt + ddta * a
        dalog_ref[...] += jnp.sum(ddta * dt, axis=0, keepdims=True) * a
        ddtr = ddt * jax.nn.sigmoid(pre)
        ddtr_ref[...] = ddtr
        dbias_ref[...] += jnp.sum(ddtr, axis=0, keepdims=True)

    rev = lambda w: pl.BlockSpec((SSD_Q, w), lambda i: (nc - 1 - i, 0))
    vec = lambda w: pl.BlockSpec((1, w), lambda i: (0, 0))
    return pl.pallas_call(
        body, name=name, grid=(nc,),
        in_specs=[rev(SSD_XBC), rev(LANES), vec(LANES), vec(LANES), vec(SSD_TOK), rev(SSD_TOK), rev(SSD_TOK)],
        out_specs=[rev(SSD_XBC), rev(LANES), vec(LANES), vec(LANES), vec(SSD_TOK)],
        out_shape=[jax.ShapeDtypeStruct((t, SSD_XBC), F32), jax.ShapeDtypeStruct((t, LANES), F32),
                   jax.ShapeDtypeStruct((1, LANES), F32), jax.ShapeDtypeStruct((1, LANES), F32),
                   jax.ShapeDtypeStruct((1, SSD_TOK), F32)],
        scratch_shapes=[pltpu.VMEM((SSD_N, SSD_TOK), F32)],
        compiler_params=_params(("arbitrary",)),
    )(xbc, dt_raw, dt_bias, a_log, dskip_lane, hs, dy)


ATT_E = 128
ATT_H = 24
ATT_W = 128
ATT_TOK = ATT_H * ATT_E
DILATED_GROUPS = ((128, 1), (512, 4), (2048, 16))
N_DIL = len(DILATED_GROUPS)
ALIBI_MAX_EXP = 8.0
MASKED = -1e30


def _alibi_slopes(group):
    n = N_DIL * ATT_H
    return [2.0 ** (-ALIBI_MAX_EXP * (group * ATT_H + h + 1) / n) for h in range(ATT_H)]


def _att_scores(qh, kk, rel, valid, slope_d):
    s = _dot(qh, kk, NT_DIMS) * (ATT_E ** -0.5) - slope_d * rel
    return jnp.where(valid, s, MASKED)


def _att_rel(j):
    shp = (ATT_W, 2 * ATT_W)
    kpos = _iota(shp, 1)
    rel = _iota(shp, 0) + ATT_W - kpos
    valid = (rel >= 0) & (rel <= ATT_W) & ((kpos >= ATT_W) | (j > 0))
    return rel.astype(F32), valid


def _dil_fwd(q, k, v, group, name):
    t = q.shape[0]
    dil = DILATED_GROUPS[group][1]
    slopes = _alibi_slopes(group)
    nb = t // dil // ATT_W

    def body(q_ref, kp_ref, kc_ref, vp_ref, vc_ref, o_ref, lse_ref):
        rel, valid = _att_rel(pl.program_id(1))
        lane = _iota((ATT_W, LANES), 1)
        lse_all = jnp.zeros((ATT_W, LANES), F32)
        for h in range(ATT_H):
            cols = slice(h * ATT_E, (h + 1) * ATT_E)
            kk = jnp.concatenate([kp_ref[:, cols], kc_ref[:, cols]], axis=0)
            vv = jnp.concatenate([vp_ref[:, cols], vc_ref[:, cols]], axis=0)
            s = _att_scores(q_ref[:, cols], kk, rel, valid, slopes[h] * dil)
            m = jnp.max(s, axis=-1, keepdims=True)
            p = jnp.exp(s - m)
            den = jnp.sum(p, axis=-1, keepdims=True)
            o_ref[:, cols] = _dot(p.astype(BF16), vv) / den
            lse_all = jnp.where(lane == h, m + jnp.log(den), lse_all)
        lse_ref[...] = lse_all

    cur = pl.BlockSpec((ATT_W, ATT_TOK), lambda r, j: (r * nb + j, 0))
    prev = pl.BlockSpec((ATT_W, ATT_TOK), lambda r, j: (r * nb + jnp.maximum(j - 1, 0), 0))
    small = pl.BlockSpec((ATT_W, LANES), lambda r, j: (r * nb + j, 0))
    return pl.pallas_call(
        body, name=name, grid=(dil, nb),
        in_specs=[cur, prev, cur, prev, cur], out_specs=[cur, small],
        out_shape=[jax.ShapeDtypeStruct((t, ATT_TOK), F32), jax.ShapeDtypeStruct((t, LANES), F32)],
        compiler_params=_params(("parallel", "parallel")),
    )(q, k, k, v, v)


def _dil_bwd(q, k, v, do, cterm, lse, group, name):
    t = q.shape[0]
    dil = DILATED_GROUPS[group][1]
    slopes = _alibi_slopes(group)
    nb = t // dil // ATT_W

    def body(q_ref, kp_ref, kc_ref, vp_ref, vc_ref, do_ref, c_ref, lse_ref,
             dq_ref, dk_ref, dv_ref, ck_ref, cv_ref):
        j = pl.program_id(1)

        @pl.when(j == 0)
        def _():
            ck_ref[...] = jnp.zeros_like(ck_ref)
            cv_ref[...] = jnp.zeros_like(cv_ref)

        @pl.when(j < nb)
        def _():
            rel, valid = _att_rel(j)
            cv_, lv = c_ref[...], lse_ref[...]
            for h in range(ATT_H):
                cols = slice(h * ATT_E, (h + 1) * ATT_E)
                qh = q_ref[:, cols]
                kk = jnp.concatenate([kp_ref[:, cols], kc_ref[:, cols]], axis=0)
                vv = jnp.concatenate([vp_ref[:, cols], vc_ref[:, cols]], axis=0)
                s = _att_scores(qh, kk, rel, valid, slopes[h] * dil)
                p = jnp.where(valid, jnp.exp(s - _col(lv, h)), 0.0)
                do = do_ref[:, cols]
                dp = _dot(do, vv, NT_DIMS)
                ds = (p * (dp + _col(cv_, h)) * (ATT_E ** -0.5)).astype(BF16)
                dq_ref[:, cols] = _dot(ds, kk).astype(dq_ref.dtype)
                dkk = _dot(ds, qh, TN_DIMS)
                dvv = _dot(p.astype(BF16), do, TN_DIMS)
                dk_ref[:, cols] = (ck_ref[:, cols] + dkk[:ATT_W]).astype(dk_ref.dtype)
                dv_ref[:, cols] = (cv_ref[:, cols] + dvv[:ATT_W]).astype(dv_ref.dtype)
                ck_ref[:, cols] = dkk[ATT_W:]
                cv_ref[:, cols] = dvv[ATT_W:]

        @pl.when(j == nb)
        def _():
            dk_ref[...] = ck_ref[...].astype(dk_ref.dtype)
            dv_ref[...] = cv_ref[...].astype(dv_ref.dtype)

    jq = lambda j: jnp.minimum(j, nb - 1)
    cur = pl.BlockSpec((ATT_W, ATT_TOK), lambda r, j: (r * nb + jq(j), 0))
    prev = pl.BlockSpec((ATT_W, ATT_TOK), lambda r, j: (r * nb + jnp.maximum(jq(j) - 1, 0), 0))
    small = pl.BlockSpec((ATT_W, LANES), lambda r, j: (r * nb + jq(j), 0))
    late = pl.BlockSpec((ATT_W, ATT_TOK), lambda r, j: (r * nb + jnp.maximum(j - 1, 0), 0))
    big = jax.ShapeDtypeStruct((t, ATT_TOK), BF16)
    return pl.pallas_call(
        body, name=name, grid=(dil, nb + 1),
        in_specs=[cur, prev, cur, prev, cur, cur, small, small], out_specs=[cur, late, late],
        out_shape=[big, big, big],
        scratch_shapes=[pltpu.VMEM((ATT_W, ATT_TOK), F32), pltpu.VMEM((ATT_W, ATT_TOK), F32)],
        compiler_params=_params(("parallel", "arbitrary")),
    )(q, k, k, v, v, do, cterm, lse)


def _combine_weights(lses):
    m = functools.reduce(jnp.maximum, lses)
    es = [jnp.exp(l - m) for l in lses]
    tot = functools.reduce(lambda a, b: a + b, es)
    return [e / tot for e in es]


DILS = tuple(d for _, d in DILATED_GROUPS)
COMBINE_ROWS = 256


def _by_residue(arr, dil):
    return arr if dil == 1 else arr.reshape(dil, arr.shape[0] // dil, arr.shape[1])


def _natural_lses(l_refs, small_refs, tm):
    vals = []
    for g, dil in enumerate(DILS):
        if dil == 1:
            vals.append(l_refs[g][...])
        else:
            _scatter_rows(l_refs[g], small_refs[g], dil, tm)
            vals.append(small_refs[g][0])
    return vals


def _combine_scratch(tm):
    return ([pltpu.VMEM((_chunks(ATT_TOK), tm, 128), F32)] * N_DIL + [pltpu.VMEM((1, tm, 128), F32)] * N_DIL)


def _combine_fwd(outs, lses, name):
    t = outs[0].shape[0]
    tm = COMBINE_ROWS

    def body(*refs):
        o_refs, l_refs, y_ref = refs[:N_DIL], refs[N_DIL:2 * N_DIL], refs[2 * N_DIL]
        big_refs, small_refs = refs[2 * N_DIL + 1:3 * N_DIL + 1], refs[3 * N_DIL + 1:]
        ws = _combine_weights(_natural_lses(l_refs, small_refs, tm))
        for g in range(1, N_DIL):
            _scatter_rows(o_refs[g], big_refs[g], DILS[g], tm)
        for h in range(ATT_H):
            cols = slice(h * ATT_E, (h + 1) * ATT_E)
            y_ref[:, cols] = (_col(ws[0], h) * o_refs[0][:, cols]
                              + sum(_col(ws[g], h) * big_refs[g][h] for g in range(1, N_DIL)))

    return pl.pallas_call(
        body, name=name, grid=(t // tm,),
        in_specs=[_perm_spec(tm, d, ATT_TOK) for d in DILS] + [_perm_spec(tm, d, LANES) for d in DILS],
        out_specs=pl.BlockSpec((tm, ATT_TOK), lambda i: (i, 0)),
        out_shape=jax.ShapeDtypeStruct((t, ATT_TOK), F32),
        scratch_shapes=_combine_scratch(tm),
        compiler_params=_params(("parallel",)),
    )(*[_by_residue(o, d) for o, d in zip(outs, DILS)], *[_by_residue(l, d) for l, d in zip(lses, DILS)])


def _combine_bwd(outs, lses, dy, name):
    t = outs[0].shape[0]
    tm = COMBINE_ROWS

    def body(*refs):
        o_refs, l_refs, dy_ref = refs[:N_DIL], refs[N_DIL:2 * N_DIL], refs[2 * N_DIL]
        do_refs, c_refs = refs[2 * N_DIL + 1:3 * N_DIL + 1], refs[3 * N_DIL + 1:4 * N_DIL + 1]
        big_refs, small_refs = refs[4 * N_DIL + 1:5 * N_DIL + 1], refs[5 * N_DIL + 1:]
        ws = _combine_weights(_natural_lses(l_refs, small_refs, tm))
        for g in range(1, N_DIL):
            _scatter_rows(o_refs[g], big_refs[g], DILS[g], tm)
        lane = _iota((tm, LANES), 1)
        sdw = jnp.zeros((tm, LANES), F32)
        for h in range(ATT_H):
            cols = slice(h * ATT_E, (h + 1) * ATT_E)
            dyh = dy_ref[:, cols]
            tot = _col(ws[0], h) * jnp.sum(dyh * o_refs[0][:, cols], axis=1, keepdims=True)
            for g in range(1, N_DIL):
                tot = tot + _col(ws[g], h) * jnp.sum(dyh * big_refs[g][h], axis=1, keepdims=True)
            sdw = jnp.where(lane == h, tot, sdw)
        for g, dil in enumerate(DILS):
            cterm = -ws[g] * sdw
            if dil == 1:
                c_refs[g][...] = cterm
            else:
                small_refs[g][0] = cterm
                _gather_rows(small_refs[g], c_refs[g], dil, tm)
            for h in range(ATT_H):
                cols = slice(h * ATT_E, (h + 1) * ATT_E)
                do = _col(ws[g], h) * dy_ref[:, cols]
                if dil == 1:
                    do_refs[g][:, cols] = do.astype(BF16)
                else:
                    big_refs[g][h] = do
            if dil > 1:
                _gather_rows(big_refs[g], do_refs[g], dil, tm)

    res = pl.pallas_call(
        body, name=name, grid=(t // tm,),
        in_specs=([_perm_spec(tm, d, ATT_TOK) for d in DILS] + [_perm_spec(tm, d, LANES) for d in DILS]
                  + [pl.BlockSpec((tm, ATT_TOK), lambda i: (i, 0))]),
        out_specs=[_perm_spec(tm, d, ATT_TOK) for d in DILS] + [_perm_spec(tm, d, LANES) for d in DILS],
        out_shape=[_perm_shape(t, d, ATT_TOK, BF16) for d in DILS] + [_perm_shape(t, d, LANES, F32) for d in DILS],
        scratch_shapes=_combine_scratch(tm),
        compiler_params=_params(("parallel",)),
    )(*[_by_residue(o, d) for o, d in zip(outs, DILS)], *[_by_residue(l, d) for l, d in zip(lses, DILS)], dy)
    return [a.reshape(t, ATT_TOK) for a in res[:N_DIL]], [a.reshape(t, LANES) for a in res[N_DIL:]]


ADAM_LR, ADAM_B1, ADAM_B2, ADAM_EPS, ADAM_WD, ADAM_STEP = 0.001, 0.9, 0.999, 1e-08, 0.01, 10


def _adamw(parts, w, m, v, name):
    r, c = w.shape
    n_parts = parts.shape[0]
    tc = _pick(c, (1024, 512, 256, 128)) if c % 128 == 0 else c
    tm = _pick(r, (128, 64, 32, 16, 8))

    def body(p_ref, w_ref, m_ref, v_ref, g_ref, d_ref, nm_ref, nv_ref):
        g = p_ref[0].astype(F32)
        for k in range(1, n_parts):
            g = g + p_ref[k].astype(F32)
        nm = ADAM_B1 * m_ref[...] + (1.0 - ADAM_B1) * g
        nv = ADAM_B2 * v_ref[...] + (1.0 - ADAM_B2) * (g * g)
        m_hat = nm / (1.0 - ADAM_B1 ** ADAM_STEP)
        v_hat = nv / (1.0 - ADAM_B2 ** ADAM_STEP)
        g_ref[...] = g
        d_ref[...] = -ADAM_LR * (m_hat / (jnp.sqrt(v_hat) + ADAM_EPS) + ADAM_WD * w_ref[...])
        nm_ref[...] = nm
        nv_ref[...] = nv

    blk = pl.BlockSpec((tm, tc), lambda i, j: (i, j))
    pblk = pl.BlockSpec((n_parts, tm, tc), lambda i, j: (0, i, j))
    return pl.pallas_call(
        body, name=name, grid=(r // tm, c // tc), in_specs=[pblk, blk, blk, blk], out_specs=[blk] * 4,
        out_shape=[jax.ShapeDtypeStruct((r, c), F32)] * 4,
        compiler_params=_params(("parallel", "parallel")),
    )(parts, w, m, v)


N_CHIP = 4
MESH_ID = pl.DeviceIdType.MESH


def _other_chips(x, y):
    return [(1 - x, y), (x, 1 - y), (1 - x, 1 - y)]


def _gather_two_level(arrays, name):
    n = len(arrays)
    per = N_DEV - 1

    def body(*refs):
        in_refs, out_refs = refs[:n], refs[n:2 * n]
        send_sems, recv_sems, local_sems = refs[2 * n:]
        x, y, c = lax.axis_index("x"), lax.axis_index("y"), lax.axis_index("c")
        sibling = (x, y, 1 - c)
        chips = _other_chips(x, y)

        def copy(a, k, block, to, src=None):
            rows = out_refs[a].at[4 * block[0] + 2 * block[1] + block[2]]
            return pltpu.make_async_remote_copy(
                src_ref=rows if src is None else src, dst_ref=rows,
                send_sem=send_sems.at[a * per + k], recv_sem=recv_sems.at[a * per + k],
                device_id=to, device_id_type=MESH_ID)

        started = []
        for a in range(n):
            local = pltpu.make_async_copy(in_refs[a], out_refs[a].at[4 * x + 2 * y + c], local_sems.at[a])
            local.start()
            started.append(local)
        sends = []
        for j, chip in enumerate(chips):
            for a in range(n):
                sends.append(copy(a, 1 + j, (x, y, c), (*chip, c), src=in_refs[a]))
                sends[-1].start()
        for a in range(n):
            sends.append(copy(a, 0, (x, y, c), sibling, src=in_refs[a]))
            sends[-1].start()
        for j, chip in enumerate(chips):
            for a in range(n):
                copy(a, 1 + j, (*chip, c), (x, y, c)).wait_recv()
                sends.append(copy(a, 4 + j, (*chip, c), sibling))
                sends[-1].start()
        for a in range(n):
            copy(a, 0, sibling, (x, y, c)).wait_recv()
            for j, chip in enumerate(chips):
                copy(a, 4 + j, (*chip, 1 - c), (x, y, c)).wait_recv()
        for cp in sends:
            cp.wait_send()
        for local in started:
            local.wait()

    any_spec = pl.BlockSpec(memory_space=pl.ANY)
    return pl.pallas_call(
        body, name=name, in_specs=[any_spec] * n, out_specs=[any_spec] * n,
        out_shape=[jax.ShapeDtypeStruct((N_DEV,) + a.shape, a.dtype) for a in arrays],
        scratch_shapes=[pltpu.SemaphoreType.DMA((n * per,)), pltpu.SemaphoreType.DMA((n * per,)),
                        pltpu.SemaphoreType.DMA((n,))],
        compiler_params=pltpu.CompilerParams(has_side_effects=True),
    )(*arrays)


def _sibling_swap(parts, name):
    n = len(parts)

    def body(*refs):
        in_refs, out_refs = refs[:n], refs[n:2 * n]
        send_sems, recv_sems = refs[2 * n:]
        x, y, c = lax.axis_index("x"), lax.axis_index("y"), lax.axis_index("c")
        sends = []
        for a in range(n):
            for q in range(N_CHIP):
                cp = pltpu.make_async_remote_copy(
                    src_ref=in_refs[a].at[2 * q + 1 - c], dst_ref=out_refs[a].at[q],
                    send_sem=send_sems.at[a * N_CHIP + q], recv_sem=recv_sems.at[a * N_CHIP + q],
                    device_id=(x, y, 1 - c), device_id_type=MESH_ID)
                cp.start()
                sends.append(cp)
        for cp in sends:
            cp.wait_recv()
        for cp in sends:
            cp.wait_send()

    any_spec = pl.BlockSpec(memory_space=pl.ANY)
    return pl.pallas_call(
        body, name=name, in_specs=[any_spec] * n, out_specs=[any_spec] * n,
        out_shape=[jax.ShapeDtypeStruct((N_CHIP,) + p.shape[1:], p.dtype) for p in parts],
        scratch_shapes=[pltpu.SemaphoreType.DMA((n * N_CHIP,)), pltpu.SemaphoreType.DMA((n * N_CHIP,))],
        compiler_params=pltpu.CompilerParams(has_side_effects=True),
    )(*parts)


def _chip_sum(part, landed, core, name):
    _, r, c = part.shape
    tc = _pick(c, (1024, 512, 256, 128)) if c % 128 == 0 else c
    tm = _pick(r, (256, 128, 64, 32, 16, 8))

    def body(core_ref, p_ref, l_ref, o_ref):
        o_ref[...] = (p_ref[...].astype(F32) + l_ref[...].astype(F32)).astype(o_ref.dtype)

    grid_spec = pltpu.PrefetchScalarGridSpec(
        num_scalar_prefetch=1, grid=(N_CHIP, r // tm, c // tc),
        in_specs=[pl.BlockSpec((None, tm, tc), lambda q, i, j, core_ref: (2 * q + core_ref[0], i, j)),
                  pl.BlockSpec((None, tm, tc), lambda q, i, j, core_ref: (q, i, j))],
        out_specs=pl.BlockSpec((None, tm, tc), lambda q, i, j, core_ref: (q, i, j)))
    return pl.pallas_call(
        body, name=name, grid_spec=grid_spec, out_shape=jax.ShapeDtypeStruct(landed.shape, landed.dtype),
        compiler_params=_params(("parallel", "parallel", "parallel")),
    )(core, part, landed)


def _chip_exchange(sums, name):
    n = len(sums)
    per = N_CHIP - 1

    def body(*refs):
        in_refs, out_refs = refs[:n], refs[n:2 * n]
        send_sems, recv_sems, local_sems = refs[2 * n:]
        x, y, c = lax.axis_index("x"), lax.axis_index("y"), lax.axis_index("c")
        mine = 2 * x + y
        started = []
        for a in range(n):
            local = pltpu.make_async_copy(in_refs[a].at[mine], out_refs[a].at[mine], local_sems.at[a])
            local.start()
            started.append(local)
        sends = []
        for j, (px, py) in enumerate(_other_chips(x, y)):
            for a in range(n):
                cp = pltpu.make_async_remote_copy(
                    src_ref=in_refs[a].at[2 * px + py], dst_ref=out_refs[a].at[mine],
                    send_sem=send_sems.at[a * per + j], recv_sem=recv_sems.at[a * per + j],
                    device_id=(px, py, c), device_id_type=MESH_ID)
                cp.start()
                sends.append((cp, a, j, 2 * px + py))
        for cp, a, j, peer in sends:
            pltpu.make_async_remote_copy(
                src_ref=out_refs[a].at[peer], dst_ref=out_refs[a].at[peer],
                send_sem=send_sems.at[a * per + j], recv_sem=recv_sems.at[a * per + j],
                device_id=(x, y, c), device_id_type=MESH_ID).wait_recv()
        for cp, _, _, _ in sends:
            cp.wait_send()
        for local in started:
            local.wait()

    any_spec = pl.BlockSpec(memory_space=pl.ANY)
    return pl.pallas_call(
        body, name=name, in_specs=[any_spec] * n, out_specs=[any_spec] * n,
        out_shape=[jax.ShapeDtypeStruct(s.shape, s.dtype) for s in sums],
        scratch_shapes=[pltpu.SemaphoreType.DMA((n * per,)), pltpu.SemaphoreType.DMA((n * per,)),
                        pltpu.SemaphoreType.DMA((n,))],
        compiler_params=pltpu.CompilerParams(has_side_effects=True),
    )(*sums)


DEPTH = 4
MEM_W = 1024
MIX_W = SSD_TOK + MEM_W
DT_PAD = LANES - SSD_H


def _is_ssd(i):
    return i % 2 == 0


def _weight_names():
    names = ["mem_norm_g", "final_norm_g"]
    for i in range(DEPTH):
        names += [f"norm_g_{i}", f"w_in_{i}"]
        if _is_ssd(i):
            names += [f"conv_w_{i}", f"conv_b_{i}", f"dt_bias_{i}", f"a_log_{i}", f"d_skip_{i}", f"ssd_norm_g_{i}"]
        names += [f"w_mem_kv_{i}", f"w_out_{i}"]
    return names


WEIGHTS = _weight_names()
INPUTS = ["x", "mem"] + WEIGHTS + ["loss_target"] + ["m_" + n for n in WEIGHTS] + ["v_" + n for n in WEIGHTS]


def _in_segments(i):
    if _is_ssd(i):
        return [("xbc", 0, SSD_XBC), ("dt", SSD_XBC, SSD_H), ("qm", SSD_XBC + SSD_H, MEM_W),
                ("z", SSD_XBC + SSD_H + MEM_W, MIX_W)]
    segs = []
    for g in range(N_DIL):
        for j, nm in enumerate("qkv"):
            segs.append((f"{nm}{g}", (3 * g + j) * ATT_TOK, ATT_TOK))
    segs += [("qm", 3 * N_DIL * ATT_TOK, MEM_W), ("z", 3 * N_DIL * ATT_TOK + MEM_W, MIX_W)]
    return segs


def _split_w_in(i, w_in):
    out = {}
    for nm, start, width in _in_segments(i):
        seg = w_in[:, start:start + width]
        out[nm] = jnp.pad(seg, ((0, 0), (0, DT_PAD))) if nm == "dt" else seg
    return out


def _join_dw_in(i, dws):
    return jnp.concatenate([dws[nm][:, :width] for nm, _, width in _in_segments(i)], axis=1)


SEG_DTYPE = {"xbc": F32, "dt": F32, "z": F32}


def _layer_dils(i):
    return (1,) if _is_ssd(i) else DILS


def _seg_order(nm):
    return int(nm[1]) if nm[0] in "qkv" and nm[1:].isdigit() else 0


def _layer_fwd(i, x, mem_b, p):
    tag = f"l{i}"
    hs = _rmsnorm_fwd(x, p["norm_g"], tag + "_norm", _layer_dils(i))
    proj = {nm: _matmul(hs[_seg_order(nm)], w, out_dtype=SEG_DTYPE.get(nm, BF16), name=f"{tag}_in_{nm}")
            for nm, w in p["win"].items()}
    sv = {"x": x, "h": hs, "proj": proj}
    if _is_ssd(i):
        xbc = _conv_fwd(proj["xbc"], p["conv_w"], p["conv_b"], tag + "_conv")
        y_tok, hs = _ssd_fwd(xbc, proj["dt"], p["dt_bias_p"], p["a_log_p"], p["dskip_lane"], tag + "_ssd")
        sv.update(xbc=xbc, hs=hs)
    else:
        outs, lses = [], []
        for g in range(N_DIL):
            o, lse = _dil_fwd(proj[f"q{g}"], proj[f"k{g}"], proj[f"v{g}"], g, f"{tag}_att{g}")
            outs.append(o)
            lses.append(lse)
        y_tok = _combine_fwd(outs, lses, tag + "_comb")
        sv.update(outs=outs, lses=lses)
    mkv = _matmul(mem_b, p["wmkv"], out_dtype=BF16, name=tag + "_mkv")
    y_mem = _memattn_fwd(proj["qm"], mkv, tag + "_mem")
    gated = _gate_fwd(y_tok, y_mem, proj["z"], p.get("ssd_norm_g"), tag + "_gate")
    x_out = _matmul(gated, p["wout"], out_dtype=F32, add=x, name=tag + "_out")
    sv.update(y_tok=y_tok, y_mem=y_mem, mkv=mkv, gated=gated)
    return x_out, sv


def _layer_bwd(i, sv, dx_out, dxb_out, dmem_n, mem_b, p):
    tag = f"l{i}b"
    proj = sv["proj"]
    gr = {}
    dgated = _matmul(dxb_out, p["wout"], tb=True, out_dtype=F32, name=tag + "_dgated")
    gr["w_out"] = _matmul(sv["gated"], dxb_out, ta=True, out_dtype=BF16, name=tag + "_dwout")
    dy_tok, dy_mem, dz, dssd_g = _gate_bwd(sv["y_tok"], sv["y_mem"], proj["z"], p.get("ssd_norm_g"), dgated, tag + "_gate")
    dq_mem, dmkv = _memattn_bwd(proj["qm"], sv["mkv"], dy_mem, tag + "_mem")
    gr["w_mem_kv"] = _matmul(mem_b, dmkv, ta=True, out_dtype=BF16, name=tag + "_dwmkv")
    dmem_n = _matmul(dmkv, p["wmkv"], tb=True, out_dtype=F32, add=dmem_n, name=tag + "_dmem")
    dproj = {"qm": dq_mem, "z": dz}
    if _is_ssd(i):
        dxbc, ddt_raw, dbias, dalog, ddsk = _ssd_bwd(sv["xbc"], proj["dt"], p["dt_bias_p"], p["a_log_p"], p["dskip_lane"],
                                                     sv["hs"], dy_tok, tag + "_ssd")
        dpre, dconv_w, dconv_b = _conv_bwd_pre(proj["xbc"], p["conv_w"], p["conv_b"], dxbc, tag + "_convpre")
        dproj["xbc"] = _conv_bwd_in(dpre, p["conv_w"], tag + "_convin")
        dproj["dt"] = ddt_raw
        gr.update(conv_w=dconv_w, conv_b=dconv_b[0], dt_bias=dbias[0, :SSD_H], a_log=dalog[0, :SSD_H],
                  d_skip=jnp.sum(ddsk.reshape(SSD_H, SSD_P), axis=1), ssd_norm_g=dssd_g[0])
    else:
        dos, cs = _combine_bwd(sv["outs"], sv["lses"], dy_tok, tag + "_comb")
        for g in range(N_DIL):
            dq, dk, dv = _dil_bwd(proj[f"q{g}"], proj[f"k{g}"], proj[f"v{g}"], dos[g], cs[g], sv["lses"][g], g,
                                  f"{tag}_att{g}")
            dproj.update({f"q{g}": dq, f"k{g}": dk, f"v{g}": dv})
    dils = _layer_dils(i)
    dhs = [None] * len(dils)
    dws = {}
    for nm, w in p["win"].items():
        o = _seg_order(nm)
        dhs[o] = _matmul(dproj[nm], w, tb=True, out_dtype=F32, add=dhs[o], name=f"{tag}_dh_{nm}")
        dws[nm] = _matmul(sv["h"][o], dproj[nm], ta=True, out_dtype=BF16, name=f"{tag}_dw_{nm}")
    gr["w_in"] = _join_dw_in(i, dws)
    dx, dxb, dnorm_g = _rmsnorm_bwd(sv["x"], p["norm_g"], dhs, dx_out, tag + "_norm", dils)
    gr["norm_g"] = dnorm_g[0]
    return dx, dxb, dmem_n, gr


def _pad_heads(v):
    return jnp.pad(v.reshape(1, SSD_H), ((0, 0), (0, DT_PAD)))


def _layer_params(i, small, w_in, w_mem_kv, w_out):
    p = {"norm_g": small[f"norm_g_{i}"], "win": _split_w_in(i, w_in), "wmkv": w_mem_kv, "wout": w_out}
    if _is_ssd(i):
        p.update(conv_w=small[f"conv_w_{i}"], conv_b=small[f"conv_b_{i}"], ssd_norm_g=small[f"ssd_norm_g_{i}"],
                 dt_bias_p=_pad_heads(small[f"dt_bias_{i}"]), a_log_p=_pad_heads(small[f"a_log_{i}"]),
                 dskip_lane=jnp.repeat(small[f"d_skip_{i}"], SSD_P).reshape(1, SSD_TOK))
    return p


def _local_step(x, mem, target, small, big):
    params = [_layer_params(i, small, *big[i]) for i in range(DEPTH)]
    mem_b = _rmsnorm_fwd(mem, small["mem_norm_g"], "mem_norm")[0]
    saved = []
    for i in range(DEPTH):
        x, sv = _layer_fwd(i, x, mem_b, params[i])
        saved.append(sv)
    loss, dx, dxb, dfinal = _final_loss(x, small["final_norm_g"], target)
    grads = {"final_norm_g": dfinal[0]}
    dmem_n = None
    for i in reversed(range(DEPTH)):
        dx, dxb, dmem_n, gr = _layer_bwd(i, saved[i], dx, dxb, dmem_n, mem_b, params[i])
        grads.update({f"{nm}_{i}": g for nm, g in gr.items()})
    _, _, dmem_g = _rmsnorm_bwd(mem, small["mem_norm_g"], [dmem_n], None, "mem_norm_b")
    grads["mem_norm_g"] = dmem_g[0]
    return loss[0, 0], dx, grads


BIG = ("w_in", "w_mem_kv", "w_out")
SMALL = [n for n in WEIGHTS if not n.startswith(BIG)]
PACK_ROWS = 8 * LANES


def _pack(vals):
    flat = jnp.concatenate([v.reshape(-1).astype(F32) for v in vals])
    padded = -(-flat.shape[0] // PACK_ROWS) * PACK_ROWS
    return jnp.pad(flat, (0, padded - flat.shape[0])).reshape(padded // LANES, LANES)


def _train_step(a, local_step):
    x, y, c = lax.axis_index("x"), lax.axis_index("y"), lax.axis_index("c")
    me = 4 * x + 2 * y + c
    big = []
    for i in range(DEPTH):
        shards = [a[f"{nm}_{i}"].astype(BF16) for nm in BIG]
        g_in, g_kv, g_out = _gather_two_level(shards, f"gather_w{i}")
        d, cs = shards[0].shape
        big.append((jnp.transpose(g_in, (1, 0, 2)).reshape(d, N_DEV * cs),
                    g_kv.reshape(N_DEV * g_kv.shape[1], g_kv.shape[2]),
                    g_out.reshape(N_DEV * g_out.shape[1], g_out.shape[2])))
    conv_names = [n for n in SMALL if n.startswith("conv_w")]
    conv_full = _gather_two_level([a[n] for n in conv_names], "gather_conv")
    small = {n: a[n] for n in SMALL}
    for n, gathered in zip(conv_names, conv_full):
        small[n] = jnp.transpose(gathered, (1, 0, 2)).reshape(gathered.shape[1], N_DEV * gathered.shape[2])

    loss_local, grad_x, grads = local_step(a["x"][0], a["mem"][0], a["loss_target"][0], small, big)
    loss = lax.psum(loss_local, ("x", "y", "c"))

    res = {}
    core = c.astype(jnp.int32).reshape(1)
    for i in range(DEPTH):
        g_in = grads[f"w_in_{i}"]
        d, cols = g_in.shape
        parts = [jnp.transpose(g_in.reshape(d, N_DEV, cols // N_DEV), (1, 0, 2))]
        for nm in BIG[1:]:
            g = grads[f"{nm}_{i}"]
            parts.append(g.reshape(N_DEV, g.shape[0] // N_DEV, g.shape[1]))
        swapped = _sibling_swap(parts, f"swap_w{i}")
        sums = [_chip_sum(p, s, core, f"chipsum_{nm}_{i}") for nm, p, s in zip(BIG, parts, swapped)]
        landed = _chip_exchange(sums, f"scatter_w{i}")
        for nm, p in zip(BIG, landed):
            n = f"{nm}_{i}"
            res[n] = _adamw(p, a[n], a["m_" + n], a["v_" + n], "adamw_" + n)

    gathered = _gather_two_level([_pack([grads[n] for n in SMALL])], "gather_small")[0]
    zero_conv = lambda pre: [jnp.zeros(small[n].shape, F32) if n in conv_names else a[pre + n] for n in SMALL]
    packed = _adamw(gathered, _pack(zero_conv("")), _pack(zero_conv("m_")), _pack(zero_conv("v_")), "adamw_small")
    off = 0
    for n in SMALL:
        size = math.prod(small[n].shape)
        if n in conv_names:
            rows, cols = a[n].shape
            whole = gathered.reshape(N_DEV, -1)[:, off:off + size].reshape(N_DEV, rows, N_DEV * cols)
            mine = lax.dynamic_slice_in_dim(whole, me * cols, cols, axis=2)
            res[n] = _adamw(mine, a[n], a["m_" + n], a["v_" + n], "adamw_" + n)
        else:
            res[n] = [o.reshape(-1)[off:off + size].reshape(a[n].shape) for o in packed]
        off += size
    outs = [loss, grad_x[None]]
    for k in range(4):
        outs += [res[n][k] for n in WEIGHTS]
    return tuple(outs)


def kernel(x, mem, mem_norm_g, final_norm_g, norm_g_0, w_in_0, conv_w_0, conv_b_0, dt_bias_0, a_log_0, d_skip_0, ssd_norm_g_0, w_mem_kv_0, w_out_0, norm_g_1, w_in_1, w_mem_kv_1, w_out_1, norm_g_2, w_in_2, conv_w_2, conv_b_2, dt_bias_2, a_log_2, d_skip_2, ssd_norm_g_2, w_mem_kv_2, w_out_2, norm_g_3, w_in_3, w_mem_kv_3, w_out_3, loss_target, m_mem_norm_g, m_final_norm_g, m_norm_g_0, m_w_in_0, m_conv_w_0, m_conv_b_0, m_dt_bias_0, m_a_log_0, m_d_skip_0, m_ssd_norm_g_0, m_w_mem_kv_0, m_w_out_0, m_norm_g_1, m_w_in_1, m_w_mem_kv_1, m_w_out_1, m_norm_g_2, m_w_in_2, m_conv_w_2, m_conv_b_2, m_dt_bias_2, m_a_log_2, m_d_skip_2, m_ssd_norm_g_2, m_w_mem_kv_2, m_w_out_2, m_norm_g_3, m_w_in_3, m_w_mem_kv_3, m_w_out_3, v_mem_norm_g, v_final_norm_g, v_norm_g_0, v_w_in_0, v_conv_w_0, v_conv_b_0, v_dt_bias_0, v_a_log_0, v_d_skip_0, v_ssd_norm_g_0, v_w_mem_kv_0, v_w_out_0, v_norm_g_1, v_w_in_1, v_w_mem_kv_1, v_w_out_1, v_norm_g_2, v_w_in_2, v_conv_w_2, v_conv_b_2, v_dt_bias_2, v_a_log_2, v_d_skip_2, v_ssd_norm_g_2, v_w_mem_kv_2, v_w_out_2, v_norm_g_3, v_w_in_3, v_w_mem_kv_3, v_w_out_3):
    vals = locals()
    return _train_step({n: vals[n] for n in INPUTS}, _local_step)
```

```python
import functools
import math

import jax
import jax.numpy as jnp
from jax import lax
from jax.experimental import pallas as pl
from jax.experimental.pallas import tpu as pltpu
from jax.experimental.pallas import tpu_sc as plsc

F32 = jnp.float32
BF16 = jnp.bfloat16
EPS = 1e-6
N_DEV = 8
VMEM_LIMIT_BYTES = 56 * 1024 * 1024


def _pick(n, prefs):
    for p in prefs:
        if n % p == 0:
            return p
    return n


def _params(sem):
    return pltpu.CompilerParams(dimension_semantics=sem, vmem_limit_bytes=VMEM_LIMIT_BYTES)


def _matmul(a, b, *, ta=False, tb=False, out_dtype=F32, add=None, name="mm"):
    if ta:
        k_dim, m_dim = a.shape
    else:
        m_dim, k_dim = a.shape
    n_dim = b.shape[0] if tb else b.shape[1]
    out_bytes = jnp.dtype(out_dtype).itemsize + (0 if add is None else add.dtype.itemsize)
    tm, tn, tk = _matmul_tiles(m_dim, n_dim, k_dim, a.dtype.itemsize, b.dtype.itemsize, out_bytes)
    nk = k_dim // tk
    dims = (((0,) if ta else (1,), (1,) if tb else (0,)), ((), ()))

    def body(*refs):
        if add is None:
            a_ref, b_ref, o_ref = refs[:3]
            add_ref = None
        else:
            a_ref, b_ref, add_ref, o_ref = refs[:4]
        part = lax.dot_general(a_ref[...].astype(BF16), b_ref[...].astype(BF16), dims,
                               preferred_element_type=F32)
        if nk == 1:
            o_ref[...] = (part if add_ref is None else part + add_ref[...].astype(F32)).astype(o_ref.dtype)
            return
        acc_ref = refs[-1]
        k = pl.program_id(2)

        @pl.when(k == 0)
        def _():
            acc_ref[...] = part if add_ref is None else part + add_ref[...].astype(F32)

        @pl.when(k > 0)
        def _():
            acc_ref[...] += part

        @pl.when(k == nk - 1)
        def _():
            o_ref[...] = acc_ref[...].astype(o_ref.dtype)

    a_spec = pl.BlockSpec((tk, tm), lambda i, j, k: (k, i)) if ta else pl.BlockSpec((tm, tk), lambda i, j, k: (i, k))
    b_spec = pl.BlockSpec((tn, tk), lambda i, j, k: (j, k)) if tb else pl.BlockSpec((tk, tn), lambda i, j, k: (k, j))
    o_spec = pl.BlockSpec((tm, tn), lambda i, j, k: (i, j))
    in_specs = [a_spec, b_spec] + ([o_spec] if add is not None else [])
    args = (a, b) + ((add,) if add is not None else ())
    return pl.pallas_call(
        body, name=name, grid=(m_dim // tm, n_dim // tn, nk),
        in_specs=in_specs, out_specs=o_spec,
        out_shape=jax.ShapeDtypeStruct((m_dim, n_dim), out_dtype),
        scratch_shapes=[pltpu.VMEM((tm, tn), F32)] if nk > 1 else [],
        compiler_params=_params(("parallel", "parallel", "arbitrary")),
    )(*args)


MATMUL_VMEM_BUDGET = 40 * 1024 * 1024


def _matmul_tiles(m_dim, n_dim, k_dim, a_bytes, b_bytes, out_bytes):
    best = None
    for tk in (k_dim, 4096, 2048, 1024, 512, 256, 128):
        if tk > k_dim or k_dim % tk:
            continue
        for tm in (1024, 512, 256, 128):
            if m_dim % tm:
                continue
            for tn in (2048, 1024, 512, 256, 128):
                if n_dim % tn:
                    continue
                vmem = 2 * (tm * tk * a_bytes + tk * tn * b_bytes + tm * tn * out_bytes) + 2 * tm * tn * 4
                if a_bytes == 4:
                    vmem += tm * tk * 2
                if vmem > MATMUL_VMEM_BUDGET:
                    continue
                score = (tm * tn * tk, tk, min(tm, tn))
                if best is None or score > best[0]:
                    best = (score, (tm, tn, tk))
    return best[1]


def _iota(shape, dim):
    return lax.broadcasted_iota(jnp.int32, shape, dim)


def _col(x, j):
    return jnp.sum(jnp.where(_iota(x.shape, 1) == j, x, 0.0), axis=1, keepdims=True)


def _silu(x):
    return x * jax.nn.sigmoid(x)


def _dsilu(x):
    s = jax.nn.sigmoid(x)
    return s * (1.0 + x * (1.0 - s))


def _chunks(width):
    return width // 128


def _scatter_rows(src_ref, nat_ref, dil, tm):
    n = tm // dil
    for cb in range(nat_ref.shape[0]):
        for r in range(dil):
            nat_ref[cb, pl.ds(r, n, stride=dil), :] = src_ref[r, :, cb * 128:(cb + 1) * 128].astype(F32)


def _gather_rows(nat_ref, dst_ref, dil, tm):
    n = tm // dil
    for cb in range(nat_ref.shape[0]):
        for r in range(dil):
            dst_ref[r, :, cb * 128:(cb + 1) * 128] = nat_ref[cb, pl.ds(r, n, stride=dil), :].astype(dst_ref.dtype)


def _load_chunks(nat_ref):
    return jnp.concatenate([nat_ref[cb] for cb in range(nat_ref.shape[0])], axis=1)


def _store_chunks(nat_ref, val):
    for cb in range(nat_ref.shape[0]):
        nat_ref[cb] = val[:, cb * 128:(cb + 1) * 128]


def _perm_spec(tm, dil, width):
    if dil == 1:
        return pl.BlockSpec((tm, width), lambda i: (i, 0))
    return pl.BlockSpec((dil, tm // dil, width), lambda i: (0, i, 0))


def _perm_shape(t, dil, width, dtype):
    return jax.ShapeDtypeStruct((t, width) if dil == 1 else (dil, t // dil, width), dtype)


def _rmsnorm_fwd(x, g, name, dils=(1,)):
    t, d = x.shape
    tm = _pick(t, (512, 256, 128))

    permuted = any(dil > 1 for dil in dils)

    def body(x_ref, g_ref, *refs):
        h_refs = refs[:len(dils)]
        xv = x_ref[...]
        rs = lax.rsqrt(jnp.mean(xv * xv, axis=-1, keepdims=True) + EPS)
        hv = xv * rs * g_ref[...]
        if permuted:
            _store_chunks(refs[-1], hv)
        for dil, h_ref in zip(dils, h_refs):
            if dil == 1:
                h_ref[...] = hv.astype(BF16)
            else:
                _gather_rows(refs[-1], h_ref, dil, tm)

    outs = pl.pallas_call(
        body, name=name, grid=(t // tm,),
        in_specs=[pl.BlockSpec((tm, d), lambda i: (i, 0)), pl.BlockSpec((1, d), lambda i: (0, 0))],
        out_specs=[_perm_spec(tm, dil, d) for dil in dils],
        out_shape=[_perm_shape(t, dil, d, BF16) for dil in dils],
        scratch_shapes=[pltpu.VMEM((_chunks(d), tm, 128), F32)] if permuted else [],
        compiler_params=_params(("parallel",)),
    )(x, g.reshape(1, d))
    return [o.reshape(t, d) for o in outs]


def _rmsnorm_bwd(x, g, dhs, dres, name, dils=(1,)):
    t, d = x.shape
    tm = _pick(t, (512, 256, 128) if len(dils) == 1 else (256, 128))
    n_in = len(dils)

    def body(*refs):
        x_ref, g_ref = refs[:2]
        dh_refs = refs[2:2 + n_in]
        dres_ref = refs[2 + n_in] if dres is not None else None
        dx_ref, dxb_ref, dg_ref = refs[-4:-1]
        nat_ref = refs[-1]
        dhv = None
        for dil, dh_ref in zip(dils, dh_refs):
            if dil == 1:
                term = dh_ref[...].astype(F32)
            else:
                _scatter_rows(dh_ref, nat_ref, dil, tm)
                term = _load_chunks(nat_ref)
            dhv = term if dhv is None else dhv + term
        xv = x_ref[...]
        r = lax.rsqrt(jnp.mean(xv * xv, axis=-1, keepdims=True) + EPS)
        xhat = xv * r
        dxh = dhv * g_ref[...]
        dx = r * (dxh - xhat * jnp.mean(dxh * xhat, axis=-1, keepdims=True))
        if dres_ref is not None:
            dx = dx + dres_ref[...]
        dx_ref[...] = dx
        dxb_ref[...] = dx.astype(BF16)
        part = jnp.sum(dhv * xhat, axis=0, keepdims=True)

        @pl.when(pl.program_id(0) == 0)
        def _():
            dg_ref[...] = part

        @pl.when(pl.program_id(0) > 0)
        def _():
            dg_ref[...] += part

    row = pl.BlockSpec((tm, d), lambda i: (i, 0))
    vec = pl.BlockSpec((1, d), lambda i: (0, 0))
    in_specs = [row, vec] + [_perm_spec(tm, dil, d) for dil in dils] + ([row] if dres is not None else [])
    dh_args = [dh if dil == 1 else dh.reshape(dil, t // dil, d) for dil, dh in zip(dils, dhs)]
    args = (x, g.reshape(1, d), *dh_args) + ((dres,) if dres is not None else ())
    return pl.pallas_call(
        body, name=name, grid=(t // tm,), in_specs=in_specs, out_specs=[row, row, vec],
        out_shape=[jax.ShapeDtypeStruct((t, d), F32), jax.ShapeDtypeStruct((t, d), BF16),
                   jax.ShapeDtypeStruct((1, d), F32)],
        scratch_shapes=[pltpu.VMEM((_chunks(d), tm, 128), F32)],
        compiler_params=_params(("arbitrary",)),
    )(*args)


def _final_loss(x, g, target, name="final_loss"):
    t, d = x.shape
    tm = _pick(t, (512, 256, 128))

    def body(x_ref, g_ref, t_ref, loss_ref, dx_ref, dxb_ref, dg_ref):
        xv = x_ref[...]
        gv = g_ref[...]
        r = lax.rsqrt(jnp.mean(xv * xv, axis=-1, keepdims=True) + EPS)
        xhat = xv * r
        e = xhat * gv - t_ref[...]
        lpart = jnp.zeros((1, 128), F32) + (0.5 / d) * jnp.sum(e * e)
        dy = e * (1.0 / d)
        dxh = dy * gv
        dx = r * (dxh - xhat * jnp.mean(dxh * xhat, axis=-1, keepdims=True))
        dx_ref[...] = dx
        dxb_ref[...] = dx.astype(BF16)
        gpart = jnp.sum(dy * xhat, axis=0, keepdims=True)

        @pl.when(pl.program_id(0) == 0)
        def _():
            dg_ref[...] = gpart
            loss_ref[...] = lpart

        @pl.when(pl.program_id(0) > 0)
        def _():
            dg_ref[...] += gpart
            loss_ref[...] += lpart

    row = pl.BlockSpec((tm, d), lambda i: (i, 0))
    vec = pl.BlockSpec((1, d), lambda i: (0, 0))
    return pl.pallas_call(
        body, name=name, grid=(t // tm,), in_specs=[row, vec, row],
        out_specs=[pl.BlockSpec((1, 128), lambda i: (0, 0)), row, row, vec],
        out_shape=[jax.ShapeDtypeStruct((1, 128), F32), jax.ShapeDtypeStruct((t, d), F32),
                   jax.ShapeDtypeStruct((t, d), BF16), jax.ShapeDtypeStruct((1, d), F32)],
        compiler_params=_params(("arbitrary",)),
    )(x, g.reshape(1, d), target)


CONV_K = 4
HALO = 8


def _shift_down(cur, prev8, s):
    rolled = pltpu.roll(cur, s, 0)
    fix = pltpu.roll(prev8, s, 0)
    head = jnp.where(_iota((HALO, cur.shape[1]), 0) < s, fix, rolled[:HALO])
    return jnp.concatenate([head, rolled[HALO:]], axis=0)


def _shift_up(cur, next8, s):
    n = cur.shape[0]
    rolled = pltpu.roll(cur, n - s, 0)
    fix = pltpu.roll(next8, HALO - s, 0)
    tail = jnp.where(_iota((HALO, cur.shape[1]), 0) >= HALO - s, fix, rolled[n - HALO:])
    return jnp.concatenate([rolled[:n - HALO], tail], axis=0)


def _conv_pre(u_ref, up_ref, w_ref, b_ref, first):
    cur = u_ref[...]
    prev8 = jnp.where(first, 0.0, up_ref[...])
    w = w_ref[...]
    shifted = [cur] + [_shift_down(cur, prev8, s) for s in (1, 2, 3)]
    pre = b_ref[...] + sum(w[CONV_K - 1 - s:CONV_K - s, :] * shifted[s] for s in range(CONV_K))
    return pre, shifted


def _conv_specs(tm, tc):
    nb = tm // HALO
    cur = pl.BlockSpec((tm, tc), lambda j, i: (i, j))
    prev = pl.BlockSpec((HALO, tc), lambda j, i: (jnp.maximum(i * nb - 1, 0), j))
    wspec = pl.BlockSpec((CONV_K, tc), lambda j, i: (0, j))
    bspec = pl.BlockSpec((1, tc), lambda j, i: (0, j))
    return cur, prev, wspec, bspec


def _conv_fwd(u, w, b, name):
    t, c = u.shape
    tm, tc = _pick(t, (512, 256, 128)), _pick(c, (1024, 512, 256, 128))
    cur, prev, wspec, bspec = _conv_specs(tm, tc)

    def body(u_ref, up_ref, w_ref, b_ref, o_ref):
        pre, _ = _conv_pre(u_ref, up_ref, w_ref, b_ref, pl.program_id(1) == 0)
        o_ref[...] = _silu(pre)

    return pl.pallas_call(
        body, name=name, grid=(c // tc, t // tm), in_specs=[cur, prev, wspec, bspec], out_specs=cur,
        out_shape=jax.ShapeDtypeStruct((t, c), F32),
        compiler_params=_params(("parallel", "parallel")),
    )(u, u, w, b.reshape(1, c))


def _conv_bwd_pre(u, w, b, dy, name):
    t, c = u.shape
    tm, tc = _pick(t, (512, 256, 128)), _pick(c, (1024, 512, 256, 128))
    cur, prev, wspec, bspec = _conv_specs(tm, tc)

    def body(u_ref, up_ref, w_ref, b_ref, dy_ref, dpre_ref, dw_ref, db_ref):
        i = pl.program_id(1)
        pre, shifted = _conv_pre(u_ref, up_ref, w_ref, b_ref, i == 0)
        dpre = dy_ref[...] * _dsilu(pre)
        dpre_ref[...] = dpre
        dw = jnp.concatenate([jnp.sum(dpre * shifted[CONV_K - 1 - k], axis=0, keepdims=True) for k in range(CONV_K)], axis=0)
        db = jnp.sum(dpre, axis=0, keepdims=True)

        @pl.when(i == 0)
        def _():
            dw_ref[...] = dw
            db_ref[...] = db

        @pl.when(i > 0)
        def _():
            dw_ref[...] += dw
            db_ref[...] += db

    return pl.pallas_call(
        body, name=name, grid=(c // tc, t // tm), in_specs=[cur, prev, wspec, bspec, cur],
        out_specs=[cur, wspec, bspec],
        out_shape=[jax.ShapeDtypeStruct((t, c), F32), jax.ShapeDtypeStruct((CONV_K, c), F32),
                   jax.ShapeDtypeStruct((1, c), F32)],
        compiler_params=_params(("parallel", "arbitrary")),
    )(u, u, w, b.reshape(1, c), dy)


def _conv_bwd_in(dpre, w, name):
    t, c = dpre.shape
    tm, tc = _pick(t, (512, 256, 128)), _pick(c, (1024, 512, 256, 128))
    nb = tm // HALO
    last = t // tm - 1
    cur = pl.BlockSpec((tm, tc), lambda j, i: (i, j))
    nxt = pl.BlockSpec((HALO, tc), lambda j, i: (jnp.minimum((i + 1) * nb, t // HALO - 1), j))
    wspec = pl.BlockSpec((CONV_K, tc), lambda j, i: (0, j))

    def body(d_ref, dn_ref, w_ref, o_ref):
        cur_v = d_ref[...]
        next8 = jnp.where(pl.program_id(1) == last, 0.0, dn_ref[...])
        wv = w_ref[...]
        acc = wv[CONV_K - 1:CONV_K, :] * cur_v
        for s in (1, 2, 3):
            acc = acc + wv[CONV_K - 1 - s:CONV_K - s, :] * _shift_up(cur_v, next8, s)
        o_ref[...] = acc.astype(o_ref.dtype)

    return pl.pallas_call(
        body, name=name, grid=(c // tc, t // tm), in_specs=[cur, nxt, wspec], out_specs=cur,
        out_shape=jax.ShapeDtypeStruct((t, c), BF16),
        compiler_params=_params(("parallel", "parallel")),
    )(dpre, dpre, w)


MEM_HEADS = 4
NT_DIMS = (((1,), (1,)), ((), ()))
TN_DIMS = (((0,), (0,)), ((), ()))


def _dot(a, b, dims=None):
    if dims is None:
        return jnp.dot(a, b, preferred_element_type=F32)
    return lax.dot_general(a, b, dims, preferred_element_type=F32)


def _memattn_probs(q, mk, scale):
    s = _dot(q, mk, NT_DIMS) * scale
    s = s - jnp.max(s, axis=-1, keepdims=True)
    p = jnp.exp(s)
    return p / jnp.sum(p, axis=-1, keepdims=True)


def _memattn_fwd(q, mkv, name):
    t, wd = q.shape
    m = mkv.shape[0]
    hd = wd // MEM_HEADS
    scale = hd ** -0.5
    tm = _pick(t, (512, 256, 128))

    def body(q_ref, mkv_ref, o_ref):
        for h in range(MEM_HEADS):
            cols = slice(h * hd, (h + 1) * hd)
            p = _memattn_probs(q_ref[:, cols], mkv_ref[:, cols], scale)
            o_ref[:, cols] = _dot(p.astype(BF16), mkv_ref[:, wd + h * hd:wd + (h + 1) * hd])

    return pl.pallas_call(
        body, name=name, grid=(t // tm,),
        in_specs=[pl.BlockSpec((tm, wd), lambda i: (i, 0)), pl.BlockSpec((m, 2 * wd), lambda i: (0, 0))],
        out_specs=pl.BlockSpec((tm, wd), lambda i: (i, 0)),
        out_shape=jax.ShapeDtypeStruct((t, wd), F32),
        compiler_params=_params(("parallel",)),
    )(q, mkv)


def _memattn_bwd(q, mkv, dy, name):
    t, wd = q.shape
    m = mkv.shape[0]
    hd = wd // MEM_HEADS
    scale = hd ** -0.5
    tm = _pick(t, (512, 256, 128))

    def body(q_ref, mkv_ref, dy_ref, dq_ref, dmkv_ref):
        i = pl.program_id(0)

        @pl.when(i == 0)
        def _():
            dmkv_ref[...] = jnp.zeros_like(dmkv_ref)

        for h in range(MEM_HEADS):
            cols = slice(h * hd, (h + 1) * hd)
            vcols = slice(wd + h * hd, wd + (h + 1) * hd)
            qh = q_ref[:, cols]
            p = _memattn_probs(qh, mkv_ref[:, cols], scale)
            dyh = dy_ref[:, cols].astype(BF16)
            dp = _dot(dyh, mkv_ref[:, vcols], NT_DIMS)
            ds = (p * (dp - jnp.sum(dp * p, axis=-1, keepdims=True)) * scale).astype(BF16)
            dq_ref[:, cols] = _dot(ds, mkv_ref[:, cols]).astype(dq_ref.dtype)
            dmkv_ref[:, cols] += _dot(ds, qh, TN_DIMS)
            dmkv_ref[:, vcols] += _dot(p.astype(BF16), dyh, TN_DIMS)

    return pl.pallas_call(
        body, name=name, grid=(t // tm,),
        in_specs=[pl.BlockSpec((tm, wd), lambda i: (i, 0)), pl.BlockSpec((m, 2 * wd), lambda i: (0, 0)),
                  pl.BlockSpec((tm, wd), lambda i: (i, 0))],
        out_specs=[pl.BlockSpec((tm, wd), lambda i: (i, 0)), pl.BlockSpec((m, 2 * wd), lambda i: (0, 0))],
        out_shape=[jax.ShapeDtypeStruct((t, wd), BF16), jax.ShapeDtypeStruct((m, 2 * wd), F32)],
        compiler_params=_params(("arbitrary",)),
    )(q, mkv, dy)


NORM_GROUPS = 8


def _gate_fwd(y_tok, y_mem, z, norm_g, name):
    t, tok = y_tok.shape
    mem = y_mem.shape[1]
    mix = tok + mem
    gw = tok // NORM_GROUPS
    tm = _pick(t, (256, 128))

    def body(*refs):
        if norm_g is None:
            yt_ref, ym_ref, z_ref, o_ref = refs
        else:
            yt_ref, ym_ref, z_ref, g_ref, o_ref = refs
        u = yt_ref[...] * _silu(z_ref[:, :tok])
        if norm_g is None:
            o_ref[:, :tok] = u.astype(o_ref.dtype)
        else:
            for k in range(NORM_GROUPS):
                uk = u[:, k * gw:(k + 1) * gw]
                r = lax.rsqrt(jnp.mean(uk * uk, axis=-1, keepdims=True) + EPS)
                o_ref[:, k * gw:(k + 1) * gw] = (uk * r * g_ref[:, k * gw:(k + 1) * gw]).astype(o_ref.dtype)
        o_ref[:, tok:] = (ym_ref[...] * _silu(z_ref[:, tok:])).astype(o_ref.dtype)

    in_specs = [pl.BlockSpec((tm, tok), lambda i: (i, 0)), pl.BlockSpec((tm, mem), lambda i: (i, 0)),
                pl.BlockSpec((tm, mix), lambda i: (i, 0))]
    args = [y_tok, y_mem, z]
    if norm_g is not None:
        in_specs.append(pl.BlockSpec((1, tok), lambda i: (0, 0)))
        args.append(norm_g.reshape(1, tok))
    return pl.pallas_call(
        body, name=name, grid=(t // tm,), in_specs=in_specs,
        out_specs=pl.BlockSpec((tm, mix), lambda i: (i, 0)),
        out_shape=jax.ShapeDtypeStruct((t, mix), BF16),
        compiler_params=_params(("parallel",)),
    )(*args)


def _gate_bwd(y_tok, y_mem, z, norm_g, dgated, name):
    t, tok = y_tok.shape
    mem = y_mem.shape[1]
    mix = tok + mem
    gw = tok // NORM_GROUPS
    tm = _pick(t, (256, 128))

    def body(*refs):
        if norm_g is None:
            yt_ref, ym_ref, z_ref, dg_ref, dyt_ref, dym_ref, dz_ref, dn_ref = refs
        else:
            yt_ref, ym_ref, z_ref, dg_ref, g_ref, dyt_ref, dym_ref, dz_ref, dn_ref = refs
        i = pl.program_id(0)
        zt = z_ref[:, :tok]
        yt = yt_ref[...]
        sz = _silu(zt)
        dout = dg_ref[:, :tok].astype(F32)
        if norm_g is None:
            du = dout
            dn = jnp.zeros((1, tok), F32)
        else:
            u = yt * sz
            dus, dns = [], []
            for k in range(NORM_GROUPS):
                uk = u[:, k * gw:(k + 1) * gw]
                r = lax.rsqrt(jnp.mean(uk * uk, axis=-1, keepdims=True) + EPS)
                nk = uk * r
                dk = dout[:, k * gw:(k + 1) * gw]
                dns.append(jnp.sum(dk * nk, axis=0, keepdims=True))
                dnk = dk * g_ref[:, k * gw:(k + 1) * gw]
                dus.append(r * (dnk - nk * jnp.mean(dnk * nk, axis=-1, keepdims=True)))
            du = jnp.concatenate(dus, axis=1)
            dn = jnp.concatenate(dns, axis=1)
        dyt_ref[...] = du * sz
        dz_ref[:, :tok] = (du * yt * _dsilu(zt)).astype(dz_ref.dtype)
        zm = z_ref[:, tok:]
        dm = dg_ref[:, tok:].astype(F32)
        dym_ref[...] = dm * _silu(zm)
        dz_ref[:, tok:] = (dm * ym_ref[...] * _dsilu(zm)).astype(dz_ref.dtype)

        @pl.when(i == 0)
        def _():
            dn_ref[...] = dn

        @pl.when(i > 0)
        def _():
            dn_ref[...] += dn

    tok_spec = pl.BlockSpec((tm, tok), lambda i: (i, 0))
    mem_spec = pl.BlockSpec((tm, mem), lambda i: (i, 0))
    mix_spec = pl.BlockSpec((tm, mix), lambda i: (i, 0))
    vec = pl.BlockSpec((1, tok), lambda i: (0, 0))
    in_specs = [tok_spec, mem_spec, mix_spec, mix_spec]
    args = [y_tok, y_mem, z, dgated]
    if norm_g is not None:
        in_specs.append(vec)
        args.append(norm_g.reshape(1, tok))
    return pl.pallas_call(
        body, name=name, grid=(t // tm,), in_specs=in_specs,
        out_specs=[tok_spec, mem_spec, mix_spec, vec],
        out_shape=[jax.ShapeDtypeStruct((t, tok), F32), jax.ShapeDtypeStruct((t, mem), F32),
                   jax.ShapeDtypeStruct((t, mix), BF16), jax.ShapeDtypeStruct((1, tok), F32)],
        compiler_params=_params(("arbitrary",)),
    )(*args)


SSD_Q = 128
SSD_N = 128
SSD_P = 64
SSD_G = 8
SSD_HPG = 6
SSD_H = SSD_G * SSD_HPG
SSD_TOK = SSD_H * SSD_P
SSD_XBC = SSD_TOK + 2 * SSD_G * SSD_N
LANES = 128
HIGHEST = lax.Precision.HIGHEST


def _softplus(x):
    return jnp.maximum(x, 0.0) + jnp.log(1.0 + jnp.exp(-jnp.abs(x)))


def _ssd_common(dtr_ref, bias_ref, alog_ref):
    sq = (SSD_Q, LANES)
    pre = dtr_ref[...] + bias_ref[...]
    dt = _softplus(pre)
    a = -jnp.exp(alog_ref[...])
    tril = (_iota(sq, 0) >= _iota(sq, 1)).astype(F32)
    acs = jnp.dot(tril, dt * a, precision=HIGHEST, preferred_element_type=F32)
    return pre, dt, a, tril, acs, acs.T


def _pair_terms(dt, acs, acs_t, h0):
    hi = _iota((SSD_Q, LANES), 1) >= SSD_P
    heads = []
    for j in range(2):
        h = h0 + j
        a_col = _col(acs, h)
        a_row = acs_t[h:h + 1, :]
        a_last = _col(acs[SSD_Q - 1:SSD_Q, :], h)
        heads.append((h, a_col, a_row, a_last, hi if j else jnp.logical_not(hi)))
    dtl = jnp.where(hi, _col(dt, h0 + 1), _col(dt, h0))
    scale = jnp.where(hi, jnp.exp(heads[1][1]), jnp.exp(heads[0][1]))
    dec_last = jnp.where(hi[:1], jnp.exp(heads[1][3]), jnp.exp(heads[0][3]))
    return heads, dtl, scale, dec_last


def _decay(a_col, a_row):
    causal = _iota((SSD_Q, SSD_Q), 0) >= _iota((SSD_Q, SSD_Q), 1)
    return jnp.where(causal, jnp.exp(jnp.minimum(a_col - a_row, 0.0)), 0.0)


def _ssd_fwd(xbc, dt_raw, dt_bias, a_log, dskip_lane, name):
    t = xbc.shape[0]
    nc = t // SSD_Q

    def body(xbc_ref, dtr_ref, bias_ref, alog_ref, dsk_ref, y_ref, hs_ref, h_ref):
        @pl.when(pl.program_id(0) == 0)
        def _():
            h_ref[...] = jnp.zeros_like(h_ref)

        _, dt, _, _, acs, acs_t = _ssd_common(dtr_ref, bias_ref, alog_ref)
        for g in range(SSD_G):
            bg_f = xbc_ref[:, SSD_TOK + g * SSD_N:SSD_TOK + (g + 1) * SSD_N]
            bg = bg_f.astype(BF16)
            cg = xbc_ref[:, SSD_TOK + SSD_G * SSD_N + g * SSD_N:SSD_TOK + SSD_G * SSD_N + (g + 1) * SSD_N].astype(BF16)
            cb = _dot(cg, bg, NT_DIMS)
            for pr in range(SSD_HPG // 2):
                h0 = g * SSD_HPG + 2 * pr
                lanes = slice(h0 * SSD_P, (h0 + 2) * SSD_P)
                heads, dtl, scale, dec_last = _pair_terms(dt, acs, acs_t, h0)
                xs = xbc_ref[:, lanes]
                xdt = xs * dtl
                hp = h_ref[:, lanes]
                hs_ref[:, lanes] = hp
                y = _dot(cg, hp.astype(BF16)) * scale + dsk_ref[:, lanes] * xs
                snew = hp * dec_last
                for _, a_col, a_row, a_last, mask in heads:
                    xm = jnp.where(mask, xdt, 0.0).astype(BF16)
                    y = y + _dot((cb * _decay(a_col, a_row)).astype(BF16), xm)
                    bw = (bg_f * jnp.exp(a_last - a_col)).astype(BF16)
                    snew = snew + _dot(bw, xm, TN_DIMS)
                y_ref[:, lanes] = y
                h_ref[:, lanes] = snew

    row = lambda w: pl.BlockSpec((SSD_Q, w), lambda c: (c, 0))
    vec = lambda w: pl.BlockSpec((1, w), lambda c: (0, 0))
    return pl.pallas_call(
        body, name=name, grid=(nc,),
        in_specs=[row(SSD_XBC), row(LANES), vec(LANES), vec(LANES), vec(SSD_TOK)],
        out_specs=[row(SSD_TOK), row(SSD_TOK)],
        out_shape=[jax.ShapeDtypeStruct((t, SSD_TOK), F32), jax.ShapeDtypeStruct((nc * SSD_N, SSD_TOK), F32)],
        scratch_shapes=[pltpu.VMEM((SSD_N, SSD_TOK), F32)],
        compiler_params=_params(("arbitrary",)),
    )(xbc, dt_raw, dt_bias, a_log, dskip_lane)


def _ssd_bwd(xbc, dt_raw, dt_bias, a_log, dskip_lane, hs, dy, name):
    t = xbc.shape[0]
    nc = t // SSD_Q
    sq = (SSD_Q, LANES)

    def body(xbc_ref, dtr_ref, bias_ref, alog_ref, dsk_ref, hs_ref, dy_ref,
             dxbc_ref, ddtr_ref, dbias_ref, dalog_ref, ddsk_ref, dh_ref):
        first = pl.program_id(0) == 0

        @pl.when(first)
        def _():
            dh_ref[...] = jnp.zeros_like(dh_ref)
            dbias_ref[...] = jnp.zeros_like(dbias_ref)
            dalog_ref[...] = jnp.zeros_like(dalog_ref)
            ddsk_ref[...] = jnp.zeros_like(ddsk_ref)

        pre, dt, a, tril, acs, acs_t = _ssd_common(dtr_ref, bias_ref, alog_ref)
        lane = _iota(sq, 1)
        sub = _iota(sq, 0)
        causal = sub >= lane
        d_acs = jnp.zeros(sq, F32)
        d_acs_row = jnp.zeros(sq, F32)
        d_last = jnp.zeros((1, LANES), F32)
        ddt = jnp.zeros(sq, F32)
        for g in range(SSD_G):
            bcols = slice(SSD_TOK + g * SSD_N, SSD_TOK + (g + 1) * SSD_N)
            ccols = slice(SSD_TOK + SSD_G * SSD_N + g * SSD_N, SSD_TOK + SSD_G * SSD_N + (g + 1) * SSD_N)
            bg_f = xbc_ref[:, bcols]
            bg = bg_f.astype(BF16)
            cg = xbc_ref[:, ccols].astype(BF16)
            cb = _dot(cg, bg, NT_DIMS)
            dcb = jnp.zeros(sq, F32)
            dbg = jnp.zeros(sq, F32)
            dcg = jnp.zeros(sq, F32)
            for pr in range(SSD_HPG // 2):
                h0 = g * SSD_HPG + 2 * pr
                lanes = slice(h0 * SSD_P, (h0 + 2) * SSD_P)
                heads, dtl, scale, dec_last = _pair_terms(dt, acs, acs_t, h0)
                xs = xbc_ref[:, lanes]
                xdt = xs * dtl
                dyv = dy_ref[:, lanes]
                hp = hs_ref[:, lanes]
                dhn = dh_ref[:, lanes]
                hp_b = hp.astype(BF16)
                dys = (dyv * scale).astype(BF16)
                yoff_dy = dyv * _dot(cg, hp_b) * scale
                dcg = dcg + _dot(dys, hp_b, NT_DIMS)
                dhc = _dot(cg, dys, TN_DIMS)
                hh = dhn * hp
                dxdt = jnp.zeros(sq, F32)
                for h, a_col, a_row, a_last, mask in heads:
                    dec = _decay(a_col, a_row)
                    m = cb * dec
                    dym = jnp.where(mask, dyv, 0.0).astype(BF16)
                    xm = jnp.where(mask, xdt, 0.0).astype(BF16)
                    dhm = jnp.where(mask, dhn, 0.0).astype(BF16)
                    w = jnp.exp(a_last - a_col)
                    dxdt = dxdt + _dot(m.astype(BF16), dym, TN_DIMS) + _dot((bg_f * w).astype(BF16), dhm)
                    dm = jnp.where(causal, _dot(dym, xm, NT_DIMS), 0.0)
                    dcb = dcb + dm * dec
                    e = dm * m
                    gj = _dot(xm, dhm, NT_DIMS)
                    dbg = dbg + w * gj
                    wdw = w * jnp.sum(bg_f * gj, axis=1, keepdims=True)
                    col = (jnp.sum(e, axis=1, keepdims=True)
                           + jnp.sum(jnp.where(mask, yoff_dy, 0.0), axis=1, keepdims=True) - wdw)
                    d_acs = d_acs + jnp.where(lane == h, col, 0.0)
                    d_acs_row = d_acs_row + jnp.where(sub == h, jnp.sum(e, axis=0, keepdims=True), 0.0)
                    last = jnp.sum(wdw) + jnp.exp(a_last) * jnp.sum(jnp.where(mask, hh, 0.0))
                    d_last = d_last + jnp.where(lane[:1] == h, last, 0.0)
                dxbc_ref[:, lanes] = dxdt * dtl + dsk_ref[:, lanes] * dyv
                tt = dxdt * xs
                for h, _, _, _, mask in heads:
                    ddt = ddt + jnp.where(lane == h, jnp.sum(jnp.where(mask, tt, 0.0), axis=1, keepdims=True), 0.0)
                ddsk_ref[:, lanes] += jnp.sum(dyv * xs, axis=0, keepdims=True)
                dh_ref[:, lanes] = dhn * dec_last + dhc
            dcb_b = dcb.astype(BF16)
            dxbc_ref[:, bcols] = dbg + _dot(dcb_b, cg, TN_DIMS)
            dxbc_ref[:, ccols] = dcg + _dot(dcb_b, bg)
        d_tot = d_acs - d_acs_row.T + jnp.where(sub == SSD_Q - 1, d_last, 0.0)
        ddta = lax.dot_general(tril, d_tot, TN_DIMS, precision=HIGHEST, preferred_element_type=F32)
        ddt = ddt + ddta * a
        dalog_ref[...] += jnp.sum(ddta * dt, axis=0, keepdims=True) * a
        ddtr = ddt * jax.nn.sigmoid(pre)
        ddtr_ref[...] = ddtr
        dbias_ref[...] += jnp.sum(ddtr, axis=0, keepdims=True)

    rev = lambda w: pl.BlockSpec((SSD_Q, w), lambda i: (nc - 1 - i, 0))
    vec = lambda w: pl.BlockSpec((1, w), lambda i: (0, 0))
    return pl.pallas_call(
        body, name=name, grid=(nc,),
        in_specs=[rev(SSD_XBC), rev(LANES), vec(LANES), vec(LANES), vec(SSD_TOK), rev(SSD_TOK), rev(SSD_TOK)],
        out_specs=[rev(SSD_XBC), rev(LANES), vec(LANES), vec(LANES), vec(SSD_TOK)],
        out_shape=[jax.ShapeDtypeStruct((t, SSD_XBC), F32), jax.ShapeDtypeStruct((t, LANES), F32),
                   jax.ShapeDtypeStruct((1, LANES), F32), jax.ShapeDtypeStruct((1, LANES), F32),
                   jax.ShapeDtypeStruct((1, SSD_TOK), F32)],
        scratch_shapes=[pltpu.VMEM((SSD_N, SSD_TOK), F32)],
        compiler_params=_params(("arbitrary",)),
    )(xbc, dt_raw, dt_bias, a_log, dskip_lane, hs, dy)


ATT_E = 128
ATT_H = 24
ATT_W = 128
ATT_TOK = ATT_H * ATT_E
DILATED_GROUPS = ((128, 1), (512, 4), (2048, 16))
N_DIL = len(DILATED_GROUPS)
ALIBI_MAX_EXP = 8.0
MASKED = -1e30


def _alibi_slopes(group):
    n = N_DIL * ATT_H
    return [2.0 ** (-ALIBI_MAX_EXP * (group * ATT_H + h + 1) / n) for h in range(ATT_H)]


def _att_scores(qh, kk, rel, valid, slope_d):
    s = _dot(qh, kk, NT_DIMS) * (ATT_E ** -0.5) - slope_d * rel
    return jnp.where(valid, s, MASKED)


def _att_rel(j):
    shp = (ATT_W, 2 * ATT_W)
    kpos = _iota(shp, 1)
    rel = _iota(shp, 0) + ATT_W - kpos
    valid = (rel >= 0) & (rel <= ATT_W) & ((kpos >= ATT_W) | (j > 0))
    return rel.astype(F32), valid


def _dil_fwd(q, k, v, group, name):
    t = q.shape[0]
    dil = DILATED_GROUPS[group][1]
    slopes = _alibi_slopes(group)
    nb = t // dil // ATT_W

    def body(q_ref, kp_ref, kc_ref, vp_ref, vc_ref, o_ref, lse_ref):
        rel, valid = _att_rel(pl.program_id(1))
        lane = _iota((ATT_W, LANES), 1)
        lse_all = jnp.zeros((ATT_W, LANES), F32)
        for h in range(ATT_H):
            cols = slice(h * ATT_E, (h + 1) * ATT_E)
            kk = jnp.concatenate([kp_ref[:, cols], kc_ref[:, cols]], axis=0)
            vv = jnp.concatenate([vp_ref[:, cols], vc_ref[:, cols]], axis=0)
            s = _att_scores(q_ref[:, cols], kk, rel, valid, slopes[h] * dil)
            m = jnp.max(s, axis=-1, keepdims=True)
            p = jnp.exp(s - m)
            den = jnp.sum(p, axis=-1, keepdims=True)
            o_ref[:, cols] = _dot(p.astype(BF16), vv) / den
            lse_all = jnp.where(lane == h, m + jnp.log(den), lse_all)
        lse_ref[...] = lse_all

    cur = pl.BlockSpec((ATT_W, ATT_TOK), lambda r, j: (r * nb + j, 0))
    prev = pl.BlockSpec((ATT_W, ATT_TOK), lambda r, j: (r * nb + jnp.maximum(j - 1, 0), 0))
    small = pl.BlockSpec((ATT_W, LANES), lambda r, j: (r * nb + j, 0))
    return pl.pallas_call(
        body, name=name, grid=(dil, nb),
        in_specs=[cur, prev, cur, prev, cur], out_specs=[cur, small],
        out_shape=[jax.ShapeDtypeStruct((t, ATT_TOK), F32), jax.ShapeDtypeStruct((t, LANES), F32)],
        compiler_params=_params(("parallel", "parallel")),
    )(q, k, k, v, v)


def _dil_bwd(q, k, v, do, cterm, lse, group, name):
    t = q.shape[0]
    dil = DILATED_GROUPS[group][1]
    slopes = _alibi_slopes(group)
    nb = t // dil // ATT_W

    def body(q_ref, kp_ref, kc_ref, vp_ref, vc_ref, do_ref, c_ref, lse_ref,
             dq_ref, dk_ref, dv_ref, ck_ref, cv_ref):
        j = pl.program_id(1)

        @pl.when(j == 0)
        def _():
            ck_ref[...] = jnp.zeros_like(ck_ref)
            cv_ref[...] = jnp.zeros_like(cv_ref)

        @pl.when(j < nb)
        def _():
            rel, valid = _att_rel(j)
            cv_, lv = c_ref[...], lse_ref[...]
            for h in range(ATT_H):
                cols = slice(h * ATT_E, (h + 1) * ATT_E)
                qh = q_ref[:, cols]
                kk = jnp.concatenate([kp_ref[:, cols], kc_ref[:, cols]], axis=0)
                vv = jnp.concatenate([vp_ref[:, cols], vc_ref[:, cols]], axis=0)
                s = _att_scores(qh, kk, rel, valid, slopes[h] * dil)
                p = jnp.where(valid, jnp.exp(s - _col(lv, h)), 0.0)
                do = do_ref[:, cols]
                dp = _dot(do, vv, NT_DIMS)
                ds = (p * (dp + _col(cv_, h)) * (ATT_E ** -0.5)).astype(BF16)
                dq_ref[:, cols] = _dot(ds, kk).astype(dq_ref.dtype)
                dkk = _dot(ds, qh, TN_DIMS)
                dvv = _dot(p.astype(BF16), do, TN_DIMS)
                dk_ref[:, cols] = (ck_ref[:, cols] + dkk[:ATT_W]).astype(dk_ref.dtype)
                dv_ref[:, cols] = (cv_ref[:, cols] + dvv[:ATT_W]).astype(dv_ref.dtype)
                ck_ref[:, cols] = dkk[ATT_W:]
                cv_ref[:, cols] = dvv[ATT_W:]

        @pl.when(j == nb)
        def _():
            dk_ref[...] = ck_ref[...].astype(dk_ref.dtype)
            dv_ref[...] = cv_ref[...].astype(dv_ref.dtype)

    jq = lambda j: jnp.minimum(j, nb - 1)
    cur = pl.BlockSpec((ATT_W, ATT_TOK), lambda r, j: (r * nb + jq(j), 0))
    prev = pl.BlockSpec((ATT_W, ATT_TOK), lambda r, j: (r * nb + jnp.maximum(jq(j) - 1, 0), 0))
    small = pl.BlockSpec((ATT_W, LANES), lambda r, j: (r * nb + jq(j), 0))
    late = pl.BlockSpec((ATT_W, ATT_TOK), lambda r, j: (r * nb + jnp.maximum(j - 1, 0), 0))
    big = jax.ShapeDtypeStruct((t, ATT_TOK), BF16)
    return pl.pallas_call(
        body, name=name, grid=(dil, nb + 1),
        in_specs=[cur, prev, cur, prev, cur, cur, small, small], out_specs=[cur, late, late],
        out_shape=[big, big, big],
        scratch_shapes=[pltpu.VMEM((ATT_W, ATT_TOK), F32), pltpu.VMEM((ATT_W, ATT_TOK), F32)],
        compiler_params=_params(("parallel", "arbitrary")),
    )(q, k, k, v, v, do, cterm, lse)


def _combine_weights(lses):
    m = functools.reduce(jnp.maximum, lses)
    es = [jnp.exp(l - m) for l in lses]
    tot = functools.reduce(lambda a, b: a + b, es)
    return [e / tot for e in es]


DILS = tuple(d for _, d in DILATED_GROUPS)
COMBINE_ROWS = 256


def _by_residue(arr, dil):
    return arr if dil == 1 else arr.reshape(dil, arr.shape[0] // dil, arr.shape[1])


def _natural_lses(l_refs, small_refs, tm):
    vals = []
    for g, dil in enumerate(DILS):
        if dil == 1:
            vals.append(l_refs[g][...])
        else:
            _scatter_rows(l_refs[g], small_refs[g], dil, tm)
            vals.append(small_refs[g][0])
    return vals


def _combine_scratch(tm):
    return ([pltpu.VMEM((_chunks(ATT_TOK), tm, 128), F32)] * N_DIL + [pltpu.VMEM((1, tm, 128), F32)] * N_DIL)


def _combine_fwd(outs, lses, name):
    t = outs[0].shape[0]
    tm = COMBINE_ROWS

    def body(*refs):
        o_refs, l_refs, y_ref = refs[:N_DIL], refs[N_DIL:2 * N_DIL], refs[2 * N_DIL]
        big_refs, small_refs = refs[2 * N_DIL + 1:3 * N_DIL + 1], refs[3 * N_DIL + 1:]
        ws = _combine_weights(_natural_lses(l_refs, small_refs, tm))
        for g in range(1, N_DIL):
            _scatter_rows(o_refs[g], big_refs[g], DILS[g], tm)
        for h in range(ATT_H):
            cols = slice(h * ATT_E, (h + 1) * ATT_E)
            y_ref[:, cols] = (_col(ws[0], h) * o_refs[0][:, cols]
                              + sum(_col(ws[g], h) * big_refs[g][h] for g in range(1, N_DIL)))

    return pl.pallas_call(
        body, name=name, grid=(t // tm,),
        in_specs=[_perm_spec(tm, d, ATT_TOK) for d in DILS] + [_perm_spec(tm, d, LANES) for d in DILS],
        out_specs=pl.BlockSpec((tm, ATT_TOK), lambda i: (i, 0)),
        out_shape=jax.ShapeDtypeStruct((t, ATT_TOK), F32),
        scratch_shapes=_combine_scratch(tm),
        compiler_params=_params(("parallel",)),
    )(*[_by_residue(o, d) for o, d in zip(outs, DILS)], *[_by_residue(l, d) for l, d in zip(lses, DILS)])


def _combine_bwd(outs, lses, dy, name):
    t = outs[0].shape[0]
    tm = COMBINE_ROWS

    def body(*refs):
        o_refs, l_refs, dy_ref = refs[:N_DIL], refs[N_DIL:2 * N_DIL], refs[2 * N_DIL]
        do_refs, c_refs = refs[2 * N_DIL + 1:3 * N_DIL + 1], refs[3 * N_DIL + 1:4 * N_DIL + 1]
        big_refs, small_refs = refs[4 * N_DIL + 1:5 * N_DIL + 1], refs[5 * N_DIL + 1:]
        ws = _combine_weights(_natural_lses(l_refs, small_refs, tm))
        for g in range(1, N_DIL):
            _scatter_rows(o_refs[g], big_refs[g], DILS[g], tm)
        lane = _iota((tm, LANES), 1)
        sdw = jnp.zeros((tm, LANES), F32)
        for h in range(ATT_H):
            cols = slice(h * ATT_E, (h + 1) * ATT_E)
            dyh = dy_ref[:, cols]
            tot = _col(ws[0], h) * jnp.sum(dyh * o_refs[0][:, cols], axis=1, keepdims=True)
            for g in range(1, N_DIL):
                tot = tot + _col(ws[g], h) * jnp.sum(dyh * big_refs[g][h], axis=1, keepdims=True)
            sdw = jnp.where(lane == h, tot, sdw)
        for g, dil in enumerate(DILS):
            cterm = -ws[g] * sdw
            if dil == 1:
                c_refs[g][...] = cterm
            else:
                small_refs[g][0] = cterm
                _gather_rows(small_refs[g], c_refs[g], dil, tm)
            for h in range(ATT_H):
                cols = slice(h * ATT_E, (h + 1) * ATT_E)
                do = _col(ws[g], h) * dy_ref[:, cols]
                if dil == 1:
                    do_refs[g][:, cols] = do.astype(BF16)
                else:
                    big_refs[g][h] = do
            if dil > 1:
                _gather_rows(big_refs[g], do_refs[g], dil, tm)

    res = pl.pallas_call(
        body, name=name, grid=(t // tm,),
        in_specs=([_perm_spec(tm, d, ATT_TOK) for d in DILS] + [_perm_spec(tm, d, LANES) for d in DILS]
                  + [pl.BlockSpec((tm, ATT_TOK), lambda i: (i, 0))]),
        out_specs=[_perm_spec(tm, d, ATT_TOK) for d in DILS] + [_perm_spec(tm, d, LANES) for d in DILS],
        out_shape=[_perm_shape(t, d, ATT_TOK, BF16) for d in DILS] + [_perm_shape(t, d, LANES, F32) for d in DILS],
        scratch_shapes=_combine_scratch(tm),
        compiler_params=_params(("parallel",)),
    )(*[_by_residue(o, d) for o, d in zip(outs, DILS)], *[_by_residue(l, d) for l, d in zip(lses, DILS)], dy)
    return [a.reshape(t, ATT_TOK) for a in res[:N_DIL]], [a.reshape(t, LANES) for a in res[N_DIL:]]


ADAM_LR, ADAM_B1, ADAM_B2, ADAM_EPS, ADAM_WD, ADAM_STEP = 0.001, 0.9, 0.999, 1e-08, 0.01, 10


def _adamw(parts, w, m, v, name):
    r, c = w.shape
    n_parts = parts.shape[0]
    tc = _pick(c, (1024, 512, 256, 128)) if c % 128 == 0 else c
    tm = _pick(r, (128, 64, 32, 16, 8))

    def body(p_ref, w_ref, m_ref, v_ref, g_ref, d_ref, nm_ref, nv_ref):
        g = p_ref[0].astype(F32)
        for k in range(1, n_parts):
            g = g + p_ref[k].astype(F32)
        nm = ADAM_B1 * m_ref[...] + (1.0 - ADAM_B1) * g
        nv = ADAM_B2 * v_ref[...] + (1.0 - ADAM_B2) * (g * g)
        m_hat = nm / (1.0 - ADAM_B1 ** ADAM_STEP)
        v_hat = nv / (1.0 - ADAM_B2 ** ADAM_STEP)
        g_ref[...] = g
        d_ref[...] = -ADAM_LR * (m_hat / (jnp.sqrt(v_hat) + ADAM_EPS) + ADAM_WD * w_ref[...])
        nm_ref[...] = nm
        nv_ref[...] = nv

    blk = pl.BlockSpec((tm, tc), lambda i, j: (i, j))
    pblk = pl.BlockSpec((n_parts, tm, tc), lambda i, j: (0, i, j))
    return pl.pallas_call(
        body, name=name, grid=(r // tm, c // tc), in_specs=[pblk, blk, blk, blk], out_specs=[blk] * 4,
        out_shape=[jax.ShapeDtypeStruct((r, c), F32)] * 4,
        compiler_params=_params(("parallel", "parallel")),
    )(parts, w, m, v)


N_CHIP = 4
MESH_ID = pl.DeviceIdType.MESH


def _other_chips(x, y):
    return [(1 - x, y), (x, 1 - y), (1 - x, 1 - y)]


GATHER_SEMS = N_DEV - 1


def _gather_copies(in_refs, out_refs, send_sems, recv_sems, local_sems, x, y, c):
    n = len(in_refs)
    sibling = (x, y, 1 - c)
    chips = _other_chips(x, y)

    def copy(a, k, block, to, src=None):
        rows = out_refs[a].at[4 * block[0] + 2 * block[1] + block[2]]
        return pltpu.make_async_remote_copy(
            src_ref=rows if src is None else src, dst_ref=rows,
            send_sem=send_sems.at[a * GATHER_SEMS + k], recv_sem=recv_sems.at[a * GATHER_SEMS + k],
            device_id=to, device_id_type=MESH_ID)

    started = []
    for a in range(n):
        local = pltpu.make_async_copy(in_refs[a], out_refs[a].at[4 * x + 2 * y + c], local_sems.at[a])
        local.start()
        started.append(local)
    sends = []
    for j, chip in enumerate(chips):
        for a in range(n):
            sends.append(copy(a, 1 + j, (x, y, c), (*chip, c), src=in_refs[a]))
            sends[-1].start()
    for a in range(n):
        sends.append(copy(a, 0, (x, y, c), sibling, src=in_refs[a]))
        sends[-1].start()
    for j, chip in enumerate(chips):
        for a in range(n):
            copy(a, 1 + j, (*chip, c), (x, y, c)).wait_recv()
            sends.append(copy(a, 4 + j, (*chip, c), sibling))
            sends[-1].start()
    for a in range(n):
        copy(a, 0, sibling, (x, y, c)).wait_recv()
        for j, chip in enumerate(chips):
            copy(a, 4 + j, (*chip, 1 - c), (x, y, c)).wait_recv()
    for cp in sends:
        cp.wait_send()
    for local in started:
        local.wait()


def _gather_two_level(arrays, name):
    n = len(arrays)
    per = GATHER_SEMS

    def body(*refs):
        x, y, c = lax.axis_index("x"), lax.axis_index("y"), lax.axis_index("c")
        _gather_copies(refs[:n], refs[n:2 * n], *refs[2 * n:], x, y, c)

    any_spec = pl.BlockSpec(memory_space=pl.ANY)
    return pl.pallas_call(
        body, name=name, in_specs=[any_spec] * n, out_specs=[any_spec] * n,
        out_shape=[jax.ShapeDtypeStruct((N_DEV,) + a.shape, a.dtype) for a in arrays],
        scratch_shapes=[pltpu.SemaphoreType.DMA((n * per,)), pltpu.SemaphoreType.DMA((n * per,)),
                        pltpu.SemaphoreType.DMA((n,))],
        compiler_params=pltpu.CompilerParams(has_side_effects=True),
    )(*arrays)


def _handshake(barrier, peers):
    for peer in peers:
        pl.semaphore_signal(barrier, inc=1, device_id=peer, device_id_type=MESH_ID)
    pl.semaphore_wait(barrier, len(peers))


def _sc_gather(arrays, name, collective_id):
    n = len(arrays)
    per = GATHER_SEMS
    hbm = pltpu.MemorySpace.HBM
    in_refs = [jax.new_ref(a, memory_space=hbm) for a in arrays]
    out_refs = [jax.empty_ref(jax.ShapeDtypeStruct((N_DEV,) + a.shape, a.dtype), memory_space=hbm) for a in arrays]

    @pl.kernel(mesh=plsc.ScalarSubcoreMesh(axis_name="seq", num_cores=1), name=name,
               scratch_types=(pltpu.SemaphoreType.DMA((n * per,)), pltpu.SemaphoreType.DMA((n * per,)),
                              pltpu.SemaphoreType.DMA((n,))),
               compiler_params=pltpu.CompilerParams(collective_id=collective_id))
    def launch(send_sems, recv_sems, local_sems):
        x, y, c = lax.axis_index("x"), lax.axis_index("y"), lax.axis_index("c")
        _handshake(pltpu.get_barrier_semaphore(), [(x ^ ((k >> 2) & 1), y ^ ((k >> 1) & 1), c ^ (k & 1))
                                                   for k in range(1, N_DEV)])
        _gather_copies(in_refs, out_refs, send_sems, recv_sems, local_sems, x, y, c)

    launch()
    return [o[...] for o in out_refs]


def _sc_chip_exchange(sums, name, collective_id):
    n = len(sums)
    per = N_CHIP - 1
    hbm = pltpu.MemorySpace.HBM
    in_refs = [jax.new_ref(s, memory_space=hbm) for s in sums]
    out_refs = [jax.empty_ref(jax.ShapeDtypeStruct(s.shape, s.dtype), memory_space=hbm) for s in sums]

    @pl.kernel(mesh=plsc.ScalarSubcoreMesh(axis_name="seq", num_cores=1), name=name,
               scratch_types=(pltpu.SemaphoreType.DMA((n * per,)), pltpu.SemaphoreType.DMA((n * per,)),
                              pltpu.SemaphoreType.DMA((n,))),
               compiler_params=pltpu.CompilerParams(collective_id=collective_id))
    def launch(send_sems, recv_sems, local_sems):
        x, y, c = lax.axis_index("x"), lax.axis_index("y"), lax.axis_index("c")
        _handshake(pltpu.get_barrier_semaphore(), [(px, py, c) for px, py in _other_chips(x, y)])
        _chip_exchange_copies(in_refs, out_refs, send_sems, recv_sems, local_sems, x, y, c)

    launch()
    return [o[...] for o in out_refs]


def _sibling_swap(parts, name):
    n = len(parts)

    def body(*refs):
        in_refs, out_refs = refs[:n], refs[n:2 * n]
        send_sems, recv_sems = refs[2 * n:]
        x, y, c = lax.axis_index("x"), lax.axis_index("y"), lax.axis_index("c")
        sends = []
        for a in range(n):
            for q in range(N_CHIP):
                cp = pltpu.make_async_remote_copy(
                    src_ref=in_refs[a].at[2 * q + 1 - c], dst_ref=out_refs[a].at[q],
                    send_sem=send_sems.at[a * N_CHIP + q], recv_sem=recv_sems.at[a * N_CHIP + q],
                    device_id=(x, y, 1 - c), device_id_type=MESH_ID)
                cp.start()
                sends.append(cp)
        for cp in sends:
            cp.wait_recv()
        for cp in sends:
            cp.wait_send()

    any_spec = pl.BlockSpec(memory_space=pl.ANY)
    return pl.pallas_call(
        body, name=name, in_specs=[any_spec] * n, out_specs=[any_spec] * n,
        out_shape=[jax.ShapeDtypeStruct((N_CHIP,) + p.shape[1:], p.dtype) for p in parts],
        scratch_shapes=[pltpu.SemaphoreType.DMA((n * N_CHIP,)), pltpu.SemaphoreType.DMA((n * N_CHIP,))],
        compiler_params=pltpu.CompilerParams(has_side_effects=True),
    )(*parts)


def _chip_sum(part, landed, core, name):
    _, r, c = part.shape
    tc = _pick(c, (1024, 512, 256, 128)) if c % 128 == 0 else c
    tm = _pick(r, (256, 128, 64, 32, 16, 8))

    def body(core_ref, p_ref, l_ref, o_ref):
        o_ref[...] = (p_ref[...].astype(F32) + l_ref[...].astype(F32)).astype(o_ref.dtype)

    grid_spec = pltpu.PrefetchScalarGridSpec(
        num_scalar_prefetch=1, grid=(N_CHIP, r // tm, c // tc),
        in_specs=[pl.BlockSpec((None, tm, tc), lambda q, i, j, core_ref: (2 * q + core_ref[0], i, j)),
                  pl.BlockSpec((None, tm, tc), lambda q, i, j, core_ref: (q, i, j))],
        out_specs=pl.BlockSpec((None, tm, tc), lambda q, i, j, core_ref: (q, i, j)))
    return pl.pallas_call(
        body, name=name, grid_spec=grid_spec, out_shape=jax.ShapeDtypeStruct(landed.shape, landed.dtype),
        compiler_params=_params(("parallel", "parallel", "parallel")),
    )(core, part, landed)


def _chip_exchange_copies(in_refs, out_refs, send_sems, recv_sems, local_sems, x, y, c):
    n = len(in_refs)
    per = N_CHIP - 1
    mine = 2 * x + y
    started = []
    for a in range(n):
        local = pltpu.make_async_copy(in_refs[a].at[mine], out_refs[a].at[mine], local_sems.at[a])
        local.start()
        started.append(local)
    sends = []
    for j, (px, py) in enumerate(_other_chips(x, y)):
        for a in range(n):
            cp = pltpu.make_async_remote_copy(
                src_ref=in_refs[a].at[2 * px + py], dst_ref=out_refs[a].at[mine],
                send_sem=send_sems.at[a * per + j], recv_sem=recv_sems.at[a * per + j],
                device_id=(px, py, c), device_id_type=MESH_ID)
            cp.start()
            sends.append((cp, a, j, 2 * px + py))
    for cp, a, j, peer in sends:
        pltpu.make_async_remote_copy(
            src_ref=out_refs[a].at[peer], dst_ref=out_refs[a].at[peer],
            send_sem=send_sems.at[a * per + j], recv_sem=recv_sems.at[a * per + j],
            device_id=(x, y, c), device_id_type=MESH_ID).wait_recv()
    for cp, _, _, _ in sends:
        cp.wait_send()
    for local in started:
        local.wait()


def _chip_exchange(sums, name):
    n = len(sums)
    per = N_CHIP - 1

    def body(*refs):
        x, y, c = lax.axis_index("x"), lax.axis_index("y"), lax.axis_index("c")
        _chip_exchange_copies(refs[:n], refs[n:2 * n], *refs[2 * n:], x, y, c)

    any_spec = pl.BlockSpec(memory_space=pl.ANY)
    return pl.pallas_call(
        body, name=name, in_specs=[any_spec] * n, out_specs=[any_spec] * n,
        out_shape=[jax.ShapeDtypeStruct(s.shape, s.dtype) for s in sums],
        scratch_shapes=[pltpu.SemaphoreType.DMA((n * per,)), pltpu.SemaphoreType.DMA((n * per,)),
                        pltpu.SemaphoreType.DMA((n,))],
        compiler_params=pltpu.CompilerParams(has_side_effects=True),
    )(*sums)


DEPTH = 4
MEM_W = 1024
MIX_W = SSD_TOK + MEM_W
DT_PAD = LANES - SSD_H


def _is_ssd(i):
    return i % 2 == 0


def _weight_names():
    names = ["mem_norm_g", "final_norm_g"]
    for i in range(DEPTH):
        names += [f"norm_g_{i}", f"w_in_{i}"]
        if _is_ssd(i):
            names += [f"conv_w_{i}", f"conv_b_{i}", f"dt_bias_{i}", f"a_log_{i}", f"d_skip_{i}", f"ssd_norm_g_{i}"]
        names += [f"w_mem_kv_{i}", f"w_out_{i}"]
    return names


WEIGHTS = _weight_names()
INPUTS = ["x", "mem"] + WEIGHTS + ["loss_target"] + ["m_" + n for n in WEIGHTS] + ["v_" + n for n in WEIGHTS]


def _in_segments(i):
    if _is_ssd(i):
        return [("xbc", 0, SSD_XBC), ("dt", SSD_XBC, SSD_H), ("qm", SSD_XBC + SSD_H, MEM_W),
                ("z", SSD_XBC + SSD_H + MEM_W, MIX_W)]
    segs = []
    for g in range(N_DIL):
        for j, nm in enumerate("qkv"):
            segs.append((f"{nm}{g}", (3 * g + j) * ATT_TOK, ATT_TOK))
    segs += [("qm", 3 * N_DIL * ATT_TOK, MEM_W), ("z", 3 * N_DIL * ATT_TOK + MEM_W, MIX_W)]
    return segs


def _split_w_in(i, w_in):
    out = {}
    for nm, start, width in _in_segments(i):
        seg = w_in[:, start:start + width]
        out[nm] = jnp.pad(seg, ((0, 0), (0, DT_PAD))) if nm == "dt" else seg
    return out


def _join_dw_in(i, dws):
    return jnp.concatenate([dws[nm][:, :width] for nm, _, width in _in_segments(i)], axis=1)


SEG_DTYPE = {"xbc": F32, "dt": F32, "z": F32}


def _layer_dils(i):
    return (1,) if _is_ssd(i) else DILS


def _seg_order(nm):
    return int(nm[1]) if nm[0] in "qkv" and nm[1:].isdigit() else 0


def _layer_fwd(i, x, mem_b, p):
    tag = f"l{i}"
    hs = _rmsnorm_fwd(x, p["norm_g"], tag + "_norm", _layer_dils(i))
    proj = {nm: _matmul(hs[_seg_order(nm)], w, out_dtype=SEG_DTYPE.get(nm, BF16), name=f"{tag}_in_{nm}")
            for nm, w in p["win"].items()}
    sv = {"x": x, "h": hs, "proj": proj}
    if _is_ssd(i):
        xbc = _conv_fwd(proj["xbc"], p["conv_w"], p["conv_b"], tag + "_conv")
        y_tok, hs = _ssd_fwd(xbc, proj["dt"], p["dt_bias_p"], p["a_log_p"], p["dskip_lane"], tag + "_ssd")
        sv.update(xbc=xbc, hs=hs)
    else:
        outs, lses = [], []
        for g in range(N_DIL):
            o, lse = _dil_fwd(proj[f"q{g}"], proj[f"k{g}"], proj[f"v{g}"], g, f"{tag}_att{g}")
            outs.append(o)
            lses.append(lse)
        y_tok = _combine_fwd(outs, lses, tag + "_comb")
        sv.update(outs=outs, lses=lses)
    mkv = _matmul(mem_b, p["wmkv"], out_dtype=BF16, name=tag + "_mkv")
    y_mem = _memattn_fwd(proj["qm"], mkv, tag + "_mem")
    gated = _gate_fwd(y_tok, y_mem, proj["z"], p.get("ssd_norm_g"), tag + "_gate")
    x_out = _matmul(gated, p["wout"], out_dtype=F32, add=x, name=tag + "_out")
    sv.update(y_tok=y_tok, y_mem=y_mem, mkv=mkv, gated=gated)
    return x_out, sv


def _layer_bwd(i, sv, dx_out, dxb_out, dmem_n, mem_b, p):
    tag = f"l{i}b"
    proj = sv["proj"]
    gr = {}
    dgated = _matmul(dxb_out, p["wout"], tb=True, out_dtype=F32, name=tag + "_dgated")
    gr["w_out"] = _matmul(sv["gated"], dxb_out, ta=True, out_dtype=BF16, name=tag + "_dwout")
    dy_tok, dy_mem, dz, dssd_g = _gate_bwd(sv["y_tok"], sv["y_mem"], proj["z"], p.get("ssd_norm_g"), dgated, tag + "_gate")
    dq_mem, dmkv = _memattn_bwd(proj["qm"], sv["mkv"], dy_mem, tag + "_mem")
    gr["w_mem_kv"] = _matmul(mem_b, dmkv, ta=True, out_dtype=BF16, name=tag + "_dwmkv")
    dmem_n = _matmul(dmkv, p["wmkv"], tb=True, out_dtype=F32, add=dmem_n, name=tag + "_dmem")
    dproj = {"qm": dq_mem, "z": dz}
    if _is_ssd(i):
        dxbc, ddt_raw, dbias, dalog, ddsk = _ssd_bwd(sv["xbc"], proj["dt"], p["dt_bias_p"], p["a_log_p"], p["dskip_lane"],
                                                     sv["hs"], dy_tok, tag + "_ssd")
        dpre, dconv_w, dconv_b = _conv_bwd_pre(proj["xbc"], p["conv_w"], p["conv_b"], dxbc, tag + "_convpre")
        dproj["xbc"] = _conv_bwd_in(dpre, p["conv_w"], tag + "_convin")
        dproj["dt"] = ddt_raw
        gr.update(conv_w=dconv_w, conv_b=dconv_b[0], dt_bias=dbias[0, :SSD_H], a_log=dalog[0, :SSD_H],
                  d_skip=jnp.sum(ddsk.reshape(SSD_H, SSD_P), axis=1), ssd_norm_g=dssd_g[0])
    else:
        dos, cs = _combine_bwd(sv["outs"], sv["lses"], dy_tok, tag + "_comb")
        for g in range(N_DIL):
            dq, dk, dv = _dil_bwd(proj[f"q{g}"], proj[f"k{g}"], proj[f"v{g}"], dos[g], cs[g], sv["lses"][g], g,
                                  f"{tag}_att{g}")
            dproj.update({f"q{g}": dq, f"k{g}": dk, f"v{g}": dv})
    dils = _layer_dils(i)
    dhs = [None] * len(dils)
    dws = {}
    for nm, w in p["win"].items():
        o = _seg_order(nm)
        dhs[o] = _matmul(dproj[nm], w, tb=True, out_dtype=F32, add=dhs[o], name=f"{tag}_dh_{nm}")
        dws[nm] = _matmul(sv["h"][o], dproj[nm], ta=True, out_dtype=BF16, name=f"{tag}_dw_{nm}")
    gr["w_in"] = _join_dw_in(i, dws)
    dx, dxb, dnorm_g = _rmsnorm_bwd(sv["x"], p["norm_g"], dhs, dx_out, tag + "_norm", dils)
    gr["norm_g"] = dnorm_g[0]
    return dx, dxb, dmem_n, gr


def _pad_heads(v):
    return jnp.pad(v.reshape(1, SSD_H), ((0, 0), (0, DT_PAD)))


def _layer_params(i, small, w_in, w_mem_kv, w_out):
    p = {"norm_g": small[f"norm_g_{i}"], "win": _split_w_in(i, w_in), "wmkv": w_mem_kv, "wout": w_out}
    if _is_ssd(i):
        p.update(conv_w=small[f"conv_w_{i}"], conv_b=small[f"conv_b_{i}"], ssd_norm_g=small[f"ssd_norm_g_{i}"],
                 dt_bias_p=_pad_heads(small[f"dt_bias_{i}"]), a_log_p=_pad_heads(small[f"a_log_{i}"]),
                 dskip_lane=jnp.repeat(small[f"d_skip_{i}"], SSD_P).reshape(1, SSD_TOK))
    return p


def _local_step(x, mem, target, small, big, emit):
    params = [_layer_params(i, small, *big[i]) for i in range(DEPTH)]
    mem_b = _rmsnorm_fwd(mem, small["mem_norm_g"], "mem_norm")[0]
    saved = []
    for i in range(DEPTH):
        x, sv = _layer_fwd(i, x, mem_b, params[i])
        saved.append(sv)
    loss, dx, dxb, dfinal = _final_loss(x, small["final_norm_g"], target)
    grads = {"final_norm_g": dfinal[0]}
    dmem_n = None
    for i in reversed(range(DEPTH)):
        dx, dxb, dmem_n, gr = _layer_bwd(i, saved[i], dx, dxb, dmem_n, mem_b, params[i])
        emit(i, {nm: gr.pop(nm) for nm in BIG})
        grads.update({f"{nm}_{i}": g for nm, g in gr.items()})
    _, _, dmem_g = _rmsnorm_bwd(mem, small["mem_norm_g"], [dmem_n], None, "mem_norm_b")
    grads["mem_norm_g"] = dmem_g[0]
    return loss[0, 0], dx, grads


BIG = ("w_in", "w_mem_kv", "w_out")
SMALL = [n for n in WEIGHTS if not n.startswith(BIG)]
PACK_ROWS = 8 * LANES
GATHER_COLLECTIVE_ID = 0
SCATTER_COLLECTIVE_ID = 4


def _pack(vals):
    flat = jnp.concatenate([v.reshape(-1).astype(F32) for v in vals])
    padded = -(-flat.shape[0] // PACK_ROWS) * PACK_ROWS
    return jnp.pad(flat, (0, padded - flat.shape[0])).reshape(padded // LANES, LANES)


def _train_step(a, local_step):
    x, y, c = lax.axis_index("x"), lax.axis_index("y"), lax.axis_index("c")
    me = 4 * x + 2 * y + c
    big = []
    for i in range(DEPTH):
        shards = [a[f"{nm}_{i}"].astype(BF16) for nm in BIG]
        if i == 0:
            g_in, g_kv, g_out = _gather_two_level(shards, f"gather_w{i}")
        else:
            g_in, g_kv, g_out = _sc_gather(shards, f"gather_w{i}", GATHER_COLLECTIVE_ID + i)
        d, cs = shards[0].shape
        big.append((jnp.transpose(g_in, (1, 0, 2)).reshape(d, N_DEV * cs),
                    g_kv.reshape(N_DEV * g_kv.shape[1], g_kv.shape[2]),
                    g_out.reshape(N_DEV * g_out.shape[1], g_out.shape[2])))
    conv_names = [n for n in SMALL if n.startswith("conv_w")]
    conv_full = _gather_two_level([a[n] for n in conv_names], "gather_conv")
    small = {n: a[n] for n in SMALL}
    for n, gathered in zip(conv_names, conv_full):
        small[n] = jnp.transpose(gathered, (1, 0, 2)).reshape(gathered.shape[1], N_DEV * gathered.shape[2])

    core = c.astype(jnp.int32).reshape(1)
    landed = {}

    def reduce_scatter(i, gr):
        d, cols = gr["w_in"].shape
        parts = [jnp.transpose(gr["w_in"].reshape(d, N_DEV, cols // N_DEV), (1, 0, 2))]
        for nm in BIG[1:]:
            parts.append(gr[nm].reshape(N_DEV, gr[nm].shape[0] // N_DEV, gr[nm].shape[1]))
        swapped = _sibling_swap(parts, f"swap_w{i}")
        sums = [_chip_sum(p, s, core, f"chipsum_{nm}_{i}") for nm, p, s in zip(BIG, parts, swapped)]
        if i == 0:
            landed[i] = _chip_exchange(sums, f"scatter_w{i}")
        else:
            landed[i] = _sc_chip_exchange(sums, f"scatter_w{i}", SCATTER_COLLECTIVE_ID + i)

    loss_local, grad_x, grads = local_step(a["x"][0], a["mem"][0], a["loss_target"][0], small, big, reduce_scatter)
    loss = lax.psum(loss_local, ("x", "y", "c"))

    res = {}
    for i in range(DEPTH):
        for nm, p in zip(BIG, landed[i]):
            n = f"{nm}_{i}"
            res[n] = _adamw(p, a[n], a["m_" + n], a["v_" + n], "adamw_" + n)

    gathered = _gather_two_level([_pack([grads[n] for n in SMALL])], "gather_small")[0]
    zero_conv = lambda pre: [jnp.zeros(small[n].shape, F32) if n in conv_names else a[pre + n] for n in SMALL]
    packed = _adamw(gathered, _pack(zero_conv("")), _pack(zero_conv("m_")), _pack(zero_conv("v_")), "adamw_small")
    off = 0
    for n in SMALL:
        size = math.prod(small[n].shape)
        if n in conv_names:
            rows, cols = a[n].shape
            whole = gathered.reshape(N_DEV, -1)[:, off:off + size].reshape(N_DEV, rows, N_DEV * cols)
            mine = lax.dynamic_slice_in_dim(whole, me * cols, cols, axis=2)
            res[n] = _adamw(mine, a[n], a["m_" + n], a["v_" + n], "adamw_" + n)
        else:
            res[n] = [o.reshape(-1)[off:off + size].reshape(a[n].shape) for o in packed]
        off += size
    outs = [loss, grad_x[None]]
    for k in range(4):
        outs += [res[n][k] for n in WEIGHTS]
    return tuple(outs)


def kernel(x, mem, mem_norm_g, final_norm_g, norm_g_0, w_in_0, conv_w_0, conv_b_0, dt_bias_0, a_log_0, d_skip_0, ssd_norm_g_0, w_mem_kv_0, w_out_0, norm_g_1, w_in_1, w_mem_kv_1, w_out_1, norm_g_2, w_in_2, conv_w_2, conv_b_2, dt_bias_2, a_log_2, d_skip_2, ssd_norm_g_2, w_mem_kv_2, w_out_2, norm_g_3, w_in_3, w_mem_kv_3, w_out_3, loss_target, m_mem_norm_g, m_final_norm_g, m_norm_g_0, m_w_in_0, m_conv_w_0, m_conv_b_0, m_dt_bias_0, m_a_log_0, m_d_skip_0, m_ssd_norm_g_0, m_w_mem_kv_0, m_w_out_0, m_norm_g_1, m_w_in_1, m_w_mem_kv_1, m_w_out_1, m_norm_g_2, m_w_in_2, m_conv_w_2, m_conv_b_2, m_dt_bias_2, m_a_log_2, m_d_skip_2, m_ssd_norm_g_2, m_w_mem_kv_2, m_w_out_2, m_norm_g_3, m_w_in_3, m_w_mem_kv_3, m_w_out_3, v_mem_norm_g, v_final_norm_g, v_norm_g_0, v_w_in_0, v_conv_w_0, v_conv_b_0, v_dt_bias_0, v_a_log_0, v_d_skip_0, v_ssd_norm_g_0, v_w_mem_kv_0, v_w_out_0, v_norm_g_1, v_w_in_1, v_w_mem_kv_1, v_w_out_1, v_norm_g_2, v_w_in_2, v_conv_w_2, v_conv_b_2, v_dt_bias_2, v_a_log_2, v_d_skip_2, v_ssd_norm_g_2, v_w_mem_kv_2, v_w_out_2, v_norm_g_3, v_w_in_3, v_w_mem_kv_3, v_w_out_3):
    vals = locals()
    return _train_step({n: vals[n] for n in INPUTS}, _local_step)
```

```python
import functools
import math

import jax
import jax.numpy as jnp
from jax import lax
from jax.experimental import pallas as pl
from jax.experimental.pallas import tpu as pltpu
from jax.experimental.pallas import tpu_sc as plsc

F32 = jnp.float32
BF16 = jnp.bfloat16
EPS = 1e-6
N_DEV = 8
VMEM_LIMIT_BYTES = 56 * 1024 * 1024


def _pick(n, prefs):
    for p in prefs:
        if n % p == 0:
            return p
    return n


def _params(sem):
    return pltpu.CompilerParams(dimension_semantics=sem, vmem_limit_bytes=VMEM_LIMIT_BYTES)


def _matmul(a, b, *, ta=False, tb=False, out_dtype=F32, add=None, name="mm"):
    if ta:
        k_dim, m_dim = a.shape
    else:
        m_dim, k_dim = a.shape
    n_dim = b.shape[0] if tb else b.shape[1]
    out_bytes = jnp.dtype(out_dtype).itemsize + (0 if add is None else add.dtype.itemsize)
    tm, tn, tk = _matmul_tiles(m_dim, n_dim, k_dim, a.dtype.itemsize, b.dtype.itemsize, out_bytes)
    nk = k_dim // tk
    dims = (((0,) if ta else (1,), (1,) if tb else (0,)), ((), ()))

    def body(*refs):
        if add is None:
            a_ref, b_ref, o_ref = refs[:3]
            add_ref = None
        else:
            a_ref, b_ref, add_ref, o_ref = refs[:4]
        part = lax.dot_general(a_ref[...].astype(BF16), b_ref[...].astype(BF16), dims,
                               preferred_element_type=F32)
        if nk == 1:
            o_ref[...] = (part if add_ref is None else part + add_ref[...].astype(F32)).astype(o_ref.dtype)
            return
        acc_ref = refs[-1]
        k = pl.program_id(2)

        @pl.when(k == 0)
        def _():
            acc_ref[...] = part if add_ref is None else part + add_ref[...].astype(F32)

        @pl.when(k > 0)
        def _():
            acc_ref[...] += part

        @pl.when(k == nk - 1)
        def _():
            o_ref[...] = acc_ref[...].astype(o_ref.dtype)

    a_spec = pl.BlockSpec((tk, tm), lambda i, j, k: (k, i)) if ta else pl.BlockSpec((tm, tk), lambda i, j, k: (i, k))
    b_spec = pl.BlockSpec((tn, tk), lambda i, j, k: (j, k)) if tb else pl.BlockSpec((tk, tn), lambda i, j, k: (k, j))
    o_spec = pl.BlockSpec((tm, tn), lambda i, j, k: (i, j))
    in_specs = [a_spec, b_spec] + ([o_spec] if add is not None else [])
    args = (a, b) + ((add,) if add is not None else ())
    return pl.pallas_call(
        body, name=name, grid=(m_dim // tm, n_dim // tn, nk),
        in_specs=in_specs, out_specs=o_spec,
        out_shape=jax.ShapeDtypeStruct((m_dim, n_dim), out_dtype),
        scratch_shapes=[pltpu.VMEM((tm, tn), F32)] if nk > 1 else [],
        compiler_params=_params(("parallel", "parallel", "arbitrary")),
    )(*args)


MATMUL_VMEM_BUDGET = 40 * 1024 * 1024


def _matmul_tiles(m_dim, n_dim, k_dim, a_bytes, b_bytes, out_bytes):
    best = None
    for tk in (k_dim, 4096, 2048, 1024, 512, 256, 128):
        if tk > k_dim or k_dim % tk:
            continue
        for tm in (1024, 512, 256, 128):
            if m_dim % tm:
                continue
            for tn in (2048, 1024, 512, 256, 128):
                if n_dim % tn:
                    continue
                vmem = 2 * (tm * tk * a_bytes + tk * tn * b_bytes + tm * tn * out_bytes) + 2 * tm * tn * 4
                if a_bytes == 4:
                    vmem += tm * tk * 2
                if vmem > MATMUL_VMEM_BUDGET:
                    continue
                score = (tm * tn * tk, tk, min(tm, tn))
                if best is None or score > best[0]:
                    best = (score, (tm, tn, tk))
    return best[1]


def _iota(shape, dim):
    return lax.broadcasted_iota(jnp.int32, shape, dim)


def _col(x, j):
    return jnp.sum(jnp.where(_iota(x.shape, 1) == j, x, 0.0), axis=1, keepdims=True)


def _silu(x):
    return x * jax.nn.sigmoid(x)


def _dsilu(x):
    s = jax.nn.sigmoid(x)
    return s * (1.0 + x * (1.0 - s))


def _chunks(width):
    return width // 128


def _scatter_rows(src_ref, nat_ref, dil, tm):
    n = tm // dil
    for cb in range(nat_ref.shape[0]):
        for r in range(dil):
            nat_ref[cb, pl.ds(r, n, stride=dil), :] = src_ref[r, :, cb * 128:(cb + 1) * 128].astype(F32)


def _gather_rows(nat_ref, dst_ref, dil, tm):
    n = tm // dil
    for cb in range(nat_ref.shape[0]):
        for r in range(dil):
            dst_ref[r, :, cb * 128:(cb + 1) * 128] = nat_ref[cb, pl.ds(r, n, stride=dil), :].astype(dst_ref.dtype)


def _load_chunks(nat_ref):
    return jnp.concatenate([nat_ref[cb] for cb in range(nat_ref.shape[0])], axis=1)


def _store_chunks(nat_ref, val):
    for cb in range(nat_ref.shape[0]):
        nat_ref[cb] = val[:, cb * 128:(cb + 1) * 128]


def _perm_spec(tm, dil, width):
    if dil == 1:
        return pl.BlockSpec((tm, width), lambda i: (i, 0))
    return pl.BlockSpec((dil, tm // dil, width), lambda i: (0, i, 0))


def _perm_shape(t, dil, width, dtype):
    return jax.ShapeDtypeStruct((t, width) if dil == 1 else (dil, t // dil, width), dtype)


def _rmsnorm_fwd(x, g, name, dils=(1,)):
    t, d = x.shape
    tm = _pick(t, (512, 256, 128))

    permuted = any(dil > 1 for dil in dils)

    def body(x_ref, g_ref, *refs):
        h_refs = refs[:len(dils)]
        xv = x_ref[...]
        rs = lax.rsqrt(jnp.mean(xv * xv, axis=-1, keepdims=True) + EPS)
        hv = xv * rs * g_ref[...]
        if permuted:
            _store_chunks(refs[-1], hv)
        for dil, h_ref in zip(dils, h_refs):
            if dil == 1:
                h_ref[...] = hv.astype(BF16)
            else:
                _gather_rows(refs[-1], h_ref, dil, tm)

    outs = pl.pallas_call(
        body, name=name, grid=(t // tm,),
        in_specs=[pl.BlockSpec((tm, d), lambda i: (i, 0)), pl.BlockSpec((1, d), lambda i: (0, 0))],
        out_specs=[_perm_spec(tm, dil, d) for dil in dils],
        out_shape=[_perm_shape(t, dil, d, BF16) for dil in dils],
        scratch_shapes=[pltpu.VMEM((_chunks(d), tm, 128), F32)] if permuted else [],
        compiler_params=_params(("parallel",)),
    )(x, g.reshape(1, d))
    return [o.reshape(t, d) for o in outs]


def _rmsnorm_bwd(x, g, dhs, dres, name, dils=(1,)):
    t, d = x.shape
    tm = _pick(t, (512, 256, 128) if len(dils) == 1 else (256, 128))
    n_in = len(dils)

    def body(*refs):
        x_ref, g_ref = refs[:2]
        dh_refs = refs[2:2 + n_in]
        dres_ref = refs[2 + n_in] if dres is not None else None
        dx_ref, dxb_ref, dg_ref = refs[-4:-1]
        nat_ref = refs[-1]
        dhv = None
        for dil, dh_ref in zip(dils, dh_refs):
            if dil == 1:
                term = dh_ref[...].astype(F32)
            else:
                _scatter_rows(dh_ref, nat_ref, dil, tm)
                term = _load_chunks(nat_ref)
            dhv = term if dhv is None else dhv + term
        xv = x_ref[...]
        r = lax.rsqrt(jnp.mean(xv * xv, axis=-1, keepdims=True) + EPS)
        xhat = xv * r
        dxh = dhv * g_ref[...]
        dx = r * (dxh - xhat * jnp.mean(dxh * xhat, axis=-1, keepdims=True))
        if dres_ref is not None:
            dx = dx + dres_ref[...]
        dx_ref[...] = dx
        dxb_ref[...] = dx.astype(BF16)
        part = jnp.sum(dhv * xhat, axis=0, keepdims=True)

        @pl.when(pl.program_id(0) == 0)
        def _():
            dg_ref[...] = part

        @pl.when(pl.program_id(0) > 0)
        def _():
            dg_ref[...] += part

    row = pl.BlockSpec((tm, d), lambda i: (i, 0))
    vec = pl.BlockSpec((1, d), lambda i: (0, 0))
    in_specs = [row, vec] + [_perm_spec(tm, dil, d) for dil in dils] + ([row] if dres is not None else [])
    dh_args = [dh if dil == 1 else dh.reshape(dil, t // dil, d) for dil, dh in zip(dils, dhs)]
    args = (x, g.reshape(1, d), *dh_args) + ((dres,) if dres is not None else ())
    return pl.pallas_call(
        body, name=name, grid=(t // tm,), in_specs=in_specs, out_specs=[row, row, vec],
        out_shape=[jax.ShapeDtypeStruct((t, d), F32), jax.ShapeDtypeStruct((t, d), BF16),
                   jax.ShapeDtypeStruct((1, d), F32)],
        scratch_shapes=[pltpu.VMEM((_chunks(d), tm, 128), F32)],
        compiler_params=_params(("arbitrary",)),
    )(*args)


def _final_loss(x, g, target, name="final_loss"):
    t, d = x.shape
    tm = _pick(t, (512, 256, 128))

    def body(x_ref, g_ref, t_ref, loss_ref, dx_ref, dxb_ref, dg_ref):
        xv = x_ref[...]
        gv = g_ref[...]
        r = lax.rsqrt(jnp.mean(xv * xv, axis=-1, keepdims=True) + EPS)
        xhat = xv * r
        e = xhat * gv - t_ref[...]
        lpart = jnp.zeros((1, 128), F32) + (0.5 / d) * jnp.sum(e * e)
        dy = e * (1.0 / d)
        dxh = dy * gv
        dx = r * (dxh - xhat * jnp.mean(dxh * xhat, axis=-1, keepdims=True))
        dx_ref[...] = dx
        dxb_ref[...] = dx.astype(BF16)
        gpart = jnp.sum(dy * xhat, axis=0, keepdims=True)

        @pl.when(pl.program_id(0) == 0)
        def _():
            dg_ref[...] = gpart
            loss_ref[...] = lpart

        @pl.when(pl.program_id(0) > 0)
        def _():
            dg_ref[...] += gpart
            loss_ref[...] += lpart

    row = pl.BlockSpec((tm, d), lambda i: (i, 0))
    vec = pl.BlockSpec((1, d), lambda i: (0, 0))
    return pl.pallas_call(
        body, name=name, grid=(t // tm,), in_specs=[row, vec, row],
        out_specs=[pl.BlockSpec((1, 128), lambda i: (0, 0)), row, row, vec],
        out_shape=[jax.ShapeDtypeStruct((1, 128), F32), jax.ShapeDtypeStruct((t, d), F32),
                   jax.ShapeDtypeStruct((t, d), BF16), jax.ShapeDtypeStruct((1, d), F32)],
        compiler_params=_params(("arbitrary",)),
    )(x, g.reshape(1, d), target)


CONV_K = 4
HALO = 8


def _shift_down(cur, prev8, s):
    rolled = pltpu.roll(cur, s, 0)
    fix = pltpu.roll(prev8, s, 0)
    head = jnp.where(_iota((HALO, cur.shape[1]), 0) < s, fix, rolled[:HALO])
    return jnp.concatenate([head, rolled[HALO:]], axis=0)


def _shift_up(cur, next8, s):
    n = cur.shape[0]
    rolled = pltpu.roll(cur, n - s, 0)
    fix = pltpu.roll(next8, HALO - s, 0)
    tail = jnp.where(_iota((HALO, cur.shape[1]), 0) >= HALO - s, fix, rolled[n - HALO:])
    return jnp.concatenate([rolled[:n - HALO], tail], axis=0)


def _conv_pre(u_ref, up_ref, w_ref, b_ref, first):
    cur = u_ref[...]
    prev8 = jnp.where(first, 0.0, up_ref[...])
    w = w_ref[...]
    shifted = [cur] + [_shift_down(cur, prev8, s) for s in (1, 2, 3)]
    pre = b_ref[...] + sum(w[CONV_K - 1 - s:CONV_K - s, :] * shifted[s] for s in range(CONV_K))
    return pre, shifted


def _conv_specs(tm, tc):
    nb = tm // HALO
    cur = pl.BlockSpec((tm, tc), lambda j, i: (i, j))
    prev = pl.BlockSpec((HALO, tc), lambda j, i: (jnp.maximum(i * nb - 1, 0), j))
    wspec = pl.BlockSpec((CONV_K, tc), lambda j, i: (0, j))
    bspec = pl.BlockSpec((1, tc), lambda j, i: (0, j))
    return cur, prev, wspec, bspec


def _conv_fwd(u, w, b, name):
    t, c = u.shape
    tm, tc = _pick(t, (512, 256, 128)), _pick(c, (1024, 512, 256, 128))
    cur, prev, wspec, bspec = _conv_specs(tm, tc)

    def body(u_ref, up_ref, w_ref, b_ref, o_ref):
        pre, _ = _conv_pre(u_ref, up_ref, w_ref, b_ref, pl.program_id(1) == 0)
        o_ref[...] = _silu(pre)

    return pl.pallas_call(
        body, name=name, grid=(c // tc, t // tm), in_specs=[cur, prev, wspec, bspec], out_specs=cur,
        out_shape=jax.ShapeDtypeStruct((t, c), F32),
        compiler_params=_params(("parallel", "parallel")),
    )(u, u, w, b.reshape(1, c))


def _conv_bwd_pre(u, w, b, dy, name):
    t, c = u.shape
    tm, tc = _pick(t, (512, 256, 128)), _pick(c, (1024, 512, 256, 128))
    cur, prev, wspec, bspec = _conv_specs(tm, tc)

    def body(u_ref, up_ref, w_ref, b_ref, dy_ref, dpre_ref, dw_ref, db_ref):
        i = pl.program_id(1)
        pre, shifted = _conv_pre(u_ref, up_ref, w_ref, b_ref, i == 0)
        dpre = dy_ref[...] * _dsilu(pre)
        dpre_ref[...] = dpre
        dw = jnp.concatenate([jnp.sum(dpre * shifted[CONV_K - 1 - k], axis=0, keepdims=True) for k in range(CONV_K)], axis=0)
        db = jnp.sum(dpre, axis=0, keepdims=True)

        @pl.when(i == 0)
        def _():
            dw_ref[...] = dw
            db_ref[...] = db

        @pl.when(i > 0)
        def _():
            dw_ref[...] += dw
            db_ref[...] += db

    return pl.pallas_call(
        body, name=name, grid=(c // tc, t // tm), in_specs=[cur, prev, wspec, bspec, cur],
        out_specs=[cur, wspec, bspec],
        out_shape=[jax.ShapeDtypeStruct((t, c), F32), jax.ShapeDtypeStruct((CONV_K, c), F32),
                   jax.ShapeDtypeStruct((1, c), F32)],
        compiler_params=_params(("parallel", "arbitrary")),
    )(u, u, w, b.reshape(1, c), dy)


def _conv_bwd_in(dpre, w, name):
    t, c = dpre.shape
    tm, tc = _pick(t, (512, 256, 128)), _pick(c, (1024, 512, 256, 128))
    nb = tm // HALO
    last = t // tm - 1
    cur = pl.BlockSpec((tm, tc), lambda j, i: (i, j))
    nxt = pl.BlockSpec((HALO, tc), lambda j, i: (jnp.minimum((i + 1) * nb, t // HALO - 1), j))
    wspec = pl.BlockSpec((CONV_K, tc), lambda j, i: (0, j))

    def body(d_ref, dn_ref, w_ref, o_ref):
        cur_v = d_ref[...]
        next8 = jnp.where(pl.program_id(1) == last, 0.0, dn_ref[...])
        wv = w_ref[...]
        acc = wv[CONV_K - 1:CONV_K, :] * cur_v
        for s in (1, 2, 3):
            acc = acc + wv[CONV_K - 1 - s:CONV_K - s, :] * _shift_up(cur_v, next8, s)
        o_ref[...] = acc.astype(o_ref.dtype)

    return pl.pallas_call(
        body, name=name, grid=(c // tc, t // tm), in_specs=[cur, nxt, wspec], out_specs=cur,
        out_shape=jax.ShapeDtypeStruct((t, c), BF16),
        compiler_params=_params(("parallel", "parallel")),
    )(dpre, dpre, w)


MEM_HEADS = 4
NT_DIMS = (((1,), (1,)), ((), ()))
TN_DIMS = (((0,), (0,)), ((), ()))


def _dot(a, b, dims=None):
    if dims is None:
        return jnp.dot(a, b, preferred_element_type=F32)
    return lax.dot_general(a, b, dims, preferred_element_type=F32)


def _memattn_probs(q, mk, scale):
    s = _dot(q, mk, NT_DIMS) * scale
    s = s - jnp.max(s, axis=-1, keepdims=True)
    p = jnp.exp(s)
    return p / jnp.sum(p, axis=-1, keepdims=True)


def _memattn_fwd(q, mkv, name):
    t, wd = q.shape
    m = mkv.shape[0]
    hd = wd // MEM_HEADS
    scale = hd ** -0.5
    tm = _pick(t, (512, 256, 128))

    def body(q_ref, mkv_ref, o_ref):
        for h in range(MEM_HEADS):
            cols = slice(h * hd, (h + 1) * hd)
            p = _memattn_probs(q_ref[:, cols], mkv_ref[:, cols], scale)
            o_ref[:, cols] = _dot(p.astype(BF16), mkv_ref[:, wd + h * hd:wd + (h + 1) * hd])

    return pl.pallas_call(
        body, name=name, grid=(t // tm,),
        in_specs=[pl.BlockSpec((tm, wd), lambda i: (i, 0)), pl.BlockSpec((m, 2 * wd), lambda i: (0, 0))],
        out_specs=pl.BlockSpec((tm, wd), lambda i: (i, 0)),
        out_shape=jax.ShapeDtypeStruct((t, wd), F32),
        compiler_params=_params(("parallel",)),
    )(q, mkv)


def _memattn_bwd(q, mkv, dy, name):
    t, wd = q.shape
    m = mkv.shape[0]
    hd = wd // MEM_HEADS
    scale = hd ** -0.5
    tm = _pick(t, (512, 256, 128))

    def body(q_ref, mkv_ref, dy_ref, dq_ref, dmkv_ref):
        i = pl.program_id(0)

        @pl.when(i == 0)
        def _():
            dmkv_ref[...] = jnp.zeros_like(dmkv_ref)

        for h in range(MEM_HEADS):
            cols = slice(h * hd, (h + 1) * hd)
            vcols = slice(wd + h * hd, wd + (h + 1) * hd)
            qh = q_ref[:, cols]
            p = _memattn_probs(qh, mkv_ref[:, cols], scale)
            dyh = dy_ref[:, cols].astype(BF16)
            dp = _dot(dyh, mkv_ref[:, vcols], NT_DIMS)
            ds = (p * (dp - jnp.sum(dp * p, axis=-1, keepdims=True)) * scale).astype(BF16)
            dq_ref[:, cols] = _dot(ds, mkv_ref[:, cols]).astype(dq_ref.dtype)
            dmkv_ref[:, cols] += _dot(ds, qh, TN_DIMS)
            dmkv_ref[:, vcols] += _dot(p.astype(BF16), dyh, TN_DIMS)

    return pl.pallas_call(
        body, name=name, grid=(t // tm,),
        in_specs=[pl.BlockSpec((tm, wd), lambda i: (i, 0)), pl.BlockSpec((m, 2 * wd), lambda i: (0, 0)),
                  pl.BlockSpec((tm, wd), lambda i: (i, 0))],
        out_specs=[pl.BlockSpec((tm, wd), lambda i: (i, 0)), pl.BlockSpec((m, 2 * wd), lambda i: (0, 0))],
        out_shape=[jax.ShapeDtypeStruct((t, wd), BF16), jax.ShapeDtypeStruct((m, 2 * wd), F32)],
        compiler_params=_params(("arbitrary",)),
    )(q, mkv, dy)


NORM_GROUPS = 8


def _gate_fwd(y_tok, y_mem, z, norm_g, name):
    t, tok = y_tok.shape
    mem = y_mem.shape[1]
    mix = tok + mem
    gw = tok // NORM_GROUPS
    tm = _pick(t, (256, 128))

    def body(*refs):
        if norm_g is None:
            yt_ref, ym_ref, z_ref, o_ref = refs
        else:
            yt_ref, ym_ref, z_ref, g_ref, o_ref = refs
        u = yt_ref[...] * _silu(z_ref[:, :tok])
        if norm_g is None:
            o_ref[:, :tok] = u.astype(o_ref.dtype)
        else:
            for k in range(NORM_GROUPS):
                uk = u[:, k * gw:(k + 1) * gw]
                r = lax.rsqrt(jnp.mean(uk * uk, axis=-1, keepdims=True) + EPS)
                o_ref[:, k * gw:(k + 1) * gw] = (uk * r * g_ref[:, k * gw:(k + 1) * gw]).astype(o_ref.dtype)
        o_ref[:, tok:] = (ym_ref[...] * _silu(z_ref[:, tok:])).astype(o_ref.dtype)

    in_specs = [pl.BlockSpec((tm, tok), lambda i: (i, 0)), pl.BlockSpec((tm, mem), lambda i: (i, 0)),
                pl.BlockSpec((tm, mix), lambda i: (i, 0))]
    args = [y_tok, y_mem, z]
    if norm_g is not None:
        in_specs.append(pl.BlockSpec((1, tok), lambda i: (0, 0)))
        args.append(norm_g.reshape(1, tok))
    return pl.pallas_call(
        body, name=name, grid=(t // tm,), in_specs=in_specs,
        out_specs=pl.BlockSpec((tm, mix), lambda i: (i, 0)),
        out_shape=jax.ShapeDtypeStruct((t, mix), BF16),
        compiler_params=_params(("parallel",)),
    )(*args)


def _gate_bwd(y_tok, y_mem, z, norm_g, dgated, name):
    t, tok = y_tok.shape
    mem = y_mem.shape[1]
    mix = tok + mem
    gw = tok // NORM_GROUPS
    tm = _pick(t, (256, 128))

    def body(*refs):
        if norm_g is None:
            yt_ref, ym_ref, z_ref, dg_ref, dyt_ref, dym_ref, dz_ref, dn_ref = refs
        else:
            yt_ref, ym_ref, z_ref, dg_ref, g_ref, dyt_ref, dym_ref, dz_ref, dn_ref = refs
        i = pl.program_id(0)
        zt = z_ref[:, :tok]
        yt = yt_ref[...]
        sz = _silu(zt)
        dout = dg_ref[:, :tok].astype(F32)
        if norm_g is None:
            du = dout
            dn = jnp.zeros((1, tok), F32)
        else:
            u = yt * sz
            dus, dns = [], []
            for k in range(NORM_GROUPS):
                uk = u[:, k * gw:(k + 1) * gw]
                r = lax.rsqrt(jnp.mean(uk * uk, axis=-1, keepdims=True) + EPS)
                nk = uk * r
                dk = dout[:, k * gw:(k + 1) * gw]
                dns.append(jnp.sum(dk * nk, axis=0, keepdims=True))
                dnk = dk * g_ref[:, k * gw:(k + 1) * gw]
                dus.append(r * (dnk - nk * jnp.mean(dnk * nk, axis=-1, keepdims=True)))
            du = jnp.concatenate(dus, axis=1)
            dn = jnp.concatenate(dns, axis=1)
        dyt_ref[...] = du * sz
        dz_ref[:, :tok] = (du * yt * _dsilu(zt)).astype(dz_ref.dtype)
        zm = z_ref[:, tok:]
        dm = dg_ref[:, tok:].astype(F32)
        dym_ref[...] = dm * _silu(zm)
        dz_ref[:, tok:] = (dm * ym_ref[...] * _dsilu(zm)).astype(dz_ref.dtype)

        @pl.when(i == 0)
        def _():
            dn_ref[...] = dn

        @pl.when(i > 0)
        def _():
            dn_ref[...] += dn

    tok_spec = pl.BlockSpec((tm, tok), lambda i: (i, 0))
    mem_spec = pl.BlockSpec((tm, mem), lambda i: (i, 0))
    mix_spec = pl.BlockSpec((tm, mix), lambda i: (i, 0))
    vec = pl.BlockSpec((1, tok), lambda i: (0, 0))
    in_specs = [tok_spec, mem_spec, mix_spec, mix_spec]
    args = [y_tok, y_mem, z, dgated]
    if norm_g is not None:
        in_specs.append(vec)
        args.append(norm_g.reshape(1, tok))
    return pl.pallas_call(
        body, name=name, grid=(t // tm,), in_specs=in_specs,
        out_specs=[tok_spec, mem_spec, mix_spec, vec],
        out_shape=[jax.ShapeDtypeStruct((t, tok), F32), jax.ShapeDtypeStruct((t, mem), F32),
                   jax.ShapeDtypeStruct((t, mix), BF16), jax.ShapeDtypeStruct((1, tok), F32)],
        compiler_params=_params(("arbitrary",)),
    )(*args)


SSD_Q = 128
SSD_N = 128
SSD_P = 64
SSD_G = 8
SSD_HPG = 6
SSD_H = SSD_G * SSD_HPG
SSD_TOK = SSD_H * SSD_P
SSD_XBC = SSD_TOK + 2 * SSD_G * SSD_N
LANES = 128
HIGHEST = lax.Precision.HIGHEST


def _softplus(x):
    return jnp.maximum(x, 0.0) + jnp.log(1.0 + jnp.exp(-jnp.abs(x)))


def _ssd_common(dtr_ref, bias_ref, alog_ref):
    sq = (SSD_Q, LANES)
    pre = dtr_ref[...] + bias_ref[...]
    dt = _softplus(pre)
    a = -jnp.exp(alog_ref[...])
    tril = (_iota(sq, 0) >= _iota(sq, 1)).astype(F32)
    acs = jnp.dot(tril, dt * a, precision=HIGHEST, preferred_element_type=F32)
    return pre, dt, a, tril, acs, acs.T


def _pair_terms(dt, acs, acs_t, h0):
    hi = _iota((SSD_Q, LANES), 1) >= SSD_P
    heads = []
    for j in range(2):
        h = h0 + j
        a_col = _col(acs, h)
        a_row = acs_t[h:h + 1, :]
        a_last = _col(acs[SSD_Q - 1:SSD_Q, :], h)
        heads.append((h, a_col, a_row, a_last, hi if j else jnp.logical_not(hi)))
    dtl = jnp.where(hi, _col(dt, h0 + 1), _col(dt, h0))
    scale = jnp.where(hi, jnp.exp(heads[1][1]), jnp.exp(heads[0][1]))
    dec_last = jnp.where(hi[:1], jnp.exp(heads[1][3]), jnp.exp(heads[0][3]))
    return heads, dtl, scale, dec_last


def _decay(a_col, a_row):
    causal = _iota((SSD_Q, SSD_Q), 0) >= _iota((SSD_Q, SSD_Q), 1)
    return jnp.where(causal, jnp.exp(jnp.minimum(a_col - a_row, 0.0)), 0.0)


def _ssd_fwd(xbc, dt_raw, dt_bias, a_log, dskip_lane, name):
    t = xbc.shape[0]
    nc = t // SSD_Q

    def body(xbc_ref, dtr_ref, bias_ref, alog_ref, dsk_ref, y_ref, hs_ref, h_ref):
        @pl.when(pl.program_id(0) == 0)
        def _():
            h_ref[...] = jnp.zeros_like(h_ref)

        _, dt, _, _, acs, acs_t = _ssd_common(dtr_ref, bias_ref, alog_ref)
        for g in range(SSD_G):
            bg_f = xbc_ref[:, SSD_TOK + g * SSD_N:SSD_TOK + (g + 1) * SSD_N]
            bg = bg_f.astype(BF16)
            cg = xbc_ref[:, SSD_TOK + SSD_G * SSD_N + g * SSD_N:SSD_TOK + SSD_G * SSD_N + (g + 1) * SSD_N].astype(BF16)
            cb = _dot(cg, bg, NT_DIMS)
            for pr in range(SSD_HPG // 2):
                h0 = g * SSD_HPG + 2 * pr
                lanes = slice(h0 * SSD_P, (h0 + 2) * SSD_P)
                heads, dtl, scale, dec_last = _pair_terms(dt, acs, acs_t, h0)
                xs = xbc_ref[:, lanes]
                xdt = xs * dtl
                hp = h_ref[:, lanes]
                hs_ref[:, lanes] = hp
                y = _dot(cg, hp.astype(BF16)) * scale + dsk_ref[:, lanes] * xs
                snew = hp * dec_last
                for _, a_col, a_row, a_last, mask in heads:
                    xm = jnp.where(mask, xdt, 0.0).astype(BF16)
                    y = y + _dot((cb * _decay(a_col, a_row)).astype(BF16), xm)
                    bw = (bg_f * jnp.exp(a_last - a_col)).astype(BF16)
                    snew = snew + _dot(bw, xm, TN_DIMS)
                y_ref[:, lanes] = y
                h_ref[:, lanes] = snew

    row = lambda w: pl.BlockSpec((SSD_Q, w), lambda c: (c, 0))
    vec = lambda w: pl.BlockSpec((1, w), lambda c: (0, 0))
    return pl.pallas_call(
        body, name=name, grid=(nc,),
        in_specs=[row(SSD_XBC), row(LANES), vec(LANES), vec(LANES), vec(SSD_TOK)],
        out_specs=[row(SSD_TOK), row(SSD_TOK)],
        out_shape=[jax.ShapeDtypeStruct((t, SSD_TOK), F32), jax.ShapeDtypeStruct((nc * SSD_N, SSD_TOK), F32)],
        scratch_shapes=[pltpu.VMEM((SSD_N, SSD_TOK), F32)],
        compiler_params=_params(("arbitrary",)),
    )(xbc, dt_raw, dt_bias, a_log, dskip_lane)


def _ssd_bwd(xbc, dt_raw, dt_bias, a_log, dskip_lane, hs, dy, name):
    t = xbc.shape[0]
    nc = t // SSD_Q
    sq = (SSD_Q, LANES)

    def body(xbc_ref, dtr_ref, bias_ref, alog_ref, dsk_ref, hs_ref, dy_ref,
             dxbc_ref, ddtr_ref, dbias_ref, dalog_ref, ddsk_ref, dh_ref):
        first = pl.program_id(0) == 0

        @pl.when(first)
        def _():
            dh_ref[...] = jnp.zeros_like(dh_ref)
            dbias_ref[...] = jnp.zeros_like(dbias_ref)
            dalog_ref[...] = jnp.zeros_like(dalog_ref)
            ddsk_ref[...] = jnp.zeros_like(ddsk_ref)

        pre, dt, a, tril, acs, acs_t = _ssd_common(dtr_ref, bias_ref, alog_ref)
        lane = _iota(sq, 1)
        sub = _iota(sq, 0)
        causal = sub >= lane
        d_acs = jnp.zeros(sq, F32)
        d_acs_row = jnp.zeros(sq, F32)
        d_last = jnp.zeros((1, LANES), F32)
        ddt = jnp.zeros(sq, F32)
        for g in range(SSD_G):
            bcols = slice(SSD_TOK + g * SSD_N, SSD_TOK + (g + 1) * SSD_N)
            ccols = slice(SSD_TOK + SSD_G * SSD_N + g * SSD_N, SSD_TOK + SSD_G * SSD_N + (g + 1) * SSD_N)
            bg_f = xbc_ref[:, bcols]
            bg = bg_f.astype(BF16)
            cg = xbc_ref[:, ccols].astype(BF16)
            cb = _dot(cg, bg, NT_DIMS)
            dcb = jnp.zeros(sq, F32)
            dbg = jnp.zeros(sq, F32)
            dcg = jnp.zeros(sq, F32)
            for pr in range(SSD_HPG // 2):
                h0 = g * SSD_HPG + 2 * pr
                lanes = slice(h0 * SSD_P, (h0 + 2) * SSD_P)
                heads, dtl, scale, dec_last = _pair_terms(dt, acs, acs_t, h0)
                xs = xbc_ref[:, lanes]
                xdt = xs * dtl
                dyv = dy_ref[:, lanes]
                hp = hs_ref[:, lanes]
                dhn = dh_ref[:, lanes]
                hp_b = hp.astype(BF16)
                dys = (dyv * scale).astype(BF16)
                yoff_dy = dyv * _dot(cg, hp_b) * scale
                dcg = dcg + _dot(dys, hp_b, NT_DIMS)
                dhc = _dot(cg, dys, TN_DIMS)
                hh = dhn * hp
                dxdt = jnp.zeros(sq, F32)
                for h, a_col, a_row, a_last, mask in heads:
                    dec = _decay(a_col, a_row)
                    m = cb * dec
                    dym = jnp.where(mask, dyv, 0.0).astype(BF16)
                    xm = jnp.where(mask, xdt, 0.0).astype(BF16)
                    dhm = jnp.where(mask, dhn, 0.0).astype(BF16)
                    w = jnp.exp(a_last - a_col)
                    dxdt = dxdt + _dot(m.astype(BF16), dym, TN_DIMS) + _dot((bg_f * w).astype(BF16), dhm)
                    dm = jnp.where(causal, _dot(dym, xm, NT_DIMS), 0.0)
                    dcb = dcb + dm * dec
                    e = dm * m
                    gj = _dot(xm, dhm, NT_DIMS)
                    dbg = dbg + w * gj
                    wdw = w * jnp.sum(bg_f * gj, axis=1, keepdims=True)
                    col = (jnp.sum(e, axis=1, keepdims=True)
                           + jnp.sum(jnp.where(mask, yoff_dy, 0.0), axis=1, keepdims=True) - wdw)
                    d_acs = d_acs + jnp.where(lane == h, col, 0.0)
                    d_acs_row = d_acs_row + jnp.where(sub == h, jnp.sum(e, axis=0, keepdims=True), 0.0)
                    last = jnp.sum(wdw) + jnp.exp(a_last) * jnp.sum(jnp.where(mask, hh, 0.0))
                    d_last = d_last + jnp.where(lane[:1] == h, last, 0.0)
                dxbc_ref[:, lanes] = dxdt * dtl + dsk_ref[:, lanes] * dyv
                tt = dxdt * xs
                for h, _, _, _, mask in heads:
                    ddt = ddt + jnp.where(lane == h, jnp.sum(jnp.where(mask, tt, 0.0), axis=1, keepdims=True), 0.0)
                ddsk_ref[:, lanes] += jnp.sum(dyv * xs, axis=0, keepdims=True)
                dh_ref[:, lanes] = dhn * dec_last + dhc
            dcb_b = dcb.astype(BF16)
            dxbc_ref[:, bcols] = dbg + _dot(dcb_b, cg, TN_DIMS)
            dxbc_ref[:, ccols] = dcg + _dot(dcb_b, bg)
        d_tot = d_acs - d_acs_row.T + jnp.where(sub == SSD_Q - 1, d_last, 0.0)
        ddta = lax.dot_general(tril, d_tot, TN_DIMS, precision=HIGHEST, preferred_element_type=F32)
        ddt = ddt + ddta * a
        dalog_ref[...] += jnp.sum(ddta * dt, axis=0, keepdims=True) * a
        ddtr = ddt * jax.nn.sigmoid(pre)
        ddtr_ref[...] = ddtr
        dbias_ref[...] += jnp.sum(ddtr, axis=0, keepdims=True)

    rev = lambda w: pl.BlockSpec((SSD_Q, w), lambda i: (nc - 1 - i, 0))
    vec = lambda w: pl.BlockSpec((1, w), lambda i: (0, 0))
    return pl.pallas_call(
        body, name=name, grid=(nc,),
        in_specs=[rev(SSD_XBC), rev(LANES), vec(LANES), vec(LANES), vec(SSD_TOK), rev(SSD_TOK), rev(SSD_TOK)],
        out_specs=[rev(SSD_XBC), rev(LANES), vec(LANES), vec(LANES), vec(SSD_TOK)],
        out_shape=[jax.ShapeDtypeStruct((t, SSD_XBC), F32), jax.ShapeDtypeStruct((t, LANES), F32),
                   jax.ShapeDtypeStruct((1, LANES), F32), jax.ShapeDtypeStruct((1, LANES), F32),
                   jax.ShapeDtypeStruct((1, SSD_TOK), F32)],
        scratch_shapes=[pltpu.VMEM((SSD_N, SSD_TOK), F32)],
        compiler_params=_params(("arbitrary",)),
    )(xbc, dt_raw, dt_bias, a_log, dskip_lane, hs, dy)


ATT_E = 128
ATT_H = 24
ATT_W = 128
ATT_TOK = ATT_H * ATT_E
DILATED_GROUPS = ((128, 1), (512, 4), (2048, 16))
N_DIL = len(DILATED_GROUPS)
ALIBI_MAX_EXP = 8.0
MASKED = -1e30


def _alibi_slopes(group):
    n = N_DIL * ATT_H
    return [2.0 ** (-ALIBI_MAX_EXP * (group * ATT_H + h + 1) / n) for h in range(ATT_H)]


def _att_scores(qh, kk, rel, valid, slope_d):
    s = _dot(qh, kk, NT_DIMS) * (ATT_E ** -0.5) - slope_d * rel
    return jnp.where(valid, s, MASKED)


def _att_rel(j):
    shp = (ATT_W, 2 * ATT_W)
    kpos = _iota(shp, 1)
    rel = _iota(shp, 0) + ATT_W - kpos
    valid = (rel >= 0) & (rel <= ATT_W) & ((kpos >= ATT_W) | (j > 0))
    return rel.astype(F32), valid


def _dil_fwd(q, k, v, group, name):
    t = q.shape[0]
    dil = DILATED_GROUPS[group][1]
    slopes = _alibi_slopes(group)
    nb = t // dil // ATT_W

    def body(q_ref, kp_ref, kc_ref, vp_ref, vc_ref, o_ref, lse_ref):
        rel, valid = _att_rel(pl.program_id(1))
        lane = _iota((ATT_W, LANES), 1)
        lse_all = jnp.zeros((ATT_W, LANES), F32)
        for h in range(ATT_H):
            cols = slice(h * ATT_E, (h + 1) * ATT_E)
            kk = jnp.concatenate([kp_ref[:, cols], kc_ref[:, cols]], axis=0)
            vv = jnp.concatenate([vp_ref[:, cols], vc_ref[:, cols]], axis=0)
            s = _att_scores(q_ref[:, cols], kk, rel, valid, slopes[h] * dil)
            m = jnp.max(s, axis=-1, keepdims=True)
            p = jnp.exp(s - m)
            den = jnp.sum(p, axis=-1, keepdims=True)
            o_ref[:, cols] = _dot(p.astype(BF16), vv) / den
            lse_all = jnp.where(lane == h, m + jnp.log(den), lse_all)
        lse_ref[...] = lse_all

    cur = pl.BlockSpec((ATT_W, ATT_TOK), lambda r, j: (r * nb + j, 0))
    prev = pl.BlockSpec((ATT_W, ATT_TOK), lambda r, j: (r * nb + jnp.maximum(j - 1, 0), 0))
    small = pl.BlockSpec((ATT_W, LANES), lambda r, j: (r * nb + j, 0))
    return pl.pallas_call(
        body, name=name, grid=(dil, nb),
        in_specs=[cur, prev, cur, prev, cur], out_specs=[cur, small],
        out_shape=[jax.ShapeDtypeStruct((t, ATT_TOK), F32), jax.ShapeDtypeStruct((t, LANES), F32)],
        compiler_params=_params(("parallel", "parallel")),
    )(q, k, k, v, v)


def _dil_bwd(q, k, v, do, cterm, lse, group, name):
    t = q.shape[0]
    dil = DILATED_GROUPS[group][1]
    slopes = _alibi_slopes(group)
    nb = t // dil // ATT_W

    def body(q_ref, kp_ref, kc_ref, vp_ref, vc_ref, do_ref, c_ref, lse_ref,
             dq_ref, dk_ref, dv_ref, ck_ref, cv_ref):
        j = pl.program_id(1)

        @pl.when(j == 0)
        def _():
            ck_ref[...] = jnp.zeros_like(ck_ref)
            cv_ref[...] = jnp.zeros_like(cv_ref)

        @pl.when(j < nb)
        def _():
            rel, valid = _att_rel(j)
            cv_, lv = c_ref[...], lse_ref[...]
            for h in range(ATT_H):
                cols = slice(h * ATT_E, (h + 1) * ATT_E)
                qh = q_ref[:, cols]
                kk = jnp.concatenate([kp_ref[:, cols], kc_ref[:, cols]], axis=0)
                vv = jnp.concatenate([vp_ref[:, cols], vc_ref[:, cols]], axis=0)
                s = _att_scores(qh, kk, rel, valid, slopes[h] * dil)
                p = jnp.where(valid, jnp.exp(s - _col(lv, h)), 0.0)
                do = do_ref[:, cols]
                dp = _dot(do, vv, NT_DIMS)
                ds = (p * (dp + _col(cv_, h)) * (ATT_E ** -0.5)).astype(BF16)
                dq_ref[:, cols] = _dot(ds, kk).astype(dq_ref.dtype)
                dkk = _dot(ds, qh, TN_DIMS)
                dvv = _dot(p.astype(BF16), do, TN_DIMS)
                dk_ref[:, cols] = (ck_ref[:, cols] + dkk[:ATT_W]).astype(dk_ref.dtype)
                dv_ref[:, cols] = (cv_ref[:, cols] + dvv[:ATT_W]).astype(dv_ref.dtype)
                ck_ref[:, cols] = dkk[ATT_W:]
                cv_ref[:, cols] = dvv[ATT_W:]

        @pl.when(j == nb)
        def _():
            dk_ref[...] = ck_ref[...].astype(dk_ref.dtype)
            dv_ref[...] = cv_ref[...].astype(dv_ref.dtype)

    jq = lambda j: jnp.minimum(j, nb - 1)
    cur = pl.BlockSpec((ATT_W, ATT_TOK), lambda r, j: (r * nb + jq(j), 0))
    prev = pl.BlockSpec((ATT_W, ATT_TOK), lambda r, j: (r * nb + jnp.maximum(jq(j) - 1, 0), 0))
    small = pl.BlockSpec((ATT_W, LANES), lambda r, j: (r * nb + jq(j), 0))
    late = pl.BlockSpec((ATT_W, ATT_TOK), lambda r, j: (r * nb + jnp.maximum(j - 1, 0), 0))
    big = jax.ShapeDtypeStruct((t, ATT_TOK), BF16)
    return pl.pallas_call(
        body, name=name, grid=(dil, nb + 1),
        in_specs=[cur, prev, cur, prev, cur, cur, small, small], out_specs=[cur, late, late],
        out_shape=[big, big, big],
        scratch_shapes=[pltpu.VMEM((ATT_W, ATT_TOK), F32), pltpu.VMEM((ATT_W, ATT_TOK), F32)],
        compiler_params=_params(("parallel", "arbitrary")),
    )(q, k, k, v, v, do, cterm, lse)


def _combine_weights(lses):
    m = functools.reduce(jnp.maximum, lses)
    es = [jnp.exp(l - m) for l in lses]
    tot = functools.reduce(lambda a, b: a + b, es)
    return [e / tot for e in es]


DILS = tuple(d for _, d in DILATED_GROUPS)
COMBINE_ROWS = 256


def _by_residue(arr, dil):
    return arr if dil == 1 else arr.reshape(dil, arr.shape[0] // dil, arr.shape[1])


def _natural_lses(l_refs, small_refs, tm):
    vals = []
    for g, dil in enumerate(DILS):
        if dil == 1:
            vals.append(l_refs[g][...])
        else:
            _scatter_rows(l_refs[g], small_refs[g], dil, tm)
            vals.append(small_refs[g][0])
    return vals


def _combine_scratch(tm):
    return ([pltpu.VMEM((_chunks(ATT_TOK), tm, 128), F32)] * N_DIL + [pltpu.VMEM((1, tm, 128), F32)] * N_DIL)


def _combine_fwd(outs, lses, name):
    t = outs[0].shape[0]
    tm = COMBINE_ROWS

    def body(*refs):
        o_refs, l_refs, y_ref = refs[:N_DIL], refs[N_DIL:2 * N_DIL], refs[2 * N_DIL]
        big_refs, small_refs = refs[2 * N_DIL + 1:3 * N_DIL + 1], refs[3 * N_DIL + 1:]
        ws = _combine_weights(_natural_lses(l_refs, small_refs, tm))
        for g in range(1, N_DIL):
            _scatter_rows(o_refs[g], big_refs[g], DILS[g], tm)
        for h in range(ATT_H):
            cols = slice(h * ATT_E, (h + 1) * ATT_E)
            y_ref[:, cols] = (_col(ws[0], h) * o_refs[0][:, cols]
                              + sum(_col(ws[g], h) * big_refs[g][h] for g in range(1, N_DIL)))

    return pl.pallas_call(
        body, name=name, grid=(t // tm,),
        in_specs=[_perm_spec(tm, d, ATT_TOK) for d in DILS] + [_perm_spec(tm, d, LANES) for d in DILS],
        out_specs=pl.BlockSpec((tm, ATT_TOK), lambda i: (i, 0)),
        out_shape=jax.ShapeDtypeStruct((t, ATT_TOK), F32),
        scratch_shapes=_combine_scratch(tm),
        compiler_params=_params(("parallel",)),
    )(*[_by_residue(o, d) for o, d in zip(outs, DILS)], *[_by_residue(l, d) for l, d in zip(lses, DILS)])


def _combine_bwd(outs, lses, dy, name):
    t = outs[0].shape[0]
    tm = COMBINE_ROWS

    def body(*refs):
        o_refs, l_refs, dy_ref = refs[:N_DIL], refs[N_DIL:2 * N_DIL], refs[2 * N_DIL]
        do_refs, c_refs = refs[2 * N_DIL + 1:3 * N_DIL + 1], refs[3 * N_DIL + 1:4 * N_DIL + 1]
        big_refs, small_refs = refs[4 * N_DIL + 1:5 * N_DIL + 1], refs[5 * N_DIL + 1:]
        ws = _combine_weights(_natural_lses(l_refs, small_refs, tm))
        for g in range(1, N_DIL):
            _scatter_rows(o_refs[g], big_refs[g], DILS[g], tm)
        lane = _iota((tm, LANES), 1)
        sdw = jnp.zeros((tm, LANES), F32)
        for h in range(ATT_H):
            cols = slice(h * ATT_E, (h + 1) * ATT_E)
            dyh = dy_ref[:, cols]
            tot = _col(ws[0], h) * jnp.sum(dyh * o_refs[0][:, cols], axis=1, keepdims=True)
            for g in range(1, N_DIL):
                tot = tot + _col(ws[g], h) * jnp.sum(dyh * big_refs[g][h], axis=1, keepdims=True)
            sdw = jnp.where(lane == h, tot, sdw)
        for g, dil in enumerate(DILS):
            cterm = -ws[g] * sdw
            if dil == 1:
                c_refs[g][...] = cterm
            else:
                small_refs[g][0] = cterm
                _gather_rows(small_refs[g], c_refs[g], dil, tm)
            for h in range(ATT_H):
                cols = slice(h * ATT_E, (h + 1) * ATT_E)
                do = _col(ws[g], h) * dy_ref[:, cols]
                if dil == 1:
                    do_refs[g][:, cols] = do.astype(BF16)
                else:
                    big_refs[g][h] = do
            if dil > 1:
                _gather_rows(big_refs[g], do_refs[g], dil, tm)

    res = pl.pallas_call(
        body, name=name, grid=(t // tm,),
        in_specs=([_perm_spec(tm, d, ATT_TOK) for d in DILS] + [_perm_spec(tm, d, LANES) for d in DILS]
                  + [pl.BlockSpec((tm, ATT_TOK), lambda i: (i, 0))]),
        out_specs=[_perm_spec(tm, d, ATT_TOK) for d in DILS] + [_perm_spec(tm, d, LANES) for d in DILS],
        out_shape=[_perm_shape(t, d, ATT_TOK, BF16) for d in DILS] + [_perm_shape(t, d, LANES, F32) for d in DILS],
        scratch_shapes=_combine_scratch(tm),
        compiler_params=_params(("parallel",)),
    )(*[_by_residue(o, d) for o, d in zip(outs, DILS)], *[_by_residue(l, d) for l, d in zip(lses, DILS)], dy)
    return [a.reshape(t, ATT_TOK) for a in res[:N_DIL]], [a.reshape(t, LANES) for a in res[N_DIL:]]


ADAM_LR, ADAM_B1, ADAM_B2, ADAM_EPS, ADAM_WD, ADAM_STEP = 0.001, 0.9, 0.999, 1e-08, 0.01, 10


def _adamw(parts, w, m, v, name):
    r, c = w.shape
    n_parts = parts.shape[0]
    tc = _pick(c, (1024, 512, 256, 128)) if c % 128 == 0 else c
    tm = _pick(r, (128, 64, 32, 16, 8))

    def body(p_ref, w_ref, m_ref, v_ref, g_ref, d_ref, nm_ref, nv_ref):
        g = p_ref[0].astype(F32)
        for k in range(1, n_parts):
            g = g + p_ref[k].astype(F32)
        nm = ADAM_B1 * m_ref[...] + (1.0 - ADAM_B1) * g
        nv = ADAM_B2 * v_ref[...] + (1.0 - ADAM_B2) * (g * g)
        m_hat = nm / (1.0 - ADAM_B1 ** ADAM_STEP)
        v_hat = nv / (1.0 - ADAM_B2 ** ADAM_STEP)
        g_ref[...] = g
        d_ref[...] = -ADAM_LR * (m_hat / (jnp.sqrt(v_hat) + ADAM_EPS) + ADAM_WD * w_ref[...])
        nm_ref[...] = nm
        nv_ref[...] = nv

    blk = pl.BlockSpec((tm, tc), lambda i, j: (i, j))
    pblk = pl.BlockSpec((n_parts, tm, tc), lambda i, j: (0, i, j))
    return pl.pallas_call(
        body, name=name, grid=(r // tm, c // tc), in_specs=[pblk, blk, blk, blk], out_specs=[blk] * 4,
        out_shape=[jax.ShapeDtypeStruct((r, c), F32)] * 4,
        compiler_params=_params(("parallel", "parallel")),
    )(parts, w, m, v)


N_CHIP = 4
MESH_ID = pl.DeviceIdType.MESH


def _other_chips(x, y):
    return [(1 - x, y), (x, 1 - y), (1 - x, 1 - y)]


GATHER_SEMS = N_DEV - 1


def _gather_copies(in_refs, out_refs, send_sems, recv_sems, local_sems, x, y, c):
    n = len(in_refs)
    sibling = (x, y, 1 - c)
    chips = _other_chips(x, y)

    def copy(a, k, block, to, src=None):
        rows = out_refs[a].at[4 * block[0] + 2 * block[1] + block[2]]
        return pltpu.make_async_remote_copy(
            src_ref=rows if src is None else src, dst_ref=rows,
            send_sem=send_sems.at[a * GATHER_SEMS + k], recv_sem=recv_sems.at[a * GATHER_SEMS + k],
            device_id=to, device_id_type=MESH_ID)

    started = []
    for a in range(n):
        local = pltpu.make_async_copy(in_refs[a], out_refs[a].at[4 * x + 2 * y + c], local_sems.at[a])
        local.start()
        started.append(local)
    sends = []
    for j, chip in enumerate(chips):
        for a in range(n):
            sends.append(copy(a, 1 + j, (x, y, c), (*chip, c), src=in_refs[a]))
            sends[-1].start()
    for a in range(n):
        sends.append(copy(a, 0, (x, y, c), sibling, src=in_refs[a]))
        sends[-1].start()
    for j, chip in enumerate(chips):
        for a in range(n):
            copy(a, 1 + j, (*chip, c), (x, y, c)).wait_recv()
            sends.append(copy(a, 4 + j, (*chip, c), sibling))
            sends[-1].start()
    for a in range(n):
        copy(a, 0, sibling, (x, y, c)).wait_recv()
        for j, chip in enumerate(chips):
            copy(a, 4 + j, (*chip, 1 - c), (x, y, c)).wait_recv()
    for cp in sends:
        cp.wait_send()
    for local in started:
        local.wait()


def _gather_two_level(arrays, name):
    n = len(arrays)
    per = GATHER_SEMS

    def body(*refs):
        x, y, c = lax.axis_index("x"), lax.axis_index("y"), lax.axis_index("c")
        _gather_copies(refs[:n], refs[n:2 * n], *refs[2 * n:], x, y, c)

    any_spec = pl.BlockSpec(memory_space=pl.ANY)
    return pl.pallas_call(
        body, name=name, in_specs=[any_spec] * n, out_specs=[any_spec] * n,
        out_shape=[jax.ShapeDtypeStruct((N_DEV,) + a.shape, a.dtype) for a in arrays],
        scratch_shapes=[pltpu.SemaphoreType.DMA((n * per,)), pltpu.SemaphoreType.DMA((n * per,)),
                        pltpu.SemaphoreType.DMA((n,))],
        compiler_params=pltpu.CompilerParams(has_side_effects=True),
    )(*arrays)


def _handshake(barrier, peers):
    for peer in peers:
        pl.semaphore_signal(barrier, inc=1, device_id=peer, device_id_type=MESH_ID)
    pl.semaphore_wait(barrier, len(peers))


def _on_sequencer(name, collective_id, arrays, out_structs, sem_counts, peers, copies):
    hbm = pltpu.MemorySpace.HBM
    in_refs = [jax.new_ref(a, memory_space=hbm) for a in arrays]
    out_refs = [jax.empty_ref(s, memory_space=hbm) for s in out_structs]

    @pl.kernel(mesh=plsc.ScalarSubcoreMesh(axis_name="seq", num_cores=1), name=name,
               scratch_types=tuple(pltpu.SemaphoreType.DMA((k,)) for k in sem_counts),
               compiler_params=pltpu.CompilerParams(collective_id=collective_id))
    def launch(*sems):
        x, y, c = lax.axis_index("x"), lax.axis_index("y"), lax.axis_index("c")
        _handshake(pltpu.get_barrier_semaphore(), peers(x, y, c))
        copies(in_refs, out_refs, *sems, x, y, c)

    launch()
    return [o[...] for o in out_refs]


def _all_others(x, y, c):
    return [(x ^ ((k >> 2) & 1), y ^ ((k >> 1) & 1), c ^ (k & 1)) for k in range(1, N_DEV)]


def _sc_gather(arrays, name, collective_id):
    n = len(arrays)
    outs = [jax.ShapeDtypeStruct((N_DEV,) + a.shape, a.dtype) for a in arrays]
    return _on_sequencer(name, collective_id, arrays, outs, (n * GATHER_SEMS, n * GATHER_SEMS, n), _all_others,
                         _gather_copies)


def _sc_chip_exchange(sums, name, collective_id):
    n = len(sums)
    per = N_CHIP - 1
    outs = [jax.ShapeDtypeStruct(s.shape, s.dtype) for s in sums]
    return _on_sequencer(name, collective_id, sums, outs, (n * per, n * per, n),
                         lambda x, y, c: [(px, py, c) for px, py in _other_chips(x, y)], _chip_exchange_copies)


def _sibling_swap_copies(in_refs, out_refs, send_sems, recv_sems, x, y, c):
    sends = []
    for a in range(len(in_refs)):
        for q in range(N_CHIP):
            cp = pltpu.make_async_remote_copy(
                src_ref=in_refs[a].at[2 * q + 1 - c], dst_ref=out_refs[a].at[q],
                send_sem=send_sems.at[a * N_CHIP + q], recv_sem=recv_sems.at[a * N_CHIP + q],
                device_id=(x, y, 1 - c), device_id_type=MESH_ID)
            cp.start()
            sends.append(cp)
    for cp in sends:
        cp.wait_recv()
    for cp in sends:
        cp.wait_send()


def _sc_sibling_swap(parts, name, collective_id):
    n = len(parts)
    outs = [jax.ShapeDtypeStruct((N_CHIP,) + p.shape[1:], p.dtype) for p in parts]
    return _on_sequencer(name, collective_id, parts, outs, (n * N_CHIP, n * N_CHIP),
                         lambda x, y, c: [(x, y, 1 - c)], _sibling_swap_copies)


def _chip_sum(part, landed, core, name):
    _, r, c = part.shape
    tc = _pick(c, (1024, 512, 256, 128)) if c % 128 == 0 else c
    tm = _pick(r, (256, 128, 64, 32, 16, 8))

    def body(core_ref, p_ref, l_ref, o_ref):
        o_ref[...] = (p_ref[...].astype(F32) + l_ref[...].astype(F32)).astype(o_ref.dtype)

    grid_spec = pltpu.PrefetchScalarGridSpec(
        num_scalar_prefetch=1, grid=(N_CHIP, r // tm, c // tc),
        in_specs=[pl.BlockSpec((None, tm, tc), lambda q, i, j, core_ref: (2 * q + core_ref[0], i, j)),
                  pl.BlockSpec((None, tm, tc), lambda q, i, j, core_ref: (q, i, j))],
        out_specs=pl.BlockSpec((None, tm, tc), lambda q, i, j, core_ref: (q, i, j)))
    return pl.pallas_call(
        body, name=name, grid_spec=grid_spec, out_shape=jax.ShapeDtypeStruct(landed.shape, landed.dtype),
        compiler_params=_params(("parallel", "parallel", "parallel")),
    )(core, part, landed)


def _chip_exchange_copies(in_refs, out_refs, send_sems, recv_sems, local_sems, x, y, c):
    n = len(in_refs)
    per = N_CHIP - 1
    mine = 2 * x + y
    started = []
    for a in range(n):
        local = pltpu.make_async_copy(in_refs[a].at[mine], out_refs[a].at[mine], local_sems.at[a])
        local.start()
        started.append(local)
    sends = []
    for j, (px, py) in enumerate(_other_chips(x, y)):
        for a in range(n):
            cp = pltpu.make_async_remote_copy(
                src_ref=in_refs[a].at[2 * px + py], dst_ref=out_refs[a].at[mine],
                send_sem=send_sems.at[a * per + j], recv_sem=recv_sems.at[a * per + j],
                device_id=(px, py, c), device_id_type=MESH_ID)
            cp.start()
            sends.append((cp, a, j, 2 * px + py))
    for cp, a, j, peer in sends:
        pltpu.make_async_remote_copy(
            src_ref=out_refs[a].at[peer], dst_ref=out_refs[a].at[peer],
            send_sem=send_sems.at[a * per + j], recv_sem=recv_sems.at[a * per + j],
            device_id=(x, y, c), device_id_type=MESH_ID).wait_recv()
    for cp, _, _, _ in sends:
        cp.wait_send()
    for local in started:
        local.wait()


DEPTH = 4
MEM_W = 1024
MIX_W = SSD_TOK + MEM_W
DT_PAD = LANES - SSD_H


def _is_ssd(i):
    return i % 2 == 0


def _weight_names():
    names = ["mem_norm_g", "final_norm_g"]
    for i in range(DEPTH):
        names += [f"norm_g_{i}", f"w_in_{i}"]
        if _is_ssd(i):
            names += [f"conv_w_{i}", f"conv_b_{i}", f"dt_bias_{i}", f"a_log_{i}", f"d_skip_{i}", f"ssd_norm_g_{i}"]
        names += [f"w_mem_kv_{i}", f"w_out_{i}"]
    return names


WEIGHTS = _weight_names()
INPUTS = ["x", "mem"] + WEIGHTS + ["loss_target"] + ["m_" + n for n in WEIGHTS] + ["v_" + n for n in WEIGHTS]


def _in_segments(i):
    if _is_ssd(i):
        return [("xbc", 0, SSD_XBC), ("dt", SSD_XBC, SSD_H), ("qm", SSD_XBC + SSD_H, MEM_W),
                ("z", SSD_XBC + SSD_H + MEM_W, MIX_W)]
    segs = []
    for g in range(N_DIL):
        for j, nm in enumerate("qkv"):
            segs.append((f"{nm}{g}", (3 * g + j) * ATT_TOK, ATT_TOK))
    segs += [("qm", 3 * N_DIL * ATT_TOK, MEM_W), ("z", 3 * N_DIL * ATT_TOK + MEM_W, MIX_W)]
    return segs


def _split_w_in(i, w_in):
    out = {}
    for nm, start, width in _in_segments(i):
        seg = w_in[:, start:start + width]
        out[nm] = jnp.pad(seg, ((0, 0), (0, DT_PAD))) if nm == "dt" else seg
    return out


def _join_dw_in(i, dws):
    return jnp.concatenate([dws[nm][:, :width] for nm, _, width in _in_segments(i)], axis=1)


SEG_DTYPE = {"xbc": F32, "dt": F32, "z": F32}


def _layer_dils(i):
    return (1,) if _is_ssd(i) else DILS


def _seg_order(nm):
    return int(nm[1]) if nm[0] in "qkv" and nm[1:].isdigit() else 0


def _layer_fwd(i, x, mem_b, p):
    tag = f"l{i}"
    hs = _rmsnorm_fwd(x, p["norm_g"], tag + "_norm", _layer_dils(i))
    proj = {nm: _matmul(hs[_seg_order(nm)], w, out_dtype=SEG_DTYPE.get(nm, BF16), name=f"{tag}_in_{nm}")
            for nm, w in p["win"].items()}
    sv = {"x": x, "h": hs, "proj": proj}
    if _is_ssd(i):
        xbc = _conv_fwd(proj["xbc"], p["conv_w"], p["conv_b"], tag + "_conv")
        y_tok, hs = _ssd_fwd(xbc, proj["dt"], p["dt_bias_p"], p["a_log_p"], p["dskip_lane"], tag + "_ssd")
        sv.update(xbc=xbc, hs=hs)
    else:
        outs, lses = [], []
        for g in range(N_DIL):
            o, lse = _dil_fwd(proj[f"q{g}"], proj[f"k{g}"], proj[f"v{g}"], g, f"{tag}_att{g}")
            outs.append(o)
            lses.append(lse)
        y_tok = _combine_fwd(outs, lses, tag + "_comb")
        sv.update(outs=outs, lses=lses)
    mkv = _matmul(mem_b, p["wmkv"], out_dtype=BF16, name=tag + "_mkv")
    y_mem = _memattn_fwd(proj["qm"], mkv, tag + "_mem")
    gated = _gate_fwd(y_tok, y_mem, proj["z"], p.get("ssd_norm_g"), tag + "_gate")
    x_out = _matmul(gated, p["wout"], out_dtype=F32, add=x, name=tag + "_out")
    sv.update(y_tok=y_tok, y_mem=y_mem, mkv=mkv, gated=gated)
    return x_out, sv


def _layer_bwd(i, sv, dx_out, dxb_out, dmem_n, mem_b, p):
    tag = f"l{i}b"
    proj = sv["proj"]
    gr = {}
    dgated = _matmul(dxb_out, p["wout"], tb=True, out_dtype=F32, name=tag + "_dgated")
    gr["w_out"] = _matmul(sv["gated"], dxb_out, ta=True, out_dtype=BF16, name=tag + "_dwout")
    dy_tok, dy_mem, dz, dssd_g = _gate_bwd(sv["y_tok"], sv["y_mem"], proj["z"], p.get("ssd_norm_g"), dgated, tag + "_gate")
    dq_mem, dmkv = _memattn_bwd(proj["qm"], sv["mkv"], dy_mem, tag + "_mem")
    gr["w_mem_kv"] = _matmul(mem_b, dmkv, ta=True, out_dtype=BF16, name=tag + "_dwmkv")
    dmem_n = _matmul(dmkv, p["wmkv"], tb=True, out_dtype=F32, add=dmem_n, name=tag + "_dmem")
    dproj = {"qm": dq_mem, "z": dz}
    if _is_ssd(i):
        dxbc, ddt_raw, dbias, dalog, ddsk = _ssd_bwd(sv["xbc"], proj["dt"], p["dt_bias_p"], p["a_log_p"], p["dskip_lane"],
                                                     sv["hs"], dy_tok, tag + "_ssd")
        dpre, dconv_w, dconv_b = _conv_bwd_pre(proj["xbc"], p["conv_w"], p["conv_b"], dxbc, tag + "_convpre")
        dproj["xbc"] = _conv_bwd_in(dpre, p["conv_w"], tag + "_convin")
        dproj["dt"] = ddt_raw
        gr.update(conv_w=dconv_w, conv_b=dconv_b[0], dt_bias=dbias[0, :SSD_H], a_log=dalog[0, :SSD_H],
                  d_skip=jnp.sum(ddsk.reshape(SSD_H, SSD_P), axis=1), ssd_norm_g=dssd_g[0])
    else:
        dos, cs = _combine_bwd(sv["outs"], sv["lses"], dy_tok, tag + "_comb")
        for g in range(N_DIL):
            dq, dk, dv = _dil_bwd(proj[f"q{g}"], proj[f"k{g}"], proj[f"v{g}"], dos[g], cs[g], sv["lses"][g], g,
                                  f"{tag}_att{g}")
            dproj.update({f"q{g}": dq, f"k{g}": dk, f"v{g}": dv})
    dils = _layer_dils(i)
    dhs = [None] * len(dils)
    dws = {}
    for nm, w in p["win"].items():
        o = _seg_order(nm)
        dhs[o] = _matmul(dproj[nm], w, tb=True, out_dtype=F32, add=dhs[o], name=f"{tag}_dh_{nm}")
        dws[nm] = _matmul(sv["h"][o], dproj[nm], ta=True, out_dtype=BF16, name=f"{tag}_dw_{nm}")
    gr["w_in"] = _join_dw_in(i, dws)
    dx, dxb, dnorm_g = _rmsnorm_bwd(sv["x"], p["norm_g"], dhs, dx_out, tag + "_norm", dils)
    gr["norm_g"] = dnorm_g[0]
    return dx, dxb, dmem_n, gr


def _pad_heads(v):
    return jnp.pad(v.reshape(1, SSD_H), ((0, 0), (0, DT_PAD)))


def _layer_params(i, small, w_in, w_mem_kv, w_out):
    p = {"norm_g": small[f"norm_g_{i}"], "win": _split_w_in(i, w_in), "wmkv": w_mem_kv, "wout": w_out}
    if _is_ssd(i):
        p.update(conv_w=small[f"conv_w_{i}"], conv_b=small[f"conv_b_{i}"], ssd_norm_g=small[f"ssd_norm_g_{i}"],
                 dt_bias_p=_pad_heads(small[f"dt_bias_{i}"]), a_log_p=_pad_heads(small[f"a_log_{i}"]),
                 dskip_lane=jnp.repeat(small[f"d_skip_{i}"], SSD_P).reshape(1, SSD_TOK))
    return p


def _local_step(x, mem, target, small, weights, emit):
    mem_b = _rmsnorm_fwd(mem, small["mem_norm_g"], "mem_norm")[0]
    params, saved = [], []
    for i in range(DEPTH):
        big, x = weights(i, x)
        params.append(_layer_params(i, small, *big))
        x, sv = _layer_fwd(i, x, mem_b, params[i])
        saved.append(sv)
    loss, dx, dxb, dfinal = _final_loss(x, small["final_norm_g"], target)
    grads = {"final_norm_g": dfinal[0]}
    dmem_n = None
    for i in reversed(range(DEPTH)):
        dx, dxb, dmem_n, gr = _layer_bwd(i, saved[i], dx, dxb, dmem_n, mem_b, params[i])
        emit(i, {nm: gr.pop(nm) for nm in BIG})
        grads.update({f"{nm}_{i}": g for nm, g in gr.items()})
    _, _, dmem_g = _rmsnorm_bwd(mem, small["mem_norm_g"], [dmem_n], None, "mem_norm_b")
    grads["mem_norm_g"] = dmem_g[0]
    return loss[0, 0], dx, grads


BIG = ("w_in", "w_mem_kv", "w_out")
SMALL = [n for n in WEIGHTS if not n.startswith(BIG)]
PACK_ROWS = 8 * LANES
GATHER_COLLECTIVE_ID = 0
SCATTER_COLLECTIVE_ID = 4
SWAP_COLLECTIVE_ID = 8


def _pack(vals):
    flat = jnp.concatenate([v.reshape(-1).astype(F32) for v in vals])
    padded = -(-flat.shape[0] // PACK_ROWS) * PACK_ROWS
    return jnp.pad(flat, (0, padded - flat.shape[0])).reshape(padded // LANES, LANES)


def _train_step(a, local_step):
    x, y, c = lax.axis_index("x"), lax.axis_index("y"), lax.axis_index("c")
    me = 4 * x + 2 * y + c
    gathered_w = {}

    def gather(i, after):
        shards = [a[f"{nm}_{i}"].astype(BF16) for nm in BIG]
        if after is not None:
            *shards, _ = lax.optimization_barrier((*shards, after))
        if i == 0:
            gathered_w[i] = _gather_two_level(shards, f"gather_w{i}")
        else:
            gathered_w[i] = _sc_gather(shards, f"gather_w{i}", GATHER_COLLECTIVE_ID + i)

    def weights(i, act):
        if i == 0:
            gather(0, None)
            gather(1, gathered_w[0][0])
        elif i == 1:
            gather(2, act)
            gather(3, act)
        *raw, act = lax.optimization_barrier((*gathered_w.pop(i), act))
        g_in, g_kv, g_out = raw
        whole = (jnp.transpose(g_in, (1, 0, 2)).reshape(g_in.shape[1], N_DEV * g_in.shape[2]),
                 g_kv.reshape(N_DEV * g_kv.shape[1], g_kv.shape[2]),
                 g_out.reshape(N_DEV * g_out.shape[1], g_out.shape[2]))
        return whole, act

    conv_names = [n for n in SMALL if n.startswith("conv_w")]
    conv_full = _gather_two_level([a[n] for n in conv_names], "gather_conv")
    small = {n: a[n] for n in SMALL}
    for n, gathered in zip(conv_names, conv_full):
        small[n] = jnp.transpose(gathered, (1, 0, 2)).reshape(gathered.shape[1], N_DEV * gathered.shape[2])

    core = c.astype(jnp.int32).reshape(1)
    landed = {}

    def reduce_scatter(i, gr):
        d, cols = gr["w_in"].shape
        parts = [jnp.transpose(gr["w_in"].reshape(d, N_DEV, cols // N_DEV), (1, 0, 2))]
        for nm in BIG[1:]:
            parts.append(gr[nm].reshape(N_DEV, gr[nm].shape[0] // N_DEV, gr[nm].shape[1]))
        swapped = _sc_sibling_swap(parts, f"swap_w{i}", SWAP_COLLECTIVE_ID + i)
        sums = [_chip_sum(p, s, core, f"chipsum_{nm}_{i}") for nm, p, s in zip(BIG, parts, swapped)]
        landed[i] = _sc_chip_exchange(sums, f"scatter_w{i}", SCATTER_COLLECTIVE_ID + i)

    loss_local, grad_x, grads = local_step(a["x"][0], a["mem"][0], a["loss_target"][0], small, weights, reduce_scatter)
    loss = lax.psum(loss_local, ("x", "y", "c"))

    res = {}
    for i in reversed(range(DEPTH)):
        for nm, p in zip(BIG, landed[i]):
            n = f"{nm}_{i}"
            res[n] = _adamw(p, a[n], a["m_" + n], a["v_" + n], "adamw_" + n)

    gathered = _gather_two_level([_pack([grads[n] for n in SMALL])], "gather_small")[0]
    zero_conv = lambda pre: [jnp.zeros(small[n].shape, F32) if n in conv_names else a[pre + n] for n in SMALL]
    packed = _adamw(gathered, _pack(zero_conv("")), _pack(zero_conv("m_")), _pack(zero_conv("v_")), "adamw_small")
    off = 0
    for n in SMALL:
        size = math.prod(small[n].shape)
        if n in conv_names:
            rows, cols = a[n].shape
            whole = gathered.reshape(N_DEV, -1)[:, off:off + size].reshape(N_DEV, rows, N_DEV * cols)
            mine = lax.dynamic_slice_in_dim(whole, me * cols, cols, axis=2)
            res[n] = _adamw(mine, a[n], a["m_" + n], a["v_" + n], "adamw_" + n)
        else:
            res[n] = [o.reshape(-1)[off:off + size].reshape(a[n].shape) for o in packed]
        off += size
    outs = [loss, grad_x[None]]
    for k in range(4):
        outs += [res[n][k] for n in WEIGHTS]
    return tuple(outs)


def kernel(x, mem, mem_norm_g, final_norm_g, norm_g_0, w_in_0, conv_w_0, conv_b_0, dt_bias_0, a_log_0, d_skip_0, ssd_norm_g_0, w_mem_kv_0, w_out_0, norm_g_1, w_in_1, w_mem_kv_1, w_out_1, norm_g_2, w_in_2, conv_w_2, conv_b_2, dt_bias_2, a_log_2, d_skip_2, ssd_norm_g_2, w_mem_kv_2, w_out_2, norm_g_3, w_in_3, w_mem_kv_3, w_out_3, loss_target, m_mem_norm_g, m_final_norm_g, m_norm_g_0, m_w_in_0, m_conv_w_0, m_conv_b_0, m_dt_bias_0, m_a_log_0, m_d_skip_0, m_ssd_norm_g_0, m_w_mem_kv_0, m_w_out_0, m_norm_g_1, m_w_in_1, m_w_mem_kv_1, m_w_out_1, m_norm_g_2, m_w_in_2, m_conv_w_2, m_conv_b_2, m_dt_bias_2, m_a_log_2, m_d_skip_2, m_ssd_norm_g_2, m_w_mem_kv_2, m_w_out_2, m_norm_g_3, m_w_in_3, m_w_mem_kv_3, m_w_out_3, v_mem_norm_g, v_final_norm_g, v_norm_g_0, v_w_in_0, v_conv_w_0, v_conv_b_0, v_dt_bias_0, v_a_log_0, v_d_skip_0, v_ssd_norm_g_0, v_w_mem_kv_0, v_w_out_0, v_norm_g_1, v_w_in_1, v_w_mem_kv_1, v_w_out_1, v_norm_g_2, v_w_in_2, v_conv_w_2, v_conv_b_2, v_dt_bias_2, v_a_log_2, v_d_skip_2, v_ssd_norm_g_2, v_w_mem_kv_2, v_w_out_2, v_norm_g_3, v_w_in_3, v_w_mem_kv_3, v_w_out_3):
    vals = locals()
    return _train_step({n: vals[n] for n in INPUTS}, _local_step)
```

```python
import functools
import math

import jax
import jax.numpy as jnp
from jax import lax
from jax.experimental import pallas as pl
from jax.experimental.pallas import tpu as pltpu
from jax.experimental.pallas import tpu_sc as plsc

F32 = jnp.float32
BF16 = jnp.bfloat16
EPS = 1e-6
N_DEV = 8
VMEM_LIMIT_BYTES = 56 * 1024 * 1024


def _pick(n, prefs):
    for p in prefs:
        if n % p == 0:
            return p
    return n


def _params(sem):
    return pltpu.CompilerParams(dimension_semantics=sem, vmem_limit_bytes=VMEM_LIMIT_BYTES)


def _matmul(a, b, *, ta=False, tb=False, out_dtype=F32, add=None, name="mm"):
    if ta:
        k_dim, m_dim = a.shape
    else:
        m_dim, k_dim = a.shape
    n_dim = b.shape[0] if tb else b.shape[1]
    out_bytes = jnp.dtype(out_dtype).itemsize + (0 if add is None else add.dtype.itemsize)
    tm, tn, tk = _matmul_tiles(m_dim, n_dim, k_dim, a.dtype.itemsize, b.dtype.itemsize, out_bytes)
    nk = k_dim // tk
    dims = (((0,) if ta else (1,), (1,) if tb else (0,)), ((), ()))

    def body(*refs):
        if add is None:
            a_ref, b_ref, o_ref = refs[:3]
            add_ref = None
        else:
            a_ref, b_ref, add_ref, o_ref = refs[:4]
        part = lax.dot_general(a_ref[...].astype(BF16), b_ref[...].astype(BF16), dims,
                               preferred_element_type=F32)
        if nk == 1:
            o_ref[...] = (part if add_ref is None else part + add_ref[...].astype(F32)).astype(o_ref.dtype)
            return
        acc_ref = refs[-1]
        k = pl.program_id(2)

        @pl.when(k == 0)
        def _():
            acc_ref[...] = part if add_ref is None else part + add_ref[...].astype(F32)

        @pl.when(k > 0)
        def _():
            acc_ref[...] += part

        @pl.when(k == nk - 1)
        def _():
            o_ref[...] = acc_ref[...].astype(o_ref.dtype)

    a_spec = pl.BlockSpec((tk, tm), lambda i, j, k: (k, i)) if ta else pl.BlockSpec((tm, tk), lambda i, j, k: (i, k))
    b_spec = pl.BlockSpec((tn, tk), lambda i, j, k: (j, k)) if tb else pl.BlockSpec((tk, tn), lambda i, j, k: (k, j))
    o_spec = pl.BlockSpec((tm, tn), lambda i, j, k: (i, j))
    in_specs = [a_spec, b_spec] + ([o_spec] if add is not None else [])
    args = (a, b) + ((add,) if add is not None else ())
    return pl.pallas_call(
        body, name=name, grid=(m_dim // tm, n_dim // tn, nk),
        in_specs=in_specs, out_specs=o_spec,
        out_shape=jax.ShapeDtypeStruct((m_dim, n_dim), out_dtype),
        scratch_shapes=[pltpu.VMEM((tm, tn), F32)] if nk > 1 else [],
        compiler_params=_params(("parallel", "parallel", "arbitrary")),
    )(*args)


MATMUL_VMEM_BUDGET = 40 * 1024 * 1024


def _matmul_tiles(m_dim, n_dim, k_dim, a_bytes, b_bytes, out_bytes):
    best = None
    for tk in (k_dim, 4096, 2048, 1024, 512, 256, 128):
        if tk > k_dim or k_dim % tk:
            continue
        for tm in (1024, 512, 256, 128):
            if m_dim % tm:
                continue
            for tn in (2048, 1024, 512, 256, 128):
                if n_dim % tn:
                    continue
                vmem = 2 * (tm * tk * a_bytes + tk * tn * b_bytes + tm * tn * out_bytes) + 2 * tm * tn * 4
                if a_bytes == 4:
                    vmem += tm * tk * 2
                if vmem > MATMUL_VMEM_BUDGET:
                    continue
                score = (tm * tn * tk, tk, min(tm, tn))
                if best is None or score > best[0]:
                    best = (score, (tm, tn, tk))
    return best[1]


def _iota(shape, dim):
    return lax.broadcasted_iota(jnp.int32, shape, dim)


def _col(x, j):
    return jnp.sum(jnp.where(_iota(x.shape, 1) == j, x, 0.0), axis=1, keepdims=True)


def _silu(x):
    return x * jax.nn.sigmoid(x)


def _dsilu(x):
    s = jax.nn.sigmoid(x)
    return s * (1.0 + x * (1.0 - s))


def _chunks(width):
    return width // 128


def _scatter_rows(src_ref, nat_ref, dil, tm):
    n = tm // dil
    for cb in range(nat_ref.shape[0]):
        for r in range(dil):
            nat_ref[cb, pl.ds(r, n, stride=dil), :] = src_ref[r, :, cb * 128:(cb + 1) * 128].astype(F32)


def _gather_rows(nat_ref, dst_ref, dil, tm):
    n = tm // dil
    for cb in range(nat_ref.shape[0]):
        for r in range(dil):
            dst_ref[r, :, cb * 128:(cb + 1) * 128] = nat_ref[cb, pl.ds(r, n, stride=dil), :].astype(dst_ref.dtype)


def _load_chunks(nat_ref):
    return jnp.concatenate([nat_ref[cb] for cb in range(nat_ref.shape[0])], axis=1)


def _store_chunks(nat_ref, val):
    for cb in range(nat_ref.shape[0]):
        nat_ref[cb] = val[:, cb * 128:(cb + 1) * 128]


def _perm_spec(tm, dil, width):
    if dil == 1:
        return pl.BlockSpec((tm, width), lambda i: (i, 0))
    return pl.BlockSpec((dil, tm // dil, width), lambda i: (0, i, 0))


def _perm_shape(t, dil, width, dtype):
    return jax.ShapeDtypeStruct((t, width) if dil == 1 else (dil, t // dil, width), dtype)


def _rmsnorm_fwd(x, g, name, dils=(1,)):
    t, d = x.shape
    tm = _pick(t, (512, 256, 128))

    permuted = any(dil > 1 for dil in dils)

    def body(x_ref, g_ref, *refs):
        h_refs = refs[:len(dils)]
        xv = x_ref[...]
        rs = lax.rsqrt(jnp.mean(xv * xv, axis=-1, keepdims=True) + EPS)
        hv = xv * rs * g_ref[...]
        if permuted:
            _store_chunks(refs[-1], hv)
        for dil, h_ref in zip(dils, h_refs):
            if dil == 1:
                h_ref[...] = hv.astype(BF16)
            else:
                _gather_rows(refs[-1], h_ref, dil, tm)

    outs = pl.pallas_call(
        body, name=name, grid=(t // tm,),
        in_specs=[pl.BlockSpec((tm, d), lambda i: (i, 0)), pl.BlockSpec((1, d), lambda i: (0, 0))],
        out_specs=[_perm_spec(tm, dil, d) for dil in dils],
        out_shape=[_perm_shape(t, dil, d, BF16) for dil in dils],
        scratch_shapes=[pltpu.VMEM((_chunks(d), tm, 128), F32)] if permuted else [],
        compiler_params=_params(("parallel",)),
    )(x, g.reshape(1, d))
    return [o.reshape(t, d) for o in outs]


def _rmsnorm_bwd(x, g, dhs, dres, name, dils=(1,)):
    t, d = x.shape
    tm = _pick(t, (512, 256, 128) if len(dils) == 1 else (256, 128))
    n_in = len(dils)

    def body(*refs):
        x_ref, g_ref = refs[:2]
        dh_refs = refs[2:2 + n_in]
        dres_ref = refs[2 + n_in] if dres is not None else None
        dx_ref, dxb_ref, dg_ref = refs[-4:-1]
        nat_ref = refs[-1]
        dhv = None
        for dil, dh_ref in zip(dils, dh_refs):
            if dil == 1:
                term = dh_ref[...].astype(F32)
            else:
                _scatter_rows(dh_ref, nat_ref, dil, tm)
                term = _load_chunks(nat_ref)
            dhv = term if dhv is None else dhv + term
        xv = x_ref[...]
        r = lax.rsqrt(jnp.mean(xv * xv, axis=-1, keepdims=True) + EPS)
        xhat = xv * r
        dxh = dhv * g_ref[...]
        dx = r * (dxh - xhat * jnp.mean(dxh * xhat, axis=-1, keepdims=True))
        if dres_ref is not None:
            dx = dx + dres_ref[...]
        dx_ref[...] = dx
        dxb_ref[...] = dx.astype(BF16)
        part = jnp.sum(dhv * xhat, axis=0, keepdims=True)

        @pl.when(pl.program_id(0) == 0)
        def _():
            dg_ref[...] = part

        @pl.when(pl.program_id(0) > 0)
        def _():
            dg_ref[...] += part

    row = pl.BlockSpec((tm, d), lambda i: (i, 0))
    vec = pl.BlockSpec((1, d), lambda i: (0, 0))
    in_specs = [row, vec] + [_perm_spec(tm, dil, d) for dil in dils] + ([row] if dres is not None else [])
    dh_args = [dh if dil == 1 else dh.reshape(dil, t // dil, d) for dil, dh in zip(dils, dhs)]
    args = (x, g.reshape(1, d), *dh_args) + ((dres,) if dres is not None else ())
    return pl.pallas_call(
        body, name=name, grid=(t // tm,), in_specs=in_specs, out_specs=[row, row, vec],
        out_shape=[jax.ShapeDtypeStruct((t, d), F32), jax.ShapeDtypeStruct((t, d), BF16),
                   jax.ShapeDtypeStruct((1, d), F32)],
        scratch_shapes=[pltpu.VMEM((_chunks(d), tm, 128), F32)],
        compiler_params=_params(("arbitrary",)),
    )(*args)


def _final_loss(x, g, target, name="final_loss"):
    t, d = x.shape
    tm = _pick(t, (512, 256, 128))

    def body(x_ref, g_ref, t_ref, loss_ref, dx_ref, dxb_ref, dg_ref):
        xv = x_ref[...]
        gv = g_ref[...]
        r = lax.rsqrt(jnp.mean(xv * xv, axis=-1, keepdims=True) + EPS)
        xhat = xv * r
        e = xhat * gv - t_ref[...]
        lpart = jnp.zeros((1, 128), F32) + (0.5 / d) * jnp.sum(e * e)
        dy = e * (1.0 / d)
        dxh = dy * gv
        dx = r * (dxh - xhat * jnp.mean(dxh * xhat, axis=-1, keepdims=True))
        dx_ref[...] = dx
        dxb_ref[...] = dx.astype(BF16)
        gpart = jnp.sum(dy * xhat, axis=0, keepdims=True)

        @pl.when(pl.program_id(0) == 0)
        def _():
            dg_ref[...] = gpart
            loss_ref[...] = lpart

        @pl.when(pl.program_id(0) > 0)
        def _():
            dg_ref[...] += gpart
            loss_ref[...] += lpart

    row = pl.BlockSpec((tm, d), lambda i: (i, 0))
    vec = pl.BlockSpec((1, d), lambda i: (0, 0))
    return pl.pallas_call(
        body, name=name, grid=(t // tm,), in_specs=[row, vec, row],
        out_specs=[pl.BlockSpec((1, 128), lambda i: (0, 0)), row, row, vec],
        out_shape=[jax.ShapeDtypeStruct((1, 128), F32), jax.ShapeDtypeStruct((t, d), F32),
                   jax.ShapeDtypeStruct((t, d), BF16), jax.ShapeDtypeStruct((1, d), F32)],
        compiler_params=_params(("arbitrary",)),
    )(x, g.reshape(1, d), target)


CONV_K = 4
HALO = 8


def _shift_down(cur, prev8, s):
    rolled = pltpu.roll(cur, s, 0)
    fix = pltpu.roll(prev8, s, 0)
    head = jnp.where(_iota((HALO, cur.shape[1]), 0) < s, fix, rolled[:HALO])
    return jnp.concatenate([head, rolled[HALO:]], axis=0)


def _shift_up(cur, next8, s):
    n = cur.shape[0]
    rolled = pltpu.roll(cur, n - s, 0)
    fix = pltpu.roll(next8, HALO - s, 0)
    tail = jnp.where(_iota((HALO, cur.shape[1]), 0) >= HALO - s, fix, rolled[n - HALO:])
    return jnp.concatenate([rolled[:n - HALO], tail], axis=0)


def _conv_pre(u_ref, up_ref, w_ref, b_ref, first):
    cur = u_ref[...]
    prev8 = jnp.where(first, 0.0, up_ref[...])
    w = w_ref[...]
    shifted = [cur] + [_shift_down(cur, prev8, s) for s in (1, 2, 3)]
    pre = b_ref[...] + sum(w[CONV_K - 1 - s:CONV_K - s, :] * shifted[s] for s in range(CONV_K))
    return pre, shifted


def _conv_specs(tm, tc):
    nb = tm // HALO
    cur = pl.BlockSpec((tm, tc), lambda j, i: (i, j))
    prev = pl.BlockSpec((HALO, tc), lambda j, i: (jnp.maximum(i * nb - 1, 0), j))
    wspec = pl.BlockSpec((CONV_K, tc), lambda j, i: (0, j))
    bspec = pl.BlockSpec((1, tc), lambda j, i: (0, j))
    return cur, prev, wspec, bspec


def _conv_fwd(u, w, b, name):
    t, c = u.shape
    tm, tc = _pick(t, (512, 256, 128)), _pick(c, (1024, 512, 256, 128))
    cur, prev, wspec, bspec = _conv_specs(tm, tc)

    def body(u_ref, up_ref, w_ref, b_ref, o_ref):
        pre, _ = _conv_pre(u_ref, up_ref, w_ref, b_ref, pl.program_id(1) == 0)
        o_ref[...] = _silu(pre)

    return pl.pallas_call(
        body, name=name, grid=(c // tc, t // tm), in_specs=[cur, prev, wspec, bspec], out_specs=cur,
        out_shape=jax.ShapeDtypeStruct((t, c), F32),
        compiler_params=_params(("parallel", "parallel")),
    )(u, u, w, b.reshape(1, c))


def _conv_bwd_pre(u, w, b, dy, name):
    t, c = u.shape
    tm, tc = _pick(t, (512, 256, 128)), _pick(c, (1024, 512, 256, 128))
    cur, prev, wspec, bspec = _conv_specs(tm, tc)

    def body(u_ref, up_ref, w_ref, b_ref, dy_ref, dpre_ref, dw_ref, db_ref):
        i = pl.program_id(1)
        pre, shifted = _conv_pre(u_ref, up_ref, w_ref, b_ref, i == 0)
        dpre = dy_ref[...] * _dsilu(pre)
        dpre_ref[...] = dpre
        dw = jnp.concatenate([jnp.sum(dpre * shifted[CONV_K - 1 - k], axis=0, keepdims=True) for k in range(CONV_K)], axis=0)
        db = jnp.sum(dpre, axis=0, keepdims=True)

        @pl.when(i == 0)
        def _():
            dw_ref[...] = dw
            db_ref[...] = db

        @pl.when(i > 0)
        def _():
            dw_ref[...] += dw
            db_ref[...] += db

    return pl.pallas_call(
        body, name=name, grid=(c // tc, t // tm), in_specs=[cur, prev, wspec, bspec, cur],
        out_specs=[cur, wspec, bspec],
        out_shape=[jax.ShapeDtypeStruct((t, c), F32), jax.ShapeDtypeStruct((CONV_K, c), F32),
                   jax.ShapeDtypeStruct((1, c), F32)],
        compiler_params=_params(("parallel", "arbitrary")),
    )(u, u, w, b.reshape(1, c), dy)


def _conv_bwd_in(dpre, w, name):
    t, c = dpre.shape
    tm, tc = _pick(t, (512, 256, 128)), _pick(c, (1024, 512, 256, 128))
    nb = tm // HALO
    last = t // tm - 1
    cur = pl.BlockSpec((tm, tc), lambda j, i: (i, j))
    nxt = pl.BlockSpec((HALO, tc), lambda j, i: (jnp.minimum((i + 1) * nb, t // HALO - 1), j))
    wspec = pl.BlockSpec((CONV_K, tc), lambda j, i: (0, j))

    def body(d_ref, dn_ref, w_ref, o_ref):
        cur_v = d_ref[...]
        next8 = jnp.where(pl.program_id(1) == last, 0.0, dn_ref[...])
        wv = w_ref[...]
        acc = wv[CONV_K - 1:CONV_K, :] * cur_v
        for s in (1, 2, 3):
            acc = acc + wv[CONV_K - 1 - s:CONV_K - s, :] * _shift_up(cur_v, next8, s)
        o_ref[...] = acc.astype(o_ref.dtype)

    return pl.pallas_call(
        body, name=name, grid=(c // tc, t // tm), in_specs=[cur, nxt, wspec], out_specs=cur,
        out_shape=jax.ShapeDtypeStruct((t, c), BF16),
        compiler_params=_params(("parallel", "parallel")),
    )(dpre, dpre, w)


MEM_HEADS = 4
NT_DIMS = (((1,), (1,)), ((), ()))
TN_DIMS = (((0,), (0,)), ((), ()))


def _dot(a, b, dims=None):
    if dims is None:
        return jnp.dot(a, b, preferred_element_type=F32)
    return lax.dot_general(a, b, dims, preferred_element_type=F32)


def _memattn_probs(q, mk, scale):
    s = _dot(q, mk, NT_DIMS) * scale
    s = s - jnp.max(s, axis=-1, keepdims=True)
    p = jnp.exp(s)
    return p / jnp.sum(p, axis=-1, keepdims=True)


def _memattn_fwd(q, mkv, name):
    t, wd = q.shape
    m = mkv.shape[0]
    hd = wd // MEM_HEADS
    scale = hd ** -0.5
    tm = _pick(t, (512, 256, 128))

    def body(q_ref, mkv_ref, o_ref):
        for h in range(MEM_HEADS):
            cols = slice(h * hd, (h + 1) * hd)
            p = _memattn_probs(q_ref[:, cols], mkv_ref[:, cols], scale)
            o_ref[:, cols] = _dot(p.astype(BF16), mkv_ref[:, wd + h * hd:wd + (h + 1) * hd])

    return pl.pallas_call(
        body, name=name, grid=(t // tm,),
        in_specs=[pl.BlockSpec((tm, wd), lambda i: (i, 0)), pl.BlockSpec((m, 2 * wd), lambda i: (0, 0))],
        out_specs=pl.BlockSpec((tm, wd), lambda i: (i, 0)),
        out_shape=jax.ShapeDtypeStruct((t, wd), F32),
        compiler_params=_params(("parallel",)),
    )(q, mkv)


def _memattn_bwd(q, mkv, dy, name):
    t, wd = q.shape
    m = mkv.shape[0]
    hd = wd // MEM_HEADS
    scale = hd ** -0.5
    tm = _pick(t, (512, 256, 128))

    def body(q_ref, mkv_ref, dy_ref, dq_ref, dmkv_ref):
        i = pl.program_id(0)

        @pl.when(i == 0)
        def _():
            dmkv_ref[...] = jnp.zeros_like(dmkv_ref)

        for h in range(MEM_HEADS):
            cols = slice(h * hd, (h + 1) * hd)
            vcols = slice(wd + h * hd, wd + (h + 1) * hd)
            qh = q_ref[:, cols]
            p = _memattn_probs(qh, mkv_ref[:, cols], scale)
            dyh = dy_ref[:, cols].astype(BF16)
            dp = _dot(dyh, mkv_ref[:, vcols], NT_DIMS)
            ds = (p * (dp - jnp.sum(dp * p, axis=-1, keepdims=True)) * scale).astype(BF16)
            dq_ref[:, cols] = _dot(ds, mkv_ref[:, cols]).astype(dq_ref.dtype)
            dmkv_ref[:, cols] += _dot(ds, qh, TN_DIMS)
            dmkv_ref[:, vcols] += _dot(p.astype(BF16), dyh, TN_DIMS)

    return pl.pallas_call(
        body, name=name, grid=(t // tm,),
        in_specs=[pl.BlockSpec((tm, wd), lambda i: (i, 0)), pl.BlockSpec((m, 2 * wd), lambda i: (0, 0)),
                  pl.BlockSpec((tm, wd), lambda i: (i, 0))],
        out_specs=[pl.BlockSpec((tm, wd), lambda i: (i, 0)), pl.BlockSpec((m, 2 * wd), lambda i: (0, 0))],
        out_shape=[jax.ShapeDtypeStruct((t, wd), BF16), jax.ShapeDtypeStruct((m, 2 * wd), F32)],
        compiler_params=_params(("arbitrary",)),
    )(q, mkv, dy)


NORM_GROUPS = 8


def _gate_fwd(y_tok, y_mem, z, norm_g, name):
    t, tok = y_tok.shape
    mem = y_mem.shape[1]
    mix = tok + mem
    gw = tok // NORM_GROUPS
    tm = _pick(t, (256, 128))

    def body(*refs):
        if norm_g is None:
            yt_ref, ym_ref, z_ref, o_ref = refs
        else:
            yt_ref, ym_ref, z_ref, g_ref, o_ref = refs
        u = yt_ref[...] * _silu(z_ref[:, :tok])
        if norm_g is None:
            o_ref[:, :tok] = u.astype(o_ref.dtype)
        else:
            for k in range(NORM_GROUPS):
                uk = u[:, k * gw:(k + 1) * gw]
                r = lax.rsqrt(jnp.mean(uk * uk, axis=-1, keepdims=True) + EPS)
                o_ref[:, k * gw:(k + 1) * gw] = (uk * r * g_ref[:, k * gw:(k + 1) * gw]).astype(o_ref.dtype)
        o_ref[:, tok:] = (ym_ref[...] * _silu(z_ref[:, tok:])).astype(o_ref.dtype)

    in_specs = [pl.BlockSpec((tm, tok), lambda i: (i, 0)), pl.BlockSpec((tm, mem), lambda i: (i, 0)),
                pl.BlockSpec((tm, mix), lambda i: (i, 0))]
    args = [y_tok, y_mem, z]
    if norm_g is not None:
        in_specs.append(pl.BlockSpec((1, tok), lambda i: (0, 0)))
        args.append(norm_g.reshape(1, tok))
    return pl.pallas_call(
        body, name=name, grid=(t // tm,), in_specs=in_specs,
        out_specs=pl.BlockSpec((tm, mix), lambda i: (i, 0)),
        out_shape=jax.ShapeDtypeStruct((t, mix), BF16),
        compiler_params=_params(("parallel",)),
    )(*args)


def _gate_bwd(y_tok, y_mem, z, norm_g, dgated, name):
    t, tok = y_tok.shape
    mem = y_mem.shape[1]
    mix = tok + mem
    gw = tok // NORM_GROUPS
    tm = _pick(t, (256, 128))

    def body(*refs):
        if norm_g is None:
            yt_ref, ym_ref, z_ref, dg_ref, dyt_ref, dym_ref, dz_ref, dn_ref = refs
        else:
            yt_ref, ym_ref, z_ref, dg_ref, g_ref, dyt_ref, dym_ref, dz_ref, dn_ref = refs
        i = pl.program_id(0)
        zt = z_ref[:, :tok]
        yt = yt_ref[...]
        sz = _silu(zt)
        dout = dg_ref[:, :tok].astype(F32)
        if norm_g is None:
            du = dout
            dn = jnp.zeros((1, tok), F32)
        else:
            u = yt * sz
            dus, dns = [], []
            for k in range(NORM_GROUPS):
                uk = u[:, k * gw:(k + 1) * gw]
                r = lax.rsqrt(jnp.mean(uk * uk, axis=-1, keepdims=True) + EPS)
                nk = uk * r
                dk = dout[:, k * gw:(k + 1) * gw]
                dns.append(jnp.sum(dk * nk, axis=0, keepdims=True))
                dnk = dk * g_ref[:, k * gw:(k + 1) * gw]
                dus.append(r * (dnk - nk * jnp.mean(dnk * nk, axis=-1, keepdims=True)))
            du = jnp.concatenate(dus, axis=1)
            dn = jnp.concatenate(dns, axis=1)
        dyt_ref[...] = du * sz
        dz_ref[:, :tok] = (du * yt * _dsilu(zt)).astype(dz_ref.dtype)
        zm = z_ref[:, tok:]
        dm = dg_ref[:, tok:].astype(F32)
        dym_ref[...] = dm * _silu(zm)
        dz_ref[:, tok:] = (dm * ym_ref[...] * _dsilu(zm)).astype(dz_ref.dtype)

        @pl.when(i == 0)
        def _():
            dn_ref[...] = dn

        @pl.when(i > 0)
        def _():
            dn_ref[...] += dn

    tok_spec = pl.BlockSpec((tm, tok), lambda i: (i, 0))
    mem_spec = pl.BlockSpec((tm, mem), lambda i: (i, 0))
    mix_spec = pl.BlockSpec((tm, mix), lambda i: (i, 0))
    vec = pl.BlockSpec((1, tok), lambda i: (0, 0))
    in_specs = [tok_spec, mem_spec, mix_spec, mix_spec]
    args = [y_tok, y_mem, z, dgated]
    if norm_g is not None:
        in_specs.append(vec)
        args.append(norm_g.reshape(1, tok))
    return pl.pallas_call(
        body, name=name, grid=(t // tm,), in_specs=in_specs,
        out_specs=[tok_spec, mem_spec, mix_spec, vec],
        out_shape=[jax.ShapeDtypeStruct((t, tok), F32), jax.ShapeDtypeStruct((t, mem), F32),
                   jax.ShapeDtypeStruct((t, mix), BF16), jax.ShapeDtypeStruct((1, tok), F32)],
        compiler_params=_params(("arbitrary",)),
    )(*args)


SSD_Q = 128
SSD_N = 128
SSD_P = 64
SSD_G = 8
SSD_HPG = 6
SSD_H = SSD_G * SSD_HPG
SSD_TOK = SSD_H * SSD_P
SSD_XBC = SSD_TOK + 2 * SSD_G * SSD_N
LANES = 128
HIGHEST = lax.Precision.HIGHEST


def _softplus(x):
    return jnp.maximum(x, 0.0) + jnp.log(1.0 + jnp.exp(-jnp.abs(x)))


def _ssd_common(dtr_ref, bias_ref, alog_ref):
    sq = (SSD_Q, LANES)
    pre = dtr_ref[...] + bias_ref[...]
    dt = _softplus(pre)
    a = -jnp.exp(alog_ref[...])
    tril = (_iota(sq, 0) >= _iota(sq, 1)).astype(F32)
    acs = jnp.dot(tril, dt * a, precision=HIGHEST, preferred_element_type=F32)
    return pre, dt, a, tril, acs, acs.T


def _pair_terms(dt, acs, acs_t, h0):
    hi = _iota((SSD_Q, LANES), 1) >= SSD_P
    heads = []
    for j in range(2):
        h = h0 + j
        a_col = _col(acs, h)
        a_row = acs_t[h:h + 1, :]
        a_last = _col(acs[SSD_Q - 1:SSD_Q, :], h)
        heads.append((h, a_col, a_row, a_last, hi if j else jnp.logical_not(hi)))
    dtl = jnp.where(hi, _col(dt, h0 + 1), _col(dt, h0))
    scale = jnp.where(hi, jnp.exp(heads[1][1]), jnp.exp(heads[0][1]))
    dec_last = jnp.where(hi[:1], jnp.exp(heads[1][3]), jnp.exp(heads[0][3]))
    return heads, dtl, scale, dec_last


def _decay(a_col, a_row):
    causal = _iota((SSD_Q, SSD_Q), 0) >= _iota((SSD_Q, SSD_Q), 1)
    return jnp.where(causal, jnp.exp(jnp.minimum(a_col - a_row, 0.0)), 0.0)


def _ssd_fwd(xbc, dt_raw, dt_bias, a_log, dskip_lane, name):
    t = xbc.shape[0]
    nc = t // SSD_Q

    def body(xbc_ref, dtr_ref, bias_ref, alog_ref, dsk_ref, y_ref, hs_ref, h_ref):
        @pl.when(pl.program_id(0) == 0)
        def _():
            h_ref[...] = jnp.zeros_like(h_ref)

        _, dt, _, _, acs, acs_t = _ssd_common(dtr_ref, bias_ref, alog_ref)
        for g in range(SSD_G):
            bg_f = xbc_ref[:, SSD_TOK + g * SSD_N:SSD_TOK + (g + 1) * SSD_N]
            bg = bg_f.astype(BF16)
            cg = xbc_ref[:, SSD_TOK + SSD_G * SSD_N + g * SSD_N:SSD_TOK + SSD_G * SSD_N + (g + 1) * SSD_N].astype(BF16)
            cb = _dot(cg, bg, NT_DIMS)
            for pr in range(SSD_HPG // 2):
                h0 = g * SSD_HPG + 2 * pr
                lanes = slice(h0 * SSD_P, (h0 + 2) * SSD_P)
                heads, dtl, scale, dec_last = _pair_terms(dt, acs, acs_t, h0)
                xs = xbc_ref[:, lanes]
                xdt = xs * dtl
                hp = h_ref[:, lanes]
                hs_ref[:, lanes] = hp
                y = _dot(cg, hp.astype(BF16)) * scale + dsk_ref[:, lanes] * xs
                snew = hp * dec_last
                for _, a_col, a_row, a_last, mask in heads:
                    xm = jnp.where(mask, xdt, 0.0).astype(BF16)
                    y = y + _dot((cb * _decay(a_col, a_row)).astype(BF16), xm)
                    bw = (bg_f * jnp.exp(a_last - a_col)).astype(BF16)
                    snew = snew + _dot(bw, xm, TN_DIMS)
                y_ref[:, lanes] = y
                h_ref[:, lanes] = snew

    row = lambda w: pl.BlockSpec((SSD_Q, w), lambda c: (c, 0))
    vec = lambda w: pl.BlockSpec((1, w), lambda c: (0, 0))
    return pl.pallas_call(
        body, name=name, grid=(nc,),
        in_specs=[row(SSD_XBC), row(LANES), vec(LANES), vec(LANES), vec(SSD_TOK)],
        out_specs=[row(SSD_TOK), row(SSD_TOK)],
        out_shape=[jax.ShapeDtypeStruct((t, SSD_TOK), F32), jax.ShapeDtypeStruct((nc * SSD_N, SSD_TOK), F32)],
        scratch_shapes=[pltpu.VMEM((SSD_N, SSD_TOK), F32)],
        compiler_params=_params(("arbitrary",)),
    )(xbc, dt_raw, dt_bias, a_log, dskip_lane)


def _ssd_bwd(xbc, dt_raw, dt_bias, a_log, dskip_lane, hs, dy, name):
    t = xbc.shape[0]
    nc = t // SSD_Q
    sq = (SSD_Q, LANES)

    def body(xbc_ref, dtr_ref, bias_ref, alog_ref, dsk_ref, hs_ref, dy_ref,
             dxbc_ref, ddtr_ref, dbias_ref, dalog_ref, ddsk_ref, dh_ref):
        first = pl.program_id(0) == 0

        @pl.when(first)
        def _():
            dh_ref[...] = jnp.zeros_like(dh_ref)
            dbias_ref[...] = jnp.zeros_like(dbias_ref)
            dalog_ref[...] = jnp.zeros_like(dalog_ref)
            ddsk_ref[...] = jnp.zeros_like(ddsk_ref)

        pre, dt, a, tril, acs, acs_t = _ssd_common(dtr_ref, bias_ref, alog_ref)
        lane = _iota(sq, 1)
        sub = _iota(sq, 0)
        causal = sub >= lane
        d_acs = jnp.zeros(sq, F32)
        d_acs_row = jnp.zeros(sq, F32)
        d_last = jnp.zeros((1, LANES), F32)
        ddt = jnp.zeros(sq, F32)
        for g in range(SSD_G):
            bcols = slice(SSD_TOK + g * SSD_N, SSD_TOK + (g + 1) * SSD_N)
            ccols = slice(SSD_TOK + SSD_G * SSD_N + g * SSD_N, SSD_TOK + SSD_G * SSD_N + (g + 1) * SSD_N)
            bg_f = xbc_ref[:, bcols]
            bg = bg_f.astype(BF16)
            cg = xbc_ref[:, ccols].astype(BF16)
            cb = _dot(cg, bg, NT_DIMS)
            dcb = jnp.zeros(sq, F32)
            dbg = jnp.zeros(sq, F32)
            dcg = jnp.zeros(sq, F32)
            for pr in range(SSD_HPG // 2):
                h0 = g * SSD_HPG + 2 * pr
                lanes = slice(h0 * SSD_P, (h0 + 2) * SSD_P)
                heads, dtl, scale, dec_last = _pair_terms(dt, acs, acs_t, h0)
                xs = xbc_ref[:, lanes]
                xdt = xs * dtl
                dyv = dy_ref[:, lanes]
                hp = hs_ref[:, lanes]
                dhn = dh_ref[:, lanes]
                hp_b = hp.astype(BF16)
                dys = (dyv * scale).astype(BF16)
                yoff_dy = dyv * _dot(cg, hp_b) * scale
                dcg = dcg + _dot(dys, hp_b, NT_DIMS)
                dhc = _dot(cg, dys, TN_DIMS)
                hh = dhn * hp
                dxdt = jnp.zeros(sq, F32)
                for h, a_col, a_row, a_last, mask in heads:
                    dec = _decay(a_col, a_row)
                    m = cb * dec
                    dym = jnp.where(mask, dyv, 0.0).astype(BF16)
                    xm = jnp.where(mask, xdt, 0.0).astype(BF16)
                    dhm = jnp.where(mask, dhn, 0.0).astype(BF16)
                    w = jnp.exp(a_last - a_col)
                    dxdt = dxdt + _dot(m.astype(BF16), dym, TN_DIMS) + _dot((bg_f * w).astype(BF16), dhm)
                    dm = jnp.where(causal, _dot(dym, xm, NT_DIMS), 0.0)
                    dcb = dcb + dm * dec
                    e = dm * m
                    gj = _dot(xm, dhm, NT_DIMS)
                    dbg = dbg + w * gj
                    wdw = w * jnp.sum(bg_f * gj, axis=1, keepdims=True)
                    col = (jnp.sum(e, axis=1, keepdims=True)
                           + jnp.sum(jnp.where(mask, yoff_dy, 0.0), axis=1, keepdims=True) - wdw)
                    d_acs = d_acs + jnp.where(lane == h, col, 0.0)
                    d_acs_row = d_acs_row + jnp.where(sub == h, jnp.sum(e, axis=0, keepdims=True), 0.0)
                    last = jnp.sum(wdw) + jnp.exp(a_last) * jnp.sum(jnp.where(mask, hh, 0.0))
                    d_last = d_last + jnp.where(lane[:1] == h, last, 0.0)
                dxbc_ref[:, lanes] = dxdt * dtl + dsk_ref[:, lanes] * dyv
                tt = dxdt * xs
                for h, _, _, _, mask in heads:
                    ddt = ddt + jnp.where(lane == h, jnp.sum(jnp.where(mask, tt, 0.0), axis=1, keepdims=True), 0.0)
                ddsk_ref[:, lanes] += jnp.sum(dyv * xs, axis=0, keepdims=True)
                dh_ref[:, lanes] = dhn * dec_last + dhc
            dcb_b = dcb.astype(BF16)
            dxbc_ref[:, bcols] = dbg + _dot(dcb_b, cg, TN_DIMS)
            dxbc_ref[:, ccols] = dcg + _dot(dcb_b, bg)
        d_tot = d_acs - d_acs_row.T + jnp.where(sub == SSD_Q - 1, d_last, 0.0)
        ddta = lax.dot_general(tril, d_tot, TN_DIMS, precision=HIGHEST, preferred_element_type=F32)
        ddt = ddt + ddta * a
        dalog_ref[...] += jnp.sum(ddta * dt, axis=0, keepdims=True) * a
        ddtr = ddt * jax.nn.sigmoid(pre)
        ddtr_ref[...] = ddtr
        dbias_ref[...] += jnp.sum(ddtr, axis=0, keepdims=True)

    rev = lambda w: pl.BlockSpec((SSD_Q, w), lambda i: (nc - 1 - i, 0))
    vec = lambda w: pl.BlockSpec((1, w), lambda i: (0, 0))
    return pl.pallas_call(
        body, name=name, grid=(nc,),
        in_specs=[rev(SSD_XBC), rev(LANES), vec(LANES), vec(LANES), vec(SSD_TOK), rev(SSD_TOK), rev(SSD_TOK)],
        out_specs=[rev(SSD_XBC), rev(LANES), vec(LANES), vec(LANES), vec(SSD_TOK)],
        out_shape=[jax.ShapeDtypeStruct((t, SSD_XBC), F32), jax.ShapeDtypeStruct((t, LANES), F32),
                   jax.ShapeDtypeStruct((1, LANES), F32), jax.ShapeDtypeStruct((1, LANES), F32),
                   jax.ShapeDtypeStruct((1, SSD_TOK), F32)],
        scratch_shapes=[pltpu.VMEM((SSD_N, SSD_TOK), F32)],
        compiler_params=_params(("arbitrary",)),
    )(xbc, dt_raw, dt_bias, a_log, dskip_lane, hs, dy)


ATT_E = 128
ATT_H = 24
ATT_W = 128
ATT_TOK = ATT_H * ATT_E
DILATED_GROUPS = ((128, 1), (512, 4), (2048, 16))
N_DIL = len(DILATED_GROUPS)
ALIBI_MAX_EXP = 8.0
MASKED = -1e30


def _alibi_slopes(group):
    n = N_DIL * ATT_H
    return [2.0 ** (-ALIBI_MAX_EXP * (group * ATT_H + h + 1) / n) for h in range(ATT_H)]


def _att_scores(qh, kk, rel, valid, slope_d):
    s = _dot(qh, kk, NT_DIMS) * (ATT_E ** -0.5) - slope_d * rel
    return jnp.where(valid, s, MASKED)


def _att_rel(j):
    shp = (ATT_W, 2 * ATT_W)
    kpos = _iota(shp, 1)
    rel = _iota(shp, 0) + ATT_W - kpos
    valid = (rel >= 0) & (rel <= ATT_W) & ((kpos >= ATT_W) | (j > 0))
    return rel.astype(F32), valid


def _dil_fwd(q, k, v, group, name):
    t = q.shape[0]
    dil = DILATED_GROUPS[group][1]
    slopes = _alibi_slopes(group)
    nb = t // dil // ATT_W

    def body(q_ref, kp_ref, kc_ref, vp_ref, vc_ref, o_ref, lse_ref):
        rel, valid = _att_rel(pl.program_id(1))
        lane = _iota((ATT_W, LANES), 1)
        lse_all = jnp.zeros((ATT_W, LANES), F32)
        for h in range(ATT_H):
            cols = slice(h * ATT_E, (h + 1) * ATT_E)
            kk = jnp.concatenate([kp_ref[:, cols], kc_ref[:, cols]], axis=0)
            vv = jnp.concatenate([vp_ref[:, cols], vc_ref[:, cols]], axis=0)
            s = _att_scores(q_ref[:, cols], kk, rel, valid, slopes[h] * dil)
            m = jnp.max(s, axis=-1, keepdims=True)
            p = jnp.exp(s - m)
            den = jnp.sum(p, axis=-1, keepdims=True)
            o_ref[:, cols] = _dot(p.astype(BF16), vv) / den
            lse_all = jnp.where(lane == h, m + jnp.log(den), lse_all)
        lse_ref[...] = lse_all

    cur = pl.BlockSpec((ATT_W, ATT_TOK), lambda r, j: (r * nb + j, 0))
    prev = pl.BlockSpec((ATT_W, ATT_TOK), lambda r, j: (r * nb + jnp.maximum(j - 1, 0), 0))
    small = pl.BlockSpec((ATT_W, LANES), lambda r, j: (r * nb + j, 0))
    return pl.pallas_call(
        body, name=name, grid=(dil, nb),
        in_specs=[cur, prev, cur, prev, cur], out_specs=[cur, small],
        out_shape=[jax.ShapeDtypeStruct((t, ATT_TOK), F32), jax.ShapeDtypeStruct((t, LANES), F32)],
        compiler_params=_params(("parallel", "parallel")),
    )(q, k, k, v, v)


def _dil_bwd(q, k, v, do, cterm, lse, group, name):
    t = q.shape[0]
    dil = DILATED_GROUPS[group][1]
    slopes = _alibi_slopes(group)
    nb = t // dil // ATT_W

    def body(q_ref, kp_ref, kc_ref, vp_ref, vc_ref, do_ref, c_ref, lse_ref,
             dq_ref, dk_ref, dv_ref, ck_ref, cv_ref):
        j = pl.program_id(1)

        @pl.when(j == 0)
        def _():
            ck_ref[...] = jnp.zeros_like(ck_ref)
            cv_ref[...] = jnp.zeros_like(cv_ref)

        @pl.when(j < nb)
        def _():
            rel, valid = _att_rel(j)
            cv_, lv = c_ref[...], lse_ref[...]
            for h in range(ATT_H):
                cols = slice(h * ATT_E, (h + 1) * ATT_E)
                qh = q_ref[:, cols]
                kk = jnp.concatenate([kp_ref[:, cols], kc_ref[:, cols]], axis=0)
                vv = jnp.concatenate([vp_ref[:, cols], vc_ref[:, cols]], axis=0)
                s = _att_scores(qh, kk, rel, valid, slopes[h] * dil)
                p = jnp.where(valid, jnp.exp(s - _col(lv, h)), 0.0)
                do = do_ref[:, cols]
                dp = _dot(do, vv, NT_DIMS)
                ds = (p * (dp + _col(cv_, h)) * (ATT_E ** -0.5)).astype(BF16)
                dq_ref[:, cols] = _dot(ds, kk).astype(dq_ref.dtype)
                dkk = _dot(ds, qh, TN_DIMS)
                dvv = _dot(p.astype(BF16), do, TN_DIMS)
                dk_ref[:, cols] = (ck_ref[:, cols] + dkk[:ATT_W]).astype(dk_ref.dtype)
                dv_ref[:, cols] = (cv_ref[:, cols] + dvv[:ATT_W]).astype(dv_ref.dtype)
                ck_ref[:, cols] = dkk[ATT_W:]
                cv_ref[:, cols] = dvv[ATT_W:]

        @pl.when(j == nb)
        def _():
            dk_ref[...] = ck_ref[...].astype(dk_ref.dtype)
            dv_ref[...] = cv_ref[...].astype(dv_ref.dtype)

    jq = lambda j: jnp.minimum(j, nb - 1)
    cur = pl.BlockSpec((ATT_W, ATT_TOK), lambda r, j: (r * nb + jq(j), 0))
    prev = pl.BlockSpec((ATT_W, ATT_TOK), lambda r, j: (r * nb + jnp.maximum(jq(j) - 1, 0), 0))
    small = pl.BlockSpec((ATT_W, LANES), lambda r, j: (r * nb + jq(j), 0))
    late = pl.BlockSpec((ATT_W, ATT_TOK), lambda r, j: (r * nb + jnp.maximum(j - 1, 0), 0))
    big = jax.ShapeDtypeStruct((t, ATT_TOK), BF16)
    return pl.pallas_call(
        body, name=name, grid=(dil, nb + 1),
        in_specs=[cur, prev, cur, prev, cur, cur, small, small], out_specs=[cur, late, late],
        out_shape=[big, big, big],
        scratch_shapes=[pltpu.VMEM((ATT_W, ATT_TOK), F32), pltpu.VMEM((ATT_W, ATT_TOK), F32)],
        compiler_params=_params(("parallel", "arbitrary")),
    )(q, k, k, v, v, do, cterm, lse)


def _combine_weights(lses):
    m = functools.reduce(jnp.maximum, lses)
    es = [jnp.exp(l - m) for l in lses]
    tot = functools.reduce(lambda a, b: a + b, es)
    return [e / tot for e in es]


DILS = tuple(d for _, d in DILATED_GROUPS)
COMBINE_ROWS = 256


def _by_residue(arr, dil):
    return arr if dil == 1 else arr.reshape(dil, arr.shape[0] // dil, arr.shape[1])


def _natural_lses(l_refs, small_refs, tm):
    vals = []
    for g, dil in enumerate(DILS):
        if dil == 1:
            vals.append(l_refs[g][...])
        else:
            _scatter_rows(l_refs[g], small_refs[g], dil, tm)
            vals.append(small_refs[g][0])
    return vals


def _combine_scratch(tm):
    return ([pltpu.VMEM((_chunks(ATT_TOK), tm, 128), F32)] * N_DIL + [pltpu.VMEM((1, tm, 128), F32)] * N_DIL)


def _combine_fwd(outs, lses, name):
    t = outs[0].shape[0]
    tm = COMBINE_ROWS

    def body(*refs):
        o_refs, l_refs, y_ref = refs[:N_DIL], refs[N_DIL:2 * N_DIL], refs[2 * N_DIL]
        big_refs, small_refs = refs[2 * N_DIL + 1:3 * N_DIL + 1], refs[3 * N_DIL + 1:]
        ws = _combine_weights(_natural_lses(l_refs, small_refs, tm))
        for g in range(1, N_DIL):
            _scatter_rows(o_refs[g], big_refs[g], DILS[g], tm)
        for h in range(ATT_H):
            cols = slice(h * ATT_E, (h + 1) * ATT_E)
            y_ref[:, cols] = (_col(ws[0], h) * o_refs[0][:, cols]
                              + sum(_col(ws[g], h) * big_refs[g][h] for g in range(1, N_DIL)))

    return pl.pallas_call(
        body, name=name, grid=(t // tm,),
        in_specs=[_perm_spec(tm, d, ATT_TOK) for d in DILS] + [_perm_spec(tm, d, LANES) for d in DILS],
        out_specs=pl.BlockSpec((tm, ATT_TOK), lambda i: (i, 0)),
        out_shape=jax.ShapeDtypeStruct((t, ATT_TOK), F32),
        scratch_shapes=_combine_scratch(tm),
        compiler_params=_params(("parallel",)),
    )(*[_by_residue(o, d) for o, d in zip(outs, DILS)], *[_by_residue(l, d) for l, d in zip(lses, DILS)])


def _combine_bwd(outs, lses, dy, name):
    t = outs[0].shape[0]
    tm = COMBINE_ROWS

    def body(*refs):
        o_refs, l_refs, dy_ref = refs[:N_DIL], refs[N_DIL:2 * N_DIL], refs[2 * N_DIL]
        do_refs, c_refs = refs[2 * N_DIL + 1:3 * N_DIL + 1], refs[3 * N_DIL + 1:4 * N_DIL + 1]
        big_refs, small_refs = refs[4 * N_DIL + 1:5 * N_DIL + 1], refs[5 * N_DIL + 1:]
        ws = _combine_weights(_natural_lses(l_refs, small_refs, tm))
        for g in range(1, N_DIL):
            _scatter_rows(o_refs[g], big_refs[g], DILS[g], tm)
        lane = _iota((tm, LANES), 1)
        sdw = jnp.zeros((tm, LANES), F32)
        for h in range(ATT_H):
            cols = slice(h * ATT_E, (h + 1) * ATT_E)
            dyh = dy_ref[:, cols]
            tot = _col(ws[0], h) * jnp.sum(dyh * o_refs[0][:, cols], axis=1, keepdims=True)
            for g in range(1, N_DIL):
                tot = tot + _col(ws[g], h) * jnp.sum(dyh * big_refs[g][h], axis=1, keepdims=True)
            sdw = jnp.where(lane == h, tot, sdw)
        for g, dil in enumerate(DILS):
            cterm = -ws[g] * sdw
            if dil == 1:
                c_refs[g][...] = cterm
            else:
                small_refs[g][0] = cterm
                _gather_rows(small_refs[g], c_refs[g], dil, tm)
            for h in range(ATT_H):
                cols = slice(h * ATT_E, (h + 1) * ATT_E)
                do = _col(ws[g], h) * dy_ref[:, cols]
                if dil == 1:
                    do_refs[g][:, cols] = do.astype(BF16)
                else:
                    big_refs[g][h] = do
            if dil > 1:
                _gather_rows(big_refs[g], do_refs[g], dil, tm)

    res = pl.pallas_call(
        body, name=name, grid=(t // tm,),
        in_specs=([_perm_spec(tm, d, ATT_TOK) for d in DILS] + [_perm_spec(tm, d, LANES) for d in DILS]
                  + [pl.BlockSpec((tm, ATT_TOK), lambda i: (i, 0))]),
        out_specs=[_perm_spec(tm, d, ATT_TOK) for d in DILS] + [_perm_spec(tm, d, LANES) for d in DILS],
        out_shape=[_perm_shape(t, d, ATT_TOK, BF16) for d in DILS] + [_perm_shape(t, d, LANES, F32) for d in DILS],
        scratch_shapes=_combine_scratch(tm),
        compiler_params=_params(("parallel",)),
    )(*[_by_residue(o, d) for o, d in zip(outs, DILS)], *[_by_residue(l, d) for l, d in zip(lses, DILS)], dy)
    return [a.reshape(t, ATT_TOK) for a in res[:N_DIL]], [a.reshape(t, LANES) for a in res[N_DIL:]]


ADAM_LR, ADAM_B1, ADAM_B2, ADAM_EPS, ADAM_WD, ADAM_STEP = 0.001, 0.9, 0.999, 1e-08, 0.01, 10


def _adamw(parts, w, m, v, name):
    r, c = w.shape
    n_parts = parts.shape[0]
    tc = _pick(c, (1024, 512, 256, 128)) if c % 128 == 0 else c
    tm = _pick(r, (128, 64, 32, 16, 8))

    def body(p_ref, w_ref, m_ref, v_ref, g_ref, d_ref, nm_ref, nv_ref):
        g = p_ref[0].astype(F32)
        for k in range(1, n_parts):
            g = g + p_ref[k].astype(F32)
        nm = ADAM_B1 * m_ref[...] + (1.0 - ADAM_B1) * g
        nv = ADAM_B2 * v_ref[...] + (1.0 - ADAM_B2) * (g * g)
        m_hat = nm / (1.0 - ADAM_B1 ** ADAM_STEP)
        v_hat = nv / (1.0 - ADAM_B2 ** ADAM_STEP)
        g_ref[...] = g
        d_ref[...] = -ADAM_LR * (m_hat / (jnp.sqrt(v_hat) + ADAM_EPS) + ADAM_WD * w_ref[...])
        nm_ref[...] = nm
        nv_ref[...] = nv

    blk = pl.BlockSpec((tm, tc), lambda i, j: (i, j))
    pblk = pl.BlockSpec((n_parts, tm, tc), lambda i, j: (0, i, j))
    return pl.pallas_call(
        body, name=name, grid=(r // tm, c // tc), in_specs=[pblk, blk, blk, blk], out_specs=[blk] * 4,
        out_shape=[jax.ShapeDtypeStruct((r, c), F32)] * 4,
        compiler_params=_params(("parallel", "parallel")),
    )(parts, w, m, v)


N_CHIP = 4
MESH_ID = pl.DeviceIdType.MESH


def _other_chips(x, y):
    return [(1 - x, y), (x, 1 - y), (1 - x, 1 - y)]


GATHER_SEMS = N_DEV - 1


def _gather_copies(in_refs, out_refs, send_sems, recv_sems, local_sems, x, y, c):
    n = len(in_refs)
    sibling = (x, y, 1 - c)
    chips = _other_chips(x, y)

    def copy(a, k, block, to, src=None):
        rows = out_refs[a].at[4 * block[0] + 2 * block[1] + block[2]]
        return pltpu.make_async_remote_copy(
            src_ref=rows if src is None else src, dst_ref=rows,
            send_sem=send_sems.at[a * GATHER_SEMS + k], recv_sem=recv_sems.at[a * GATHER_SEMS + k],
            device_id=to, device_id_type=MESH_ID)

    started = []
    for a in range(n):
        local = pltpu.make_async_copy(in_refs[a], out_refs[a].at[4 * x + 2 * y + c], local_sems.at[a])
        local.start()
        started.append(local)
    sends = []
    for j, chip in enumerate(chips):
        for a in range(n):
            sends.append(copy(a, 1 + j, (x, y, c), (*chip, c), src=in_refs[a]))
            sends[-1].start()
    for a in range(n):
        sends.append(copy(a, 0, (x, y, c), sibling, src=in_refs[a]))
        sends[-1].start()
    for j, chip in enumerate(chips):
        for a in range(n):
            copy(a, 1 + j, (*chip, c), (x, y, c)).wait_recv()
            sends.append(copy(a, 4 + j, (*chip, c), sibling))
            sends[-1].start()
    for a in range(n):
        copy(a, 0, sibling, (x, y, c)).wait_recv()
        for j, chip in enumerate(chips):
            copy(a, 4 + j, (*chip, 1 - c), (x, y, c)).wait_recv()
    for cp in sends:
        cp.wait_send()
    for local in started:
        local.wait()


def _gather_two_level(arrays, name):
    n = len(arrays)
    per = GATHER_SEMS

    def body(*refs):
        x, y, c = lax.axis_index("x"), lax.axis_index("y"), lax.axis_index("c")
        _gather_copies(refs[:n], refs[n:2 * n], *refs[2 * n:], x, y, c)

    any_spec = pl.BlockSpec(memory_space=pl.ANY)
    return pl.pallas_call(
        body, name=name, in_specs=[any_spec] * n, out_specs=[any_spec] * n,
        out_shape=[jax.ShapeDtypeStruct((N_DEV,) + a.shape, a.dtype) for a in arrays],
        scratch_shapes=[pltpu.SemaphoreType.DMA((n * per,)), pltpu.SemaphoreType.DMA((n * per,)),
                        pltpu.SemaphoreType.DMA((n,))],
        compiler_params=pltpu.CompilerParams(has_side_effects=True),
    )(*arrays)


def _handshake(barrier, peers):
    for peer in peers:
        pl.semaphore_signal(barrier, inc=1, device_id=peer, device_id_type=MESH_ID)
    pl.semaphore_wait(barrier, len(peers))


def _on_sequencer(name, collective_id, arrays, out_structs, sem_counts, peers, copies):
    hbm = pltpu.MemorySpace.HBM
    in_refs = [jax.new_ref(a, memory_space=hbm) for a in arrays]
    out_refs = [jax.empty_ref(s, memory_space=hbm) for s in out_structs]

    @pl.kernel(mesh=plsc.ScalarSubcoreMesh(axis_name="seq", num_cores=1), name=name,
               scratch_types=tuple(pltpu.SemaphoreType.DMA((k,)) for k in sem_counts),
               compiler_params=pltpu.CompilerParams(collective_id=collective_id))
    def launch(*sems):
        x, y, c = lax.axis_index("x"), lax.axis_index("y"), lax.axis_index("c")
        _handshake(pltpu.get_barrier_semaphore(), peers(x, y, c))
        copies(in_refs, out_refs, *sems, x, y, c)

    launch()
    return [o[...] for o in out_refs]


def _all_others(x, y, c):
    return [(x ^ ((k >> 2) & 1), y ^ ((k >> 1) & 1), c ^ (k & 1)) for k in range(1, N_DEV)]


def _sc_gather(arrays, name, collective_id):
    n = len(arrays)
    outs = [jax.ShapeDtypeStruct((N_DEV,) + a.shape, a.dtype) for a in arrays]
    return _on_sequencer(name, collective_id, arrays, outs, (n * GATHER_SEMS, n * GATHER_SEMS, n), _all_others,
                         _gather_copies)


def _sc_chip_exchange(sums, name, collective_id):
    n = len(sums)
    per = N_CHIP - 1
    outs = [jax.ShapeDtypeStruct(s.shape, s.dtype) for s in sums]
    return _on_sequencer(name, collective_id, sums, outs, (n * per, n * per, n),
                         lambda x, y, c: [(px, py, c) for px, py in _other_chips(x, y)], _chip_exchange_copies)


def _sibling_swap_copies(in_refs, out_refs, send_sems, recv_sems, x, y, c):
    sends = []
    for a in range(len(in_refs)):
        for q in range(N_CHIP):
            cp = pltpu.make_async_remote_copy(
                src_ref=in_refs[a].at[2 * q + 1 - c], dst_ref=out_refs[a].at[q],
                send_sem=send_sems.at[a * N_CHIP + q], recv_sem=recv_sems.at[a * N_CHIP + q],
                device_id=(x, y, 1 - c), device_id_type=MESH_ID)
            cp.start()
            sends.append(cp)
    for cp in sends:
        cp.wait_recv()
    for cp in sends:
        cp.wait_send()


def _sc_sibling_swap(parts, name, collective_id):
    n = len(parts)
    outs = [jax.ShapeDtypeStruct((N_CHIP,) + p.shape[1:], p.dtype) for p in parts]
    return _on_sequencer(name, collective_id, parts, outs, (n * N_CHIP, n * N_CHIP),
                         lambda x, y, c: [(x, y, 1 - c)], _sibling_swap_copies)


def _chip_sum(part, landed, core, name):
    _, r, c = part.shape
    tc = _pick(c, (1024, 512, 256, 128)) if c % 128 == 0 else c
    tm = _pick(r, (256, 128, 64, 32, 16, 8))

    def body(core_ref, p_ref, l_ref, o_ref):
        o_ref[...] = (p_ref[...].astype(F32) + l_ref[...].astype(F32)).astype(o_ref.dtype)

    grid_spec = pltpu.PrefetchScalarGridSpec(
        num_scalar_prefetch=1, grid=(N_CHIP, r // tm, c // tc),
        in_specs=[pl.BlockSpec((None, tm, tc), lambda q, i, j, core_ref: (2 * q + core_ref[0], i, j)),
                  pl.BlockSpec((None, tm, tc), lambda q, i, j, core_ref: (q, i, j))],
        out_specs=pl.BlockSpec((None, tm, tc), lambda q, i, j, core_ref: (q, i, j)))
    return pl.pallas_call(
        body, name=name, grid_spec=grid_spec, out_shape=jax.ShapeDtypeStruct(landed.shape, landed.dtype),
        compiler_params=_params(("parallel", "parallel", "parallel")),
    )(core, part, landed)


def _chip_exchange_copies(in_refs, out_refs, send_sems, recv_sems, local_sems, x, y, c):
    n = len(in_refs)
    per = N_CHIP - 1
    mine = 2 * x + y
    started = []
    for a in range(n):
        local = pltpu.make_async_copy(in_refs[a].at[mine], out_refs[a].at[mine], local_sems.at[a])
        local.start()
        started.append(local)
    sends = []
    for j, (px, py) in enumerate(_other_chips(x, y)):
        for a in range(n):
            cp = pltpu.make_async_remote_copy(
                src_ref=in_refs[a].at[2 * px + py], dst_ref=out_refs[a].at[mine],
                send_sem=send_sems.at[a * per + j], recv_sem=recv_sems.at[a * per + j],
                device_id=(px, py, c), device_id_type=MESH_ID)
            cp.start()
            sends.append((cp, a, j, 2 * px + py))
    for cp, a, j, peer in sends:
        pltpu.make_async_remote_copy(
            src_ref=out_refs[a].at[peer], dst_ref=out_refs[a].at[peer],
            send_sem=send_sems.at[a * per + j], recv_sem=recv_sems.at[a * per + j],
            device_id=(x, y, c), device_id_type=MESH_ID).wait_recv()
    for cp, _, _, _ in sends:
        cp.wait_send()
    for local in started:
        local.wait()


DEPTH = 4
MEM_W = 1024
MIX_W = SSD_TOK + MEM_W
DT_PAD = LANES - SSD_H


def _is_ssd(i):
    return i % 2 == 0


def _weight_names():
    names = ["mem_norm_g", "final_norm_g"]
    for i in range(DEPTH):
        names += [f"norm_g_{i}", f"w_in_{i}"]
        if _is_ssd(i):
            names += [f"conv_w_{i}", f"conv_b_{i}", f"dt_bias_{i}", f"a_log_{i}", f"d_skip_{i}", f"ssd_norm_g_{i}"]
        names += [f"w_mem_kv_{i}", f"w_out_{i}"]
    return names


WEIGHTS = _weight_names()
INPUTS = ["x", "mem"] + WEIGHTS + ["loss_target"] + ["m_" + n for n in WEIGHTS] + ["v_" + n for n in WEIGHTS]


def _in_segments(i):
    if _is_ssd(i):
        return [("xbc", 0, SSD_XBC), ("dt", SSD_XBC, SSD_H), ("qm", SSD_XBC + SSD_H, MEM_W),
                ("z", SSD_XBC + SSD_H + MEM_W, MIX_W)]
    segs = []
    for g in range(N_DIL):
        for j, nm in enumerate("qkv"):
            segs.append((f"{nm}{g}", (3 * g + j) * ATT_TOK, ATT_TOK))
    segs += [("qm", 3 * N_DIL * ATT_TOK, MEM_W), ("z", 3 * N_DIL * ATT_TOK + MEM_W, MIX_W)]
    return segs


def _split_w_in(i, w_in):
    out = {}
    for nm, start, width in _in_segments(i):
        seg = w_in[:, start:start + width]
        out[nm] = jnp.pad(seg, ((0, 0), (0, DT_PAD))) if nm == "dt" else seg
    return out


def _join_dw_in(i, dws):
    return jnp.concatenate([dws[nm][:, :width] for nm, _, width in _in_segments(i)], axis=1)


SEG_DTYPE = {"xbc": F32, "dt": F32, "z": F32}


def _layer_dils(i):
    return (1,) if _is_ssd(i) else DILS


def _seg_order(nm):
    return int(nm[1]) if nm[0] in "qkv" and nm[1:].isdigit() else 0


def _layer_fwd(i, x, mem_b, p):
    tag = f"l{i}"
    hs = _rmsnorm_fwd(x, p["norm_g"], tag + "_norm", _layer_dils(i))
    proj = {nm: _matmul(hs[_seg_order(nm)], w, out_dtype=SEG_DTYPE.get(nm, BF16), name=f"{tag}_in_{nm}")
            for nm, w in p["win"].items()}
    sv = {"x": x, "h": hs, "proj": proj}
    if _is_ssd(i):
        xbc = _conv_fwd(proj["xbc"], p["conv_w"], p["conv_b"], tag + "_conv")
        y_tok, hs = _ssd_fwd(xbc, proj["dt"], p["dt_bias_p"], p["a_log_p"], p["dskip_lane"], tag + "_ssd")
        sv.update(xbc=xbc, hs=hs)
    else:
        outs, lses = [], []
        for g in range(N_DIL):
            o, lse = _dil_fwd(proj[f"q{g}"], proj[f"k{g}"], proj[f"v{g}"], g, f"{tag}_att{g}")
            outs.append(o)
            lses.append(lse)
        y_tok = _combine_fwd(outs, lses, tag + "_comb")
        sv.update(outs=outs, lses=lses)
    mkv = _matmul(mem_b, p["wmkv"], out_dtype=BF16, name=tag + "_mkv")
    y_mem = _memattn_fwd(proj["qm"], mkv, tag + "_mem")
    gated = _gate_fwd(y_tok, y_mem, proj["z"], p.get("ssd_norm_g"), tag + "_gate")
    x_out = _matmul(gated, p["wout"], out_dtype=F32, add=x, name=tag + "_out")
    sv.update(y_tok=y_tok, y_mem=y_mem, mkv=mkv, gated=gated)
    return x_out, sv


def _layer_bwd(i, sv, dx_out, dxb_out, dmem_n, mem_b, p):
    tag = f"l{i}b"
    proj = sv["proj"]
    gr = {}
    dgated = _matmul(dxb_out, p["wout"], tb=True, out_dtype=F32, name=tag + "_dgated")
    gr["w_out"] = _matmul(sv["gated"], dxb_out, ta=True, out_dtype=BF16, name=tag + "_dwout")
    dy_tok, dy_mem, dz, dssd_g = _gate_bwd(sv["y_tok"], sv["y_mem"], proj["z"], p.get("ssd_norm_g"), dgated, tag + "_gate")
    dq_mem, dmkv = _memattn_bwd(proj["qm"], sv["mkv"], dy_mem, tag + "_mem")
    gr["w_mem_kv"] = _matmul(mem_b, dmkv, ta=True, out_dtype=BF16, name=tag + "_dwmkv")
    dmem_n = _matmul(dmkv, p["wmkv"], tb=True, out_dtype=F32, add=dmem_n, name=tag + "_dmem")
    dproj = {"qm": dq_mem, "z": dz}
    if _is_ssd(i):
        dxbc, ddt_raw, dbias, dalog, ddsk = _ssd_bwd(sv["xbc"], proj["dt"], p["dt_bias_p"], p["a_log_p"], p["dskip_lane"],
                                                     sv["hs"], dy_tok, tag + "_ssd")
        dpre, dconv_w, dconv_b = _conv_bwd_pre(proj["xbc"], p["conv_w"], p["conv_b"], dxbc, tag + "_convpre")
        dproj["xbc"] = _conv_bwd_in(dpre, p["conv_w"], tag + "_convin")
        dproj["dt"] = ddt_raw
        gr.update(conv_w=dconv_w, conv_b=dconv_b[0], dt_bias=dbias[0, :SSD_H], a_log=dalog[0, :SSD_H],
                  d_skip=jnp.sum(ddsk.reshape(SSD_H, SSD_P), axis=1), ssd_norm_g=dssd_g[0])
    else:
        dos, cs = _combine_bwd(sv["outs"], sv["lses"], dy_tok, tag + "_comb")
        for g in range(N_DIL):
            dq, dk, dv = _dil_bwd(proj[f"q{g}"], proj[f"k{g}"], proj[f"v{g}"], dos[g], cs[g], sv["lses"][g], g,
                                  f"{tag}_att{g}")
            dproj.update({f"q{g}": dq, f"k{g}": dk, f"v{g}": dv})
    dils = _layer_dils(i)
    dhs = [None] * len(dils)
    dws = {}
    for nm, w in p["win"].items():
        o = _seg_order(nm)
        dhs[o] = _matmul(dproj[nm], w, tb=True, out_dtype=F32, add=dhs[o], name=f"{tag}_dh_{nm}")
        dws[nm] = _matmul(sv["h"][o], dproj[nm], ta=True, out_dtype=BF16, name=f"{tag}_dw_{nm}")
    gr["w_in"] = _join_dw_in(i, dws)
    dx, dxb, dnorm_g = _rmsnorm_bwd(sv["x"], p["norm_g"], dhs, dx_out, tag + "_norm", dils)
    gr["norm_g"] = dnorm_g[0]
    return dx, dxb, dmem_n, gr


def _pad_heads(v):
    return jnp.pad(v.reshape(1, SSD_H), ((0, 0), (0, DT_PAD)))


def _layer_params(i, small, w_in, w_mem_kv, w_out):
    p = {"norm_g": small[f"norm_g_{i}"], "win": _split_w_in(i, w_in), "wmkv": w_mem_kv, "wout": w_out}
    if _is_ssd(i):
        p.update(conv_w=small[f"conv_w_{i}"], conv_b=small[f"conv_b_{i}"], ssd_norm_g=small[f"ssd_norm_g_{i}"],
                 dt_bias_p=_pad_heads(small[f"dt_bias_{i}"]), a_log_p=_pad_heads(small[f"a_log_{i}"]),
                 dskip_lane=jnp.repeat(small[f"d_skip_{i}"], SSD_P).reshape(1, SSD_TOK))
    return p


def _local_step(x, mem, target, small, weights, emit):
    mem_b = _rmsnorm_fwd(mem, small["mem_norm_g"], "mem_norm")[0]
    params, saved = [], []
    for i in range(DEPTH):
        big, x = weights(i, x)
        params.append(_layer_params(i, small, *big))
        x, sv = _layer_fwd(i, x, mem_b, params[i])
        saved.append(sv)
    loss, dx, dxb, dfinal = _final_loss(x, small["final_norm_g"], target)
    grads = {"final_norm_g": dfinal[0]}
    dmem_n = None
    for i in reversed(range(DEPTH)):
        dx, dxb, dmem_n, gr = _layer_bwd(i, saved[i], dx, dxb, dmem_n, mem_b, params[i])
        emit(i, {nm: gr.pop(nm) for nm in BIG})
        grads.update({f"{nm}_{i}": g for nm, g in gr.items()})
    _, _, dmem_g = _rmsnorm_bwd(mem, small["mem_norm_g"], [dmem_n], None, "mem_norm_b")
    grads["mem_norm_g"] = dmem_g[0]
    return loss[0, 0], dx, grads


BIG = ("w_in", "w_mem_kv", "w_out")
SMALL = [n for n in WEIGHTS if not n.startswith(BIG)]
PACK_ROWS = 8 * LANES
GATHER_COLLECTIVE_ID = 0
SCATTER_COLLECTIVE_ID = 4
SWAP_COLLECTIVE_ID = 8


def _pack(vals):
    flat = jnp.concatenate([v.reshape(-1).astype(F32) for v in vals])
    padded = -(-flat.shape[0] // PACK_ROWS) * PACK_ROWS
    return jnp.pad(flat, (0, padded - flat.shape[0])).reshape(padded // LANES, LANES)


def _train_step(a, local_step):
    x, y, c = lax.axis_index("x"), lax.axis_index("y"), lax.axis_index("c")
    me = 4 * x + 2 * y + c
    gathered_w = {}
    conv_names = [n for n in SMALL if n.startswith("conv_w")]
    conv_full = _gather_two_level([a[n] for n in conv_names], "gather_conv")

    def gather(i, after):
        shards = [a[f"{nm}_{i}"].astype(BF16) for nm in BIG]
        if after:
            shards = lax.optimization_barrier((*shards, *after))[:len(shards)]
        if i == 0:
            gathered_w[i] = _gather_two_level(shards, f"gather_w{i}")
        else:
            gathered_w[i] = _sc_gather(shards, f"gather_w{i}", GATHER_COLLECTIVE_ID + i)

    def weights(i, act):
        if i == 0:
            gather(0, ())
            gather(1, (gathered_w[0][0], conv_full[0]))
        elif i == 1:
            gather(2, (act,))
            gather(3, (act,))
        *raw, act = lax.optimization_barrier((*gathered_w.pop(i), act))
        g_in, g_kv, g_out = raw
        whole = (jnp.transpose(g_in, (1, 0, 2)).reshape(g_in.shape[1], N_DEV * g_in.shape[2]),
                 g_kv.reshape(N_DEV * g_kv.shape[1], g_kv.shape[2]),
                 g_out.reshape(N_DEV * g_out.shape[1], g_out.shape[2]))
        return whole, act

    small = {n: a[n] for n in SMALL}
    for n, gathered in zip(conv_names, conv_full):
        small[n] = jnp.transpose(gathered, (1, 0, 2)).reshape(gathered.shape[1], N_DEV * gathered.shape[2])

    core = c.astype(jnp.int32).reshape(1)
    landed = {}

    def reduce_scatter(i, gr):
        d, cols = gr["w_in"].shape
        parts = [jnp.transpose(gr["w_in"].reshape(d, N_DEV, cols // N_DEV), (1, 0, 2))]
        for nm in BIG[1:]:
            parts.append(gr[nm].reshape(N_DEV, gr[nm].shape[0] // N_DEV, gr[nm].shape[1]))
        swapped = _sc_sibling_swap(parts, f"swap_w{i}", SWAP_COLLECTIVE_ID + i)
        sums = [_chip_sum(p, s, core, f"chipsum_{nm}_{i}") for nm, p, s in zip(BIG, parts, swapped)]
        landed[i] = _sc_chip_exchange(sums, f"scatter_w{i}", SCATTER_COLLECTIVE_ID + i)

    loss_local, grad_x, grads = local_step(a["x"][0], a["mem"][0], a["loss_target"][0], small, weights, reduce_scatter)
    loss = lax.psum(loss_local, ("x", "y", "c"))

    res = {}
    for i in reversed(range(DEPTH)):
        for nm, p in zip(BIG, landed[i]):
            n = f"{nm}_{i}"
            res[n] = _adamw(p, a[n], a["m_" + n], a["v_" + n], "adamw_" + n)

    gathered = _gather_two_level([_pack([grads[n] for n in SMALL])], "gather_small")[0]
    zero_conv = lambda pre: [jnp.zeros(small[n].shape, F32) if n in conv_names else a[pre + n] for n in SMALL]
    packed = _adamw(gathered, _pack(zero_conv("")), _pack(zero_conv("m_")), _pack(zero_conv("v_")), "adamw_small")
    off = 0
    for n in SMALL:
        size = math.prod(small[n].shape)
        if n in conv_names:
            rows, cols = a[n].shape
            whole = gathered.reshape(N_DEV, -1)[:, off:off + size].reshape(N_DEV, rows, N_DEV * cols)
            mine = lax.dynamic_slice_in_dim(whole, me * cols, cols, axis=2)
            res[n] = _adamw(mine, a[n], a["m_" + n], a["v_" + n], "adamw_" + n)
        else:
            res[n] = [o.reshape(-1)[off:off + size].reshape(a[n].shape) for o in packed]
        off += size
    outs = [loss, grad_x[None]]
    for k in range(4):
        outs += [res[n][k] for n in WEIGHTS]
    return tuple(outs)


def kernel(x, mem, mem_norm_g, final_norm_g, norm_g_0, w_in_0, conv_w_0, conv_b_0, dt_bias_0, a_log_0, d_skip_0, ssd_norm_g_0, w_mem_kv_0, w_out_0, norm_g_1, w_in_1, w_mem_kv_1, w_out_1, norm_g_2, w_in_2, conv_w_2, conv_b_2, dt_bias_2, a_log_2, d_skip_2, ssd_norm_g_2, w_mem_kv_2, w_out_2, norm_g_3, w_in_3, w_mem_kv_3, w_out_3, loss_target, m_mem_norm_g, m_final_norm_g, m_norm_g_0, m_w_in_0, m_conv_w_0, m_conv_b_0, m_dt_bias_0, m_a_log_0, m_d_skip_0, m_ssd_norm_g_0, m_w_mem_kv_0, m_w_out_0, m_norm_g_1, m_w_in_1, m_w_mem_kv_1, m_w_out_1, m_norm_g_2, m_w_in_2, m_conv_w_2, m_conv_b_2, m_dt_bias_2, m_a_log_2, m_d_skip_2, m_ssd_norm_g_2, m_w_mem_kv_2, m_w_out_2, m_norm_g_3, m_w_in_3, m_w_mem_kv_3, m_w_out_3, v_mem_norm_g, v_final_norm_g, v_norm_g_0, v_w_in_0, v_conv_w_0, v_conv_b_0, v_dt_bias_0, v_a_log_0, v_d_skip_0, v_ssd_norm_g_0, v_w_mem_kv_0, v_w_out_0, v_norm_g_1, v_w_in_1, v_w_mem_kv_1, v_w_out_1, v_norm_g_2, v_w_in_2, v_conv_w_2, v_conv_b_2, v_dt_bias_2, v_a_log_2, v_d_skip_2, v_ssd_norm_g_2, v_w_mem_kv_2, v_w_out_2, v_norm_g_3, v_w_in_3, v_w_mem_kv_3, v_w_out_3):
    vals = locals()
    return _train_step({n: vals[n] for n in INPUTS}, _local_step)
```

```python
import functools
import math

import jax
import jax.numpy as jnp
from jax import lax
from jax.experimental import pallas as pl
from jax.experimental.pallas import tpu as pltpu
from jax.experimental.pallas import tpu_sc as plsc

F32 = jnp.float32
BF16 = jnp.bfloat16
EPS = 1e-6
N_DEV = 8
VMEM_LIMIT_BYTES = 56 * 1024 * 1024


def _pick(n, prefs):
    for p in prefs:
        if n % p == 0:
            return p
    return n


def _params(sem):
    return pltpu.CompilerParams(dimension_semantics=sem, vmem_limit_bytes=VMEM_LIMIT_BYTES)


def _matmul(a, b, *, ta=False, tb=False, out_dtype=F32, add=None, name="mm"):
    if ta:
        k_dim, m_dim = a.shape
    else:
        m_dim, k_dim = a.shape
    n_dim = b.shape[0] if tb else b.shape[1]
    out_bytes = jnp.dtype(out_dtype).itemsize + (0 if add is None else add.dtype.itemsize)
    tm, tn, tk = _matmul_tiles(m_dim, n_dim, k_dim, a.dtype.itemsize, b.dtype.itemsize, out_bytes)
    nk = k_dim // tk
    dims = (((0,) if ta else (1,), (1,) if tb else (0,)), ((), ()))

    def body(*refs):
        if add is None:
            a_ref, b_ref, o_ref = refs[:3]
            add_ref = None
        else:
            a_ref, b_ref, add_ref, o_ref = refs[:4]
        part = lax.dot_general(a_ref[...].astype(BF16), b_ref[...].astype(BF16), dims,
                               preferred_element_type=F32)
        if nk == 1:
            o_ref[...] = (part if add_ref is None else part + add_ref[...].astype(F32)).astype(o_ref.dtype)
            return
        acc_ref = refs[-1]
        k = pl.program_id(2)

        @pl.when(k == 0)
        def _():
            acc_ref[...] = part if add_ref is None else part + add_ref[...].astype(F32)

        @pl.when(k > 0)
        def _():
            acc_ref[...] += part

        @pl.when(k == nk - 1)
        def _():
            o_ref[...] = acc_ref[...].astype(o_ref.dtype)

    a_spec = pl.BlockSpec((tk, tm), lambda i, j, k: (k, i)) if ta else pl.BlockSpec((tm, tk), lambda i, j, k: (i, k))
    b_spec = pl.BlockSpec((tn, tk), lambda i, j, k: (j, k)) if tb else pl.BlockSpec((tk, tn), lambda i, j, k: (k, j))
    o_spec = pl.BlockSpec((tm, tn), lambda i, j, k: (i, j))
    in_specs = [a_spec, b_spec] + ([o_spec] if add is not None else [])
    args = (a, b) + ((add,) if add is not None else ())
    return pl.pallas_call(
        body, name=name, grid=(m_dim // tm, n_dim // tn, nk),
        in_specs=in_specs, out_specs=o_spec,
        out_shape=jax.ShapeDtypeStruct((m_dim, n_dim), out_dtype),
        scratch_shapes=[pltpu.VMEM((tm, tn), F32)] if nk > 1 else [],
        compiler_params=_params(("parallel", "parallel", "arbitrary")),
    )(*args)


MATMUL_VMEM_BUDGET = 40 * 1024 * 1024


def _matmul_tiles(m_dim, n_dim, k_dim, a_bytes, b_bytes, out_bytes):
    best = None
    for tk in (k_dim, 4096, 2048, 1024, 512, 256, 128):
        if tk > k_dim or k_dim % tk:
            continue
        for tm in (1024, 512, 256, 128):
            if m_dim % tm:
                continue
            for tn in (2048, 1024, 512, 256, 128):
                if n_dim % tn:
                    continue
                vmem = 2 * (tm * tk * a_bytes + tk * tn * b_bytes + tm * tn * out_bytes) + 2 * tm * tn * 4
                if a_bytes == 4:
                    vmem += tm * tk * 2
                if vmem > MATMUL_VMEM_BUDGET:
                    continue
                score = (tm * tn * tk, tk, min(tm, tn))
                if best is None or score > best[0]:
                    best = (score, (tm, tn, tk))
    return best[1]


def _iota(shape, dim):
    return lax.broadcasted_iota(jnp.int32, shape, dim)


def _col(x, j):
    return jnp.sum(jnp.where(_iota(x.shape, 1) == j, x, 0.0), axis=1, keepdims=True)


def _silu(x):
    return x * jax.nn.sigmoid(x)


def _dsilu(x):
    s = jax.nn.sigmoid(x)
    return s * (1.0 + x * (1.0 - s))


def _chunks(width):
    return width // 128


def _scatter_rows(src_ref, nat_ref, dil, tm):
    n = tm // dil
    for cb in range(nat_ref.shape[0]):
        for r in range(dil):
            nat_ref[cb, pl.ds(r, n, stride=dil), :] = src_ref[r, :, cb * 128:(cb + 1) * 128].astype(F32)


def _gather_rows(nat_ref, dst_ref, dil, tm):
    n = tm // dil
    for cb in range(nat_ref.shape[0]):
        for r in range(dil):
            dst_ref[r, :, cb * 128:(cb + 1) * 128] = nat_ref[cb, pl.ds(r, n, stride=dil), :].astype(dst_ref.dtype)


def _load_chunks(nat_ref):
    return jnp.concatenate([nat_ref[cb] for cb in range(nat_ref.shape[0])], axis=1)


def _store_chunks(nat_ref, val):
    for cb in range(nat_ref.shape[0]):
        nat_ref[cb] = val[:, cb * 128:(cb + 1) * 128]


def _perm_spec(tm, dil, width):
    if dil == 1:
        return pl.BlockSpec((tm, width), lambda i: (i, 0))
    return pl.BlockSpec((dil, tm // dil, width), lambda i: (0, i, 0))


def _perm_shape(t, dil, width, dtype):
    return jax.ShapeDtypeStruct((t, width) if dil == 1 else (dil, t // dil, width), dtype)


def _rmsnorm_fwd(x, g, name, dils=(1,)):
    t, d = x.shape
    tm = _pick(t, (512, 256, 128))

    permuted = any(dil > 1 for dil in dils)

    def body(x_ref, g_ref, *refs):
        h_refs = refs[:len(dils)]
        xv = x_ref[...]
        rs = lax.rsqrt(jnp.mean(xv * xv, axis=-1, keepdims=True) + EPS)
        hv = xv * rs * g_ref[...]
        if permuted:
            _store_chunks(refs[-1], hv)
        for dil, h_ref in zip(dils, h_refs):
            if dil == 1:
                h_ref[...] = hv.astype(BF16)
            else:
                _gather_rows(refs[-1], h_ref, dil, tm)

    outs = pl.pallas_call(
        body, name=name, grid=(t // tm,),
        in_specs=[pl.BlockSpec((tm, d), lambda i: (i, 0)), pl.BlockSpec((1, d), lambda i: (0, 0))],
        out_specs=[_perm_spec(tm, dil, d) for dil in dils],
        out_shape=[_perm_shape(t, dil, d, BF16) for dil in dils],
        scratch_shapes=[pltpu.VMEM((_chunks(d), tm, 128), F32)] if permuted else [],
        compiler_params=_params(("parallel",)),
    )(x, g.reshape(1, d))
    return [o.reshape(t, d) for o in outs]


def _rmsnorm_bwd(x, g, dhs, dres, name, dils=(1,)):
    t, d = x.shape
    tm = _pick(t, (512, 256, 128) if len(dils) == 1 else (256, 128))
    n_in = len(dils)

    def body(*refs):
        x_ref, g_ref = refs[:2]
        dh_refs = refs[2:2 + n_in]
        dres_ref = refs[2 + n_in] if dres is not None else None
        dx_ref, dxb_ref, dg_ref = refs[-4:-1]
        nat_ref = refs[-1]
        dhv = None
        for dil, dh_ref in zip(dils, dh_refs):
            if dil == 1:
                term = dh_ref[...].astype(F32)
            else:
                _scatter_rows(dh_ref, nat_ref, dil, tm)
                term = _load_chunks(nat_ref)
            dhv = term if dhv is None else dhv + term
        xv = x_ref[...]
        r = lax.rsqrt(jnp.mean(xv * xv, axis=-1, keepdims=True) + EPS)
        xhat = xv * r
        dxh = dhv * g_ref[...]
        dx = r * (dxh - xhat * jnp.mean(dxh * xhat, axis=-1, keepdims=True))
        if dres_ref is not None:
            dx = dx + dres_ref[...]
        dx_ref[...] = dx
        dxb_ref[...] = dx.astype(BF16)
        part = jnp.sum(dhv * xhat, axis=0, keepdims=True)

        @pl.when(pl.program_id(0) == 0)
        def _():
            dg_ref[...] = part

        @pl.when(pl.program_id(0) > 0)
        def _():
            dg_ref[...] += part

    row = pl.BlockSpec((tm, d), lambda i: (i, 0))
    vec = pl.BlockSpec((1, d), lambda i: (0, 0))
    in_specs = [row, vec] + [_perm_spec(tm, dil, d) for dil in dils] + ([row] if dres is not None else [])
    dh_args = [dh if dil == 1 else dh.reshape(dil, t // dil, d) for dil, dh in zip(dils, dhs)]
    args = (x, g.reshape(1, d), *dh_args) + ((dres,) if dres is not None else ())
    return pl.pallas_call(
        body, name=name, grid=(t // tm,), in_specs=in_specs, out_specs=[row, row, vec],
        out_shape=[jax.ShapeDtypeStruct((t, d), F32), jax.ShapeDtypeStruct((t, d), BF16),
                   jax.ShapeDtypeStruct((1, d), F32)],
        scratch_shapes=[pltpu.VMEM((_chunks(d), tm, 128), F32)],
        compiler_params=_params(("arbitrary",)),
    )(*args)


def _final_loss(x, g, target, name="final_loss"):
    t, d = x.shape
    tm = _pick(t, (512, 256, 128))

    def body(x_ref, g_ref, t_ref, loss_ref, dx_ref, dxb_ref, dg_ref):
        xv = x_ref[...]
        gv = g_ref[...]
        r = lax.rsqrt(jnp.mean(xv * xv, axis=-1, keepdims=True) + EPS)
        xhat = xv * r
        e = xhat * gv - t_ref[...]
        lpart = jnp.zeros((1, 128), F32) + (0.5 / d) * jnp.sum(e * e)
        dy = e * (1.0 / d)
        dxh = dy * gv
        dx = r * (dxh - xhat * jnp.mean(dxh * xhat, axis=-1, keepdims=True))
        dx_ref[...] = dx
        dxb_ref[...] = dx.astype(BF16)
        gpart = jnp.sum(dy * xhat, axis=0, keepdims=True)

        @pl.when(pl.program_id(0) == 0)
        def _():
            dg_ref[...] = gpart
            loss_ref[...] = lpart

        @pl.when(pl.program_id(0) > 0)
        def _():
            dg_ref[...] += gpart
            loss_ref[...] += lpart

    row = pl.BlockSpec((tm, d), lambda i: (i, 0))
    vec = pl.BlockSpec((1, d), lambda i: (0, 0))
    return pl.pallas_call(
        body, name=name, grid=(t // tm,), in_specs=[row, vec, row],
        out_specs=[pl.BlockSpec((1, 128), lambda i: (0, 0)), row, row, vec],
        out_shape=[jax.ShapeDtypeStruct((1, 128), F32), jax.ShapeDtypeStruct((t, d), F32),
                   jax.ShapeDtypeStruct((t, d), BF16), jax.ShapeDtypeStruct((1, d), F32)],
        compiler_params=_params(("arbitrary",)),
    )(x, g.reshape(1, d), target)


CONV_K = 4
HALO = 8


def _shift_down(cur, prev8, s):
    rolled = pltpu.roll(cur, s, 0)
    fix = pltpu.roll(prev8, s, 0)
    head = jnp.where(_iota((HALO, cur.shape[1]), 0) < s, fix, rolled[:HALO])
    return jnp.concatenate([head, rolled[HALO:]], axis=0)


def _shift_up(cur, next8, s):
    n = cur.shape[0]
    rolled = pltpu.roll(cur, n - s, 0)
    fix = pltpu.roll(next8, HALO - s, 0)
    tail = jnp.where(_iota((HALO, cur.shape[1]), 0) >= HALO - s, fix, rolled[n - HALO:])
    return jnp.concatenate([rolled[:n - HALO], tail], axis=0)


def _conv_pre(u_ref, up_ref, w_ref, b_ref, first):
    cur = u_ref[...]
    prev8 = jnp.where(first, 0.0, up_ref[...])
    w = w_ref[...]
    shifted = [cur] + [_shift_down(cur, prev8, s) for s in (1, 2, 3)]
    pre = b_ref[...] + sum(w[CONV_K - 1 - s:CONV_K - s, :] * shifted[s] for s in range(CONV_K))
    return pre, shifted


def _conv_specs(tm, tc):
    nb = tm // HALO
    cur = pl.BlockSpec((tm, tc), lambda j, i: (i, j))
    prev = pl.BlockSpec((HALO, tc), lambda j, i: (jnp.maximum(i * nb - 1, 0), j))
    wspec = pl.BlockSpec((CONV_K, tc), lambda j, i: (0, j))
    bspec = pl.BlockSpec((1, tc), lambda j, i: (0, j))
    return cur, prev, wspec, bspec


def _conv_fwd(u, w, b, name):
    t, c = u.shape
    tm, tc = _pick(t, (512, 256, 128)), _pick(c, (1024, 512, 256, 128))
    cur, prev, wspec, bspec = _conv_specs(tm, tc)

    def body(u_ref, up_ref, w_ref, b_ref, o_ref):
        pre, _ = _conv_pre(u_ref, up_ref, w_ref, b_ref, pl.program_id(1) == 0)
        o_ref[...] = _silu(pre)

    return pl.pallas_call(
        body, name=name, grid=(c // tc, t // tm), in_specs=[cur, prev, wspec, bspec], out_specs=cur,
        out_shape=jax.ShapeDtypeStruct((t, c), F32),
        compiler_params=_params(("parallel", "parallel")),
    )(u, u, w, b.reshape(1, c))


def _conv_bwd_pre(u, w, b, dy, name):
    t, c = u.shape
    tm, tc = _pick(t, (512, 256, 128)), _pick(c, (1024, 512, 256, 128))
    cur, prev, wspec, bspec = _conv_specs(tm, tc)

    def body(u_ref, up_ref, w_ref, b_ref, dy_ref, dpre_ref, dw_ref, db_ref):
        i = pl.program_id(1)
        pre, shifted = _conv_pre(u_ref, up_ref, w_ref, b_ref, i == 0)
        dpre = dy_ref[...] * _dsilu(pre)
        dpre_ref[...] = dpre
        dw = jnp.concatenate([jnp.sum(dpre * shifted[CONV_K - 1 - k], axis=0, keepdims=True) for k in range(CONV_K)], axis=0)
        db = jnp.sum(dpre, axis=0, keepdims=True)

        @pl.when(i == 0)
        def _():
            dw_ref[...] = dw
            db_ref[...] = db

        @pl.when(i > 0)
        def _():
            dw_ref[...] += dw
            db_ref[...] += db

    return pl.pallas_call(
        body, name=name, grid=(c // tc, t // tm), in_specs=[cur, prev, wspec, bspec, cur],
        out_specs=[cur, wspec, bspec],
        out_shape=[jax.ShapeDtypeStruct((t, c), F32), jax.ShapeDtypeStruct((CONV_K, c), F32),
                   jax.ShapeDtypeStruct((1, c), F32)],
        compiler_params=_params(("parallel", "arbitrary")),
    )(u, u, w, b.reshape(1, c), dy)


def _conv_bwd_in(dpre, w, name):
    t, c = dpre.shape
    tm, tc = _pick(t, (512, 256, 128)), _pick(c, (1024, 512, 256, 128))
    nb = tm // HALO
    last = t // tm - 1
    cur = pl.BlockSpec((tm, tc), lambda j, i: (i, j))
    nxt = pl.BlockSpec((HALO, tc), lambda j, i: (jnp.minimum((i + 1) * nb, t // HALO - 1), j))
    wspec = pl.BlockSpec((CONV_K, tc), lambda j, i: (0, j))

    def body(d_ref, dn_ref, w_ref, o_ref):
        cur_v = d_ref[...]
        next8 = jnp.where(pl.program_id(1) == last, 0.0, dn_ref[...])
        wv = w_ref[...]
        acc = wv[CONV_K - 1:CONV_K, :] * cur_v
        for s in (1, 2, 3):
            acc = acc + wv[CONV_K - 1 - s:CONV_K - s, :] * _shift_up(cur_v, next8, s)
        o_ref[...] = acc.astype(o_ref.dtype)

    return pl.pallas_call(
        body, name=name, grid=(c // tc, t // tm), in_specs=[cur, nxt, wspec], out_specs=cur,
        out_shape=jax.ShapeDtypeStruct((t, c), BF16),
        compiler_params=_params(("parallel", "parallel")),
    )(dpre, dpre, w)


MEM_HEADS = 4
NT_DIMS = (((1,), (1,)), ((), ()))
TN_DIMS = (((0,), (0,)), ((), ()))


def _dot(a, b, dims=None):
    if dims is None:
        return jnp.dot(a, b, preferred_element_type=F32)
    return lax.dot_general(a, b, dims, preferred_element_type=F32)


def _memattn_probs(q, mk, scale):
    s = _dot(q, mk, NT_DIMS) * scale
    s = s - jnp.max(s, axis=-1, keepdims=True)
    p = jnp.exp(s)
    return p / jnp.sum(p, axis=-1, keepdims=True)


def _memattn_fwd(q, mkv, name):
    t, wd = q.shape
    m = mkv.shape[0]
    hd = wd // MEM_HEADS
    scale = hd ** -0.5
    tm = _pick(t, (512, 256, 128))

    def body(q_ref, mkv_ref, o_ref):
        for h in range(MEM_HEADS):
            cols = slice(h * hd, (h + 1) * hd)
            p = _memattn_probs(q_ref[:, cols], mkv_ref[:, cols], scale)
            o_ref[:, cols] = _dot(p.astype(BF16), mkv_ref[:, wd + h * hd:wd + (h + 1) * hd])

    return pl.pallas_call(
        body, name=name, grid=(t // tm,),
        in_specs=[pl.BlockSpec((tm, wd), lambda i: (i, 0)), pl.BlockSpec((m, 2 * wd), lambda i: (0, 0))],
        out_specs=pl.BlockSpec((tm, wd), lambda i: (i, 0)),
        out_shape=jax.ShapeDtypeStruct((t, wd), F32),
        compiler_params=_params(("parallel",)),
    )(q, mkv)


def _memattn_bwd(q, mkv, dy, name):
    t, wd = q.shape
    m = mkv.shape[0]
    hd = wd // MEM_HEADS
    scale = hd ** -0.5
    tm = _pick(t, (512, 256, 128))

    def body(q_ref, mkv_ref, dy_ref, dq_ref, dmkv_ref):
        i = pl.program_id(0)

        @pl.when(i == 0)
        def _():
            dmkv_ref[...] = jnp.zeros_like(dmkv_ref)

        for h in range(MEM_HEADS):
            cols = slice(h * hd, (h + 1) * hd)
            vcols = slice(wd + h * hd, wd + (h + 1) * hd)
            qh = q_ref[:, cols]
            p = _memattn_probs(qh, mkv_ref[:, cols], scale)
            dyh = dy_ref[:, cols].astype(BF16)
            dp = _dot(dyh, mkv_ref[:, vcols], NT_DIMS)
            ds = (p * (dp - jnp.sum(dp * p, axis=-1, keepdims=True)) * scale).astype(BF16)
            dq_ref[:, cols] = _dot(ds, mkv_ref[:, cols]).astype(dq_ref.dtype)
            dmkv_ref[:, cols] += _dot(ds, qh, TN_DIMS)
            dmkv_ref[:, vcols] += _dot(p.astype(BF16), dyh, TN_DIMS)

    return pl.pallas_call(
        body, name=name, grid=(t // tm,),
        in_specs=[pl.BlockSpec((tm, wd), lambda i: (i, 0)), pl.BlockSpec((m, 2 * wd), lambda i: (0, 0)),
                  pl.BlockSpec((tm, wd), lambda i: (i, 0))],
        out_specs=[pl.BlockSpec((tm, wd), lambda i: (i, 0)), pl.BlockSpec((m, 2 * wd), lambda i: (0, 0))],
        out_shape=[jax.ShapeDtypeStruct((t, wd), BF16), jax.ShapeDtypeStruct((m, 2 * wd), F32)],
        compiler_params=_params(("arbitrary",)),
    )(q, mkv, dy)


NORM_GROUPS = 8


def _gate_fwd(y_tok, y_mem, z, norm_g, name):
    t, tok = y_tok.shape
    mem = y_mem.shape[1]
    mix = tok + mem
    gw = tok // NORM_GROUPS
    tm = _pick(t, (256, 128))

    def body(*refs):
        if norm_g is None:
            yt_ref, ym_ref, z_ref, o_ref = refs
        else:
            yt_ref, ym_ref, z_ref, g_ref, o_ref = refs
        u = yt_ref[...] * _silu(z_ref[:, :tok])
        if norm_g is None:
            o_ref[:, :tok] = u.astype(o_ref.dtype)
        else:
            for k in range(NORM_GROUPS):
                uk = u[:, k * gw:(k + 1) * gw]
                r = lax.rsqrt(jnp.mean(uk * uk, axis=-1, keepdims=True) + EPS)
                o_ref[:, k * gw:(k + 1) * gw] = (uk * r * g_ref[:, k * gw:(k + 1) * gw]).astype(o_ref.dtype)
        o_ref[:, tok:] = (ym_ref[...] * _silu(z_ref[:, tok:])).astype(o_ref.dtype)

    in_specs = [pl.BlockSpec((tm, tok), lambda i: (i, 0)), pl.BlockSpec((tm, mem), lambda i: (i, 0)),
                pl.BlockSpec((tm, mix), lambda i: (i, 0))]
    args = [y_tok, y_mem, z]
    if norm_g is not None:
        in_specs.append(pl.BlockSpec((1, tok), lambda i: (0, 0)))
        args.append(norm_g.reshape(1, tok))
    return pl.pallas_call(
        body, name=name, grid=(t // tm,), in_specs=in_specs,
        out_specs=pl.BlockSpec((tm, mix), lambda i: (i, 0)),
        out_shape=jax.ShapeDtypeStruct((t, mix), BF16),
        compiler_params=_params(("parallel",)),
    )(*args)


def _gate_bwd(y_tok, y_mem, z, norm_g, dgated, name):
    t, tok = y_tok.shape
    mem = y_mem.shape[1]
    mix = tok + mem
    gw = tok // NORM_GROUPS
    tm = _pick(t, (256, 128))

    def body(*refs):
        if norm_g is None:
            yt_ref, ym_ref, z_ref, dg_ref, dyt_ref, dym_ref, dz_ref, dn_ref = refs
        else:
            yt_ref, ym_ref, z_ref, dg_ref, g_ref, dyt_ref, dym_ref, dz_ref, dn_ref = refs
        i = pl.program_id(0)
        zt = z_ref[:, :tok]
        yt = yt_ref[...]
        sz = _silu(zt)
        dout = dg_ref[:, :tok].astype(F32)
        if norm_g is None:
            du = dout
            dn = jnp.zeros((1, tok), F32)
        else:
            u = yt * sz
            dus, dns = [], []
            for k in range(NORM_GROUPS):
                uk = u[:, k * gw:(k + 1) * gw]
                r = lax.rsqrt(jnp.mean(uk * uk, axis=-1, keepdims=True) + EPS)
                nk = uk * r
                dk = dout[:, k * gw:(k + 1) * gw]
                dns.append(jnp.sum(dk * nk, axis=0, keepdims=True))
                dnk = dk * g_ref[:, k * gw:(k + 1) * gw]
                dus.append(r * (dnk - nk * jnp.mean(dnk * nk, axis=-1, keepdims=True)))
            du = jnp.concatenate(dus, axis=1)
            dn = jnp.concatenate(dns, axis=1)
        dyt_ref[...] = du * sz
        dz_ref[:, :tok] = (du * yt * _dsilu(zt)).astype(dz_ref.dtype)
        zm = z_ref[:, tok:]
        dm = dg_ref[:, tok:].astype(F32)
        dym_ref[...] = dm * _silu(zm)
        dz_ref[:, tok:] = (dm * ym_ref[...] * _dsilu(zm)).astype(dz_ref.dtype)

        @pl.when(i == 0)
        def _():
            dn_ref[...] = dn

        @pl.when(i > 0)
        def _():
            dn_ref[...] += dn

    tok_spec = pl.BlockSpec((tm, tok), lambda i: (i, 0))
    mem_spec = pl.BlockSpec((tm, mem), lambda i: (i, 0))
    mix_spec = pl.BlockSpec((tm, mix), lambda i: (i, 0))
    vec = pl.BlockSpec((1, tok), lambda i: (0, 0))
    in_specs = [tok_spec, mem_spec, mix_spec, mix_spec]
    args = [y_tok, y_mem, z, dgated]
    if norm_g is not None:
        in_specs.append(vec)
        args.append(norm_g.reshape(1, tok))
    return pl.pallas_call(
        body, name=name, grid=(t // tm,), in_specs=in_specs,
        out_specs=[tok_spec, mem_spec, mix_spec, vec],
        out_shape=[jax.ShapeDtypeStruct((t, tok), F32), jax.ShapeDtypeStruct((t, mem), F32),
                   jax.ShapeDtypeStruct((t, mix), BF16), jax.ShapeDtypeStruct((1, tok), F32)],
        compiler_params=_params(("arbitrary",)),
    )(*args)


SSD_Q = 128
SSD_N = 128
SSD_P = 64
SSD_G = 8
SSD_HPG = 6
SSD_H = SSD_G * SSD_HPG
SSD_TOK = SSD_H * SSD_P
SSD_XBC = SSD_TOK + 2 * SSD_G * SSD_N
LANES = 128
HIGHEST = lax.Precision.HIGHEST


def _softplus(x):
    return jnp.maximum(x, 0.0) + jnp.log(1.0 + jnp.exp(-jnp.abs(x)))


def _ssd_common(dtr_ref, bias_ref, alog_ref):
    sq = (SSD_Q, LANES)
    pre = dtr_ref[...] + bias_ref[...]
    dt = _softplus(pre)
    a = -jnp.exp(alog_ref[...])
    tril = (_iota(sq, 0) >= _iota(sq, 1)).astype(F32)
    acs = jnp.dot(tril, dt * a, precision=HIGHEST, preferred_element_type=F32)
    return pre, dt, a, tril, acs, acs.T


def _pair_terms(dt, acs, acs_t, h0):
    hi = _iota((SSD_Q, LANES), 1) >= SSD_P
    heads = []
    for j in range(2):
        h = h0 + j
        a_col = _col(acs, h)
        a_row = acs_t[h:h + 1, :]
        a_last = _col(acs[SSD_Q - 1:SSD_Q, :], h)
        heads.append((h, a_col, a_row, a_last, hi if j else jnp.logical_not(hi)))
    dtl = jnp.where(hi, _col(dt, h0 + 1), _col(dt, h0))
    scale = jnp.where(hi, jnp.exp(heads[1][1]), jnp.exp(heads[0][1]))
    dec_last = jnp.where(hi[:1], jnp.exp(heads[1][3]), jnp.exp(heads[0][3]))
    return heads, dtl, scale, dec_last


def _decay(a_col, a_row):
    causal = _iota((SSD_Q, SSD_Q), 0) >= _iota((SSD_Q, SSD_Q), 1)
    return jnp.where(causal, jnp.exp(jnp.minimum(a_col - a_row, 0.0)), 0.0)


def _ssd_fwd(xbc, dt_raw, dt_bias, a_log, dskip_lane, name):
    t = xbc.shape[0]
    nc = t // SSD_Q

    def body(xbc_ref, dtr_ref, bias_ref, alog_ref, dsk_ref, y_ref, hs_ref, h_ref):
        @pl.when(pl.program_id(0) == 0)
        def _():
            h_ref[...] = jnp.zeros_like(h_ref)

        _, dt, _, _, acs, acs_t = _ssd_common(dtr_ref, bias_ref, alog_ref)
        for g in range(SSD_G):
            bg_f = xbc_ref[:, SSD_TOK + g * SSD_N:SSD_TOK + (g + 1) * SSD_N]
            bg = bg_f.astype(BF16)
            cg = xbc_ref[:, SSD_TOK + SSD_G * SSD_N + g * SSD_N:SSD_TOK + SSD_G * SSD_N + (g + 1) * SSD_N].astype(BF16)
            cb = _dot(cg, bg, NT_DIMS)
            for pr in range(SSD_HPG // 2):
                h0 = g * SSD_HPG + 2 * pr
                lanes = slice(h0 * SSD_P, (h0 + 2) * SSD_P)
                heads, dtl, scale, dec_last = _pair_terms(dt, acs, acs_t, h0)
                xs = xbc_ref[:, lanes]
                xdt = xs * dtl
                hp = h_ref[:, lanes]
                hs_ref[:, lanes] = hp
                y = _dot(cg, hp.astype(BF16)) * scale + dsk_ref[:, lanes] * xs
                snew = hp * dec_last
                for _, a_col, a_row, a_last, mask in heads:
                    xm = jnp.where(mask, xdt, 0.0).astype(BF16)
                    y = y + _dot((cb * _decay(a_col, a_row)).astype(BF16), xm)
                    bw = (bg_f * jnp.exp(a_last - a_col)).astype(BF16)
                    snew = snew + _dot(bw, xm, TN_DIMS)
                y_ref[:, lanes] = y
                h_ref[:, lanes] = snew

    row = lambda w: pl.BlockSpec((SSD_Q, w), lambda c: (c, 0))
    vec = lambda w: pl.BlockSpec((1, w), lambda c: (0, 0))
    return pl.pallas_call(
        body, name=name, grid=(nc,),
        in_specs=[row(SSD_XBC), row(LANES), vec(LANES), vec(LANES), vec(SSD_TOK)],
        out_specs=[row(SSD_TOK), row(SSD_TOK)],
        out_shape=[jax.ShapeDtypeStruct((t, SSD_TOK), F32), jax.ShapeDtypeStruct((nc * SSD_N, SSD_TOK), F32)],
        scratch_shapes=[pltpu.VMEM((SSD_N, SSD_TOK), F32)],
        compiler_params=_params(("arbitrary",)),
    )(xbc, dt_raw, dt_bias, a_log, dskip_lane)


def _ssd_bwd(xbc, dt_raw, dt_bias, a_log, dskip_lane, hs, dy, name):
    t = xbc.shape[0]
    nc = t // SSD_Q
    sq = (SSD_Q, LANES)

    def body(xbc_ref, dtr_ref, bias_ref, alog_ref, dsk_ref, hs_ref, dy_ref,
             dxbc_ref, ddtr_ref, dbias_ref, dalog_ref, ddsk_ref, dh_ref):
        first = pl.program_id(0) == 0

        @pl.when(first)
        def _():
            dh_ref[...] = jnp.zeros_like(dh_ref)
            dbias_ref[...] = jnp.zeros_like(dbias_ref)
            dalog_ref[...] = jnp.zeros_like(dalog_ref)
            ddsk_ref[...] = jnp.zeros_like(ddsk_ref)

        pre, dt, a, tril, acs, acs_t = _ssd_common(dtr_ref, bias_ref, alog_ref)
        lane = _iota(sq, 1)
        sub = _iota(sq, 0)
        causal = sub >= lane
        d_acs = jnp.zeros(sq, F32)
        d_acs_row = jnp.zeros(sq, F32)
        d_last = jnp.zeros((1, LANES), F32)
        ddt = jnp.zeros(sq, F32)
        for g in range(SSD_G):
            bcols = slice(SSD_TOK + g * SSD_N, SSD_TOK + (g + 1) * SSD_N)
            ccols = slice(SSD_TOK + SSD_G * SSD_N + g * SSD_N, SSD_TOK + SSD_G * SSD_N + (g + 1) * SSD_N)
            bg_f = xbc_ref[:, bcols]
            bg = bg_f.astype(BF16)
            cg = xbc_ref[:, ccols].astype(BF16)
            cb = _dot(cg, bg, NT_DIMS)
            dcb = jnp.zeros(sq, F32)
            dbg = jnp.zeros(sq, F32)
            dcg = jnp.zeros(sq, F32)
            for pr in range(SSD_HPG // 2):
                h0 = g * SSD_HPG + 2 * pr
                lanes = slice(h0 * SSD_P, (h0 + 2) * SSD_P)
                heads, dtl, scale, dec_last = _pair_terms(dt, acs, acs_t, h0)
                xs = xbc_ref[:, lanes]
                xdt = xs * dtl
                dyv = dy_ref[:, lanes]
                hp = hs_ref[:, lanes]
                dhn = dh_ref[:, lanes]
                hp_b = hp.astype(BF16)
                dys = (dyv * scale).astype(BF16)
                yoff_dy = dyv * _dot(cg, hp_b) * scale
                dcg = dcg + _dot(dys, hp_b, NT_DIMS)
                dhc = _dot(cg, dys, TN_DIMS)
                hh = dhn * hp
                dxdt = jnp.zeros(sq, F32)
                for h, a_col, a_row, a_last, mask in heads:
                    dec = _decay(a_col, a_row)
                    m = cb * dec
                    dym = jnp.where(mask, dyv, 0.0).astype(BF16)
                    xm = jnp.where(mask, xdt, 0.0).astype(BF16)
                    dhm = jnp.where(mask, dhn, 0.0).astype(BF16)
                    w = jnp.exp(a_last - a_col)
                    dxdt = dxdt + _dot(m.astype(BF16), dym, TN_DIMS) + _dot((bg_f * w).astype(BF16), dhm)
                    dm = jnp.where(causal, _dot(dym, xm, NT_DIMS), 0.0)
                    dcb = dcb + dm * dec
                    e = dm * m
                    gj = _dot(xm, dhm, NT_DIMS)
                    dbg = dbg + w * gj
                    wdw = w * jnp.sum(bg_f * gj, axis=1, keepdims=True)
                    col = (jnp.sum(e, axis=1, keepdims=True)
                           + jnp.sum(jnp.where(mask, yoff_dy, 0.0), axis=1, keepdims=True) - wdw)
                    d_acs = d_acs + jnp.where(lane == h, col, 0.0)
                    d_acs_row = d_acs_row + jnp.where(sub == h, jnp.sum(e, axis=0, keepdims=True), 0.0)
                    last = jnp.sum(wdw) + jnp.exp(a_last) * jnp.sum(jnp.where(mask, hh, 0.0))
                    d_last = d_last + jnp.where(lane[:1] == h, last, 0.0)
                dxbc_ref[:, lanes] = dxdt * dtl + dsk_ref[:, lanes] * dyv
                tt = dxdt * xs
                for h, _, _, _, mask in heads:
                    ddt = ddt + jnp.where(lane == h, jnp.sum(jnp.where(mask, tt, 0.0), axis=1, keepdims=True), 0.0)
                ddsk_ref[:, lanes] += jnp.sum(dyv * xs, axis=0, keepdims=True)
                dh_ref[:, lanes] = dhn * dec_last + dhc
            dcb_b = dcb.astype(BF16)
            dxbc_ref[:, bcols] = dbg + _dot(dcb_b, cg, TN_DIMS)
            dxbc_ref[:, ccols] = dcg + _dot(dcb_b, bg)
        d_tot = d_acs - d_acs_row.T + jnp.where(sub == SSD_Q - 1, d_last, 0.0)
        ddta = lax.dot_general(tril, d_tot, TN_DIMS, precision=HIGHEST, preferred_element_type=F32)
        ddt = ddt + ddta * a
        dalog_ref[...] += jnp.sum(ddta * dt, axis=0, keepdims=True) * a
        ddtr = ddt * jax.nn.sigmoid(pre)
        ddtr_ref[...] = ddtr
        dbias_ref[...] += jnp.sum(ddtr, axis=0, keepdims=True)

    rev = lambda w: pl.BlockSpec((SSD_Q, w), lambda i: (nc - 1 - i, 0))
    vec = lambda w: pl.BlockSpec((1, w), lambda i: (0, 0))
    return pl.pallas_call(
        body, name=name, grid=(nc,),
        in_specs=[rev(SSD_XBC), rev(LANES), vec(LANES), vec(LANES), vec(SSD_TOK), rev(SSD_TOK), rev(SSD_TOK)],
        out_specs=[rev(SSD_XBC), rev(LANES), vec(LANES), vec(LANES), vec(SSD_TOK)],
        out_shape=[jax.ShapeDtypeStruct((t, SSD_XBC), F32), jax.ShapeDtypeStruct((t, LANES), F32),
                   jax.ShapeDtypeStruct((1, LANES), F32), jax.ShapeDtypeStruct((1, LANES), F32),
                   jax.ShapeDtypeStruct((1, SSD_TOK), F32)],
        scratch_shapes=[pltpu.VMEM((SSD_N, SSD_TOK), F32)],
        compiler_params=_params(("arbitrary",)),
    )(xbc, dt_raw, dt_bias, a_log, dskip_lane, hs, dy)


ATT_E = 128
ATT_H = 24
ATT_W = 128
ATT_TOK = ATT_H * ATT_E
DILATED_GROUPS = ((128, 1), (512, 4), (2048, 16))
N_DIL = len(DILATED_GROUPS)
ALIBI_MAX_EXP = 8.0
MASKED = -1e30


def _alibi_slopes(group):
    n = N_DIL * ATT_H
    return [2.0 ** (-ALIBI_MAX_EXP * (group * ATT_H + h + 1) / n) for h in range(ATT_H)]


def _att_scores(qh, kk, rel, valid, slope_d):
    s = _dot(qh, kk, NT_DIMS) * (ATT_E ** -0.5) - slope_d * rel
    return jnp.where(valid, s, MASKED)


def _att_rel(j):
    shp = (ATT_W, 2 * ATT_W)
    kpos = _iota(shp, 1)
    rel = _iota(shp, 0) + ATT_W - kpos
    valid = (rel >= 0) & (rel <= ATT_W) & ((kpos >= ATT_W) | (j > 0))
    return rel.astype(F32), valid


def _dil_fwd(q, k, v, group, name):
    t = q.shape[0]
    dil = DILATED_GROUPS[group][1]
    slopes = _alibi_slopes(group)
    nb = t // dil // ATT_W

    def body(q_ref, kp_ref, kc_ref, vp_ref, vc_ref, o_ref, lse_ref):
        rel, valid = _att_rel(pl.program_id(1))
        lane = _iota((ATT_W, LANES), 1)
        lse_all = jnp.zeros((ATT_W, LANES), F32)
        for h in range(ATT_H):
            cols = slice(h * ATT_E, (h + 1) * ATT_E)
            kk = jnp.concatenate([kp_ref[:, cols], kc_ref[:, cols]], axis=0)
            vv = jnp.concatenate([vp_ref[:, cols], vc_ref[:, cols]], axis=0)
            s = _att_scores(q_ref[:, cols], kk, rel, valid, slopes[h] * dil)
            m = jnp.max(s, axis=-1, keepdims=True)
            p = jnp.exp(s - m)
            den = jnp.sum(p, axis=-1, keepdims=True)
            o_ref[:, cols] = _dot(p.astype(BF16), vv) / den
            lse_all = jnp.where(lane == h, m + jnp.log(den), lse_all)
        lse_ref[...] = lse_all

    cur = pl.BlockSpec((ATT_W, ATT_TOK), lambda r, j: (r * nb + j, 0))
    prev = pl.BlockSpec((ATT_W, ATT_TOK), lambda r, j: (r * nb + jnp.maximum(j - 1, 0), 0))
    small = pl.BlockSpec((ATT_W, LANES), lambda r, j: (r * nb + j, 0))
    return pl.pallas_call(
        body, name=name, grid=(dil, nb),
        in_specs=[cur, prev, cur, prev, cur], out_specs=[cur, small],
        out_shape=[jax.ShapeDtypeStruct((t, ATT_TOK), F32), jax.ShapeDtypeStruct((t, LANES), F32)],
        compiler_params=_params(("parallel", "parallel")),
    )(q, k, k, v, v)


def _dil_bwd(q, k, v, do, cterm, lse, group, name):
    t = q.shape[0]
    dil = DILATED_GROUPS[group][1]
    slopes = _alibi_slopes(group)
    nb = t // dil // ATT_W

    def body(q_ref, kp_ref, kc_ref, vp_ref, vc_ref, do_ref, c_ref, lse_ref,
             dq_ref, dk_ref, dv_ref, ck_ref, cv_ref):
        j = pl.program_id(1)

        @pl.when(j == 0)
        def _():
            ck_ref[...] = jnp.zeros_like(ck_ref)
            cv_ref[...] = jnp.zeros_like(cv_ref)

        @pl.when(j < nb)
        def _():
            rel, valid = _att_rel(j)
            cv_, lv = c_ref[...], lse_ref[...]
            for h in range(ATT_H):
                cols = slice(h * ATT_E, (h + 1) * ATT_E)
                qh = q_ref[:, cols]
                kk = jnp.concatenate([kp_ref[:, cols], kc_ref[:, cols]], axis=0)
                vv = jnp.concatenate([vp_ref[:, cols], vc_ref[:, cols]], axis=0)
                s = _att_scores(qh, kk, rel, valid, slopes[h] * dil)
                p = jnp.where(valid, jnp.exp(s - _col(lv, h)), 0.0)
                do = do_ref[:, cols]
                dp = _dot(do, vv, NT_DIMS)
                ds = (p * (dp + _col(cv_, h)) * (ATT_E ** -0.5)).astype(BF16)
                dq_ref[:, cols] = _dot(ds, kk).astype(dq_ref.dtype)
                dkk = _dot(ds, qh, TN_DIMS)
                dvv = _dot(p.astype(BF16), do, TN_DIMS)
                dk_ref[:, cols] = (ck_ref[:, cols] + dkk[:ATT_W]).astype(dk_ref.dtype)
                dv_ref[:, cols] = (cv_ref[:, cols] + dvv[:ATT_W]).astype(dv_ref.dtype)
                ck_ref[:, cols] = dkk[ATT_W:]
                cv_ref[:, cols] = dvv[ATT_W:]

        @pl.when(j == nb)
        def _():
            dk_ref[...] = ck_ref[...].astype(dk_ref.dtype)
            dv_ref[...] = cv_ref[...].astype(dv_ref.dtype)

    jq = lambda j: jnp.minimum(j, nb - 1)
    cur = pl.BlockSpec((ATT_W, ATT_TOK), lambda r, j: (r * nb + jq(j), 0))
    prev = pl.BlockSpec((ATT_W, ATT_TOK), lambda r, j: (r * nb + jnp.maximum(jq(j) - 1, 0), 0))
    small = pl.BlockSpec((ATT_W, LANES), lambda r, j: (r * nb + jq(j), 0))
    late = pl.BlockSpec((ATT_W, ATT_TOK), lambda r, j: (r * nb + jnp.maximum(j - 1, 0), 0))
    big = jax.ShapeDtypeStruct((t, ATT_TOK), BF16)
    return pl.pallas_call(
        body, name=name, grid=(dil, nb + 1),
        in_specs=[cur, prev, cur, prev, cur, cur, small, small], out_specs=[cur, late, late],
        out_shape=[big, big, big],
        scratch_shapes=[pltpu.VMEM((ATT_W, ATT_TOK), F32), pltpu.VMEM((ATT_W, ATT_TOK), F32)],
        compiler_params=_params(("parallel", "arbitrary")),
    )(q, k, k, v, v, do, cterm, lse)


def _combine_weights(lses):
    m = functools.reduce(jnp.maximum, lses)
    es = [jnp.exp(l - m) for l in lses]
    tot = functools.reduce(lambda a, b: a + b, es)
    return [e / tot for e in es]


DILS = tuple(d for _, d in DILATED_GROUPS)
COMBINE_ROWS = 256


def _by_residue(arr, dil):
    return arr if dil == 1 else arr.reshape(dil, arr.shape[0] // dil, arr.shape[1])


def _natural_lses(l_refs, small_refs, tm):
    vals = []
    for g, dil in enumerate(DILS):
        if dil == 1:
            vals.append(l_refs[g][...])
        else:
            _scatter_rows(l_refs[g], small_refs[g], dil, tm)
            vals.append(small_refs[g][0])
    return vals


def _combine_scratch(tm):
    return ([pltpu.VMEM((_chunks(ATT_TOK), tm, 128), F32)] * N_DIL + [pltpu.VMEM((1, tm, 128), F32)] * N_DIL)


def _combine_fwd(outs, lses, name):
    t = outs[0].shape[0]
    tm = COMBINE_ROWS

    def body(*refs):
        o_refs, l_refs, y_ref = refs[:N_DIL], refs[N_DIL:2 * N_DIL], refs[2 * N_DIL]
        big_refs, small_refs = refs[2 * N_DIL + 1:3 * N_DIL + 1], refs[3 * N_DIL + 1:]
        ws = _combine_weights(_natural_lses(l_refs, small_refs, tm))
        for g in range(1, N_DIL):
            _scatter_rows(o_refs[g], big_refs[g], DILS[g], tm)
        for h in range(ATT_H):
            cols = slice(h * ATT_E, (h + 1) * ATT_E)
            y_ref[:, cols] = (_col(ws[0], h) * o_refs[0][:, cols]
                              + sum(_col(ws[g], h) * big_refs[g][h] for g in range(1, N_DIL)))

    return pl.pallas_call(
        body, name=name, grid=(t // tm,),
        in_specs=[_perm_spec(tm, d, ATT_TOK) for d in DILS] + [_perm_spec(tm, d, LANES) for d in DILS],
        out_specs=pl.BlockSpec((tm, ATT_TOK), lambda i: (i, 0)),
        out_shape=jax.ShapeDtypeStruct((t, ATT_TOK), F32),
        scratch_shapes=_combine_scratch(tm),
        compiler_params=_params(("parallel",)),
    )(*[_by_residue(o, d) for o, d in zip(outs, DILS)], *[_by_residue(l, d) for l, d in zip(lses, DILS)])


def _combine_bwd(outs, lses, dy, name):
    t = outs[0].shape[0]
    tm = COMBINE_ROWS

    def body(*refs):
        o_refs, l_refs, dy_ref = refs[:N_DIL], refs[N_DIL:2 * N_DIL], refs[2 * N_DIL]
        do_refs, c_refs = refs[2 * N_DIL + 1:3 * N_DIL + 1], refs[3 * N_DIL + 1:4 * N_DIL + 1]
        big_refs, small_refs = refs[4 * N_DIL + 1:5 * N_DIL + 1], refs[5 * N_DIL + 1:]
        ws = _combine_weights(_natural_lses(l_refs, small_refs, tm))
        for g in range(1, N_DIL):
            _scatter_rows(o_refs[g], big_refs[g], DILS[g], tm)
        lane = _iota((tm, LANES), 1)
        sdw = jnp.zeros((tm, LANES), F32)
        for h in range(ATT_H):
            cols = slice(h * ATT_E, (h + 1) * ATT_E)
            dyh = dy_ref[:, cols]
            tot = _col(ws[0], h) * jnp.sum(dyh * o_refs[0][:, cols], axis=1, keepdims=True)
            for g in range(1, N_DIL):
                tot = tot + _col(ws[g], h) * jnp.sum(dyh * big_refs[g][h], axis=1, keepdims=True)
            sdw = jnp.where(lane == h, tot, sdw)
        for g, dil in enumerate(DILS):
            cterm = -ws[g] * sdw
            if dil == 1:
                c_refs[g][...] = cterm
            else:
                small_refs[g][0] = cterm
                _gather_rows(small_refs[g], c_refs[g], dil, tm)
            for h in range(ATT_H):
                cols = slice(h * ATT_E, (h + 1) * ATT_E)
                do = _col(ws[g], h) * dy_ref[:, cols]
                if dil == 1:
                    do_refs[g][:, cols] = do.astype(BF16)
                else:
                    big_refs[g][h] = do
            if dil > 1:
                _gather_rows(big_refs[g], do_refs[g], dil, tm)

    res = pl.pallas_call(
        body, name=name, grid=(t // tm,),
        in_specs=([_perm_spec(tm, d, ATT_TOK) for d in DILS] + [_perm_spec(tm, d, LANES) for d in DILS]
                  + [pl.BlockSpec((tm, ATT_TOK), lambda i: (i, 0))]),
        out_specs=[_perm_spec(tm, d, ATT_TOK) for d in DILS] + [_perm_spec(tm, d, LANES) for d in DILS],
        out_shape=[_perm_shape(t, d, ATT_TOK, BF16) for d in DILS] + [_perm_shape(t, d, LANES, F32) for d in DILS],
        scratch_shapes=_combine_scratch(tm),
        compiler_params=_params(("parallel",)),
    )(*[_by_residue(o, d) for o, d in zip(outs, DILS)], *[_by_residue(l, d) for l, d in zip(lses, DILS)], dy)
    return [a.reshape(t, ATT_TOK) for a in res[:N_DIL]], [a.reshape(t, LANES) for a in res[N_DIL:]]


ADAM_LR, ADAM_B1, ADAM_B2, ADAM_EPS, ADAM_WD, ADAM_STEP = 0.001, 0.9, 0.999, 1e-08, 0.01, 10


def _adamw(parts, w, m, v, name):
    r, c = w.shape
    n_parts = parts.shape[0]
    tc = _pick(c, (1024, 512, 256, 128)) if c % 128 == 0 else c
    tm = _pick(r, (128, 64, 32, 16, 8))

    def body(p_ref, w_ref, m_ref, v_ref, g_ref, d_ref, nm_ref, nv_ref):
        g = p_ref[0].astype(F32)
        for k in range(1, n_parts):
            g = g + p_ref[k].astype(F32)
        nm = ADAM_B1 * m_ref[...] + (1.0 - ADAM_B1) * g
        nv = ADAM_B2 * v_ref[...] + (1.0 - ADAM_B2) * (g * g)
        m_hat = nm / (1.0 - ADAM_B1 ** ADAM_STEP)
        v_hat = nv / (1.0 - ADAM_B2 ** ADAM_STEP)
        g_ref[...] = g
        d_ref[...] = -ADAM_LR * (m_hat / (jnp.sqrt(v_hat) + ADAM_EPS) + ADAM_WD * w_ref[...])
        nm_ref[...] = nm
        nv_ref[...] = nv

    blk = pl.BlockSpec((tm, tc), lambda i, j: (i, j))
    pblk = pl.BlockSpec((n_parts, tm, tc), lambda i, j: (0, i, j))
    return pl.pallas_call(
        body, name=name, grid=(r // tm, c // tc), in_specs=[pblk, blk, blk, blk], out_specs=[blk] * 4,
        out_shape=[jax.ShapeDtypeStruct((r, c), F32)] * 4,
        compiler_params=_params(("parallel", "parallel")),
    )(parts, w, m, v)


N_CHIP = 4
MESH_ID = pl.DeviceIdType.MESH


def _other_chips(x, y):
    return [(1 - x, y), (x, 1 - y), (1 - x, 1 - y)]


GATHER_SEMS = N_DEV - 1


def _gather_copies(in_refs, out_refs, send_sems, recv_sems, local_sems, x, y, c):
    n = len(in_refs)
    sibling = (x, y, 1 - c)
    chips = _other_chips(x, y)

    def copy(a, k, block, to, src=None):
        rows = out_refs[a].at[4 * block[0] + 2 * block[1] + block[2]]
        return pltpu.make_async_remote_copy(
            src_ref=rows if src is None else src, dst_ref=rows,
            send_sem=send_sems.at[a * GATHER_SEMS + k], recv_sem=recv_sems.at[a * GATHER_SEMS + k],
            device_id=to, device_id_type=MESH_ID)

    started = []
    for a in range(n):
        local = pltpu.make_async_copy(in_refs[a], out_refs[a].at[4 * x + 2 * y + c], local_sems.at[a])
        local.start()
        started.append(local)
    sends = []
    for j, chip in enumerate(chips):
        for a in range(n):
            sends.append(copy(a, 1 + j, (x, y, c), (*chip, c), src=in_refs[a]))
            sends[-1].start()
    for a in range(n):
        sends.append(copy(a, 0, (x, y, c), sibling, src=in_refs[a]))
        sends[-1].start()
    for j, chip in enumerate(chips):
        for a in range(n):
            copy(a, 1 + j, (*chip, c), (x, y, c)).wait_recv()
            sends.append(copy(a, 4 + j, (*chip, c), sibling))
            sends[-1].start()
    for a in range(n):
        copy(a, 0, sibling, (x, y, c)).wait_recv()
        for j, chip in enumerate(chips):
            copy(a, 4 + j, (*chip, 1 - c), (x, y, c)).wait_recv()
    for cp in sends:
        cp.wait_send()
    for local in started:
        local.wait()


def _gather_two_level(arrays, name):
    n = len(arrays)
    per = GATHER_SEMS

    def body(*refs):
        x, y, c = lax.axis_index("x"), lax.axis_index("y"), lax.axis_index("c")
        _gather_copies(refs[:n], refs[n:2 * n], *refs[2 * n:], x, y, c)

    any_spec = pl.BlockSpec(memory_space=pl.ANY)
    return pl.pallas_call(
        body, name=name, in_specs=[any_spec] * n, out_specs=[any_spec] * n,
        out_shape=[jax.ShapeDtypeStruct((N_DEV,) + a.shape, a.dtype) for a in arrays],
        scratch_shapes=[pltpu.SemaphoreType.DMA((n * per,)), pltpu.SemaphoreType.DMA((n * per,)),
                        pltpu.SemaphoreType.DMA((n,))],
        compiler_params=pltpu.CompilerParams(has_side_effects=True),
    )(*arrays)


def _handshake(barrier, peers):
    for peer in peers:
        pl.semaphore_signal(barrier, inc=1, device_id=peer, device_id_type=MESH_ID)
    pl.semaphore_wait(barrier, len(peers))


def _on_sequencer(name, collective_id, arrays, out_structs, sem_counts, peers, copies):
    hbm = pltpu.MemorySpace.HBM
    in_refs = [jax.new_ref(a, memory_space=hbm) for a in arrays]
    out_refs = [jax.empty_ref(s, memory_space=hbm) for s in out_structs]

    @pl.kernel(mesh=plsc.ScalarSubcoreMesh(axis_name="seq", num_cores=1), name=name,
               scratch_types=tuple(pltpu.SemaphoreType.DMA((k,)) for k in sem_counts),
               compiler_params=pltpu.CompilerParams(collective_id=collective_id))
    def launch(*sems):
        x, y, c = lax.axis_index("x"), lax.axis_index("y"), lax.axis_index("c")
        _handshake(pltpu.get_barrier_semaphore(), peers(x, y, c))
        copies(in_refs, out_refs, *sems, x, y, c)

    launch()
    return [o[...] for o in out_refs]


def _all_others(x, y, c):
    return [(x ^ ((k >> 2) & 1), y ^ ((k >> 1) & 1), c ^ (k & 1)) for k in range(1, N_DEV)]


def _sc_gather(arrays, name, collective_id):
    n = len(arrays)
    outs = [jax.ShapeDtypeStruct((N_DEV,) + a.shape, a.dtype) for a in arrays]
    return _on_sequencer(name, collective_id, arrays, outs, (n * GATHER_SEMS, n * GATHER_SEMS, n), _all_others,
                         _gather_copies)


def _sc_chip_exchange(sums, name, collective_id):
    n = len(sums)
    per = N_CHIP - 1
    outs = [jax.ShapeDtypeStruct(s.shape, s.dtype) for s in sums]
    return _on_sequencer(name, collective_id, sums, outs, (n * per, n * per, n),
                         lambda x, y, c: [(px, py, c) for px, py in _other_chips(x, y)], _chip_exchange_copies)


def _sibling_swap_copies(in_refs, out_refs, send_sems, recv_sems, x, y, c):
    sends = []
    for a in range(len(in_refs)):
        for q in range(N_CHIP):
            cp = pltpu.make_async_remote_copy(
                src_ref=in_refs[a].at[2 * q + 1 - c], dst_ref=out_refs[a].at[q],
                send_sem=send_sems.at[a * N_CHIP + q], recv_sem=recv_sems.at[a * N_CHIP + q],
                device_id=(x, y, 1 - c), device_id_type=MESH_ID)
            cp.start()
            sends.append(cp)
    for cp in sends:
        cp.wait_recv()
    for cp in sends:
        cp.wait_send()


def _sc_sibling_swap(parts, name, collective_id):
    n = len(parts)
    outs = [jax.ShapeDtypeStruct((N_CHIP,) + p.shape[1:], p.dtype) for p in parts]
    return _on_sequencer(name, collective_id, parts, outs, (n * N_CHIP, n * N_CHIP),
                         lambda x, y, c: [(x, y, 1 - c)], _sibling_swap_copies)


def _chip_sum(part, landed, core, name):
    _, r, c = part.shape
    tc = _pick(c, (1024, 512, 256, 128)) if c % 128 == 0 else c
    tm = _pick(r, (256, 128, 64, 32, 16, 8))

    def body(core_ref, p_ref, l_ref, o_ref):
        o_ref[...] = (p_ref[...].astype(F32) + l_ref[...].astype(F32)).astype(o_ref.dtype)

    grid_spec = pltpu.PrefetchScalarGridSpec(
        num_scalar_prefetch=1, grid=(N_CHIP, r // tm, c // tc),
        in_specs=[pl.BlockSpec((None, tm, tc), lambda q, i, j, core_ref: (2 * q + core_ref[0], i, j)),
                  pl.BlockSpec((None, tm, tc), lambda q, i, j, core_ref: (q, i, j))],
        out_specs=pl.BlockSpec((None, tm, tc), lambda q, i, j, core_ref: (q, i, j)))
    return pl.pallas_call(
        body, name=name, grid_spec=grid_spec, out_shape=jax.ShapeDtypeStruct(landed.shape, landed.dtype),
        compiler_params=_params(("parallel", "parallel", "parallel")),
    )(core, part, landed)


def _chip_exchange_copies(in_refs, out_refs, send_sems, recv_sems, local_sems, x, y, c):
    n = len(in_refs)
    per = N_CHIP - 1
    mine = 2 * x + y
    started = []
    for a in range(n):
        local = pltpu.make_async_copy(in_refs[a].at[mine], out_refs[a].at[mine], local_sems.at[a])
        local.start()
        started.append(local)
    sends = []
    for j, (px, py) in enumerate(_other_chips(x, y)):
        for a in range(n):
            cp = pltpu.make_async_remote_copy(
                src_ref=in_refs[a].at[2 * px + py], dst_ref=out_refs[a].at[mine],
                send_sem=send_sems.at[a * per + j], recv_sem=recv_sems.at[a * per + j],
                device_id=(px, py, c), device_id_type=MESH_ID)
            cp.start()
            sends.append((cp, a, j, 2 * px + py))
    for cp, a, j, peer in sends:
        pltpu.make_async_remote_copy(
            src_ref=out_refs[a].at[peer], dst_ref=out_refs[a].at[peer],
            send_sem=send_sems.at[a * per + j], recv_sem=recv_sems.at[a * per + j],
            device_id=(x, y, c), device_id_type=MESH_ID).wait_recv()
    for cp, _, _, _ in sends:
        cp.wait_send()
    for local in started:
        local.wait()


DEPTH = 4
MEM_W = 1024
MIX_W = SSD_TOK + MEM_W
DT_PAD = LANES - SSD_H


def _is_ssd(i):
    return i % 2 == 0


def _weight_names():
    names = ["mem_norm_g", "final_norm_g"]
    for i in range(DEPTH):
        names += [f"norm_g_{i}", f"w_in_{i}"]
        if _is_ssd(i):
            names += [f"conv_w_{i}", f"conv_b_{i}", f"dt_bias_{i}", f"a_log_{i}", f"d_skip_{i}", f"ssd_norm_g_{i}"]
        names += [f"w_mem_kv_{i}", f"w_out_{i}"]
    return names


WEIGHTS = _weight_names()
INPUTS = ["x", "mem"] + WEIGHTS + ["loss_target"] + ["m_" + n for n in WEIGHTS] + ["v_" + n for n in WEIGHTS]


def _in_segments(i):
    if _is_ssd(i):
        return [("xbc", 0, SSD_XBC), ("dt", SSD_XBC, SSD_H), ("qm", SSD_XBC + SSD_H, MEM_W),
                ("z", SSD_XBC + SSD_H + MEM_W, MIX_W)]
    segs = []
    for g in range(N_DIL):
        for j, nm in enumerate("qkv"):
            segs.append((f"{nm}{g}", (3 * g + j) * ATT_TOK, ATT_TOK))
    segs += [("qm", 3 * N_DIL * ATT_TOK, MEM_W), ("z", 3 * N_DIL * ATT_TOK + MEM_W, MIX_W)]
    return segs


def _split_w_in(i, w_in):
    out = {}
    for nm, start, width in _in_segments(i):
        seg = w_in[:, start:start + width]
        out[nm] = jnp.pad(seg, ((0, 0), (0, DT_PAD))) if nm == "dt" else seg
    return out


def _join_dw_in(i, dws):
    return jnp.concatenate([dws[nm][:, :width] for nm, _, width in _in_segments(i)], axis=1)


SEG_DTYPE = {"xbc": F32, "dt": F32, "z": F32}


def _layer_dils(i):
    return (1,) if _is_ssd(i) else DILS


def _seg_order(nm):
    return int(nm[1]) if nm[0] in "qkv" and nm[1:].isdigit() else 0


def _layer_fwd(i, x, mem_b, p):
    tag = f"l{i}"
    hs = _rmsnorm_fwd(x, p["norm_g"], tag + "_norm", _layer_dils(i))
    proj = {nm: _matmul(hs[_seg_order(nm)], w, out_dtype=SEG_DTYPE.get(nm, BF16), name=f"{tag}_in_{nm}")
            for nm, w in p["win"].items()}
    sv = {"x": x, "h": hs, "proj": proj}
    if _is_ssd(i):
        xbc = _conv_fwd(proj["xbc"], p["conv_w"], p["conv_b"], tag + "_conv")
        y_tok, hs = _ssd_fwd(xbc, proj["dt"], p["dt_bias_p"], p["a_log_p"], p["dskip_lane"], tag + "_ssd")
        sv.update(xbc=xbc, hs=hs)
    else:
        outs, lses = [], []
        for g in range(N_DIL):
            o, lse = _dil_fwd(proj[f"q{g}"], proj[f"k{g}"], proj[f"v{g}"], g, f"{tag}_att{g}")
            outs.append(o)
            lses.append(lse)
        y_tok = _combine_fwd(outs, lses, tag + "_comb")
        sv.update(outs=outs, lses=lses)
    mkv = _matmul(mem_b, p["wmkv"], out_dtype=BF16, name=tag + "_mkv")
    y_mem = _memattn_fwd(proj["qm"], mkv, tag + "_mem")
    gated = _gate_fwd(y_tok, y_mem, proj["z"], p.get("ssd_norm_g"), tag + "_gate")
    x_out = _matmul(gated, p["wout"], out_dtype=F32, add=x, name=tag + "_out")
    sv.update(y_tok=y_tok, y_mem=y_mem, mkv=mkv, gated=gated)
    return x_out, sv


def _layer_bwd(i, sv, dx_out, dxb_out, dmem_n, mem_b, p):
    tag = f"l{i}b"
    proj = sv["proj"]
    gr = {}
    dgated = _matmul(dxb_out, p["wout"], tb=True, out_dtype=F32, name=tag + "_dgated")
    gr["w_out"] = _matmul(sv["gated"], dxb_out, ta=True, out_dtype=BF16, name=tag + "_dwout")
    dy_tok, dy_mem, dz, dssd_g = _gate_bwd(sv["y_tok"], sv["y_mem"], proj["z"], p.get("ssd_norm_g"), dgated, tag + "_gate")
    dq_mem, dmkv = _memattn_bwd(proj["qm"], sv["mkv"], dy_mem, tag + "_mem")
    gr["w_mem_kv"] = _matmul(mem_b, dmkv, ta=True, out_dtype=BF16, name=tag + "_dwmkv")
    dmem_n = _matmul(dmkv, p["wmkv"], tb=True, out_dtype=F32, add=dmem_n, name=tag + "_dmem")
    dproj = {"qm": dq_mem, "z": dz}
    if _is_ssd(i):
        dxbc, ddt_raw, dbias, dalog, ddsk = _ssd_bwd(sv["xbc"], proj["dt"], p["dt_bias_p"], p["a_log_p"], p["dskip_lane"],
                                                     sv["hs"], dy_tok, tag + "_ssd")
        dpre, dconv_w, dconv_b = _conv_bwd_pre(proj["xbc"], p["conv_w"], p["conv_b"], dxbc, tag + "_convpre")
        dproj["xbc"] = _conv_bwd_in(dpre, p["conv_w"], tag + "_convin")
        dproj["dt"] = ddt_raw
        gr.update(conv_w=dconv_w, conv_b=dconv_b[0], dt_bias=dbias[0, :SSD_H], a_log=dalog[0, :SSD_H],
                  d_skip=jnp.sum(ddsk.reshape(SSD_H, SSD_P), axis=1), ssd_norm_g=dssd_g[0])
    else:
        dos, cs = _combine_bwd(sv["outs"], sv["lses"], dy_tok, tag + "_comb")
        for g in range(N_DIL):
            dq, dk, dv = _dil_bwd(proj[f"q{g}"], proj[f"k{g}"], proj[f"v{g}"], dos[g], cs[g], sv["lses"][g], g,
                                  f"{tag}_att{g}")
            dproj.update({f"q{g}": dq, f"k{g}": dk, f"v{g}": dv})
    dils = _layer_dils(i)
    dhs = [None] * len(dils)
    dws = {}
    for nm, w in p["win"].items():
        o = _seg_order(nm)
        dhs[o] = _matmul(dproj[nm], w, tb=True, out_dtype=F32, add=dhs[o], name=f"{tag}_dh_{nm}")
        dws[nm] = _matmul(sv["h"][o], dproj[nm], ta=True, out_dtype=BF16, name=f"{tag}_dw_{nm}")
    gr["w_in"] = _join_dw_in(i, dws)
    dx, dxb, dnorm_g = _rmsnorm_bwd(sv["x"], p["norm_g"], dhs, dx_out, tag + "_norm", dils)
    gr["norm_g"] = dnorm_g[0]
    return dx, dxb, dmem_n, gr


def _pad_heads(v):
    return jnp.pad(v.reshape(1, SSD_H), ((0, 0), (0, DT_PAD)))


def _layer_params(i, small, w_in, w_mem_kv, w_out):
    p = {"norm_g": small[f"norm_g_{i}"], "win": _split_w_in(i, w_in), "wmkv": w_mem_kv, "wout": w_out}
    if _is_ssd(i):
        p.update(conv_w=small[f"conv_w_{i}"], conv_b=small[f"conv_b_{i}"], ssd_norm_g=small[f"ssd_norm_g_{i}"],
                 dt_bias_p=_pad_heads(small[f"dt_bias_{i}"]), a_log_p=_pad_heads(small[f"a_log_{i}"]),
                 dskip_lane=jnp.repeat(small[f"d_skip_{i}"], SSD_P).reshape(1, SSD_TOK))
    return p


def _local_step(x, mem, target, small, weights, emit):
    mem_b = _rmsnorm_fwd(mem, small["mem_norm_g"], "mem_norm")[0]
    params, saved = [], []
    for i in range(DEPTH):
        big, x = weights(i, x)
        params.append(_layer_params(i, small, *big))
        x, sv = _layer_fwd(i, x, mem_b, params[i])
        saved.append(sv)
    loss, dx, dxb, dfinal = _final_loss(x, small["final_norm_g"], target)
    grads = {"final_norm_g": dfinal[0]}
    dmem_n = None
    for i in reversed(range(DEPTH)):
        dx, dxb, dmem_n, gr = _layer_bwd(i, saved[i], dx, dxb, dmem_n, mem_b, params[i])
        dx, dxb = emit(i, {nm: gr.pop(nm) for nm in BIG}, dx, dxb)
        grads.update({f"{nm}_{i}": g for nm, g in gr.items()})
    _, _, dmem_g = _rmsnorm_bwd(mem, small["mem_norm_g"], [dmem_n], None, "mem_norm_b")
    grads["mem_norm_g"] = dmem_g[0]
    return loss[0, 0], dx, grads


BIG = ("w_in", "w_mem_kv", "w_out")
SMALL = [n for n in WEIGHTS if not n.startswith(BIG)]
PACK_ROWS = 8 * LANES
GATHER_COLLECTIVE_ID = 0
SCATTER_COLLECTIVE_ID = 4
SWAP_COLLECTIVE_ID = 8


def _pack(vals):
    flat = jnp.concatenate([v.reshape(-1).astype(F32) for v in vals])
    padded = -(-flat.shape[0] // PACK_ROWS) * PACK_ROWS
    return jnp.pad(flat, (0, padded - flat.shape[0])).reshape(padded // LANES, LANES)


def _train_step(a, local_step):
    x, y, c = lax.axis_index("x"), lax.axis_index("y"), lax.axis_index("c")
    me = 4 * x + 2 * y + c
    gathered_w = {}
    conv_names = [n for n in SMALL if n.startswith("conv_w")]
    conv_full = _gather_two_level([a[n] for n in conv_names], "gather_conv")

    def gather(i, after):
        shards = [a[f"{nm}_{i}"].astype(BF16) for nm in BIG]
        if after:
            shards = lax.optimization_barrier((*shards, *after))[:len(shards)]
        if i == 0:
            gathered_w[i] = _gather_two_level(shards, f"gather_w{i}")
        else:
            gathered_w[i] = _sc_gather(shards, f"gather_w{i}", GATHER_COLLECTIVE_ID + i)

    def weights(i, act):
        if i == 0:
            gather(0, ())
            gather(1, (gathered_w[0][0], conv_full[0]))
        elif i == 1:
            gather(2, (act,))
            gather(3, (act,))
        *raw, act = lax.optimization_barrier((*gathered_w.pop(i), act))
        g_in, g_kv, g_out = raw
        whole = (jnp.transpose(g_in, (1, 0, 2)).reshape(g_in.shape[1], N_DEV * g_in.shape[2]),
                 g_kv.reshape(N_DEV * g_kv.shape[1], g_kv.shape[2]),
                 g_out.reshape(N_DEV * g_out.shape[1], g_out.shape[2]))
        return whole, act

    small = {n: a[n] for n in SMALL}
    for n, gathered in zip(conv_names, conv_full):
        small[n] = jnp.transpose(gathered, (1, 0, 2)).reshape(gathered.shape[1], N_DEV * gathered.shape[2])

    core = c.astype(jnp.int32).reshape(1)
    landed = {}

    def reduce_scatter(i, gr, dx, dxb):
        d, cols = gr["w_in"].shape
        parts = [jnp.transpose(gr["w_in"].reshape(d, N_DEV, cols // N_DEV), (1, 0, 2))]
        for nm in BIG[1:]:
            parts.append(gr[nm].reshape(N_DEV, gr[nm].shape[0] // N_DEV, gr[nm].shape[1]))
        swapped = _sc_sibling_swap(parts, f"swap_w{i}", SWAP_COLLECTIVE_ID + i)
        sums = [_chip_sum(p, s, core, f"chipsum_{nm}_{i}") for nm, p, s in zip(BIG, parts, swapped)]
        landed[i] = _sc_chip_exchange(sums, f"scatter_w{i}", SCATTER_COLLECTIVE_ID + i)
        if i == 1:
            tied = lax.optimization_barrier((dx, dxb, *landed[3], *landed[2]))
            dx, dxb = tied[:2]
            landed[3], landed[2] = list(tied[2:2 + len(BIG)]), list(tied[2 + len(BIG):])
        return dx, dxb

    loss_local, grad_x, grads = local_step(a["x"][0], a["mem"][0], a["loss_target"][0], small, weights, reduce_scatter)
    loss = lax.psum(loss_local, ("x", "y", "c"))

    res = {}
    for i in reversed(range(DEPTH)):
        for nm, p in zip(BIG, landed[i]):
            n = f"{nm}_{i}"
            res[n] = _adamw(p, a[n], a["m_" + n], a["v_" + n], "adamw_" + n)

    gathered = _gather_two_level([_pack([grads[n] for n in SMALL])], "gather_small")[0]
    zero_conv = lambda pre: [jnp.zeros(small[n].shape, F32) if n in conv_names else a[pre + n] for n in SMALL]
    packed = _adamw(gathered, _pack(zero_conv("")), _pack(zero_conv("m_")), _pack(zero_conv("v_")), "adamw_small")
    off = 0
    for n in SMALL:
        size = math.prod(small[n].shape)
        if n in conv_names:
            rows, cols = a[n].shape
            whole = gathered.reshape(N_DEV, -1)[:, off:off + size].reshape(N_DEV, rows, N_DEV * cols)
            mine = lax.dynamic_slice_in_dim(whole, me * cols, cols, axis=2)
            res[n] = _adamw(mine, a[n], a["m_" + n], a["v_" + n], "adamw_" + n)
        else:
            res[n] = [o.reshape(-1)[off:off + size].reshape(a[n].shape) for o in packed]
        off += size
    outs = [loss, grad_x[None]]
    for k in range(4):
        outs += [res[n][k] for n in WEIGHTS]
    return tuple(outs)


def kernel(x, mem, mem_norm_g, final_norm_g, norm_g_0, w_in_0, conv_w_0, conv_b_0, dt_bias_0, a_log_0, d_skip_0, ssd_norm_g_0, w_mem_kv_0, w_out_0, norm_g_1, w_in_1, w_mem_kv_1, w_out_1, norm_g_2, w_in_2, conv_w_2, conv_b_2, dt_bias_2, a_log_2, d_skip_2, ssd_norm_g_2, w_mem_kv_2, w_out_2, norm_g_3, w_in_3, w_mem_kv_3, w_out_3, loss_target, m_mem_norm_g, m_final_norm_g, m_norm_g_0, m_w_in_0, m_conv_w_0, m_conv_b_0, m_dt_bias_0, m_a_log_0, m_d_skip_0, m_ssd_norm_g_0, m_w_mem_kv_0, m_w_out_0, m_norm_g_1, m_w_in_1, m_w_mem_kv_1, m_w_out_1, m_norm_g_2, m_w_in_2, m_conv_w_2, m_conv_b_2, m_dt_bias_2, m_a_log_2, m_d_skip_2, m_ssd_norm_g_2, m_w_mem_kv_2, m_w_out_2, m_norm_g_3, m_w_in_3, m_w_mem_kv_3, m_w_out_3, v_mem_norm_g, v_final_norm_g, v_norm_g_0, v_w_in_0, v_conv_w_0, v_conv_b_0, v_dt_bias_0, v_a_log_0, v_d_skip_0, v_ssd_norm_g_0, v_w_mem_kv_0, v_w_out_0, v_norm_g_1, v_w_in_1, v_w_mem_kv_1, v_w_out_1, v_norm_g_2, v_w_in_2, v_conv_w_2, v_conv_b_2, v_dt_bias_2, v_a_log_2, v_d_skip_2, v_ssd_norm_g_2, v_w_mem_kv_2, v_w_out_2, v_norm_g_3, v_w_in_3, v_w_mem_kv_3, v_w_out_3):
    vals = locals()
    return _train_step({n: vals[n] for n in INPUTS}, _local_step)
```

```python
import functools
import math

import jax
import jax.numpy as jnp
from jax import lax
from jax.experimental import pallas as pl
from jax.experimental.pallas import tpu as pltpu
from jax.experimental.pallas import tpu_sc as plsc

F32 = jnp.float32
BF16 = jnp.bfloat16
EPS = 1e-6
N_DEV = 8
VMEM_LIMIT_BYTES = 56 * 1024 * 1024


def _pick(n, prefs):
    for p in prefs:
        if n % p == 0:
            return p
    return n


def _params(sem):
    return pltpu.CompilerParams(dimension_semantics=sem, vmem_limit_bytes=VMEM_LIMIT_BYTES)


def _matmul(a, b, *, ta=False, tb=False, out_dtype=F32, add=None, name="mm"):
    if ta:
        k_dim, m_dim = a.shape
    else:
        m_dim, k_dim = a.shape
    n_dim = b.shape[0] if tb else b.shape[1]
    out_bytes = jnp.dtype(out_dtype).itemsize + (0 if add is None else add.dtype.itemsize)
    tm, tn, tk = _matmul_tiles(m_dim, n_dim, k_dim, a.dtype.itemsize, b.dtype.itemsize, out_bytes)
    nk = k_dim // tk
    dims = (((0,) if ta else (1,), (1,) if tb else (0,)), ((), ()))

    def body(*refs):
        if add is None:
            a_ref, b_ref, o_ref = refs[:3]
            add_ref = None
        else:
            a_ref, b_ref, add_ref, o_ref = refs[:4]
        part = lax.dot_general(a_ref[...].astype(BF16), b_ref[...].astype(BF16), dims,
                               preferred_element_type=F32)
        if nk == 1:
            o_ref[...] = (part if add_ref is None else part + add_ref[...].astype(F32)).astype(o_ref.dtype)
            return
        acc_ref = refs[-1]
        k = pl.program_id(2)

        @pl.when(k == 0)
        def _():
            acc_ref[...] = part if add_ref is None else part + add_ref[...].astype(F32)

        @pl.when(k > 0)
        def _():
            acc_ref[...] += part

        @pl.when(k == nk - 1)
        def _():
            o_ref[...] = acc_ref[...].astype(o_ref.dtype)

    a_spec = pl.BlockSpec((tk, tm), lambda i, j, k: (k, i)) if ta else pl.BlockSpec((tm, tk), lambda i, j, k: (i, k))
    b_spec = pl.BlockSpec((tn, tk), lambda i, j, k: (j, k)) if tb else pl.BlockSpec((tk, tn), lambda i, j, k: (k, j))
    o_spec = pl.BlockSpec((tm, tn), lambda i, j, k: (i, j))
    in_specs = [a_spec, b_spec] + ([o_spec] if add is not None else [])
    args = (a, b) + ((add,) if add is not None else ())
    return pl.pallas_call(
        body, name=name, grid=(m_dim // tm, n_dim // tn, nk),
        in_specs=in_specs, out_specs=o_spec,
        out_shape=jax.ShapeDtypeStruct((m_dim, n_dim), out_dtype),
        scratch_shapes=[pltpu.VMEM((tm, tn), F32)] if nk > 1 else [],
        compiler_params=_params(("parallel", "parallel", "arbitrary")),
    )(*args)


MATMUL_VMEM_BUDGET = 40 * 1024 * 1024


def _matmul_tiles(m_dim, n_dim, k_dim, a_bytes, b_bytes, out_bytes):
    best = None
    for tk in (k_dim, 4096, 2048, 1024, 512, 256, 128):
        if tk > k_dim or k_dim % tk:
            continue
        for tm in (1024, 512, 256, 128):
            if m_dim % tm:
                continue
            for tn in (2048, 1024, 512, 256, 128):
                if n_dim % tn:
                    continue
                vmem = 2 * (tm * tk * a_bytes + tk * tn * b_bytes + tm * tn * out_bytes) + 2 * tm * tn * 4
                if a_bytes == 4:
                    vmem += tm * tk * 2
                if vmem > MATMUL_VMEM_BUDGET:
                    continue
                score = (tm * tn * tk, tk, min(tm, tn))
                if best is None or score > best[0]:
                    best = (score, (tm, tn, tk))
    return best[1]


def _iota(shape, dim):
    return lax.broadcasted_iota(jnp.int32, shape, dim)


def _col(x, j):
    return jnp.sum(jnp.where(_iota(x.shape, 1) == j, x, 0.0), axis=1, keepdims=True)


def _silu(x):
    return x * jax.nn.sigmoid(x)


def _dsilu(x):
    s = jax.nn.sigmoid(x)
    return s * (1.0 + x * (1.0 - s))


def _chunks(width):
    return width // 128


def _scatter_rows(src_ref, nat_ref, dil, tm):
    n = tm // dil
    for cb in range(nat_ref.shape[0]):
        for r in range(dil):
            nat_ref[cb, pl.ds(r, n, stride=dil), :] = src_ref[r, :, cb * 128:(cb + 1) * 128].astype(F32)


def _gather_rows(nat_ref, dst_ref, dil, tm):
    n = tm // dil
    for cb in range(nat_ref.shape[0]):
        for r in range(dil):
            dst_ref[r, :, cb * 128:(cb + 1) * 128] = nat_ref[cb, pl.ds(r, n, stride=dil), :].astype(dst_ref.dtype)


def _load_chunks(nat_ref):
    return jnp.concatenate([nat_ref[cb] for cb in range(nat_ref.shape[0])], axis=1)


def _store_chunks(nat_ref, val):
    for cb in range(nat_ref.shape[0]):
        nat_ref[cb] = val[:, cb * 128:(cb + 1) * 128]


def _perm_spec(tm, dil, width):
    if dil == 1:
        return pl.BlockSpec((tm, width), lambda i: (i, 0))
    return pl.BlockSpec((dil, tm // dil, width), lambda i: (0, i, 0))


def _perm_shape(t, dil, width, dtype):
    return jax.ShapeDtypeStruct((t, width) if dil == 1 else (dil, t // dil, width), dtype)


def _rmsnorm_fwd(x, g, name, dils=(1,)):
    t, d = x.shape
    tm = _pick(t, (512, 256, 128))

    permuted = any(dil > 1 for dil in dils)

    def body(x_ref, g_ref, *refs):
        h_refs = refs[:len(dils)]
        xv = x_ref[...]
        rs = lax.rsqrt(jnp.mean(xv * xv, axis=-1, keepdims=True) + EPS)
        hv = xv * rs * g_ref[...]
        if permuted:
            _store_chunks(refs[-1], hv)
        for dil, h_ref in zip(dils, h_refs):
            if dil == 1:
                h_ref[...] = hv.astype(BF16)
            else:
                _gather_rows(refs[-1], h_ref, dil, tm)

    outs = pl.pallas_call(
        body, name=name, grid=(t // tm,),
        in_specs=[pl.BlockSpec((tm, d), lambda i: (i, 0)), pl.BlockSpec((1, d), lambda i: (0, 0))],
        out_specs=[_perm_spec(tm, dil, d) for dil in dils],
        out_shape=[_perm_shape(t, dil, d, BF16) for dil in dils],
        scratch_shapes=[pltpu.VMEM((_chunks(d), tm, 128), F32)] if permuted else [],
        compiler_params=_params(("parallel",)),
    )(x, g.reshape(1, d))
    return [o.reshape(t, d) for o in outs]


def _rmsnorm_bwd(x, g, dhs, dres, name, dils=(1,)):
    t, d = x.shape
    tm = _pick(t, (512, 256, 128) if len(dils) == 1 else (256, 128))
    n_in = len(dils)

    def body(*refs):
        x_ref, g_ref = refs[:2]
        dh_refs = refs[2:2 + n_in]
        dres_ref = refs[2 + n_in] if dres is not None else None
        dx_ref, dxb_ref, dg_ref = refs[-4:-1]
        nat_ref = refs[-1]
        dhv = None
        for dil, dh_ref in zip(dils, dh_refs):
            if dil == 1:
                term = dh_ref[...].astype(F32)
            else:
                _scatter_rows(dh_ref, nat_ref, dil, tm)
                term = _load_chunks(nat_ref)
            dhv = term if dhv is None else dhv + term
        xv = x_ref[...]
        r = lax.rsqrt(jnp.mean(xv * xv, axis=-1, keepdims=True) + EPS)
        xhat = xv * r
        dxh = dhv * g_ref[...]
        dx = r * (dxh - xhat * jnp.mean(dxh * xhat, axis=-1, keepdims=True))
        if dres_ref is not None:
            dx = dx + dres_ref[...]
        dx_ref[...] = dx
        dxb_ref[...] = dx.astype(BF16)
        part = jnp.sum(dhv * xhat, axis=0, keepdims=True)

        @pl.when(pl.program_id(0) == 0)
        def _():
            dg_ref[...] = part

        @pl.when(pl.program_id(0) > 0)
        def _():
            dg_ref[...] += part

    row = pl.BlockSpec((tm, d), lambda i: (i, 0))
    vec = pl.BlockSpec((1, d), lambda i: (0, 0))
    in_specs = [row, vec] + [_perm_spec(tm, dil, d) for dil in dils] + ([row] if dres is not None else [])
    dh_args = [dh if dil == 1 else dh.reshape(dil, t // dil, d) for dil, dh in zip(dils, dhs)]
    args = (x, g.reshape(1, d), *dh_args) + ((dres,) if dres is not None else ())
    return pl.pallas_call(
        body, name=name, grid=(t // tm,), in_specs=in_specs, out_specs=[row, row, vec],
        out_shape=[jax.ShapeDtypeStruct((t, d), F32), jax.ShapeDtypeStruct((t, d), BF16),
                   jax.ShapeDtypeStruct((1, d), F32)],
        scratch_shapes=[pltpu.VMEM((_chunks(d), tm, 128), F32)],
        compiler_params=_params(("arbitrary",)),
    )(*args)


def _final_loss(x, g, target, name="final_loss"):
    t, d = x.shape
    tm = _pick(t, (512, 256, 128))

    def body(x_ref, g_ref, t_ref, loss_ref, dx_ref, dxb_ref, dg_ref):
        xv = x_ref[...]
        gv = g_ref[...]
        r = lax.rsqrt(jnp.mean(xv * xv, axis=-1, keepdims=True) + EPS)
        xhat = xv * r
        e = xhat * gv - t_ref[...]
        lpart = jnp.zeros((1, 128), F32) + (0.5 / d) * jnp.sum(e * e)
        dy = e * (1.0 / d)
        dxh = dy * gv
        dx = r * (dxh - xhat * jnp.mean(dxh * xhat, axis=-1, keepdims=True))
        dx_ref[...] = dx
        dxb_ref[...] = dx.astype(BF16)
        gpart = jnp.sum(dy * xhat, axis=0, keepdims=True)

        @pl.when(pl.program_id(0) == 0)
        def _():
            dg_ref[...] = gpart
            loss_ref[...] = lpart

        @pl.when(pl.program_id(0) > 0)
        def _():
            dg_ref[...] += gpart
            loss_ref[...] += lpart

    row = pl.BlockSpec((tm, d), lambda i: (i, 0))
    vec = pl.BlockSpec((1, d), lambda i: (0, 0))
    return pl.pallas_call(
        body, name=name, grid=(t // tm,), in_specs=[row, vec, row],
        out_specs=[pl.BlockSpec((1, 128), lambda i: (0, 0)), row, row, vec],
        out_shape=[jax.ShapeDtypeStruct((1, 128), F32), jax.ShapeDtypeStruct((t, d), F32),
                   jax.ShapeDtypeStruct((t, d), BF16), jax.ShapeDtypeStruct((1, d), F32)],
        compiler_params=_params(("arbitrary",)),
    )(x, g.reshape(1, d), target)


CONV_K = 4
HALO = 8


def _shift_down(cur, prev8, s):
    rolled = pltpu.roll(cur, s, 0)
    fix = pltpu.roll(prev8, s, 0)
    head = jnp.where(_iota((HALO, cur.shape[1]), 0) < s, fix, rolled[:HALO])
    return jnp.concatenate([head, rolled[HALO:]], axis=0)


def _shift_up(cur, next8, s):
    n = cur.shape[0]
    rolled = pltpu.roll(cur, n - s, 0)
    fix = pltpu.roll(next8, HALO - s, 0)
    tail = jnp.where(_iota((HALO, cur.shape[1]), 0) >= HALO - s, fix, rolled[n - HALO:])
    return jnp.concatenate([rolled[:n - HALO], tail], axis=0)


def _conv_pre(u_ref, up_ref, w_ref, b_ref, first):
    cur = u_ref[...]
    prev8 = jnp.where(first, 0.0, up_ref[...])
    w = w_ref[...]
    shifted = [cur] + [_shift_down(cur, prev8, s) for s in (1, 2, 3)]
    pre = b_ref[...] + sum(w[CONV_K - 1 - s:CONV_K - s, :] * shifted[s] for s in range(CONV_K))
    return pre, shifted


def _conv_specs(tm, tc):
    nb = tm // HALO
    cur = pl.BlockSpec((tm, tc), lambda j, i: (i, j))
    prev = pl.BlockSpec((HALO, tc), lambda j, i: (jnp.maximum(i * nb - 1, 0), j))
    wspec = pl.BlockSpec((CONV_K, tc), lambda j, i: (0, j))
    bspec = pl.BlockSpec((1, tc), lambda j, i: (0, j))
    return cur, prev, wspec, bspec


def _conv_fwd(u, w, b, name):
    t, c = u.shape
    tm, tc = _pick(t, (512, 256, 128)), _pick(c, (1024, 512, 256, 128))
    cur, prev, wspec, bspec = _conv_specs(tm, tc)

    def body(u_ref, up_ref, w_ref, b_ref, o_ref):
        pre, _ = _conv_pre(u_ref, up_ref, w_ref, b_ref, pl.program_id(1) == 0)
        o_ref[...] = _silu(pre)

    return pl.pallas_call(
        body, name=name, grid=(c // tc, t // tm), in_specs=[cur, prev, wspec, bspec], out_specs=cur,
        out_shape=jax.ShapeDtypeStruct((t, c), F32),
        compiler_params=_params(("parallel", "parallel")),
    )(u, u, w, b.reshape(1, c))


def _conv_bwd_pre(u, w, b, dy, name):
    t, c = u.shape
    tm, tc = _pick(t, (512, 256, 128)), _pick(c, (1024, 512, 256, 128))
    cur, prev, wspec, bspec = _conv_specs(tm, tc)

    def body(u_ref, up_ref, w_ref, b_ref, dy_ref, dpre_ref, dw_ref, db_ref):
        i = pl.program_id(1)
        pre, shifted = _conv_pre(u_ref, up_ref, w_ref, b_ref, i == 0)
        dpre = dy_ref[...] * _dsilu(pre)
        dpre_ref[...] = dpre
        dw = jnp.concatenate([jnp.sum(dpre * shifted[CONV_K - 1 - k], axis=0, keepdims=True) for k in range(CONV_K)], axis=0)
        db = jnp.sum(dpre, axis=0, keepdims=True)

        @pl.when(i == 0)
        def _():
            dw_ref[...] = dw
            db_ref[...] = db

        @pl.when(i > 0)
        def _():
            dw_ref[...] += dw
            db_ref[...] += db

    return pl.pallas_call(
        body, name=name, grid=(c // tc, t // tm), in_specs=[cur, prev, wspec, bspec, cur],
        out_specs=[cur, wspec, bspec],
        out_shape=[jax.ShapeDtypeStruct((t, c), F32), jax.ShapeDtypeStruct((CONV_K, c), F32),
                   jax.ShapeDtypeStruct((1, c), F32)],
        compiler_params=_params(("parallel", "arbitrary")),
    )(u, u, w, b.reshape(1, c), dy)


def _conv_bwd_in(dpre, w, name):
    t, c = dpre.shape
    tm, tc = _pick(t, (512, 256, 128)), _pick(c, (1024, 512, 256, 128))
    nb = tm // HALO
    last = t // tm - 1
    cur = pl.BlockSpec((tm, tc), lambda j, i: (i, j))
    nxt = pl.BlockSpec((HALO, tc), lambda j, i: (jnp.minimum((i + 1) * nb, t // HALO - 1), j))
    wspec = pl.BlockSpec((CONV_K, tc), lambda j, i: (0, j))

    def body(d_ref, dn_ref, w_ref, o_ref):
        cur_v = d_ref[...]
        next8 = jnp.where(pl.program_id(1) == last, 0.0, dn_ref[...])
        wv = w_ref[...]
        acc = wv[CONV_K - 1:CONV_K, :] * cur_v
        for s in (1, 2, 3):
            acc = acc + wv[CONV_K - 1 - s:CONV_K - s, :] * _shift_up(cur_v, next8, s)
        o_ref[...] = acc.astype(o_ref.dtype)

    return pl.pallas_call(
        body, name=name, grid=(c // tc, t // tm), in_specs=[cur, nxt, wspec], out_specs=cur,
        out_shape=jax.ShapeDtypeStruct((t, c), BF16),
        compiler_params=_params(("parallel", "parallel")),
    )(dpre, dpre, w)


MEM_HEADS = 4
NT_DIMS = (((1,), (1,)), ((), ()))
TN_DIMS = (((0,), (0,)), ((), ()))


def _dot(a, b, dims=None):
    if dims is None:
        return jnp.dot(a, b, preferred_element_type=F32)
    return lax.dot_general(a, b, dims, preferred_element_type=F32)


def _memattn_probs(q, mk, scale):
    s = _dot(q, mk, NT_DIMS) * scale
    s = s - jnp.max(s, axis=-1, keepdims=True)
    p = jnp.exp(s)
    return p / jnp.sum(p, axis=-1, keepdims=True)


def _memattn_fwd(q, mkv, name):
    t, wd = q.shape
    m = mkv.shape[0]
    hd = wd // MEM_HEADS
    scale = hd ** -0.5
    tm = _pick(t, (512, 256, 128))

    def body(q_ref, mkv_ref, o_ref):
        for h in range(MEM_HEADS):
            cols = slice(h * hd, (h + 1) * hd)
            p = _memattn_probs(q_ref[:, cols], mkv_ref[:, cols], scale)
            o_ref[:, cols] = _dot(p.astype(BF16), mkv_ref[:, wd + h * hd:wd + (h + 1) * hd])

    return pl.pallas_call(
        body, name=name, grid=(t // tm,),
        in_specs=[pl.BlockSpec((tm, wd), lambda i: (i, 0)), pl.BlockSpec((m, 2 * wd), lambda i: (0, 0))],
        out_specs=pl.BlockSpec((tm, wd), lambda i: (i, 0)),
        out_shape=jax.ShapeDtypeStruct((t, wd), F32),
        compiler_params=_params(("parallel",)),
    )(q, mkv)


def _memattn_bwd(q, mkv, dy, name):
    t, wd = q.shape
    m = mkv.shape[0]
    hd = wd // MEM_HEADS
    scale = hd ** -0.5
    tm = _pick(t, (512, 256, 128))

    def body(q_ref, mkv_ref, dy_ref, dq_ref, dmkv_ref):
        i = pl.program_id(0)

        @pl.when(i == 0)
        def _():
            dmkv_ref[...] = jnp.zeros_like(dmkv_ref)

        for h in range(MEM_HEADS):
            cols = slice(h * hd, (h + 1) * hd)
            vcols = slice(wd + h * hd, wd + (h + 1) * hd)
            qh = q_ref[:, cols]
            p = _memattn_probs(qh, mkv_ref[:, cols], scale)
            dyh = dy_ref[:, cols].astype(BF16)
            dp = _dot(dyh, mkv_ref[:, vcols], NT_DIMS)
            ds = (p * (dp - jnp.sum(dp * p, axis=-1, keepdims=True)) * scale).astype(BF16)
            dq_ref[:, cols] = _dot(ds, mkv_ref[:, cols]).astype(dq_ref.dtype)
            dmkv_ref[:, cols] += _dot(ds, qh, TN_DIMS)
            dmkv_ref[:, vcols] += _dot(p.astype(BF16), dyh, TN_DIMS)

    return pl.pallas_call(
        body, name=name, grid=(t // tm,),
        in_specs=[pl.BlockSpec((tm, wd), lambda i: (i, 0)), pl.BlockSpec((m, 2 * wd), lambda i: (0, 0)),
                  pl.BlockSpec((tm, wd), lambda i: (i, 0))],
        out_specs=[pl.BlockSpec((tm, wd), lambda i: (i, 0)), pl.BlockSpec((m, 2 * wd), lambda i: (0, 0))],
        out_shape=[jax.ShapeDtypeStruct((t, wd), BF16), jax.ShapeDtypeStruct((m, 2 * wd), F32)],
        compiler_params=_params(("arbitrary",)),
    )(q, mkv, dy)


NORM_GROUPS = 8


def _gate_fwd(y_tok, y_mem, z, norm_g, name):
    t, tok = y_tok.shape
    mem = y_mem.shape[1]
    mix = tok + mem
    gw = tok // NORM_GROUPS
    tm = _pick(t, (256, 128))

    def body(*refs):
        if norm_g is None:
            yt_ref, ym_ref, z_ref, o_ref = refs
        else:
            yt_ref, ym_ref, z_ref, g_ref, o_ref = refs
        u = yt_ref[...] * _silu(z_ref[:, :tok])
        if norm_g is None:
            o_ref[:, :tok] = u.astype(o_ref.dtype)
        else:
            for k in range(NORM_GROUPS):
                uk = u[:, k * gw:(k + 1) * gw]
                r = lax.rsqrt(jnp.mean(uk * uk, axis=-1, keepdims=True) + EPS)
                o_ref[:, k * gw:(k + 1) * gw] = (uk * r * g_ref[:, k * gw:(k + 1) * gw]).astype(o_ref.dtype)
        o_ref[:, tok:] = (ym_ref[...] * _silu(z_ref[:, tok:])).astype(o_ref.dtype)

    in_specs = [pl.BlockSpec((tm, tok), lambda i: (i, 0)), pl.BlockSpec((tm, mem), lambda i: (i, 0)),
                pl.BlockSpec((tm, mix), lambda i: (i, 0))]
    args = [y_tok, y_mem, z]
    if norm_g is not None:
        in_specs.append(pl.BlockSpec((1, tok), lambda i: (0, 0)))
        args.append(norm_g.reshape(1, tok))
    return pl.pallas_call(
        body, name=name, grid=(t // tm,), in_specs=in_specs,
        out_specs=pl.BlockSpec((tm, mix), lambda i: (i, 0)),
        out_shape=jax.ShapeDtypeStruct((t, mix), BF16),
        compiler_params=_params(("parallel",)),
    )(*args)


def _gate_bwd(y_tok, y_mem, z, norm_g, dgated, name):
    t, tok = y_tok.shape
    mem = y_mem.shape[1]
    mix = tok + mem
    gw = tok // NORM_GROUPS
    tm = _pick(t, (256, 128))

    def body(*refs):
        if norm_g is None:
            yt_ref, ym_ref, z_ref, dg_ref, dyt_ref, dym_ref, dz_ref, dn_ref = refs
        else:
            yt_ref, ym_ref, z_ref, dg_ref, g_ref, dyt_ref, dym_ref, dz_ref, dn_ref = refs
        i = pl.program_id(0)
        zt = z_ref[:, :tok]
        yt = yt_ref[...]
        sz = _silu(zt)
        dout = dg_ref[:, :tok].astype(F32)
        if norm_g is None:
            du = dout
            dn = jnp.zeros((1, tok), F32)
        else:
            u = yt * sz
            dus, dns = [], []
            for k in range(NORM_GROUPS):
                uk = u[:, k * gw:(k + 1) * gw]
                r = lax.rsqrt(jnp.mean(uk * uk, axis=-1, keepdims=True) + EPS)
                nk = uk * r
                dk = dout[:, k * gw:(k + 1) * gw]
                dns.append(jnp.sum(dk * nk, axis=0, keepdims=True))
                dnk = dk * g_ref[:, k * gw:(k + 1) * gw]
                dus.append(r * (dnk - nk * jnp.mean(dnk * nk, axis=-1, keepdims=True)))
            du = jnp.concatenate(dus, axis=1)
            dn = jnp.concatenate(dns, axis=1)
        dyt_ref[...] = du * sz
        dz_ref[:, :tok] = (du * yt * _dsilu(zt)).astype(dz_ref.dtype)
        zm = z_ref[:, tok:]
        dm = dg_ref[:, tok:].astype(F32)
        dym_ref[...] = dm * _silu(zm)
        dz_ref[:, tok:] = (dm * ym_ref[...] * _dsilu(zm)).astype(dz_ref.dtype)

        @pl.when(i == 0)
        def _():
            dn_ref[...] = dn

        @pl.when(i > 0)
        def _():
            dn_ref[...] += dn

    tok_spec = pl.BlockSpec((tm, tok), lambda i: (i, 0))
    mem_spec = pl.BlockSpec((tm, mem), lambda i: (i, 0))
    mix_spec = pl.BlockSpec((tm, mix), lambda i: (i, 0))
    vec = pl.BlockSpec((1, tok), lambda i: (0, 0))
    in_specs = [tok_spec, mem_spec, mix_spec, mix_spec]
    args = [y_tok, y_mem, z, dgated]
    if norm_g is not None:
        in_specs.append(vec)
        args.append(norm_g.reshape(1, tok))
    return pl.pallas_call(
        body, name=name, grid=(t // tm,), in_specs=in_specs,
        out_specs=[tok_spec, mem_spec, mix_spec, vec],
        out_shape=[jax.ShapeDtypeStruct((t, tok), F32), jax.ShapeDtypeStruct((t, mem), F32),
                   jax.ShapeDtypeStruct((t, mix), BF16), jax.ShapeDtypeStruct((1, tok), F32)],
        compiler_params=_params(("arbitrary",)),
    )(*args)


SSD_Q = 128
SSD_N = 128
SSD_P = 64
SSD_G = 8
SSD_HPG = 6
SSD_H = SSD_G * SSD_HPG
SSD_TOK = SSD_H * SSD_P
SSD_XBC = SSD_TOK + 2 * SSD_G * SSD_N
LANES = 128
HIGHEST = lax.Precision.HIGHEST


def _softplus(x):
    return jnp.maximum(x, 0.0) + jnp.log(1.0 + jnp.exp(-jnp.abs(x)))


def _ssd_common(dtr_ref, bias_ref, alog_ref):
    sq = (SSD_Q, LANES)
    pre = dtr_ref[...] + bias_ref[...]
    dt = _softplus(pre)
    a = -jnp.exp(alog_ref[...])
    tril = (_iota(sq, 0) >= _iota(sq, 1)).astype(F32)
    acs = jnp.dot(tril, dt * a, precision=HIGHEST, preferred_element_type=F32)
    return pre, dt, a, tril, acs, acs.T


def _pair_terms(dt, acs, acs_t, h0):
    hi = _iota((SSD_Q, LANES), 1) >= SSD_P
    heads = []
    for j in range(2):
        h = h0 + j
        a_col = _col(acs, h)
        a_row = acs_t[h:h + 1, :]
        a_last = _col(acs[SSD_Q - 1:SSD_Q, :], h)
        heads.append((h, a_col, a_row, a_last, hi if j else jnp.logical_not(hi)))
    dtl = jnp.where(hi, _col(dt, h0 + 1), _col(dt, h0))
    scale = jnp.where(hi, jnp.exp(heads[1][1]), jnp.exp(heads[0][1]))
    dec_last = jnp.where(hi[:1], jnp.exp(heads[1][3]), jnp.exp(heads[0][3]))
    return heads, dtl, scale, dec_last


def _decay(a_col, a_row):
    causal = _iota((SSD_Q, SSD_Q), 0) >= _iota((SSD_Q, SSD_Q), 1)
    return jnp.where(causal, jnp.exp(jnp.minimum(a_col - a_row, 0.0)), 0.0)


def _ssd_fwd(xbc, dt_raw, dt_bias, a_log, dskip_lane, name):
    t = xbc.shape[0]
    nc = t // SSD_Q

    def body(xbc_ref, dtr_ref, bias_ref, alog_ref, dsk_ref, y_ref, hs_ref, h_ref):
        @pl.when(pl.program_id(0) == 0)
        def _():
            h_ref[...] = jnp.zeros_like(h_ref)

        _, dt, _, _, acs, acs_t = _ssd_common(dtr_ref, bias_ref, alog_ref)
        for g in range(SSD_G):
            bg_f = xbc_ref[:, SSD_TOK + g * SSD_N:SSD_TOK + (g + 1) * SSD_N]
            bg = bg_f.astype(BF16)
            cg = xbc_ref[:, SSD_TOK + SSD_G * SSD_N + g * SSD_N:SSD_TOK + SSD_G * SSD_N + (g + 1) * SSD_N].astype(BF16)
            cb = _dot(cg, bg, NT_DIMS)
            for pr in range(SSD_HPG // 2):
                h0 = g * SSD_HPG + 2 * pr
                lanes = slice(h0 * SSD_P, (h0 + 2) * SSD_P)
                heads, dtl, scale, dec_last = _pair_terms(dt, acs, acs_t, h0)
                xs = xbc_ref[:, lanes]
                xdt = xs * dtl
                hp = h_ref[:, lanes]
                hs_ref[:, lanes] = hp
                y = _dot(cg, hp.astype(BF16)) * scale + dsk_ref[:, lanes] * xs
                snew = hp * dec_last
                for _, a_col, a_row, a_last, mask in heads:
                    xm = jnp.where(mask, xdt, 0.0).astype(BF16)
                    y = y + _dot((cb * _decay(a_col, a_row)).astype(BF16), xm)
                    bw = (bg_f * jnp.exp(a_last - a_col)).astype(BF16)
                    snew = snew + _dot(bw, xm, TN_DIMS)
                y_ref[:, lanes] = y
                h_ref[:, lanes] = snew

    row = lambda w: pl.BlockSpec((SSD_Q, w), lambda c: (c, 0))
    vec = lambda w: pl.BlockSpec((1, w), lambda c: (0, 0))
    return pl.pallas_call(
        body, name=name, grid=(nc,),
        in_specs=[row(SSD_XBC), row(LANES), vec(LANES), vec(LANES), vec(SSD_TOK)],
        out_specs=[row(SSD_TOK), row(SSD_TOK)],
        out_shape=[jax.ShapeDtypeStruct((t, SSD_TOK), F32), jax.ShapeDtypeStruct((nc * SSD_N, SSD_TOK), F32)],
        scratch_shapes=[pltpu.VMEM((SSD_N, SSD_TOK), F32)],
        compiler_params=_params(("arbitrary",)),
    )(xbc, dt_raw, dt_bias, a_log, dskip_lane)


def _ssd_bwd(xbc, dt_raw, dt_bias, a_log, dskip_lane, hs, dy, name):
    t = xbc.shape[0]
    nc = t // SSD_Q
    sq = (SSD_Q, LANES)

    def body(xbc_ref, dtr_ref, bias_ref, alog_ref, dsk_ref, hs_ref, dy_ref,
             dxbc_ref, ddtr_ref, dbias_ref, dalog_ref, ddsk_ref, dh_ref):
        first = pl.program_id(0) == 0

        @pl.when(first)
        def _():
            dh_ref[...] = jnp.zeros_like(dh_ref)
            dbias_ref[...] = jnp.zeros_like(dbias_ref)
            dalog_ref[...] = jnp.zeros_like(dalog_ref)
            ddsk_ref[...] = jnp.zeros_like(ddsk_ref)

        pre, dt, a, tril, acs, acs_t = _ssd_common(dtr_ref, bias_ref, alog_ref)
        lane = _iota(sq, 1)
        sub = _iota(sq, 0)
        causal = sub >= lane
        d_acs = jnp.zeros(sq, F32)
        d_acs_row = jnp.zeros(sq, F32)
        d_last = jnp.zeros((1, LANES), F32)
        ddt = jnp.zeros(sq, F32)
        for g in range(SSD_G):
            bcols = slice(SSD_TOK + g * SSD_N, SSD_TOK + (g + 1) * SSD_N)
            ccols = slice(SSD_TOK + SSD_G * SSD_N + g * SSD_N, SSD_TOK + SSD_G * SSD_N + (g + 1) * SSD_N)
            bg_f = xbc_ref[:, bcols]
            bg = bg_f.astype(BF16)
            cg = xbc_ref[:, ccols].astype(BF16)
            cb = _dot(cg, bg, NT_DIMS)
            dcb = jnp.zeros(sq, F32)
            dbg = jnp.zeros(sq, F32)
            dcg = jnp.zeros(sq, F32)
            for pr in range(SSD_HPG // 2):
                h0 = g * SSD_HPG + 2 * pr
                lanes = slice(h0 * SSD_P, (h0 + 2) * SSD_P)
                heads, dtl, scale, dec_last = _pair_terms(dt, acs, acs_t, h0)
                xs = xbc_ref[:, lanes]
                xdt = xs * dtl
                dyv = dy_ref[:, lanes]
                hp = hs_ref[:, lanes]
                dhn = dh_ref[:, lanes]
                hp_b = hp.astype(BF16)
                dys = (dyv * scale).astype(BF16)
                yoff_dy = dyv * _dot(cg, hp_b) * scale
                dcg = dcg + _dot(dys, hp_b, NT_DIMS)
                dhc = _dot(cg, dys, TN_DIMS)
                hh = dhn * hp
                dxdt = jnp.zeros(sq, F32)
                for h, a_col, a_row, a_last, mask in heads:
                    dec = _decay(a_col, a_row)
                    m = cb * dec
                    dym = jnp.where(mask, dyv, 0.0).astype(BF16)
                    xm = jnp.where(mask, xdt, 0.0).astype(BF16)
                    dhm = jnp.where(mask, dhn, 0.0).astype(BF16)
                    w = jnp.exp(a_last - a_col)
                    dxdt = dxdt + _dot(m.astype(BF16), dym, TN_DIMS) + _dot((bg_f * w).astype(BF16), dhm)
                    dm = jnp.where(causal, _dot(dym, xm, NT_DIMS), 0.0)
                    dcb = dcb + dm * dec
                    e = dm * m
                    gj = _dot(xm, dhm, NT_DIMS)
                    dbg = dbg + w * gj
                    wdw = w * jnp.sum(bg_f * gj, axis=1, keepdims=True)
                    col = (jnp.sum(e, axis=1, keepdims=True)
                           + jnp.sum(jnp.where(mask, yoff_dy, 0.0), axis=1, keepdims=True) - wdw)
                    d_acs = d_acs + jnp.where(lane == h, col, 0.0)
                    d_acs_row = d_acs_row + jnp.where(sub == h, jnp.sum(e, axis=0, keepdims=True), 0.0)
                    last = jnp.sum(wdw) + jnp.exp(a_last) * jnp.sum(jnp.where(mask, hh, 0.0))
                    d_last = d_last + jnp.where(lane[:1] == h, last, 0.0)
                dxbc_ref[:, lanes] = dxdt * dtl + dsk_ref[:, lanes] * dyv
                tt = dxdt * xs
                for h, _, _, _, mask in heads:
                    ddt = ddt + jnp.where(lane == h, jnp.sum(jnp.where(mask, tt, 0.0), axis=1, keepdims=True), 0.0)
                ddsk_ref[:, lanes] += jnp.sum(dyv * xs, axis=0, keepdims=True)
                dh_ref[:, lanes] = dhn * dec_last + dhc
            dcb_b = dcb.astype(BF16)
            dxbc_ref[:, bcols] = dbg + _dot(dcb_b, cg, TN_DIMS)
            dxbc_ref[:, ccols] = dcg + _dot(dcb_b, bg)
        d_tot = d_acs - d_acs_row.T + jnp.where(sub == SSD_Q - 1, d_last, 0.0)
        ddta = lax.dot_general(tril, d_tot, TN_DIMS, precision=HIGHEST, preferred_element_type=F32)
        ddt = ddt + ddta * a
        dalog_ref[...] += jnp.sum(ddta * dt, axis=0, keepdims=True) * a
        ddtr = ddt * jax.nn.sigmoid(pre)
        ddtr_ref[...] = ddtr
        dbias_ref[...] += jnp.sum(ddtr, axis=0, keepdims=True)

    rev = lambda w: pl.BlockSpec((SSD_Q, w), lambda i: (nc - 1 - i, 0))
    vec = lambda w: pl.BlockSpec((1, w), lambda i: (0, 0))
    return pl.pallas_call(
        body, name=name, grid=(nc,),
        in_specs=[rev(SSD_XBC), rev(LANES), vec(LANES), vec(LANES), vec(SSD_TOK), rev(SSD_TOK), rev(SSD_TOK)],
        out_specs=[rev(SSD_XBC), rev(LANES), vec(LANES), vec(LANES), vec(SSD_TOK)],
        out_shape=[jax.ShapeDtypeStruct((t, SSD_XBC), F32), jax.ShapeDtypeStruct((t, LANES), F32),
                   jax.ShapeDtypeStruct((1, LANES), F32), jax.ShapeDtypeStruct((1, LANES), F32),
                   jax.ShapeDtypeStruct((1, SSD_TOK), F32)],
        scratch_shapes=[pltpu.VMEM((SSD_N, SSD_TOK), F32)],
        compiler_params=_params(("arbitrary",)),
    )(xbc, dt_raw, dt_bias, a_log, dskip_lane, hs, dy)


ATT_E = 128
ATT_H = 24
ATT_W = 128
ATT_TOK = ATT_H * ATT_E
DILATED_GROUPS = ((128, 1), (512, 4), (2048, 16))
N_DIL = len(DILATED_GROUPS)
ALIBI_MAX_EXP = 8.0
MASKED = -1e30


def _alibi_slopes(group):
    n = N_DIL * ATT_H
    return [2.0 ** (-ALIBI_MAX_EXP * (group * ATT_H + h + 1) / n) for h in range(ATT_H)]


def _att_scores(qh, kk, rel, valid, slope_d):
    s = _dot(qh, kk, NT_DIMS) * (ATT_E ** -0.5) - slope_d * rel
    return jnp.where(valid, s, MASKED)


def _att_rel(j):
    shp = (ATT_W, 2 * ATT_W)
    kpos = _iota(shp, 1)
    rel = _iota(shp, 0) + ATT_W - kpos
    valid = (rel >= 0) & (rel <= ATT_W) & ((kpos >= ATT_W) | (j > 0))
    return rel.astype(F32), valid


def _dil_fwd(q, k, v, group, name):
    t = q.shape[0]
    dil = DILATED_GROUPS[group][1]
    slopes = _alibi_slopes(group)
    nb = t // dil // ATT_W

    def body(q_ref, kp_ref, kc_ref, vp_ref, vc_ref, o_ref, lse_ref):
        rel, valid = _att_rel(pl.program_id(1))
        lane = _iota((ATT_W, LANES), 1)
        lse_all = jnp.zeros((ATT_W, LANES), F32)
        for h in range(ATT_H):
            cols = slice(h * ATT_E, (h + 1) * ATT_E)
            kk = jnp.concatenate([kp_ref[:, cols], kc_ref[:, cols]], axis=0)
            vv = jnp.concatenate([vp_ref[:, cols], vc_ref[:, cols]], axis=0)
            s = _att_scores(q_ref[:, cols], kk, rel, valid, slopes[h] * dil)
            m = jnp.max(s, axis=-1, keepdims=True)
            p = jnp.exp(s - m)
            den = jnp.sum(p, axis=-1, keepdims=True)
            o_ref[:, cols] = _dot(p.astype(BF16), vv) / den
            lse_all = jnp.where(lane == h, m + jnp.log(den), lse_all)
        lse_ref[...] = lse_all

    cur = pl.BlockSpec((ATT_W, ATT_TOK), lambda r, j: (r * nb + j, 0))
    prev = pl.BlockSpec((ATT_W, ATT_TOK), lambda r, j: (r * nb + jnp.maximum(j - 1, 0), 0))
    small = pl.BlockSpec((ATT_W, LANES), lambda r, j: (r * nb + j, 0))
    return pl.pallas_call(
        body, name=name, grid=(dil, nb),
        in_specs=[cur, prev, cur, prev, cur], out_specs=[cur, small],
        out_shape=[jax.ShapeDtypeStruct((t, ATT_TOK), F32), jax.ShapeDtypeStruct((t, LANES), F32)],
        compiler_params=_params(("parallel", "parallel")),
    )(q, k, k, v, v)


def _dil_bwd(q, k, v, do, cterm, lse, group, name):
    t = q.shape[0]
    dil = DILATED_GROUPS[group][1]
    slopes = _alibi_slopes(group)
    nb = t // dil // ATT_W

    def body(q_ref, kp_ref, kc_ref, vp_ref, vc_ref, do_ref, c_ref, lse_ref,
             dq_ref, dk_ref, dv_ref, ck_ref, cv_ref):
        j = pl.program_id(1)

        @pl.when(j == 0)
        def _():
            ck_ref[...] = jnp.zeros_like(ck_ref)
            cv_ref[...] = jnp.zeros_like(cv_ref)

        @pl.when(j < nb)
        def _():
            rel, valid = _att_rel(j)
            cv_, lv = c_ref[...], lse_ref[...]
            for h in range(ATT_H):
                cols = slice(h * ATT_E, (h + 1) * ATT_E)
                qh = q_ref[:, cols]
                kk = jnp.concatenate([kp_ref[:, cols], kc_ref[:, cols]], axis=0)
                vv = jnp.concatenate([vp_ref[:, cols], vc_ref[:, cols]], axis=0)
                s = _att_scores(qh, kk, rel, valid, slopes[h] * dil)
                p = jnp.where(valid, jnp.exp(s - _col(lv, h)), 0.0)
                do = do_ref[:, cols]
                dp = _dot(do, vv, NT_DIMS)
                ds = (p * (dp + _col(cv_, h)) * (ATT_E ** -0.5)).astype(BF16)
                dq_ref[:, cols] = _dot(ds, kk).astype(dq_ref.dtype)
                dkk = _dot(ds, qh, TN_DIMS)
                dvv = _dot(p.astype(BF16), do, TN_DIMS)
                dk_ref[:, cols] = (ck_ref[:, cols] + dkk[:ATT_W]).astype(dk_ref.dtype)
                dv_ref[:, cols] = (cv_ref[:, cols] + dvv[:ATT_W]).astype(dv_ref.dtype)
                ck_ref[:, cols] = dkk[ATT_W:]
                cv_ref[:, cols] = dvv[ATT_W:]

        @pl.when(j == nb)
        def _():
            dk_ref[...] = ck_ref[...].astype(dk_ref.dtype)
            dv_ref[...] = cv_ref[...].astype(dv_ref.dtype)

    jq = lambda j: jnp.minimum(j, nb - 1)
    cur = pl.BlockSpec((ATT_W, ATT_TOK), lambda r, j: (r * nb + jq(j), 0))
    prev = pl.BlockSpec((ATT_W, ATT_TOK), lambda r, j: (r * nb + jnp.maximum(jq(j) - 1, 0), 0))
    small = pl.BlockSpec((ATT_W, LANES), lambda r, j: (r * nb + jq(j), 0))
    late = pl.BlockSpec((ATT_W, ATT_TOK), lambda r, j: (r * nb + jnp.maximum(j - 1, 0), 0))
    big = jax.ShapeDtypeStruct((t, ATT_TOK), BF16)
    return pl.pallas_call(
        body, name=name, grid=(dil, nb + 1),
        in_specs=[cur, prev, cur, prev, cur, cur, small, small], out_specs=[cur, late, late],
        out_shape=[big, big, big],
        scratch_shapes=[pltpu.VMEM((ATT_W, ATT_TOK), F32), pltpu.VMEM((ATT_W, ATT_TOK), F32)],
        compiler_params=_params(("parallel", "arbitrary")),
    )(q, k, k, v, v, do, cterm, lse)


def _combine_weights(lses):
    m = functools.reduce(jnp.maximum, lses)
    es = [jnp.exp(l - m) for l in lses]
    tot = functools.reduce(lambda a, b: a + b, es)
    return [e / tot for e in es]


DILS = tuple(d for _, d in DILATED_GROUPS)
COMBINE_ROWS = 256


def _by_residue(arr, dil):
    return arr if dil == 1 else arr.reshape(dil, arr.shape[0] // dil, arr.shape[1])


def _natural_lses(l_refs, small_refs, tm):
    vals = []
    for g, dil in enumerate(DILS):
        if dil == 1:
            vals.append(l_refs[g][...])
        else:
            _scatter_rows(l_refs[g], small_refs[g], dil, tm)
            vals.append(small_refs[g][0])
    return vals


def _combine_scratch(tm):
    return ([pltpu.VMEM((_chunks(ATT_TOK), tm, 128), F32)] * N_DIL + [pltpu.VMEM((1, tm, 128), F32)] * N_DIL)


def _combine_fwd(outs, lses, name):
    t = outs[0].shape[0]
    tm = COMBINE_ROWS

    def body(*refs):
        o_refs, l_refs, y_ref = refs[:N_DIL], refs[N_DIL:2 * N_DIL], refs[2 * N_DIL]
        big_refs, small_refs = refs[2 * N_DIL + 1:3 * N_DIL + 1], refs[3 * N_DIL + 1:]
        ws = _combine_weights(_natural_lses(l_refs, small_refs, tm))
        for g in range(1, N_DIL):
            _scatter_rows(o_refs[g], big_refs[g], DILS[g], tm)
        for h in range(ATT_H):
            cols = slice(h * ATT_E, (h + 1) * ATT_E)
            y_ref[:, cols] = (_col(ws[0], h) * o_refs[0][:, cols]
                              + sum(_col(ws[g], h) * big_refs[g][h] for g in range(1, N_DIL)))

    return pl.pallas_call(
        body, name=name, grid=(t // tm,),
        in_specs=[_perm_spec(tm, d, ATT_TOK) for d in DILS] + [_perm_spec(tm, d, LANES) for d in DILS],
        out_specs=pl.BlockSpec((tm, ATT_TOK), lambda i: (i, 0)),
        out_shape=jax.ShapeDtypeStruct((t, ATT_TOK), F32),
        scratch_shapes=_combine_scratch(tm),
        compiler_params=_params(("parallel",)),
    )(*[_by_residue(o, d) for o, d in zip(outs, DILS)], *[_by_residue(l, d) for l, d in zip(lses, DILS)])


def _combine_bwd(outs, lses, dy, name):
    t = outs[0].shape[0]
    tm = COMBINE_ROWS

    def body(*refs):
        o_refs, l_refs, dy_ref = refs[:N_DIL], refs[N_DIL:2 * N_DIL], refs[2 * N_DIL]
        do_refs, c_refs = refs[2 * N_DIL + 1:3 * N_DIL + 1], refs[3 * N_DIL + 1:4 * N_DIL + 1]
        big_refs, small_refs = refs[4 * N_DIL + 1:5 * N_DIL + 1], refs[5 * N_DIL + 1:]
        ws = _combine_weights(_natural_lses(l_refs, small_refs, tm))
        for g in range(1, N_DIL):
            _scatter_rows(o_refs[g], big_refs[g], DILS[g], tm)
        lane = _iota((tm, LANES), 1)
        sdw = jnp.zeros((tm, LANES), F32)
        for h in range(ATT_H):
            cols = slice(h * ATT_E, (h + 1) * ATT_E)
            dyh = dy_ref[:, cols]
            tot = _col(ws[0], h) * jnp.sum(dyh * o_refs[0][:, cols], axis=1, keepdims=True)
            for g in range(1, N_DIL):
                tot = tot + _col(ws[g], h) * jnp.sum(dyh * big_refs[g][h], axis=1, keepdims=True)
            sdw = jnp.where(lane == h, tot, sdw)
        for g, dil in enumerate(DILS):
            cterm = -ws[g] * sdw
            if dil == 1:
                c_refs[g][...] = cterm
            else:
                small_refs[g][0] = cterm
                _gather_rows(small_refs[g], c_refs[g], dil, tm)
            for h in range(ATT_H):
                cols = slice(h * ATT_E, (h + 1) * ATT_E)
                do = _col(ws[g], h) * dy_ref[:, cols]
                if dil == 1:
                    do_refs[g][:, cols] = do.astype(BF16)
                else:
                    big_refs[g][h] = do
            if dil > 1:
                _gather_rows(big_refs[g], do_refs[g], dil, tm)

    res = pl.pallas_call(
        body, name=name, grid=(t // tm,),
        in_specs=([_perm_spec(tm, d, ATT_TOK) for d in DILS] + [_perm_spec(tm, d, LANES) for d in DILS]
                  + [pl.BlockSpec((tm, ATT_TOK), lambda i: (i, 0))]),
        out_specs=[_perm_spec(tm, d, ATT_TOK) for d in DILS] + [_perm_spec(tm, d, LANES) for d in DILS],
        out_shape=[_perm_shape(t, d, ATT_TOK, BF16) for d in DILS] + [_perm_shape(t, d, LANES, F32) for d in DILS],
        scratch_shapes=_combine_scratch(tm),
        compiler_params=_params(("parallel",)),
    )(*[_by_residue(o, d) for o, d in zip(outs, DILS)], *[_by_residue(l, d) for l, d in zip(lses, DILS)], dy)
    return [a.reshape(t, ATT_TOK) for a in res[:N_DIL]], [a.reshape(t, LANES) for a in res[N_DIL:]]


ADAM_LR, ADAM_B1, ADAM_B2, ADAM_EPS, ADAM_WD, ADAM_STEP = 0.001, 0.9, 0.999, 1e-08, 0.01, 10


def _adamw(parts, w, m, v, name):
    r, c = w.shape
    n_parts = parts.shape[0]
    tc = _pick(c, (1024, 512, 256, 128)) if c % 128 == 0 else c
    tm = _pick(r, (128, 64, 32, 16, 8))

    def body(p_ref, w_ref, m_ref, v_ref, g_ref, d_ref, nm_ref, nv_ref):
        g = p_ref[0].astype(F32)
        for k in range(1, n_parts):
            g = g + p_ref[k].astype(F32)
        nm = ADAM_B1 * m_ref[...] + (1.0 - ADAM_B1) * g
        nv = ADAM_B2 * v_ref[...] + (1.0 - ADAM_B2) * (g * g)
        m_hat = nm / (1.0 - ADAM_B1 ** ADAM_STEP)
        v_hat = nv / (1.0 - ADAM_B2 ** ADAM_STEP)
        g_ref[...] = g
        d_ref[...] = -ADAM_LR * (m_hat / (jnp.sqrt(v_hat) + ADAM_EPS) + ADAM_WD * w_ref[...])
        nm_ref[...] = nm
        nv_ref[...] = nv

    blk = pl.BlockSpec((tm, tc), lambda i, j: (i, j))
    pblk = pl.BlockSpec((n_parts, tm, tc), lambda i, j: (0, i, j))
    return pl.pallas_call(
        body, name=name, grid=(r // tm, c // tc), in_specs=[pblk, blk, blk, blk], out_specs=[blk] * 4,
        out_shape=[jax.ShapeDtypeStruct((r, c), F32)] * 4,
        compiler_params=_params(("parallel", "parallel")),
    )(parts, w, m, v)


N_CHIP = 4
MESH_ID = pl.DeviceIdType.MESH


def _other_chips(x, y):
    return [(1 - x, y), (x, 1 - y), (1 - x, 1 - y)]


GATHER_SEMS = N_DEV - 1


def _gather_copies(in_refs, out_refs, send_sems, recv_sems, local_sems, x, y, c):
    n = len(in_refs)
    sibling = (x, y, 1 - c)
    chips = _other_chips(x, y)

    def copy(a, k, block, to, src=None):
        rows = out_refs[a].at[4 * block[0] + 2 * block[1] + block[2]]
        return pltpu.make_async_remote_copy(
            src_ref=rows if src is None else src, dst_ref=rows,
            send_sem=send_sems.at[a * GATHER_SEMS + k], recv_sem=recv_sems.at[a * GATHER_SEMS + k],
            device_id=to, device_id_type=MESH_ID)

    started = []
    for a in range(n):
        local = pltpu.make_async_copy(in_refs[a], out_refs[a].at[4 * x + 2 * y + c], local_sems.at[a])
        local.start()
        started.append(local)
    sends = []
    for j, chip in enumerate(chips):
        for a in range(n):
            sends.append(copy(a, 1 + j, (x, y, c), (*chip, c), src=in_refs[a]))
            sends[-1].start()
    for a in range(n):
        sends.append(copy(a, 0, (x, y, c), sibling, src=in_refs[a]))
        sends[-1].start()
    for j, chip in enumerate(chips):
        for a in range(n):
            copy(a, 1 + j, (*chip, c), (x, y, c)).wait_recv()
            sends.append(copy(a, 4 + j, (*chip, c), sibling))
            sends[-1].start()
    for a in range(n):
        copy(a, 0, sibling, (x, y, c)).wait_recv()
        for j, chip in enumerate(chips):
            copy(a, 4 + j, (*chip, 1 - c), (x, y, c)).wait_recv()
    for cp in sends:
        cp.wait_send()
    for local in started:
        local.wait()


def _gather_two_level(arrays, name):
    n = len(arrays)
    per = GATHER_SEMS

    def body(*refs):
        x, y, c = lax.axis_index("x"), lax.axis_index("y"), lax.axis_index("c")
        _gather_copies(refs[:n], refs[n:2 * n], *refs[2 * n:], x, y, c)

    any_spec = pl.BlockSpec(memory_space=pl.ANY)
    return pl.pallas_call(
        body, name=name, in_specs=[any_spec] * n, out_specs=[any_spec] * n,
        out_shape=[jax.ShapeDtypeStruct((N_DEV,) + a.shape, a.dtype) for a in arrays],
        scratch_shapes=[pltpu.SemaphoreType.DMA((n * per,)), pltpu.SemaphoreType.DMA((n * per,)),
                        pltpu.SemaphoreType.DMA((n,))],
        compiler_params=pltpu.CompilerParams(has_side_effects=True),
    )(*arrays)


def _handshake(barrier, peers):
    for peer in peers:
        pl.semaphore_signal(barrier, inc=1, device_id=peer, device_id_type=MESH_ID)
    pl.semaphore_wait(barrier, len(peers))


def _on_sequencer(name, collective_id, arrays, out_structs, sem_counts, peers, copies):
    hbm = pltpu.MemorySpace.HBM
    in_refs = [jax.new_ref(a, memory_space=hbm) for a in arrays]
    out_refs = [jax.empty_ref(s, memory_space=hbm) for s in out_structs]

    @pl.kernel(mesh=plsc.ScalarSubcoreMesh(axis_name="seq", num_cores=1), name=name,
               scratch_types=tuple(pltpu.SemaphoreType.DMA((k,)) for k in sem_counts),
               compiler_params=pltpu.CompilerParams(collective_id=collective_id))
    def launch(*sems):
        x, y, c = lax.axis_index("x"), lax.axis_index("y"), lax.axis_index("c")
        _handshake(pltpu.get_barrier_semaphore(), peers(x, y, c))
        copies(in_refs, out_refs, *sems, x, y, c)

    launch()
    return [o[...] for o in out_refs]


def _all_others(x, y, c):
    return [(x ^ ((k >> 2) & 1), y ^ ((k >> 1) & 1), c ^ (k & 1)) for k in range(1, N_DEV)]


def _sc_gather(arrays, name, collective_id):
    n = len(arrays)
    outs = [jax.ShapeDtypeStruct((N_DEV,) + a.shape, a.dtype) for a in arrays]
    return _on_sequencer(name, collective_id, arrays, outs, (n * GATHER_SEMS, n * GATHER_SEMS, n), _all_others,
                         _gather_copies)


def _sc_chip_exchange(sums, name, collective_id):
    n = len(sums)
    per = N_CHIP - 1
    outs = [jax.ShapeDtypeStruct(s.shape, s.dtype) for s in sums]
    return _on_sequencer(name, collective_id, sums, outs, (n * per, n * per, n),
                         lambda x, y, c: [(px, py, c) for px, py in _other_chips(x, y)], _chip_exchange_copies)


def _sibling_swap_copies(in_refs, out_refs, send_sems, recv_sems, x, y, c):
    sends = []
    for a in range(len(in_refs)):
        for q in range(N_CHIP):
            cp = pltpu.make_async_remote_copy(
                src_ref=in_refs[a].at[2 * q + 1 - c], dst_ref=out_refs[a].at[q],
                send_sem=send_sems.at[a * N_CHIP + q], recv_sem=recv_sems.at[a * N_CHIP + q],
                device_id=(x, y, 1 - c), device_id_type=MESH_ID)
            cp.start()
            sends.append(cp)
    for cp in sends:
        cp.wait_recv()
    for cp in sends:
        cp.wait_send()


def _sc_sibling_swap(parts, name, collective_id):
    n = len(parts)
    outs = [jax.ShapeDtypeStruct((N_CHIP,) + p.shape[1:], p.dtype) for p in parts]
    return _on_sequencer(name, collective_id, parts, outs, (n * N_CHIP, n * N_CHIP),
                         lambda x, y, c: [(x, y, 1 - c)], _sibling_swap_copies)


def _chip_sum(part, landed, core, name):
    _, r, c = part.shape
    tc = _pick(c, (1024, 512, 256, 128)) if c % 128 == 0 else c
    tm = _pick(r, (256, 128, 64, 32, 16, 8))

    def body(core_ref, p_ref, l_ref, o_ref):
        o_ref[...] = (p_ref[...].astype(F32) + l_ref[...].astype(F32)).astype(o_ref.dtype)

    grid_spec = pltpu.PrefetchScalarGridSpec(
        num_scalar_prefetch=1, grid=(N_CHIP, r // tm, c // tc),
        in_specs=[pl.BlockSpec((None, tm, tc), lambda q, i, j, core_ref: (2 * q + core_ref[0], i, j)),
                  pl.BlockSpec((None, tm, tc), lambda q, i, j, core_ref: (q, i, j))],
        out_specs=pl.BlockSpec((None, tm, tc), lambda q, i, j, core_ref: (q, i, j)))
    return pl.pallas_call(
        body, name=name, grid_spec=grid_spec, out_shape=jax.ShapeDtypeStruct(landed.shape, landed.dtype),
        compiler_params=_params(("parallel", "parallel", "parallel")),
    )(core, part, landed)


def _chip_exchange_copies(in_refs, out_refs, send_sems, recv_sems, local_sems, x, y, c):
    n = len(in_refs)
    per = N_CHIP - 1
    mine = 2 * x + y
    started = []
    for a in range(n):
        local = pltpu.make_async_copy(in_refs[a].at[mine], out_refs[a].at[mine], local_sems.at[a])
        local.start()
        started.append(local)
    sends = []
    for j, (px, py) in enumerate(_other_chips(x, y)):
        for a in range(n):
            cp = pltpu.make_async_remote_copy(
                src_ref=in_refs[a].at[2 * px + py], dst_ref=out_refs[a].at[mine],
                send_sem=send_sems.at[a * per + j], recv_sem=recv_sems.at[a * per + j],
                device_id=(px, py, c), device_id_type=MESH_ID)
            cp.start()
            sends.append((cp, a, j, 2 * px + py))
    for cp, a, j, peer in sends:
        pltpu.make_async_remote_copy(
            src_ref=out_refs[a].at[peer], dst_ref=out_refs[a].at[peer],
            send_sem=send_sems.at[a * per + j], recv_sem=recv_sems.at[a * per + j],
            device_id=(x, y, c), device_id_type=MESH_ID).wait_recv()
    for cp, _, _, _ in sends:
        cp.wait_send()
    for local in started:
        local.wait()


DEPTH = 4
MEM_W = 1024
MIX_W = SSD_TOK + MEM_W
DT_PAD = LANES - SSD_H


def _is_ssd(i):
    return i % 2 == 0


def _weight_names():
    names = ["mem_norm_g", "final_norm_g"]
    for i in range(DEPTH):
        names += [f"norm_g_{i}", f"w_in_{i}"]
        if _is_ssd(i):
            names += [f"conv_w_{i}", f"conv_b_{i}", f"dt_bias_{i}", f"a_log_{i}", f"d_skip_{i}", f"ssd_norm_g_{i}"]
        names += [f"w_mem_kv_{i}", f"w_out_{i}"]
    return names


WEIGHTS = _weight_names()
INPUTS = ["x", "mem"] + WEIGHTS + ["loss_target"] + ["m_" + n for n in WEIGHTS] + ["v_" + n for n in WEIGHTS]


def _in_segments(i):
    if _is_ssd(i):
        return [("xbc", 0, SSD_XBC), ("dt", SSD_XBC, SSD_H), ("qm", SSD_XBC + SSD_H, MEM_W),
                ("z", SSD_XBC + SSD_H + MEM_W, MIX_W)]
    segs = []
    for g in range(N_DIL):
        for j, nm in enumerate("qkv"):
            segs.append((f"{nm}{g}", (3 * g + j) * ATT_TOK, ATT_TOK))
    segs += [("qm", 3 * N_DIL * ATT_TOK, MEM_W), ("z", 3 * N_DIL * ATT_TOK + MEM_W, MIX_W)]
    return segs


def _split_w_in(i, w_in):
    out = {}
    for nm, start, width in _in_segments(i):
        seg = w_in[:, start:start + width]
        out[nm] = jnp.pad(seg, ((0, 0), (0, DT_PAD))) if nm == "dt" else seg
    return out


def _join_dw_in(i, dws):
    return jnp.concatenate([dws[nm][:, :width] for nm, _, width in _in_segments(i)], axis=1)


SEG_DTYPE = {"xbc": F32, "dt": F32, "z": F32}


def _layer_dils(i):
    return (1,) if _is_ssd(i) else DILS


def _seg_order(nm):
    return int(nm[1]) if nm[0] in "qkv" and nm[1:].isdigit() else 0


def _layer_fwd(i, x, mem_b, p):
    tag = f"l{i}"
    hs = _rmsnorm_fwd(x, p["norm_g"], tag + "_norm", _layer_dils(i))
    proj = {nm: _matmul(hs[_seg_order(nm)], w, out_dtype=SEG_DTYPE.get(nm, BF16), name=f"{tag}_in_{nm}")
            for nm, w in p["win"].items()}
    sv = {"x": x, "h": hs, "proj": proj}
    if _is_ssd(i):
        xbc = _conv_fwd(proj["xbc"], p["conv_w"], p["conv_b"], tag + "_conv")
        y_tok, hs = _ssd_fwd(xbc, proj["dt"], p["dt_bias_p"], p["a_log_p"], p["dskip_lane"], tag + "_ssd")
        sv.update(xbc=xbc, hs=hs)
    else:
        outs, lses = [], []
        for g in range(N_DIL):
            o, lse = _dil_fwd(proj[f"q{g}"], proj[f"k{g}"], proj[f"v{g}"], g, f"{tag}_att{g}")
            outs.append(o)
            lses.append(lse)
        y_tok = _combine_fwd(outs, lses, tag + "_comb")
        sv.update(outs=outs, lses=lses)
    mkv = _matmul(mem_b, p["wmkv"], out_dtype=BF16, name=tag + "_mkv")
    y_mem = _memattn_fwd(proj["qm"], mkv, tag + "_mem")
    gated = _gate_fwd(y_tok, y_mem, proj["z"], p.get("ssd_norm_g"), tag + "_gate")
    x_out = _matmul(gated, p["wout"], out_dtype=F32, add=x, name=tag + "_out")
    sv.update(y_tok=y_tok, y_mem=y_mem, mkv=mkv, gated=gated)
    return x_out, sv


def _layer_bwd(i, sv, dx_out, dxb_out, dmem_n, mem_b, p):
    tag = f"l{i}b"
    proj = sv["proj"]
    gr = {}
    dgated = _matmul(dxb_out, p["wout"], tb=True, out_dtype=BF16, name=tag + "_dgated")
    gr["w_out"] = _matmul(sv["gated"], dxb_out, ta=True, out_dtype=BF16, name=tag + "_dwout")
    dy_tok, dy_mem, dz, dssd_g = _gate_bwd(sv["y_tok"], sv["y_mem"], proj["z"], p.get("ssd_norm_g"), dgated, tag + "_gate")
    dq_mem, dmkv = _memattn_bwd(proj["qm"], sv["mkv"], dy_mem, tag + "_mem")
    gr["w_mem_kv"] = _matmul(mem_b, dmkv, ta=True, out_dtype=BF16, name=tag + "_dwmkv")
    dmem_n = _matmul(dmkv, p["wmkv"], tb=True, out_dtype=F32, add=dmem_n, name=tag + "_dmem")
    dproj = {"qm": dq_mem, "z": dz}
    if _is_ssd(i):
        dxbc, ddt_raw, dbias, dalog, ddsk = _ssd_bwd(sv["xbc"], proj["dt"], p["dt_bias_p"], p["a_log_p"], p["dskip_lane"],
                                                     sv["hs"], dy_tok, tag + "_ssd")
        dpre, dconv_w, dconv_b = _conv_bwd_pre(proj["xbc"], p["conv_w"], p["conv_b"], dxbc, tag + "_convpre")
        dproj["xbc"] = _conv_bwd_in(dpre, p["conv_w"], tag + "_convin")
        dproj["dt"] = ddt_raw
        gr.update(conv_w=dconv_w, conv_b=dconv_b[0], dt_bias=dbias[0, :SSD_H], a_log=dalog[0, :SSD_H],
                  d_skip=jnp.sum(ddsk.reshape(SSD_H, SSD_P), axis=1), ssd_norm_g=dssd_g[0])
    else:
        dos, cs = _combine_bwd(sv["outs"], sv["lses"], dy_tok, tag + "_comb")
        for g in range(N_DIL):
            dq, dk, dv = _dil_bwd(proj[f"q{g}"], proj[f"k{g}"], proj[f"v{g}"], dos[g], cs[g], sv["lses"][g], g,
                                  f"{tag}_att{g}")
            dproj.update({f"q{g}": dq, f"k{g}": dk, f"v{g}": dv})
    dils = _layer_dils(i)
    dhs = [None] * len(dils)
    dws = {}
    for nm, w in p["win"].items():
        o = _seg_order(nm)
        dhs[o] = _matmul(dproj[nm], w, tb=True, out_dtype=F32, add=dhs[o], name=f"{tag}_dh_{nm}")
        dws[nm] = _matmul(sv["h"][o], dproj[nm], ta=True, out_dtype=BF16, name=f"{tag}_dw_{nm}")
    gr["w_in"] = _join_dw_in(i, dws)
    dx, dxb, dnorm_g = _rmsnorm_bwd(sv["x"], p["norm_g"], dhs, dx_out, tag + "_norm", dils)
    gr["norm_g"] = dnorm_g[0]
    return dx, dxb, dmem_n, gr


def _pad_heads(v):
    return jnp.pad(v.reshape(1, SSD_H), ((0, 0), (0, DT_PAD)))


def _layer_params(i, small, w_in, w_mem_kv, w_out):
    p = {"norm_g": small[f"norm_g_{i}"], "win": _split_w_in(i, w_in), "wmkv": w_mem_kv, "wout": w_out}
    if _is_ssd(i):
        p.update(conv_w=small[f"conv_w_{i}"], conv_b=small[f"conv_b_{i}"], ssd_norm_g=small[f"ssd_norm_g_{i}"],
                 dt_bias_p=_pad_heads(small[f"dt_bias_{i}"]), a_log_p=_pad_heads(small[f"a_log_{i}"]),
                 dskip_lane=jnp.repeat(small[f"d_skip_{i}"], SSD_P).reshape(1, SSD_TOK))
    return p


def _local_step(x, mem, target, small, weights, emit):
    mem_b = _rmsnorm_fwd(mem, small["mem_norm_g"], "mem_norm")[0]
    params, saved = [], []
    for i in range(DEPTH):
        big, x = weights(i, x)
        params.append(_layer_params(i, small, *big))
        x, sv = _layer_fwd(i, x, mem_b, params[i])
        saved.append(sv)
    loss, dx, dxb, dfinal = _final_loss(x, small["final_norm_g"], target)
    grads = {"final_norm_g": dfinal[0]}
    dmem_n = None
    for i in reversed(range(DEPTH)):
        dx, dxb, dmem_n, gr = _layer_bwd(i, saved[i], dx, dxb, dmem_n, mem_b, params[i])
        dx, dxb = emit(i, {nm: gr.pop(nm) for nm in BIG}, dx, dxb)
        grads.update({f"{nm}_{i}": g for nm, g in gr.items()})
    _, _, dmem_g = _rmsnorm_bwd(mem, small["mem_norm_g"], [dmem_n], None, "mem_norm_b")
    grads["mem_norm_g"] = dmem_g[0]
    return loss[0, 0], dx, grads


BIG = ("w_in", "w_mem_kv", "w_out")
SMALL = [n for n in WEIGHTS if not n.startswith(BIG)]
PACK_ROWS = 8 * LANES
GATHER_COLLECTIVE_ID = 0
SCATTER_COLLECTIVE_ID = 4
SWAP_COLLECTIVE_ID = 8


def _pack(vals):
    flat = jnp.concatenate([v.reshape(-1).astype(F32) for v in vals])
    padded = -(-flat.shape[0] // PACK_ROWS) * PACK_ROWS
    return jnp.pad(flat, (0, padded - flat.shape[0])).reshape(padded // LANES, LANES)


def _train_step(a, local_step):
    x, y, c = lax.axis_index("x"), lax.axis_index("y"), lax.axis_index("c")
    me = 4 * x + 2 * y + c
    gathered_w = {}
    conv_names = [n for n in SMALL if n.startswith("conv_w")]
    conv_full = _gather_two_level([a[n] for n in conv_names], "gather_conv")

    def gather(i, after):
        shards = [a[f"{nm}_{i}"].astype(BF16) for nm in BIG]
        if after:
            shards = lax.optimization_barrier((*shards, *after))[:len(shards)]
        if i == 0:
            g_in = _gather_two_level(shards[:1], "gather_w0_in")
            rest = lax.optimization_barrier((*shards[1:], g_in[0], conv_full[0]))[:len(shards) - 1]
            gathered_w[i] = [*g_in, *_sc_gather(rest, "gather_w0_rest", GATHER_COLLECTIVE_ID)]
        else:
            gathered_w[i] = _sc_gather(shards, f"gather_w{i}", GATHER_COLLECTIVE_ID + i)

    def weights(i, act):
        if i == 0:
            gather(0, ())
            gather(1, (gathered_w[0][0], conv_full[0]))
        elif i == 1:
            gather(2, (act,))
            gather(3, (act,))
        raw = gathered_w.pop(i)
        if i > 0:
            *raw, act = lax.optimization_barrier((*raw, act))
        g_in, g_kv, g_out = raw
        whole = (jnp.transpose(g_in, (1, 0, 2)).reshape(g_in.shape[1], N_DEV * g_in.shape[2]),
                 g_kv.reshape(N_DEV * g_kv.shape[1], g_kv.shape[2]),
                 g_out.reshape(N_DEV * g_out.shape[1], g_out.shape[2]))
        return whole, act

    small = {n: a[n] for n in SMALL}
    for n, gathered in zip(conv_names, conv_full):
        small[n] = jnp.transpose(gathered, (1, 0, 2)).reshape(gathered.shape[1], N_DEV * gathered.shape[2])

    core = c.astype(jnp.int32).reshape(1)
    landed = {}

    def reduce_scatter(i, gr, dx, dxb):
        d, cols = gr["w_in"].shape
        parts = [jnp.transpose(gr["w_in"].reshape(d, N_DEV, cols // N_DEV), (1, 0, 2))]
        for nm in BIG[1:]:
            parts.append(gr[nm].reshape(N_DEV, gr[nm].shape[0] // N_DEV, gr[nm].shape[1]))
        swapped = _sc_sibling_swap(parts, f"swap_w{i}", SWAP_COLLECTIVE_ID + i)
        sums = [_chip_sum(p, s, core, f"chipsum_{nm}_{i}") for nm, p, s in zip(BIG, parts, swapped)]
        landed[i] = _sc_chip_exchange(sums, f"scatter_w{i}", SCATTER_COLLECTIVE_ID + i)
        if i == 1:
            tied = lax.optimization_barrier((dx, dxb, *landed[3], *landed[2]))
            dx, dxb = tied[:2]
            landed[3], landed[2] = list(tied[2:2 + len(BIG)]), list(tied[2 + len(BIG):])
        return dx, dxb

    loss_local, grad_x, grads = local_step(a["x"][0], a["mem"][0], a["loss_target"][0], small, weights, reduce_scatter)
    loss = lax.psum(loss_local, ("x", "y", "c"))

    res = {}
    for i in reversed(range(DEPTH)):
        for nm, p in zip(BIG, landed[i]):
            n = f"{nm}_{i}"
            res[n] = _adamw(p, a[n], a["m_" + n], a["v_" + n], "adamw_" + n)

    gathered = _gather_two_level([_pack([grads[n] for n in SMALL])], "gather_small")[0]
    zero_conv = lambda pre: [jnp.zeros(small[n].shape, F32) if n in conv_names else a[pre + n] for n in SMALL]
    packed = _adamw(gathered, _pack(zero_conv("")), _pack(zero_conv("m_")), _pack(zero_conv("v_")), "adamw_small")
    off = 0
    for n in SMALL:
        size = math.prod(small[n].shape)
        if n in conv_names:
            rows, cols = a[n].shape
            whole = gathered.reshape(N_DEV, -1)[:, off:off + size].reshape(N_DEV, rows, N_DEV * cols)
            mine = lax.dynamic_slice_in_dim(whole, me * cols, cols, axis=2)
            res[n] = _adamw(mine, a[n], a["m_" + n], a["v_" + n], "adamw_" + n)
        else:
            res[n] = [o.reshape(-1)[off:off + size].reshape(a[n].shape) for o in packed]
        off += size
    outs = [loss, grad_x[None]]
    for k in range(4):
        outs += [res[n][k] for n in WEIGHTS]
    return tuple(outs)


def kernel(x, mem, mem_norm_g, final_norm_g, norm_g_0, w_in_0, conv_w_0, conv_b_0, dt_bias_0, a_log_0, d_skip_0, ssd_norm_g_0, w_mem_kv_0, w_out_0, norm_g_1, w_in_1, w_mem_kv_1, w_out_1, norm_g_2, w_in_2, conv_w_2, conv_b_2, dt_bias_2, a_log_2, d_skip_2, ssd_norm_g_2, w_mem_kv_2, w_out_2, norm_g_3, w_in_3, w_mem_kv_3, w_out_3, loss_target, m_mem_norm_g, m_final_norm_g, m_norm_g_0, m_w_in_0, m_conv_w_0, m_conv_b_0, m_dt_bias_0, m_a_log_0, m_d_skip_0, m_ssd_norm_g_0, m_w_mem_kv_0, m_w_out_0, m_norm_g_1, m_w_in_1, m_w_mem_kv_1, m_w_out_1, m_norm_g_2, m_w_in_2, m_conv_w_2, m_conv_b_2, m_dt_bias_2, m_a_log_2, m_d_skip_2, m_ssd_norm_g_2, m_w_mem_kv_2, m_w_out_2, m_norm_g_3, m_w_in_3, m_w_mem_kv_3, m_w_out_3, v_mem_norm_g, v_final_norm_g, v_norm_g_0, v_w_in_0, v_conv_w_0, v_conv_b_0, v_dt_bias_0, v_a_log_0, v_d_skip_0, v_ssd_norm_g_0, v_w_mem_kv_0, v_w_out_0, v_norm_g_1, v_w_in_1, v_w_mem_kv_1, v_w_out_1, v_norm_g_2, v_w_in_2, v_conv_w_2, v_conv_b_2, v_dt_bias_2, v_a_log_2, v_d_skip_2, v_ssd_norm_g_2, v_w_mem_kv_2, v_w_out_2, v_norm_g_3, v_w_in_3, v_w_mem_kv_3, v_w_out_3):
    vals = locals()
    return _train_step({n: vals[n] for n in INPUTS}, _local_step)
```

```python
import functools
import math

import jax
import jax.numpy as jnp
from jax import lax
from jax.experimental import pallas as pl
from jax.experimental.pallas import tpu as pltpu
from jax.experimental.pallas import tpu_sc as plsc

F32 = jnp.float32
BF16 = jnp.bfloat16
EPS = 1e-6
N_DEV = 8
VMEM_LIMIT_BYTES = 56 * 1024 * 1024


def _pick(n, prefs):
    for p in prefs:
        if n % p == 0:
            return p
    return n


def _params(sem):
    return pltpu.CompilerParams(dimension_semantics=sem, vmem_limit_bytes=VMEM_LIMIT_BYTES)


def _matmul(a, b, *, ta=False, tb=False, out_dtype=F32, add=None, name="mm"):
    if ta:
        k_dim, m_dim = a.shape
    else:
        m_dim, k_dim = a.shape
    n_dim = b.shape[0] if tb else b.shape[1]
    out_bytes = jnp.dtype(out_dtype).itemsize + (0 if add is None else add.dtype.itemsize)
    tm, tn, tk = _matmul_tiles(m_dim, n_dim, k_dim, a.dtype.itemsize, b.dtype.itemsize, out_bytes)
    nk = k_dim // tk
    dims = (((0,) if ta else (1,), (1,) if tb else (0,)), ((), ()))

    def body(*refs):
        if add is None:
            a_ref, b_ref, o_ref = refs[:3]
            add_ref = None
        else:
            a_ref, b_ref, add_ref, o_ref = refs[:4]
        part = lax.dot_general(a_ref[...].astype(BF16), b_ref[...].astype(BF16), dims,
                               preferred_element_type=F32)
        if nk == 1:
            o_ref[...] = (part if add_ref is None else part + add_ref[...].astype(F32)).astype(o_ref.dtype)
            return
        acc_ref = refs[-1]
        k = pl.program_id(2)

        @pl.when(k == 0)
        def _():
            acc_ref[...] = part if add_ref is None else part + add_ref[...].astype(F32)

        @pl.when(k > 0)
        def _():
            acc_ref[...] += part

        @pl.when(k == nk - 1)
        def _():
            o_ref[...] = acc_ref[...].astype(o_ref.dtype)

    a_spec = pl.BlockSpec((tk, tm), lambda i, j, k: (k, i)) if ta else pl.BlockSpec((tm, tk), lambda i, j, k: (i, k))
    b_spec = pl.BlockSpec((tn, tk), lambda i, j, k: (j, k)) if tb else pl.BlockSpec((tk, tn), lambda i, j, k: (k, j))
    o_spec = pl.BlockSpec((tm, tn), lambda i, j, k: (i, j))
    in_specs = [a_spec, b_spec] + ([o_spec] if add is not None else [])
    args = (a, b) + ((add,) if add is not None else ())
    return pl.pallas_call(
        body, name=name, grid=(m_dim // tm, n_dim // tn, nk),
        in_specs=in_specs, out_specs=o_spec,
        out_shape=jax.ShapeDtypeStruct((m_dim, n_dim), out_dtype),
        scratch_shapes=[pltpu.VMEM((tm, tn), F32)] if nk > 1 else [],
        compiler_params=_params(("parallel", "parallel", "arbitrary")),
    )(*args)


MATMUL_VMEM_BUDGET = 40 * 1024 * 1024


def _matmul_tiles(m_dim, n_dim, k_dim, a_bytes, b_bytes, out_bytes):
    best = None
    for tk in (k_dim, 4096, 2048, 1024, 512, 256, 128):
        if tk > k_dim or k_dim % tk:
            continue
        for tm in (1024, 512, 256, 128):
            if m_dim % tm:
                continue
            for tn in (2048, 1024, 512, 256, 128):
                if n_dim % tn:
                    continue
                vmem = 2 * (tm * tk * a_bytes + tk * tn * b_bytes + tm * tn * out_bytes) + 2 * tm * tn * 4
                if a_bytes == 4:
                    vmem += tm * tk * 2
                if vmem > MATMUL_VMEM_BUDGET:
                    continue
                score = (tm * tn * tk, tk, min(tm, tn))
                if best is None or score > best[0]:
                    best = (score, (tm, tn, tk))
    return best[1]


def _iota(shape, dim):
    return lax.broadcasted_iota(jnp.int32, shape, dim)


def _col(x, j):
    return jnp.sum(jnp.where(_iota(x.shape, 1) == j, x, 0.0), axis=1, keepdims=True)


def _silu(x):
    return x * jax.nn.sigmoid(x)


def _dsilu(x):
    s = jax.nn.sigmoid(x)
    return s * (1.0 + x * (1.0 - s))


def _chunks(width):
    return width // 128


def _scatter_rows(src_ref, nat_ref, dil, tm):
    n = tm // dil
    for cb in range(nat_ref.shape[0]):
        for r in range(dil):
            nat_ref[cb, pl.ds(r, n, stride=dil), :] = src_ref[r, :, cb * 128:(cb + 1) * 128].astype(F32)


def _gather_rows(nat_ref, dst_ref, dil, tm):
    n = tm // dil
    for cb in range(nat_ref.shape[0]):
        for r in range(dil):
            dst_ref[r, :, cb * 128:(cb + 1) * 128] = nat_ref[cb, pl.ds(r, n, stride=dil), :].astype(dst_ref.dtype)


def _load_chunks(nat_ref):
    return jnp.concatenate([nat_ref[cb] for cb in range(nat_ref.shape[0])], axis=1)


def _store_chunks(nat_ref, val):
    for cb in range(nat_ref.shape[0]):
        nat_ref[cb] = val[:, cb * 128:(cb + 1) * 128]


def _perm_spec(tm, dil, width):
    if dil == 1:
        return pl.BlockSpec((tm, width), lambda i: (i, 0))
    return pl.BlockSpec((dil, tm // dil, width), lambda i: (0, i, 0))


def _perm_shape(t, dil, width, dtype):
    return jax.ShapeDtypeStruct((t, width) if dil == 1 else (dil, t // dil, width), dtype)


def _rmsnorm_fwd(x, g, name, dils=(1,)):
    t, d = x.shape
    tm = _pick(t, (512, 256, 128))

    permuted = any(dil > 1 for dil in dils)

    def body(x_ref, g_ref, *refs):
        h_refs = refs[:len(dils)]
        xv = x_ref[...]
        rs = lax.rsqrt(jnp.mean(xv * xv, axis=-1, keepdims=True) + EPS)
        hv = xv * rs * g_ref[...]
        if permuted:
            _store_chunks(refs[-1], hv)
        for dil, h_ref in zip(dils, h_refs):
            if dil == 1:
                h_ref[...] = hv.astype(BF16)
            else:
                _gather_rows(refs[-1], h_ref, dil, tm)

    outs = pl.pallas_call(
        body, name=name, grid=(t // tm,),
        in_specs=[pl.BlockSpec((tm, d), lambda i: (i, 0)), pl.BlockSpec((1, d), lambda i: (0, 0))],
        out_specs=[_perm_spec(tm, dil, d) for dil in dils],
        out_shape=[_perm_shape(t, dil, d, BF16) for dil in dils],
        scratch_shapes=[pltpu.VMEM((_chunks(d), tm, 128), F32)] if permuted else [],
        compiler_params=_params(("parallel",)),
    )(x, g.reshape(1, d))
    return [o.reshape(t, d) for o in outs]


def _rmsnorm_bwd(x, g, dhs, dres, name, dils=(1,)):
    t, d = x.shape
    tm = _pick(t, (512, 256, 128) if len(dils) == 1 else (256, 128))
    n_in = len(dils)

    def body(*refs):
        x_ref, g_ref = refs[:2]
        dh_refs = refs[2:2 + n_in]
        dres_ref = refs[2 + n_in] if dres is not None else None
        dx_ref, dxb_ref, dg_ref = refs[-4:-1]
        nat_ref = refs[-1]
        dhv = None
        for dil, dh_ref in zip(dils, dh_refs):
            if dil == 1:
                term = dh_ref[...].astype(F32)
            else:
                _scatter_rows(dh_ref, nat_ref, dil, tm)
                term = _load_chunks(nat_ref)
            dhv = term if dhv is None else dhv + term
        xv = x_ref[...]
        r = lax.rsqrt(jnp.mean(xv * xv, axis=-1, keepdims=True) + EPS)
        xhat = xv * r
        dxh = dhv * g_ref[...]
        dx = r * (dxh - xhat * jnp.mean(dxh * xhat, axis=-1, keepdims=True))
        if dres_ref is not None:
            dx = dx + dres_ref[...]
        dx_ref[...] = dx
        dxb_ref[...] = dx.astype(BF16)
        part = jnp.sum(dhv * xhat, axis=0, keepdims=True)

        @pl.when(pl.program_id(0) == 0)
        def _():
            dg_ref[...] = part

        @pl.when(pl.program_id(0) > 0)
        def _():
            dg_ref[...] += part

    row = pl.BlockSpec((tm, d), lambda i: (i, 0))
    vec = pl.BlockSpec((1, d), lambda i: (0, 0))
    in_specs = [row, vec] + [_perm_spec(tm, dil, d) for dil in dils] + ([row] if dres is not None else [])
    dh_args = [dh if dil == 1 else dh.reshape(dil, t // dil, d) for dil, dh in zip(dils, dhs)]
    args = (x, g.reshape(1, d), *dh_args) + ((dres,) if dres is not None else ())
    return pl.pallas_call(
        body, name=name, grid=(t // tm,), in_specs=in_specs, out_specs=[row, row, vec],
        out_shape=[jax.ShapeDtypeStruct((t, d), F32), jax.ShapeDtypeStruct((t, d), BF16),
                   jax.ShapeDtypeStruct((1, d), F32)],
        scratch_shapes=[pltpu.VMEM((_chunks(d), tm, 128), F32)],
        compiler_params=_params(("arbitrary",)),
    )(*args)


def _final_loss(x, g, target, name="final_loss"):
    t, d = x.shape
    tm = _pick(t, (512, 256, 128))

    def body(x_ref, g_ref, t_ref, loss_ref, dx_ref, dxb_ref, dg_ref):
        xv = x_ref[...]
        gv = g_ref[...]
        r = lax.rsqrt(jnp.mean(xv * xv, axis=-1, keepdims=True) + EPS)
        xhat = xv * r
        e = xhat * gv - t_ref[...]
        lpart = jnp.zeros((1, 128), F32) + (0.5 / d) * jnp.sum(e * e)
        dy = e * (1.0 / d)
        dxh = dy * gv
        dx = r * (dxh - xhat * jnp.mean(dxh * xhat, axis=-1, keepdims=True))
        dx_ref[...] = dx
        dxb_ref[...] = dx.astype(BF16)
        gpart = jnp.sum(dy * xhat, axis=0, keepdims=True)

        @pl.when(pl.program_id(0) == 0)
        def _():
            dg_ref[...] = gpart
            loss_ref[...] = lpart

        @pl.when(pl.program_id(0) > 0)
        def _():
            dg_ref[...] += gpart
            loss_ref[...] += lpart

    row = pl.BlockSpec((tm, d), lambda i: (i, 0))
    vec = pl.BlockSpec((1, d), lambda i: (0, 0))
    return pl.pallas_call(
        body, name=name, grid=(t // tm,), in_specs=[row, vec, row],
        out_specs=[pl.BlockSpec((1, 128), lambda i: (0, 0)), row, row, vec],
        out_shape=[jax.ShapeDtypeStruct((1, 128), F32), jax.ShapeDtypeStruct((t, d), F32),
                   jax.ShapeDtypeStruct((t, d), BF16), jax.ShapeDtypeStruct((1, d), F32)],
        compiler_params=_params(("arbitrary",)),
    )(x, g.reshape(1, d), target)


CONV_K = 4
HALO = 8


def _shift_down(cur, prev8, s):
    rolled = pltpu.roll(cur, s, 0)
    fix = pltpu.roll(prev8, s, 0)
    head = jnp.where(_iota((HALO, cur.shape[1]), 0) < s, fix, rolled[:HALO])
    return jnp.concatenate([head, rolled[HALO:]], axis=0)


def _shift_up(cur, next8, s):
    n = cur.shape[0]
    rolled = pltpu.roll(cur, n - s, 0)
    fix = pltpu.roll(next8, HALO - s, 0)
    tail = jnp.where(_iota((HALO, cur.shape[1]), 0) >= HALO - s, fix, rolled[n - HALO:])
    return jnp.concatenate([rolled[:n - HALO], tail], axis=0)


def _conv_pre(u_ref, up_ref, w_ref, b_ref, first):
    cur = u_ref[...]
    prev8 = jnp.where(first, 0.0, up_ref[...])
    w = w_ref[...]
    shifted = [cur] + [_shift_down(cur, prev8, s) for s in (1, 2, 3)]
    pre = b_ref[...] + sum(w[CONV_K - 1 - s:CONV_K - s, :] * shifted[s] for s in range(CONV_K))
    return pre, shifted


def _conv_specs(tm, tc):
    nb = tm // HALO
    cur = pl.BlockSpec((tm, tc), lambda j, i: (i, j))
    prev = pl.BlockSpec((HALO, tc), lambda j, i: (jnp.maximum(i * nb - 1, 0), j))
    wspec = pl.BlockSpec((CONV_K, tc), lambda j, i: (0, j))
    bspec = pl.BlockSpec((1, tc), lambda j, i: (0, j))
    return cur, prev, wspec, bspec


def _conv_fwd(u, w, b, name):
    t, c = u.shape
    tm, tc = _pick(t, (512, 256, 128)), _pick(c, (1024, 512, 256, 128))
    cur, prev, wspec, bspec = _conv_specs(tm, tc)

    def body(u_ref, up_ref, w_ref, b_ref, o_ref):
        pre, _ = _conv_pre(u_ref, up_ref, w_ref, b_ref, pl.program_id(1) == 0)
        o_ref[...] = _silu(pre)

    return pl.pallas_call(
        body, name=name, grid=(c // tc, t // tm), in_specs=[cur, prev, wspec, bspec], out_specs=cur,
        out_shape=jax.ShapeDtypeStruct((t, c), F32),
        compiler_params=_params(("parallel", "parallel")),
    )(u, u, w, b.reshape(1, c))


def _conv_bwd_pre(u, w, b, dy, name):
    t, c = u.shape
    tm, tc = _pick(t, (512, 256, 128)), _pick(c, (1024, 512, 256, 128))
    cur, prev, wspec, bspec = _conv_specs(tm, tc)

    def body(u_ref, up_ref, w_ref, b_ref, dy_ref, dpre_ref, dw_ref, db_ref):
        i = pl.program_id(1)
        pre, shifted = _conv_pre(u_ref, up_ref, w_ref, b_ref, i == 0)
        dpre = dy_ref[...] * _dsilu(pre)
        dpre_ref[...] = dpre
        dw = jnp.concatenate([jnp.sum(dpre * shifted[CONV_K - 1 - k], axis=0, keepdims=True) for k in range(CONV_K)], axis=0)
        db = jnp.sum(dpre, axis=0, keepdims=True)

        @pl.when(i == 0)
        def _():
            dw_ref[...] = dw
            db_ref[...] = db

        @pl.when(i > 0)
        def _():
            dw_ref[...] += dw
            db_ref[...] += db

    return pl.pallas_call(
        body, name=name, grid=(c // tc, t // tm), in_specs=[cur, prev, wspec, bspec, cur],
        out_specs=[cur, wspec, bspec],
        out_shape=[jax.ShapeDtypeStruct((t, c), F32), jax.ShapeDtypeStruct((CONV_K, c), F32),
                   jax.ShapeDtypeStruct((1, c), F32)],
        compiler_params=_params(("parallel", "arbitrary")),
    )(u, u, w, b.reshape(1, c), dy)


def _conv_bwd_in(dpre, w, name):
    t, c = dpre.shape
    tm, tc = _pick(t, (512, 256, 128)), _pick(c, (1024, 512, 256, 128))
    nb = tm // HALO
    last = t // tm - 1
    cur = pl.BlockSpec((tm, tc), lambda j, i: (i, j))
    nxt = pl.BlockSpec((HALO, tc), lambda j, i: (jnp.minimum((i + 1) * nb, t // HALO - 1), j))
    wspec = pl.BlockSpec((CONV_K, tc), lambda j, i: (0, j))

    def body(d_ref, dn_ref, w_ref, o_ref):
        cur_v = d_ref[...]
        next8 = jnp.where(pl.program_id(1) == last, 0.0, dn_ref[...])
        wv = w_ref[...]
        acc = wv[CONV_K - 1:CONV_K, :] * cur_v
        for s in (1, 2, 3):
            acc = acc + wv[CONV_K - 1 - s:CONV_K - s, :] * _shift_up(cur_v, next8, s)
        o_ref[...] = acc.astype(o_ref.dtype)

    return pl.pallas_call(
        body, name=name, grid=(c // tc, t // tm), in_specs=[cur, nxt, wspec], out_specs=cur,
        out_shape=jax.ShapeDtypeStruct((t, c), BF16),
        compiler_params=_params(("parallel", "parallel")),
    )(dpre, dpre, w)


MEM_HEADS = 4
NT_DIMS = (((1,), (1,)), ((), ()))
TN_DIMS = (((0,), (0,)), ((), ()))


def _dot(a, b, dims=None):
    if dims is None:
        return jnp.dot(a, b, preferred_element_type=F32)
    return lax.dot_general(a, b, dims, preferred_element_type=F32)


def _memattn_probs(q, mk, scale):
    s = _dot(q, mk, NT_DIMS) * scale
    s = s - jnp.max(s, axis=-1, keepdims=True)
    p = jnp.exp(s)
    return p / jnp.sum(p, axis=-1, keepdims=True)


def _memattn_fwd(q, mkv, name):
    t, wd = q.shape
    m = mkv.shape[0]
    hd = wd // MEM_HEADS
    scale = hd ** -0.5
    tm = _pick(t, (512, 256, 128))

    def body(q_ref, mkv_ref, o_ref):
        for h in range(MEM_HEADS):
            cols = slice(h * hd, (h + 1) * hd)
            p = _memattn_probs(q_ref[:, cols], mkv_ref[:, cols], scale)
            o_ref[:, cols] = _dot(p.astype(BF16), mkv_ref[:, wd + h * hd:wd + (h + 1) * hd])

    return pl.pallas_call(
        body, name=name, grid=(t // tm,),
        in_specs=[pl.BlockSpec((tm, wd), lambda i: (i, 0)), pl.BlockSpec((m, 2 * wd), lambda i: (0, 0))],
        out_specs=pl.BlockSpec((tm, wd), lambda i: (i, 0)),
        out_shape=jax.ShapeDtypeStruct((t, wd), F32),
        compiler_params=_params(("parallel",)),
    )(q, mkv)


def _memattn_bwd(q, mkv, dy, name):
    t, wd = q.shape
    m = mkv.shape[0]
    hd = wd // MEM_HEADS
    scale = hd ** -0.5
    tm = _pick(t, (512, 256, 128))

    def body(q_ref, mkv_ref, dy_ref, dq_ref, dmkv_ref):
        i = pl.program_id(0)

        @pl.when(i == 0)
        def _():
            dmkv_ref[...] = jnp.zeros_like(dmkv_ref)

        for h in range(MEM_HEADS):
            cols = slice(h * hd, (h + 1) * hd)
            vcols = slice(wd + h * hd, wd + (h + 1) * hd)
            qh = q_ref[:, cols]
            p = _memattn_probs(qh, mkv_ref[:, cols], scale)
            dyh = dy_ref[:, cols].astype(BF16)
            dp = _dot(dyh, mkv_ref[:, vcols], NT_DIMS)
            ds = (p * (dp - jnp.sum(dp * p, axis=-1, keepdims=True)) * scale).astype(BF16)
            dq_ref[:, cols] = _dot(ds, mkv_ref[:, cols]).astype(dq_ref.dtype)
            dmkv_ref[:, cols] += _dot(ds, qh, TN_DIMS)
            dmkv_ref[:, vcols] += _dot(p.astype(BF16), dyh, TN_DIMS)

    return pl.pallas_call(
        body, name=name, grid=(t // tm,),
        in_specs=[pl.BlockSpec((tm, wd), lambda i: (i, 0)), pl.BlockSpec((m, 2 * wd), lambda i: (0, 0)),
                  pl.BlockSpec((tm, wd), lambda i: (i, 0))],
        out_specs=[pl.BlockSpec((tm, wd), lambda i: (i, 0)), pl.BlockSpec((m, 2 * wd), lambda i: (0, 0))],
        out_shape=[jax.ShapeDtypeStruct((t, wd), BF16), jax.ShapeDtypeStruct((m, 2 * wd), F32)],
        compiler_params=_params(("arbitrary",)),
    )(q, mkv, dy)


NORM_GROUPS = 8


def _gate_fwd(y_tok, y_mem, z, norm_g, name):
    t, tok = y_tok.shape
    mem = y_mem.shape[1]
    mix = tok + mem
    gw = tok // NORM_GROUPS
    tm = _pick(t, (256, 128))

    def body(*refs):
        if norm_g is None:
            yt_ref, ym_ref, z_ref, o_ref = refs
        else:
            yt_ref, ym_ref, z_ref, g_ref, o_ref = refs
        u = yt_ref[...] * _silu(z_ref[:, :tok])
        if norm_g is None:
            o_ref[:, :tok] = u.astype(o_ref.dtype)
        else:
            for k in range(NORM_GROUPS):
                uk = u[:, k * gw:(k + 1) * gw]
                r = lax.rsqrt(jnp.mean(uk * uk, axis=-1, keepdims=True) + EPS)
                o_ref[:, k * gw:(k + 1) * gw] = (uk * r * g_ref[:, k * gw:(k + 1) * gw]).astype(o_ref.dtype)
        o_ref[:, tok:] = (ym_ref[...] * _silu(z_ref[:, tok:])).astype(o_ref.dtype)

    in_specs = [pl.BlockSpec((tm, tok), lambda i: (i, 0)), pl.BlockSpec((tm, mem), lambda i: (i, 0)),
                pl.BlockSpec((tm, mix), lambda i: (i, 0))]
    args = [y_tok, y_mem, z]
    if norm_g is not None:
        in_specs.append(pl.BlockSpec((1, tok), lambda i: (0, 0)))
        args.append(norm_g.reshape(1, tok))
    return pl.pallas_call(
        body, name=name, grid=(t // tm,), in_specs=in_specs,
        out_specs=pl.BlockSpec((tm, mix), lambda i: (i, 0)),
        out_shape=jax.ShapeDtypeStruct((t, mix), BF16),
        compiler_params=_params(("parallel",)),
    )(*args)


def _gate_bwd(y_tok, y_mem, z, norm_g, dgated, name):
    t, tok = y_tok.shape
    mem = y_mem.shape[1]
    mix = tok + mem
    gw = tok // NORM_GROUPS
    tm = _pick(t, (256, 128))

    def body(*refs):
        if norm_g is None:
            yt_ref, ym_ref, z_ref, dg_ref, dyt_ref, dym_ref, dz_ref, dn_ref = refs
        else:
            yt_ref, ym_ref, z_ref, dg_ref, g_ref, dyt_ref, dym_ref, dz_ref, dn_ref = refs
        i = pl.program_id(0)
        zt = z_ref[:, :tok]
        yt = yt_ref[...]
        sz = _silu(zt)
        dout = dg_ref[:, :tok].astype(F32)
        if norm_g is None:
            du = dout
            dn = jnp.zeros((1, tok), F32)
        else:
            u = yt * sz
            dus, dns = [], []
            for k in range(NORM_GROUPS):
                uk = u[:, k * gw:(k + 1) * gw]
                r = lax.rsqrt(jnp.mean(uk * uk, axis=-1, keepdims=True) + EPS)
                nk = uk * r
                dk = dout[:, k * gw:(k + 1) * gw]
                dns.append(jnp.sum(dk * nk, axis=0, keepdims=True))
                dnk = dk * g_ref[:, k * gw:(k + 1) * gw]
                dus.append(r * (dnk - nk * jnp.mean(dnk * nk, axis=-1, keepdims=True)))
            du = jnp.concatenate(dus, axis=1)
            dn = jnp.concatenate(dns, axis=1)
        dyt_ref[...] = du * sz
        dz_ref[:, :tok] = (du * yt * _dsilu(zt)).astype(dz_ref.dtype)
        zm = z_ref[:, tok:]
        dm = dg_ref[:, tok:].astype(F32)
        dym_ref[...] = dm * _silu(zm)
        dz_ref[:, tok:] = (dm * ym_ref[...] * _dsilu(zm)).astype(dz_ref.dtype)

        @pl.when(i == 0)
        def _():
            dn_ref[...] = dn

        @pl.when(i > 0)
        def _():
            dn_ref[...] += dn

    tok_spec = pl.BlockSpec((tm, tok), lambda i: (i, 0))
    mem_spec = pl.BlockSpec((tm, mem), lambda i: (i, 0))
    mix_spec = pl.BlockSpec((tm, mix), lambda i: (i, 0))
    vec = pl.BlockSpec((1, tok), lambda i: (0, 0))
    in_specs = [tok_spec, mem_spec, mix_spec, mix_spec]
    args = [y_tok, y_mem, z, dgated]
    if norm_g is not None:
        in_specs.append(vec)
        args.append(norm_g.reshape(1, tok))
    return pl.pallas_call(
        body, name=name, grid=(t // tm,), in_specs=in_specs,
        out_specs=[tok_spec, mem_spec, mix_spec, vec],
        out_shape=[jax.ShapeDtypeStruct((t, tok), F32), jax.ShapeDtypeStruct((t, mem), F32),
                   jax.ShapeDtypeStruct((t, mix), BF16), jax.ShapeDtypeStruct((1, tok), F32)],
        compiler_params=_params(("arbitrary",)),
    )(*args)


SSD_Q = 128
SSD_N = 128
SSD_P = 64
SSD_G = 8
SSD_HPG = 6
SSD_H = SSD_G * SSD_HPG
SSD_TOK = SSD_H * SSD_P
SSD_XBC = SSD_TOK + 2 * SSD_G * SSD_N
LANES = 128
HIGHEST = lax.Precision.HIGHEST


def _softplus(x):
    return jnp.maximum(x, 0.0) + jnp.log(1.0 + jnp.exp(-jnp.abs(x)))


def _ssd_common(dtr_ref, bias_ref, alog_ref):
    sq = (SSD_Q, LANES)
    pre = dtr_ref[...] + bias_ref[...]
    dt = _softplus(pre)
    a = -jnp.exp(alog_ref[...])
    tril = (_iota(sq, 0) >= _iota(sq, 1)).astype(F32)
    acs = jnp.dot(tril, dt * a, precision=HIGHEST, preferred_element_type=F32)
    return pre, dt, a, tril, acs, acs.T


def _pair_terms(dt, acs, acs_t, h0):
    hi = _iota((SSD_Q, LANES), 1) >= SSD_P
    heads = []
    for j in range(2):
        h = h0 + j
        a_col = _col(acs, h)
        a_row = acs_t[h:h + 1, :]
        a_last = _col(acs[SSD_Q - 1:SSD_Q, :], h)
        heads.append((h, a_col, a_row, a_last, hi if j else jnp.logical_not(hi)))
    dtl = jnp.where(hi, _col(dt, h0 + 1), _col(dt, h0))
    scale = jnp.where(hi, jnp.exp(heads[1][1]), jnp.exp(heads[0][1]))
    dec_last = jnp.where(hi[:1], jnp.exp(heads[1][3]), jnp.exp(heads[0][3]))
    return heads, dtl, scale, dec_last


def _decay(a_col, a_row):
    causal = _iota((SSD_Q, SSD_Q), 0) >= _iota((SSD_Q, SSD_Q), 1)
    return jnp.where(causal, jnp.exp(jnp.minimum(a_col - a_row, 0.0)), 0.0)


def _ssd_fwd(xbc, dt_raw, dt_bias, a_log, dskip_lane, name):
    t = xbc.shape[0]
    nc = t // SSD_Q

    def body(xbc_ref, dtr_ref, bias_ref, alog_ref, dsk_ref, y_ref, hs_ref, h_ref):
        @pl.when(pl.program_id(0) == 0)
        def _():
            h_ref[...] = jnp.zeros_like(h_ref)

        _, dt, _, _, acs, acs_t = _ssd_common(dtr_ref, bias_ref, alog_ref)
        for g in range(SSD_G):
            bg_f = xbc_ref[:, SSD_TOK + g * SSD_N:SSD_TOK + (g + 1) * SSD_N]
            bg = bg_f.astype(BF16)
            cg = xbc_ref[:, SSD_TOK + SSD_G * SSD_N + g * SSD_N:SSD_TOK + SSD_G * SSD_N + (g + 1) * SSD_N].astype(BF16)
            cb = _dot(cg, bg, NT_DIMS)
            for pr in range(SSD_HPG // 2):
                h0 = g * SSD_HPG + 2 * pr
                lanes = slice(h0 * SSD_P, (h0 + 2) * SSD_P)
                heads, dtl, scale, dec_last = _pair_terms(dt, acs, acs_t, h0)
                xs = xbc_ref[:, lanes]
                xdt = xs * dtl
                hp = h_ref[:, lanes]
                hs_ref[:, lanes] = hp
                y = _dot(cg, hp.astype(BF16)) * scale + dsk_ref[:, lanes] * xs
                snew = hp * dec_last
                for _, a_col, a_row, a_last, mask in heads:
                    xm = jnp.where(mask, xdt, 0.0).astype(BF16)
                    y = y + _dot((cb * _decay(a_col, a_row)).astype(BF16), xm)
                    bw = (bg_f * jnp.exp(a_last - a_col)).astype(BF16)
                    snew = snew + _dot(bw, xm, TN_DIMS)
                y_ref[:, lanes] = y
                h_ref[:, lanes] = snew

    row = lambda w: pl.BlockSpec((SSD_Q, w), lambda c: (c, 0))
    vec = lambda w: pl.BlockSpec((1, w), lambda c: (0, 0))
    return pl.pallas_call(
        body, name=name, grid=(nc,),
        in_specs=[row(SSD_XBC), row(LANES), vec(LANES), vec(LANES), vec(SSD_TOK)],
        out_specs=[row(SSD_TOK), row(SSD_TOK)],
        out_shape=[jax.ShapeDtypeStruct((t, SSD_TOK), F32), jax.ShapeDtypeStruct((nc * SSD_N, SSD_TOK), F32)],
        scratch_shapes=[pltpu.VMEM((SSD_N, SSD_TOK), F32)],
        compiler_params=_params(("arbitrary",)),
    )(xbc, dt_raw, dt_bias, a_log, dskip_lane)


def _ssd_bwd(xbc, dt_raw, dt_bias, a_log, dskip_lane, hs, dy, name):
    t = xbc.shape[0]
    nc = t // SSD_Q
    sq = (SSD_Q, LANES)

    def body(xbc_ref, dtr_ref, bias_ref, alog_ref, dsk_ref, hs_ref, dy_ref,
             dxbc_ref, ddtr_ref, dbias_ref, dalog_ref, ddsk_ref, dh_ref):
        first = pl.program_id(0) == 0

        @pl.when(first)
        def _():
            dh_ref[...] = jnp.zeros_like(dh_ref)
            dbias_ref[...] = jnp.zeros_like(dbias_ref)
            dalog_ref[...] = jnp.zeros_like(dalog_ref)
            ddsk_ref[...] = jnp.zeros_like(ddsk_ref)

        pre, dt, a, tril, acs, acs_t = _ssd_common(dtr_ref, bias_ref, alog_ref)
        lane = _iota(sq, 1)
        sub = _iota(sq, 0)
        causal = sub >= lane
        d_acs = jnp.zeros(sq, F32)
        d_acs_row = jnp.zeros(sq, F32)
        d_last = jnp.zeros((1, LANES), F32)
        ddt = jnp.zeros(sq, F32)
        for g in range(SSD_G):
            bcols = slice(SSD_TOK + g * SSD_N, SSD_TOK + (g + 1) * SSD_N)
            ccols = slice(SSD_TOK + SSD_G * SSD_N + g * SSD_N, SSD_TOK + SSD_G * SSD_N + (g + 1) * SSD_N)
            bg_f = xbc_ref[:, bcols]
            bg = bg_f.astype(BF16)
            cg = xbc_ref[:, ccols].astype(BF16)
            cb = _dot(cg, bg, NT_DIMS)
            dcb = jnp.zeros(sq, F32)
            dbg = jnp.zeros(sq, F32)
            dcg = jnp.zeros(sq, F32)
            for pr in range(SSD_HPG // 2):
                h0 = g * SSD_HPG + 2 * pr
                lanes = slice(h0 * SSD_P, (h0 + 2) * SSD_P)
                heads, dtl, scale, dec_last = _pair_terms(dt, acs, acs_t, h0)
                xs = xbc_ref[:, lanes]
                xdt = xs * dtl
                dyv = dy_ref[:, lanes]
                hp = hs_ref[:, lanes]
                dhn = dh_ref[:, lanes]
                hp_b = hp.astype(BF16)
                dys = (dyv * scale).astype(BF16)
                yoff_dy = dyv * _dot(cg, hp_b) * scale
                dcg = dcg + _dot(dys, hp_b, NT_DIMS)
                dhc = _dot(cg, dys, TN_DIMS)
                hh = dhn * hp
                dxdt = jnp.zeros(sq, F32)
                for h, a_col, a_row, a_last, mask in heads:
                    dec = _decay(a_col, a_row)
                    m = cb * dec
                    dym = jnp.where(mask, dyv, 0.0).astype(BF16)
                    xm = jnp.where(mask, xdt, 0.0).astype(BF16)
                    dhm = jnp.where(mask, dhn, 0.0).astype(BF16)
                    w = jnp.exp(a_last - a_col)
                    dxdt = dxdt + _dot(m.astype(BF16), dym, TN_DIMS) + _dot((bg_f * w).astype(BF16), dhm)
                    dm = jnp.where(causal, _dot(dym, xm, NT_DIMS), 0.0)
                    dcb = dcb + dm * dec
                    e = dm * m
                    gj = _dot(xm, dhm, NT_DIMS)
                    dbg = dbg + w * gj
                    wdw = w * jnp.sum(bg_f * gj, axis=1, keepdims=True)
                    col = (jnp.sum(e, axis=1, keepdims=True)
                           + jnp.sum(jnp.where(mask, yoff_dy, 0.0), axis=1, keepdims=True) - wdw)
                    d_acs = d_acs + jnp.where(lane == h, col, 0.0)
                    d_acs_row = d_acs_row + jnp.where(sub == h, jnp.sum(e, axis=0, keepdims=True), 0.0)
                    last = jnp.sum(wdw) + jnp.exp(a_last) * jnp.sum(jnp.where(mask, hh, 0.0))
                    d_last = d_last + jnp.where(lane[:1] == h, last, 0.0)
                dxbc_ref[:, lanes] = dxdt * dtl + dsk_ref[:, lanes] * dyv
                tt = dxdt * xs
                for h, _, _, _, mask in heads:
                    ddt = ddt + jnp.where(lane == h, jnp.sum(jnp.where(mask, tt, 0.0), axis=1, keepdims=True), 0.0)
                ddsk_ref[:, lanes] += jnp.sum(dyv * xs, axis=0, keepdims=True)
                dh_ref[:, lanes] = dhn * dec_last + dhc
            dcb_b = dcb.astype(BF16)
            dxbc_ref[:, bcols] = dbg + _dot(dcb_b, cg, TN_DIMS)
            dxbc_ref[:, ccols] = dcg + _dot(dcb_b, bg)
        d_tot = d_acs - d_acs_row.T + jnp.where(sub == SSD_Q - 1, d_last, 0.0)
        ddta = lax.dot_general(tril, d_tot, TN_DIMS, precision=HIGHEST, preferred_element_type=F32)
        ddt = ddt + ddta * a
        dalog_ref[...] += jnp.sum(ddta * dt, axis=0, keepdims=True) * a
        ddtr = ddt * jax.nn.sigmoid(pre)
        ddtr_ref[...] = ddtr
        dbias_ref[...] += jnp.sum(ddtr, axis=0, keepdims=True)

    rev = lambda w: pl.BlockSpec((SSD_Q, w), lambda i: (nc - 1 - i, 0))
    vec = lambda w: pl.BlockSpec((1, w), lambda i: (0, 0))
    return pl.pallas_call(
        body, name=name, grid=(nc,),
        in_specs=[rev(SSD_XBC), rev(LANES), vec(LANES), vec(LANES), vec(SSD_TOK), rev(SSD_TOK), rev(SSD_TOK)],
        out_specs=[rev(SSD_XBC), rev(LANES), vec(LANES), vec(LANES), vec(SSD_TOK)],
        out_shape=[jax.ShapeDtypeStruct((t, SSD_XBC), F32), jax.ShapeDtypeStruct((t, LANES), F32),
                   jax.ShapeDtypeStruct((1, LANES), F32), jax.ShapeDtypeStruct((1, LANES), F32),
                   jax.ShapeDtypeStruct((1, SSD_TOK), F32)],
        scratch_shapes=[pltpu.VMEM((SSD_N, SSD_TOK), F32)],
        compiler_params=_params(("arbitrary",)),
    )(xbc, dt_raw, dt_bias, a_log, dskip_lane, hs, dy)


ATT_E = 128
ATT_H = 24
ATT_W = 128
ATT_TOK = ATT_H * ATT_E
DILATED_GROUPS = ((128, 1), (512, 4), (2048, 16))
N_DIL = len(DILATED_GROUPS)
ALIBI_MAX_EXP = 8.0
MASKED = -1e30


def _alibi_slopes(group):
    n = N_DIL * ATT_H
    return [2.0 ** (-ALIBI_MAX_EXP * (group * ATT_H + h + 1) / n) for h in range(ATT_H)]


def _att_scores(qh, kk, rel, valid, slope_d):
    s = _dot(qh, kk, NT_DIMS) * (ATT_E ** -0.5) - slope_d * rel
    return jnp.where(valid, s, MASKED)


def _att_rel(j):
    shp = (ATT_W, 2 * ATT_W)
    kpos = _iota(shp, 1)
    rel = _iota(shp, 0) + ATT_W - kpos
    valid = (rel >= 0) & (rel <= ATT_W) & ((kpos >= ATT_W) | (j > 0))
    return rel.astype(F32), valid


def _dil_fwd(q, k, v, group, name):
    t = q.shape[0]
    dil = DILATED_GROUPS[group][1]
    slopes = _alibi_slopes(group)
    nb = t // dil // ATT_W

    def body(q_ref, kp_ref, kc_ref, vp_ref, vc_ref, o_ref, lse_ref):
        rel, valid = _att_rel(pl.program_id(1))
        lane = _iota((ATT_W, LANES), 1)
        lse_all = jnp.zeros((ATT_W, LANES), F32)
        for h in range(ATT_H):
            cols = slice(h * ATT_E, (h + 1) * ATT_E)
            kk = jnp.concatenate([kp_ref[:, cols], kc_ref[:, cols]], axis=0)
            vv = jnp.concatenate([vp_ref[:, cols], vc_ref[:, cols]], axis=0)
            s = _att_scores(q_ref[:, cols], kk, rel, valid, slopes[h] * dil)
            m = jnp.max(s, axis=-1, keepdims=True)
            p = jnp.exp(s - m)
            den = jnp.sum(p, axis=-1, keepdims=True)
            o_ref[:, cols] = _dot(p.astype(BF16), vv) / den
            lse_all = jnp.where(lane == h, m + jnp.log(den), lse_all)
        lse_ref[...] = lse_all

    cur = pl.BlockSpec((ATT_W, ATT_TOK), lambda r, j: (r * nb + j, 0))
    prev = pl.BlockSpec((ATT_W, ATT_TOK), lambda r, j: (r * nb + jnp.maximum(j - 1, 0), 0))
    small = pl.BlockSpec((ATT_W, LANES), lambda r, j: (r * nb + j, 0))
    return pl.pallas_call(
        body, name=name, grid=(dil, nb),
        in_specs=[cur, prev, cur, prev, cur], out_specs=[cur, small],
        out_shape=[jax.ShapeDtypeStruct((t, ATT_TOK), F32), jax.ShapeDtypeStruct((t, LANES), F32)],
        compiler_params=_params(("parallel", "parallel")),
    )(q, k, k, v, v)


def _dil_bwd(q, k, v, do, cterm, lse, group, name):
    t = q.shape[0]
    dil = DILATED_GROUPS[group][1]
    slopes = _alibi_slopes(group)
    nb = t // dil // ATT_W

    def body(q_ref, kp_ref, kc_ref, vp_ref, vc_ref, do_ref, c_ref, lse_ref,
             dq_ref, dk_ref, dv_ref, ck_ref, cv_ref):
        j = pl.program_id(1)

        @pl.when(j == 0)
        def _():
            ck_ref[...] = jnp.zeros_like(ck_ref)
            cv_ref[...] = jnp.zeros_like(cv_ref)

        @pl.when(j < nb)
        def _():
            rel, valid = _att_rel(j)
            cv_, lv = c_ref[...], lse_ref[...]
            for h in range(ATT_H):
                cols = slice(h * ATT_E, (h + 1) * ATT_E)
                qh = q_ref[:, cols]
                kk = jnp.concatenate([kp_ref[:, cols], kc_ref[:, cols]], axis=0)
                vv = jnp.concatenate([vp_ref[:, cols], vc_ref[:, cols]], axis=0)
                s = _att_scores(qh, kk, rel, valid, slopes[h] * dil)
                p = jnp.where(valid, jnp.exp(s - _col(lv, h)), 0.0)
                do = do_ref[:, cols]
                dp = _dot(do, vv, NT_DIMS)
                ds = (p * (dp + _col(cv_, h)) * (ATT_E ** -0.5)).astype(BF16)
                dq_ref[:, cols] = _dot(ds, kk).astype(dq_ref.dtype)
                dkk = _dot(ds, qh, TN_DIMS)
                dvv = _dot(p.astype(BF16), do, TN_DIMS)
                dk_ref[:, cols] = (ck_ref[:, cols] + dkk[:ATT_W]).astype(dk_ref.dtype)
                dv_ref[:, cols] = (cv_ref[:, cols] + dvv[:ATT_W]).astype(dv_ref.dtype)
                ck_ref[:, cols] = dkk[ATT_W:]
                cv_ref[:, cols] = dvv[ATT_W:]

        @pl.when(j == nb)
        def _():
            dk_ref[...] = ck_ref[...].astype(dk_ref.dtype)
            dv_ref[...] = cv_ref[...].astype(dv_ref.dtype)

    jq = lambda j: jnp.minimum(j, nb - 1)
    cur = pl.BlockSpec((ATT_W, ATT_TOK), lambda r, j: (r * nb + jq(j), 0))
    prev = pl.BlockSpec((ATT_W, ATT_TOK), lambda r, j: (r * nb + jnp.maximum(jq(j) - 1, 0), 0))
    small = pl.BlockSpec((ATT_W, LANES), lambda r, j: (r * nb + jq(j), 0))
    late = pl.BlockSpec((ATT_W, ATT_TOK), lambda r, j: (r * nb + jnp.maximum(j - 1, 0), 0))
    big = jax.ShapeDtypeStruct((t, ATT_TOK), BF16)
    return pl.pallas_call(
        body, name=name, grid=(dil, nb + 1),
        in_specs=[cur, prev, cur, prev, cur, cur, small, small], out_specs=[cur, late, late],
        out_shape=[big, big, big],
        scratch_shapes=[pltpu.VMEM((ATT_W, ATT_TOK), F32), pltpu.VMEM((ATT_W, ATT_TOK), F32)],
        compiler_params=_params(("parallel", "arbitrary")),
    )(q, k, k, v, v, do, cterm, lse)


def _combine_weights(lses):
    m = functools.reduce(jnp.maximum, lses)
    es = [jnp.exp(l - m) for l in lses]
    tot = functools.reduce(lambda a, b: a + b, es)
    return [e / tot for e in es]


DILS = tuple(d for _, d in DILATED_GROUPS)
COMBINE_ROWS = 256


def _by_residue(arr, dil):
    return arr if dil == 1 else arr.reshape(dil, arr.shape[0] // dil, arr.shape[1])


def _natural_lses(l_refs, small_refs, tm):
    vals = []
    for g, dil in enumerate(DILS):
        if dil == 1:
            vals.append(l_refs[g][...])
        else:
            _scatter_rows(l_refs[g], small_refs[g], dil, tm)
            vals.append(small_refs[g][0])
    return vals


def _combine_scratch(tm):
    return ([pltpu.VMEM((_chunks(ATT_TOK), tm, 128), F32)] * N_DIL + [pltpu.VMEM((1, tm, 128), F32)] * N_DIL)


def _combine_fwd(outs, lses, name):
    t = outs[0].shape[0]
    tm = COMBINE_ROWS

    def body(*refs):
        o_refs, l_refs, y_ref = refs[:N_DIL], refs[N_DIL:2 * N_DIL], refs[2 * N_DIL]
        big_refs, small_refs = refs[2 * N_DIL + 1:3 * N_DIL + 1], refs[3 * N_DIL + 1:]
        ws = _combine_weights(_natural_lses(l_refs, small_refs, tm))
        for g in range(1, N_DIL):
            _scatter_rows(o_refs[g], big_refs[g], DILS[g], tm)
        for h in range(ATT_H):
            cols = slice(h * ATT_E, (h + 1) * ATT_E)
            y_ref[:, cols] = (_col(ws[0], h) * o_refs[0][:, cols]
                              + sum(_col(ws[g], h) * big_refs[g][h] for g in range(1, N_DIL)))

    return pl.pallas_call(
        body, name=name, grid=(t // tm,),
        in_specs=[_perm_spec(tm, d, ATT_TOK) for d in DILS] + [_perm_spec(tm, d, LANES) for d in DILS],
        out_specs=pl.BlockSpec((tm, ATT_TOK), lambda i: (i, 0)),
        out_shape=jax.ShapeDtypeStruct((t, ATT_TOK), F32),
        scratch_shapes=_combine_scratch(tm),
        compiler_params=_params(("parallel",)),
    )(*[_by_residue(o, d) for o, d in zip(outs, DILS)], *[_by_residue(l, d) for l, d in zip(lses, DILS)])


def _combine_bwd(outs, lses, dy, name):
    t = outs[0].shape[0]
    tm = COMBINE_ROWS

    def body(*refs):
        o_refs, l_refs, dy_ref = refs[:N_DIL], refs[N_DIL:2 * N_DIL], refs[2 * N_DIL]
        do_refs, c_refs = refs[2 * N_DIL + 1:3 * N_DIL + 1], refs[3 * N_DIL + 1:4 * N_DIL + 1]
        big_refs, small_refs = refs[4 * N_DIL + 1:5 * N_DIL + 1], refs[5 * N_DIL + 1:]
        ws = _combine_weights(_natural_lses(l_refs, small_refs, tm))
        for g in range(1, N_DIL):
            _scatter_rows(o_refs[g], big_refs[g], DILS[g], tm)
        lane = _iota((tm, LANES), 1)
        sdw = jnp.zeros((tm, LANES), F32)
        for h in range(ATT_H):
            cols = slice(h * ATT_E, (h + 1) * ATT_E)
            dyh = dy_ref[:, cols]
            tot = _col(ws[0], h) * jnp.sum(dyh * o_refs[0][:, cols], axis=1, keepdims=True)
            for g in range(1, N_DIL):
                tot = tot + _col(ws[g], h) * jnp.sum(dyh * big_refs[g][h], axis=1, keepdims=True)
            sdw = jnp.where(lane == h, tot, sdw)
        for g, dil in enumerate(DILS):
            cterm = -ws[g] * sdw
            if dil == 1:
                c_refs[g][...] = cterm
            else:
                small_refs[g][0] = cterm
                _gather_rows(small_refs[g], c_refs[g], dil, tm)
            for h in range(ATT_H):
                cols = slice(h * ATT_E, (h + 1) * ATT_E)
                do = _col(ws[g], h) * dy_ref[:, cols]
                if dil == 1:
                    do_refs[g][:, cols] = do.astype(BF16)
                else:
                    big_refs[g][h] = do
            if dil > 1:
                _gather_rows(big_refs[g], do_refs[g], dil, tm)

    res = pl.pallas_call(
        body, name=name, grid=(t // tm,),
        in_specs=([_perm_spec(tm, d, ATT_TOK) for d in DILS] + [_perm_spec(tm, d, LANES) for d in DILS]
                  + [pl.BlockSpec((tm, ATT_TOK), lambda i: (i, 0))]),
        out_specs=[_perm_spec(tm, d, ATT_TOK) for d in DILS] + [_perm_spec(tm, d, LANES) for d in DILS],
        out_shape=[_perm_shape(t, d, ATT_TOK, BF16) for d in DILS] + [_perm_shape(t, d, LANES, F32) for d in DILS],
        scratch_shapes=_combine_scratch(tm),
        compiler_params=_params(("parallel",)),
    )(*[_by_residue(o, d) for o, d in zip(outs, DILS)], *[_by_residue(l, d) for l, d in zip(lses, DILS)], dy)
    return [a.reshape(t, ATT_TOK) for a in res[:N_DIL]], [a.reshape(t, LANES) for a in res[N_DIL:]]


ADAM_LR, ADAM_B1, ADAM_B2, ADAM_EPS, ADAM_WD, ADAM_STEP = 0.001, 0.9, 0.999, 1e-08, 0.01, 10


def _adamw(parts, w, m, v, name):
    r, c = w.shape
    n_parts = parts.shape[0]
    tc = _pick(c, (1024, 512, 256, 128)) if c % 128 == 0 else c
    tm = _pick(r, (128, 64, 32, 16, 8))

    def body(p_ref, w_ref, m_ref, v_ref, g_ref, d_ref, nm_ref, nv_ref):
        g = p_ref[0].astype(F32)
        for k in range(1, n_parts):
            g = g + p_ref[k].astype(F32)
        nm = ADAM_B1 * m_ref[...] + (1.0 - ADAM_B1) * g
        nv = ADAM_B2 * v_ref[...] + (1.0 - ADAM_B2) * (g * g)
        m_hat = nm / (1.0 - ADAM_B1 ** ADAM_STEP)
        v_hat = nv / (1.0 - ADAM_B2 ** ADAM_STEP)
        g_ref[...] = g
        d_ref[...] = -ADAM_LR * (m_hat / (jnp.sqrt(v_hat) + ADAM_EPS) + ADAM_WD * w_ref[...])
        nm_ref[...] = nm
        nv_ref[...] = nv

    blk = pl.BlockSpec((tm, tc), lambda i, j: (i, j))
    pblk = pl.BlockSpec((n_parts, tm, tc), lambda i, j: (0, i, j))
    return pl.pallas_call(
        body, name=name, grid=(r // tm, c // tc), in_specs=[pblk, blk, blk, blk], out_specs=[blk] * 4,
        out_shape=[jax.ShapeDtypeStruct((r, c), F32)] * 4,
        compiler_params=_params(("parallel", "parallel")),
    )(parts, w, m, v)


N_CHIP = 4
MESH_ID = pl.DeviceIdType.MESH


def _other_chips(x, y):
    return [(1 - x, y), (x, 1 - y), (1 - x, 1 - y)]


GATHER_SEMS = N_DEV - 1


def _gather_copies(in_refs, out_refs, send_sems, recv_sems, local_sems, x, y, c):
    n = len(in_refs)
    sibling = (x, y, 1 - c)
    chips = _other_chips(x, y)

    def copy(a, k, block, to, src=None):
        rows = out_refs[a].at[4 * block[0] + 2 * block[1] + block[2]]
        return pltpu.make_async_remote_copy(
            src_ref=rows if src is None else src, dst_ref=rows,
            send_sem=send_sems.at[a * GATHER_SEMS + k], recv_sem=recv_sems.at[a * GATHER_SEMS + k],
            device_id=to, device_id_type=MESH_ID)

    started = []
    for a in range(n):
        local = pltpu.make_async_copy(in_refs[a], out_refs[a].at[4 * x + 2 * y + c], local_sems.at[a])
        local.start()
        started.append(local)
    sends = []
    for j, chip in enumerate(chips):
        for a in range(n):
            sends.append(copy(a, 1 + j, (x, y, c), (*chip, c), src=in_refs[a]))
            sends[-1].start()
    for a in range(n):
        sends.append(copy(a, 0, (x, y, c), sibling, src=in_refs[a]))
        sends[-1].start()
    for j, chip in enumerate(chips):
        for a in range(n):
            copy(a, 1 + j, (*chip, c), (x, y, c)).wait_recv()
            sends.append(copy(a, 4 + j, (*chip, c), sibling))
            sends[-1].start()
    for a in range(n):
        copy(a, 0, sibling, (x, y, c)).wait_recv()
        for j, chip in enumerate(chips):
            copy(a, 4 + j, (*chip, 1 - c), (x, y, c)).wait_recv()
    for cp in sends:
        cp.wait_send()
    for local in started:
        local.wait()


def _gather_two_level(arrays, name):
    n = len(arrays)
    per = GATHER_SEMS

    def body(*refs):
        x, y, c = lax.axis_index("x"), lax.axis_index("y"), lax.axis_index("c")
        _gather_copies(refs[:n], refs[n:2 * n], *refs[2 * n:], x, y, c)

    any_spec = pl.BlockSpec(memory_space=pl.ANY)
    return pl.pallas_call(
        body, name=name, in_specs=[any_spec] * n, out_specs=[any_spec] * n,
        out_shape=[jax.ShapeDtypeStruct((N_DEV,) + a.shape, a.dtype) for a in arrays],
        scratch_shapes=[pltpu.SemaphoreType.DMA((n * per,)), pltpu.SemaphoreType.DMA((n * per,)),
                        pltpu.SemaphoreType.DMA((n,))],
        compiler_params=pltpu.CompilerParams(has_side_effects=True),
    )(*arrays)


def _handshake(barrier, peers):
    for peer in peers:
        pl.semaphore_signal(barrier, inc=1, device_id=peer, device_id_type=MESH_ID)
    pl.semaphore_wait(barrier, len(peers))


def _on_sequencer(name, collective_id, arrays, out_structs, sem_counts, peers, copies):
    hbm = pltpu.MemorySpace.HBM
    in_refs = [jax.new_ref(a, memory_space=hbm) for a in arrays]
    out_refs = [jax.empty_ref(s, memory_space=hbm) for s in out_structs]

    @pl.kernel(mesh=plsc.ScalarSubcoreMesh(axis_name="seq", num_cores=1), name=name,
               scratch_types=tuple(pltpu.SemaphoreType.DMA((k,)) for k in sem_counts),
               compiler_params=pltpu.CompilerParams(collective_id=collective_id))
    def launch(*sems):
        x, y, c = lax.axis_index("x"), lax.axis_index("y"), lax.axis_index("c")
        _handshake(pltpu.get_barrier_semaphore(), peers(x, y, c))
        copies(in_refs, out_refs, *sems, x, y, c)

    launch()
    return [o[...] for o in out_refs]


def _all_others(x, y, c):
    return [(x ^ ((k >> 2) & 1), y ^ ((k >> 1) & 1), c ^ (k & 1)) for k in range(1, N_DEV)]


def _sc_gather(arrays, name, collective_id):
    n = len(arrays)
    outs = [jax.ShapeDtypeStruct((N_DEV,) + a.shape, a.dtype) for a in arrays]
    return _on_sequencer(name, collective_id, arrays, outs, (n * GATHER_SEMS, n * GATHER_SEMS, n), _all_others,
                         _gather_copies)


def _sc_chip_exchange(sums, name, collective_id):
    n = len(sums)
    per = N_CHIP - 1
    outs = [jax.ShapeDtypeStruct(s.shape, s.dtype) for s in sums]
    return _on_sequencer(name, collective_id, sums, outs, (n * per, n * per, n),
                         lambda x, y, c: [(px, py, c) for px, py in _other_chips(x, y)], _chip_exchange_copies)


def _sibling_swap_copies(in_refs, out_refs, send_sems, recv_sems, x, y, c):
    sends = []
    for a in range(len(in_refs)):
        for q in range(N_CHIP):
            cp = pltpu.make_async_remote_copy(
                src_ref=in_refs[a].at[2 * q + 1 - c], dst_ref=out_refs[a].at[q],
                send_sem=send_sems.at[a * N_CHIP + q], recv_sem=recv_sems.at[a * N_CHIP + q],
                device_id=(x, y, 1 - c), device_id_type=MESH_ID)
            cp.start()
            sends.append(cp)
    for cp in sends:
        cp.wait_recv()
    for cp in sends:
        cp.wait_send()


def _sc_sibling_swap(parts, name, collective_id):
    n = len(parts)
    outs = [jax.ShapeDtypeStruct((N_CHIP,) + p.shape[1:], p.dtype) for p in parts]
    return _on_sequencer(name, collective_id, parts, outs, (n * N_CHIP, n * N_CHIP),
                         lambda x, y, c: [(x, y, 1 - c)], _sibling_swap_copies)


def _chip_sum(part, landed, core, name):
    _, r, c = part.shape
    tc = _pick(c, (1024, 512, 256, 128)) if c % 128 == 0 else c
    tm = _pick(r, (256, 128, 64, 32, 16, 8))

    def body(core_ref, p_ref, l_ref, o_ref):
        o_ref[...] = (p_ref[...].astype(F32) + l_ref[...].astype(F32)).astype(o_ref.dtype)

    grid_spec = pltpu.PrefetchScalarGridSpec(
        num_scalar_prefetch=1, grid=(N_CHIP, r // tm, c // tc),
        in_specs=[pl.BlockSpec((None, tm, tc), lambda q, i, j, core_ref: (2 * q + core_ref[0], i, j)),
                  pl.BlockSpec((None, tm, tc), lambda q, i, j, core_ref: (q, i, j))],
        out_specs=pl.BlockSpec((None, tm, tc), lambda q, i, j, core_ref: (q, i, j)))
    return pl.pallas_call(
        body, name=name, grid_spec=grid_spec, out_shape=jax.ShapeDtypeStruct(landed.shape, landed.dtype),
        compiler_params=_params(("parallel", "parallel", "parallel")),
    )(core, part, landed)


def _chip_exchange_copies(in_refs, out_refs, send_sems, recv_sems, local_sems, x, y, c):
    n = len(in_refs)
    per = N_CHIP - 1
    mine = 2 * x + y
    started = []
    for a in range(n):
        local = pltpu.make_async_copy(in_refs[a].at[mine], out_refs[a].at[mine], local_sems.at[a])
        local.start()
        started.append(local)
    sends = []
    for j, (px, py) in enumerate(_other_chips(x, y)):
        for a in range(n):
            cp = pltpu.make_async_remote_copy(
                src_ref=in_refs[a].at[2 * px + py], dst_ref=out_refs[a].at[mine],
                send_sem=send_sems.at[a * per + j], recv_sem=recv_sems.at[a * per + j],
                device_id=(px, py, c), device_id_type=MESH_ID)
            cp.start()
            sends.append((cp, a, j, 2 * px + py))
    for cp, a, j, peer in sends:
        pltpu.make_async_remote_copy(
            src_ref=out_refs[a].at[peer], dst_ref=out_refs[a].at[peer],
            send_sem=send_sems.at[a * per + j], recv_sem=recv_sems.at[a * per + j],
            device_id=(x, y, c), device_id_type=MESH_ID).wait_recv()
    for cp, _, _, _ in sends:
        cp.wait_send()
    for local in started:
        local.wait()


DEPTH = 4
MEM_W = 1024
MIX_W = SSD_TOK + MEM_W
DT_PAD = LANES - SSD_H


def _is_ssd(i):
    return i % 2 == 0


def _weight_names():
    names = ["mem_norm_g", "final_norm_g"]
    for i in range(DEPTH):
        names += [f"norm_g_{i}", f"w_in_{i}"]
        if _is_ssd(i):
            names += [f"conv_w_{i}", f"conv_b_{i}", f"dt_bias_{i}", f"a_log_{i}", f"d_skip_{i}", f"ssd_norm_g_{i}"]
        names += [f"w_mem_kv_{i}", f"w_out_{i}"]
    return names


WEIGHTS = _weight_names()
INPUTS = ["x", "mem"] + WEIGHTS + ["loss_target"] + ["m_" + n for n in WEIGHTS] + ["v_" + n for n in WEIGHTS]


def _in_segments(i):
    if _is_ssd(i):
        return [("xbc", 0, SSD_XBC), ("dt", SSD_XBC, SSD_H), ("qm", SSD_XBC + SSD_H, MEM_W),
                ("z", SSD_XBC + SSD_H + MEM_W, MIX_W)]
    segs = []
    for g in range(N_DIL):
        for j, nm in enumerate("qkv"):
            segs.append((f"{nm}{g}", (3 * g + j) * ATT_TOK, ATT_TOK))
    segs += [("qm", 3 * N_DIL * ATT_TOK, MEM_W), ("z", 3 * N_DIL * ATT_TOK + MEM_W, MIX_W)]
    return segs


def _split_w_in(i, w_in):
    out = {}
    for nm, start, width in _in_segments(i):
        seg = w_in[:, start:start + width]
        out[nm] = jnp.pad(seg, ((0, 0), (0, DT_PAD))) if nm == "dt" else seg
    return out


def _join_dw_in(i, dws):
    return jnp.concatenate([dws[nm][:, :width] for nm, _, width in _in_segments(i)], axis=1)


SEG_DTYPE = {"xbc": F32, "dt": F32, "z": F32}


def _layer_dils(i):
    return (1,) if _is_ssd(i) else DILS


def _seg_order(nm):
    return int(nm[1]) if nm[0] in "qkv" and nm[1:].isdigit() else 0


def _layer_fwd(i, x, mem_b, p):
    tag = f"l{i}"
    hs = _rmsnorm_fwd(x, p["norm_g"], tag + "_norm", _layer_dils(i))
    proj = {nm: _matmul(hs[_seg_order(nm)], w, out_dtype=SEG_DTYPE.get(nm, BF16), name=f"{tag}_in_{nm}")
            for nm, w in p["win"].items()}
    sv = {"x": x, "h": hs, "proj": proj}
    if _is_ssd(i):
        xbc = _conv_fwd(proj["xbc"], p["conv_w"], p["conv_b"], tag + "_conv")
        y_tok, hs = _ssd_fwd(xbc, proj["dt"], p["dt_bias_p"], p["a_log_p"], p["dskip_lane"], tag + "_ssd")
        sv.update(xbc=xbc, hs=hs)
    else:
        outs, lses = [], []
        for g in range(N_DIL):
            o, lse = _dil_fwd(proj[f"q{g}"], proj[f"k{g}"], proj[f"v{g}"], g, f"{tag}_att{g}")
            outs.append(o)
            lses.append(lse)
        y_tok = _combine_fwd(outs, lses, tag + "_comb")
        sv.update(outs=outs, lses=lses)
    mkv = _matmul(mem_b, p["wmkv"], out_dtype=BF16, name=tag + "_mkv")
    y_mem = _memattn_fwd(proj["qm"], mkv, tag + "_mem")
    gated = _gate_fwd(y_tok, y_mem, proj["z"], p.get("ssd_norm_g"), tag + "_gate")
    x_out = _matmul(gated, p["wout"], out_dtype=F32, add=x, name=tag + "_out")
    sv.update(y_tok=y_tok, y_mem=y_mem, mkv=mkv, gated=gated)
    return x_out, sv


def _layer_bwd(i, sv, dx_out, dxb_out, dmem_n, mem_b, p):
    tag = f"l{i}b"
    proj = sv["proj"]
    gr = {}
    dgated = _matmul(dxb_out, p["wout"], tb=True, out_dtype=BF16, name=tag + "_dgated")
    gr["w_out"] = _matmul(sv["gated"], dxb_out, ta=True, out_dtype=BF16, name=tag + "_dwout")
    dy_tok, dy_mem, dz, dssd_g = _gate_bwd(sv["y_tok"], sv["y_mem"], proj["z"], p.get("ssd_norm_g"), dgated, tag + "_gate")
    dq_mem, dmkv = _memattn_bwd(proj["qm"], sv["mkv"], dy_mem, tag + "_mem")
    gr["w_mem_kv"] = _matmul(mem_b, dmkv, ta=True, out_dtype=BF16, name=tag + "_dwmkv")
    dmem_n = _matmul(dmkv, p["wmkv"], tb=True, out_dtype=F32, add=dmem_n, name=tag + "_dmem")
    dproj = {"qm": dq_mem, "z": dz}
    if _is_ssd(i):
        dxbc, ddt_raw, dbias, dalog, ddsk = _ssd_bwd(sv["xbc"], proj["dt"], p["dt_bias_p"], p["a_log_p"], p["dskip_lane"],
                                                     sv["hs"], dy_tok, tag + "_ssd")
        dpre, dconv_w, dconv_b = _conv_bwd_pre(proj["xbc"], p["conv_w"], p["conv_b"], dxbc, tag + "_convpre")
        dproj["xbc"] = _conv_bwd_in(dpre, p["conv_w"], tag + "_convin")
        dproj["dt"] = ddt_raw
        gr.update(conv_w=dconv_w, conv_b=dconv_b[0], dt_bias=dbias[0, :SSD_H], a_log=dalog[0, :SSD_H],
                  d_skip=jnp.sum(ddsk.reshape(SSD_H, SSD_P), axis=1), ssd_norm_g=dssd_g[0])
    else:
        dos, cs = _combine_bwd(sv["outs"], sv["lses"], dy_tok, tag + "_comb")
        for g in range(N_DIL):
            dq, dk, dv = _dil_bwd(proj[f"q{g}"], proj[f"k{g}"], proj[f"v{g}"], dos[g], cs[g], sv["lses"][g], g,
                                  f"{tag}_att{g}")
            dproj.update({f"q{g}": dq, f"k{g}": dk, f"v{g}": dv})
    dils = _layer_dils(i)
    dhs = [None] * len(dils)
    dws = {}
    for nm, w in p["win"].items():
        o = _seg_order(nm)
        dhs[o] = _matmul(dproj[nm], w, tb=True, out_dtype=F32, add=dhs[o], name=f"{tag}_dh_{nm}")
        dws[nm] = _matmul(sv["h"][o], dproj[nm], ta=True, out_dtype=BF16, name=f"{tag}_dw_{nm}")
    gr["w_in"] = _join_dw_in(i, dws)
    dx, dxb, dnorm_g = _rmsnorm_bwd(sv["x"], p["norm_g"], dhs, dx_out, tag + "_norm", dils)
    gr["norm_g"] = dnorm_g[0]
    return dx, dxb, dmem_n, gr


def _pad_heads(v):
    return jnp.pad(v.reshape(1, SSD_H), ((0, 0), (0, DT_PAD)))


def _layer_params(i, small, w_in, w_mem_kv, w_out):
    p = {"norm_g": small[f"norm_g_{i}"], "win": _split_w_in(i, w_in), "wmkv": w_mem_kv, "wout": w_out}
    if _is_ssd(i):
        p.update(conv_w=small[f"conv_w_{i}"], conv_b=small[f"conv_b_{i}"], ssd_norm_g=small[f"ssd_norm_g_{i}"],
                 dt_bias_p=_pad_heads(small[f"dt_bias_{i}"]), a_log_p=_pad_heads(small[f"a_log_{i}"]),
                 dskip_lane=jnp.repeat(small[f"d_skip_{i}"], SSD_P).reshape(1, SSD_TOK))
    return p


def _local_step(x, mem, target, small, weights, emit):
    mem_b = _rmsnorm_fwd(mem, small["mem_norm_g"], "mem_norm")[0]
    params, saved = [], []
    for i in range(DEPTH):
        big, x = weights(i, x)
        params.append(_layer_params(i, small, *big))
        x, sv = _layer_fwd(i, x, mem_b, params[i])
        saved.append(sv)
    loss, dx, dxb, dfinal = _final_loss(x, small["final_norm_g"], target)
    grads = {"final_norm_g": dfinal[0]}
    dmem_n = None
    for i in reversed(range(DEPTH)):
        dx, dxb, dmem_n, gr = _layer_bwd(i, saved[i], dx, dxb, dmem_n, mem_b, params[i])
        dx, dxb = emit(i, {nm: gr.pop(nm) for nm in BIG}, dx, dxb)
        grads.update({f"{nm}_{i}": g for nm, g in gr.items()})
    _, _, dmem_g = _rmsnorm_bwd(mem, small["mem_norm_g"], [dmem_n], None, "mem_norm_b")
    grads["mem_norm_g"] = dmem_g[0]
    return loss[0, 0], dx, grads


BIG = ("w_in", "w_mem_kv", "w_out")
SMALL = [n for n in WEIGHTS if not n.startswith(BIG)]
PACK_ROWS = 8 * LANES
GATHER_COLLECTIVE_ID = 0
SCATTER_COLLECTIVE_ID = 4
SWAP_COLLECTIVE_ID = 8


def _pack(vals):
    flat = jnp.concatenate([v.reshape(-1).astype(F32) for v in vals])
    padded = -(-flat.shape[0] // PACK_ROWS) * PACK_ROWS
    return jnp.pad(flat, (0, padded - flat.shape[0])).reshape(padded // LANES, LANES)


def _train_step(a, local_step):
    x, y, c = lax.axis_index("x"), lax.axis_index("y"), lax.axis_index("c")
    me = 4 * x + 2 * y + c
    gathered_w = {}
    conv_names = [n for n in SMALL if n.startswith("conv_w")]
    conv_full = _gather_two_level([a[n] for n in conv_names], "gather_conv")

    def gather(i, after):
        shards = [a[f"{nm}_{i}"].astype(BF16) for nm in BIG]
        if after:
            shards = lax.optimization_barrier((*shards, *after))[:len(shards)]
        if i == 0:
            g_in = _gather_two_level(shards[:1], "gather_w0_in")
            rest = lax.optimization_barrier((*shards[1:], g_in[0], conv_full[0]))[:len(shards) - 1]
            gathered_w[i] = [*g_in, *_sc_gather(rest, "gather_w0_rest", GATHER_COLLECTIVE_ID)]
        else:
            gathered_w[i] = _sc_gather(shards, f"gather_w{i}", GATHER_COLLECTIVE_ID + i)

    def weights(i, act):
        if i == 0:
            gather(0, ())
            gather(1, (gathered_w[0][0], conv_full[0]))
        elif i == 1:
            gather(2, (act,))
            gather(3, (act,))
        raw = gathered_w.pop(i)
        if i > 0:
            *raw, act = lax.optimization_barrier((*raw, act))
        g_in, g_kv, g_out = raw
        whole = (jnp.transpose(g_in, (1, 0, 2)).reshape(g_in.shape[1], N_DEV * g_in.shape[2]),
                 g_kv.reshape(N_DEV * g_kv.shape[1], g_kv.shape[2]),
                 g_out.reshape(N_DEV * g_out.shape[1], g_out.shape[2]))
        return whole, act

    small = {n: a[n] for n in SMALL}
    for n, gathered in zip(conv_names, conv_full):
        small[n] = jnp.transpose(gathered, (1, 0, 2)).reshape(gathered.shape[1], N_DEV * gathered.shape[2])

    core = c.astype(jnp.int32).reshape(1)
    landed = {}

    def reduce_scatter(i, gr, dx, dxb):
        d, cols = gr["w_in"].shape
        parts = [jnp.transpose(gr["w_in"].reshape(d, N_DEV, cols // N_DEV), (1, 0, 2))]
        for nm in BIG[1:]:
            parts.append(gr[nm].reshape(N_DEV, gr[nm].shape[0] // N_DEV, gr[nm].shape[1]))
        swapped = _sc_sibling_swap(parts, f"swap_w{i}", SWAP_COLLECTIVE_ID + i)
        sums = [_chip_sum(p, s, core, f"chipsum_{nm}_{i}") for nm, p, s in zip(BIG, parts, swapped)]
        landed[i] = _sc_chip_exchange(sums, f"scatter_w{i}", SCATTER_COLLECTIVE_ID + i)
        if i == 1:
            tied = lax.optimization_barrier((dx, dxb, *landed[3], *landed[2]))
            dx, dxb = tied[:2]
            landed[3], landed[2] = list(tied[2:2 + len(BIG)]), list(tied[2 + len(BIG):])
        return dx, dxb

    loss_local, grad_x, grads = local_step(a["x"][0], a["mem"][0], a["loss_target"][0], small, weights, reduce_scatter)

    res = {}
    for i in reversed(range(DEPTH)):
        for nm, p in zip(BIG, landed[i]):
            n = f"{nm}_{i}"
            res[n] = _adamw(p, a[n], a["m_" + n], a["v_" + n], "adamw_" + n)

    gathered = _gather_two_level([_pack([grads[n] for n in SMALL] + [loss_local.reshape(1)])], "gather_small")[0]
    zero_conv = lambda pre: ([jnp.zeros(small[n].shape, F32) if n in conv_names else a[pre + n] for n in SMALL]
                             + [jnp.zeros((1,), F32)])
    packed = _adamw(gathered, _pack(zero_conv("")), _pack(zero_conv("m_")), _pack(zero_conv("v_")), "adamw_small")
    loss = packed[0].reshape(-1)[sum(math.prod(small[n].shape) for n in SMALL)]
    off = 0
    for n in SMALL:
        size = math.prod(small[n].shape)
        if n in conv_names:
            rows, cols = a[n].shape
            whole = gathered.reshape(N_DEV, -1)[:, off:off + size].reshape(N_DEV, rows, N_DEV * cols)
            mine = lax.dynamic_slice_in_dim(whole, me * cols, cols, axis=2)
            res[n] = _adamw(mine, a[n], a["m_" + n], a["v_" + n], "adamw_" + n)
        else:
            res[n] = [o.reshape(-1)[off:off + size].reshape(a[n].shape) for o in packed]
        off += size
    outs = [loss, grad_x[None]]
    for k in range(4):
        outs += [res[n][k] for n in WEIGHTS]
    return tuple(outs)


def kernel(x, mem, mem_norm_g, final_norm_g, norm_g_0, w_in_0, conv_w_0, conv_b_0, dt_bias_0, a_log_0, d_skip_0, ssd_norm_g_0, w_mem_kv_0, w_out_0, norm_g_1, w_in_1, w_mem_kv_1, w_out_1, norm_g_2, w_in_2, conv_w_2, conv_b_2, dt_bias_2, a_log_2, d_skip_2, ssd_norm_g_2, w_mem_kv_2, w_out_2, norm_g_3, w_in_3, w_mem_kv_3, w_out_3, loss_target, m_mem_norm_g, m_final_norm_g, m_norm_g_0, m_w_in_0, m_conv_w_0, m_conv_b_0, m_dt_bias_0, m_a_log_0, m_d_skip_0, m_ssd_norm_g_0, m_w_mem_kv_0, m_w_out_0, m_norm_g_1, m_w_in_1, m_w_mem_kv_1, m_w_out_1, m_norm_g_2, m_w_in_2, m_conv_w_2, m_conv_b_2, m_dt_bias_2, m_a_log_2, m_d_skip_2, m_ssd_norm_g_2, m_w_mem_kv_2, m_w_out_2, m_norm_g_3, m_w_in_3, m_w_mem_kv_3, m_w_out_3, v_mem_norm_g, v_final_norm_g, v_norm_g_0, v_w_in_0, v_conv_w_0, v_conv_b_0, v_dt_bias_0, v_a_log_0, v_d_skip_0, v_ssd_norm_g_0, v_w_mem_kv_0, v_w_out_0, v_norm_g_1, v_w_in_1, v_w_mem_kv_1, v_w_out_1, v_norm_g_2, v_w_in_2, v_conv_w_2, v_conv_b_2, v_dt_bias_2, v_a_log_2, v_d_skip_2, v_ssd_norm_g_2, v_w_mem_kv_2, v_w_out_2, v_norm_g_3, v_w_in_3, v_w_mem_kv_3, v_w_out_3):
    vals = locals()
    return _train_step({n: vals[n] for n in INPUTS}, _local_step)
```
